```python
import jax, jax.numpy as jnp
from jax import lax
import numpy as np

D_MODEL = 1024
BATCH = 2
SEQ = 8192
DEPTH = 1

N_META = 16
CONV_CH = 512
CONV_WIDTH = 31
N_HEADS = 8
HEAD_DIM = 64
ATTN_W = N_HEADS * HEAD_DIM
MIX_W = CONV_CH + ATTN_W
IDX_HEADS = 8
IDX_DIM = 64
INDEX_TOPK = 256
Q_BLOCK = 128
ROPE_DIM = HEAD_DIM // 4
ROPE_THETA = 500000.0
N_EXPERTS = 64
TOP_K = 8
N_GROUPS = 8
TOPK_GROUPS = 4
D_EXPERT = 256
D_SHARED = 256
ROUTED_SCALE = 2.5
EXPERT_BLOCK = 256
LN_EPS = 1e-5
DN_ALPHA = (2.0 * DEPTH) ** 0.25
DN_BETA = (8.0 * DEPTH) ** -0.25
SPLIT_SIZES = (CONV_CH, CONV_CH, ATTN_W, ATTN_W, ATTN_W, IDX_HEADS * IDX_DIM, IDX_DIM, IDX_HEADS)
IN_W = sum(SPLIT_SIZES)
SPLIT_POINTS = [int(v) for v in np.cumsum(SPLIT_SIZES)[:-1]]

kernel_name = 'hymba_conformer_dsa_moe_deepnorm'


def layer_norm(x, g, b):
    xf = x.astype(jnp.float32)
    mu = jnp.mean(xf, axis=-1, keepdims=True)
    xc = xf - mu
    var = jnp.mean(xc * xc, axis=-1, keepdims=True)
    y = xc * lax.rsqrt(var + LN_EPS) * g.astype(jnp.float32) + b.astype(jnp.float32)
    return y.astype(x.dtype)


def rope_partial(x, pos):
    half = ROPE_DIM // 2
    inv = jnp.power(ROPE_THETA, -2.0 * jnp.arange(half, dtype=jnp.float32) / ROPE_DIM)
    ang = pos[:, None] * inv[None, :]
    cos = jnp.cos(ang)[:, None, :]
    sin = jnp.sin(ang)[:, None, :]
    xr = x[..., :ROPE_DIM].astype(jnp.float32)
    x1, x2 = xr[..., :half], xr[..., half:]
    rot = jnp.concatenate([x1 * cos - x2 * sin, x2 * cos + x1 * sin], axis=-1).astype(x.dtype)
    return jnp.concatenate([rot, x[..., ROPE_DIM:]], axis=-1)


def conformer_conv_group(a, g, conv_w, conv_b, lnc_g, lnc_b):
    u = a * jax.nn.sigmoid(g)
    u = lax.conv_general_dilated(u, conv_w[:, None, :].astype(u.dtype), (1,),
                                 [(CONV_WIDTH - 1, 0)],
                                 dimension_numbers=('NWC', 'WIO', 'NWC'),
                                 feature_group_count=CONV_CH) + conv_b
    return jax.nn.silu(layer_norm(u, lnc_g, lnc_b))


def dsa_attention(q, k, v, qi, ki, wi, k_sel):
    B, T = k.shape[0], k.shape[1]
    nb = -(-T // Q_BLOCK)
    pad = nb * Q_BLOCK - T

    def to_blocks(t):
        t = jnp.pad(t, [(0, 0), (0, pad)] + [(0, 0)] * (t.ndim - 2))
        return jnp.moveaxis(t.reshape((B, nb, Q_BLOCK) + t.shape[2:]), 1, 0)

    qpos = jnp.arange(nb * Q_BLOCK).reshape(nb, Q_BLOCK)
    kpos = jnp.arange(T)
    ki32 = ki.astype(jnp.float32)
    bidx = jnp.arange(B)[:, None, None]

    def one_block(args):
        qb, qib, wib, qp = args
        s = jnp.einsum('bqhd,bkd->bqhk', qib.astype(jnp.float32), ki32)
        score = jnp.einsum('bqh,bqhk->bqk', wib.astype(jnp.float32), jax.nn.relu(s)) * (IDX_DIM ** -0.5)
        causal = kpos[None, :] <= qp[:, None]
        score = jnp.where(causal[None], score, -jnp.inf)
        _, sel = lax.top_k(score, k_sel)
        valid = sel <= qp[None, :, None]
        kg = k[bidx, sel]
        vg = v[bidx, sel]
        logits = jnp.einsum('bqhd,bqkhd->bqhk', qb.astype(jnp.float32), kg.astype(jnp.float32)) * (HEAD_DIM ** -0.5)
        logits = jnp.where(valid[:, :, None, :], logits, -jnp.inf)
        p = jax.nn.softmax(logits, axis=-1)
        return jnp.einsum('bqhk,bqkhd->bqhd', p.astype(v.dtype), vg)

    out = lax.map(one_block, (to_blocks(q), to_blocks(qi), to_blocks(wi), qpos))
    out = jnp.moveaxis(out, 0, 1).reshape(B, nb * Q_BLOCK, ATTN_W)
    return out[:, :T]


def hybrid_mixer(h, w_in, conv_w, conv_b, lnc_g, lnc_b, lnk_g, lnk_b, w_out, k_sel):
    B, T, _ = h.shape
    proj = h @ w_in
    a, g, q, k, v, qi, ki, wi = jnp.split(proj, SPLIT_POINTS, axis=-1)
    conv_out = conformer_conv_group(a, g, conv_w, conv_b, lnc_g, lnc_b)
    pos = jnp.arange(T, dtype=jnp.float32)
    q = rope_partial(q.reshape(B, T, N_HEADS, HEAD_DIM), pos)
    k = rope_partial(k.reshape(B, T, N_HEADS, HEAD_DIM), pos)
    v = v.reshape(B, T, N_HEADS, HEAD_DIM)
    qi = rope_partial(qi.reshape(B, T, IDX_HEADS, IDX_DIM), pos)
    ki = rope_partial(layer_norm(ki, lnk_g, lnk_b)[:, :, None, :], pos)[:, :, 0, :]
    wi = wi * (IDX_HEADS ** -0.5)
    attn_out = dsa_attention(q, k, v, qi, ki, wi, k_sel)
    return jnp.concatenate([conv_out, attn_out.astype(conv_out.dtype)], axis=-1) @ w_out


def route(xf, w_router, r_bias):
    N = xf.shape[0]
    scores = jax.nn.sigmoid(xf.astype(jnp.float32) @ w_router.astype(jnp.float32))
    biased = scores + r_bias.astype(jnp.float32)
    grp = biased.reshape(N, N_GROUPS, N_EXPERTS // N_GROUPS)
    grp_score = jnp.sum(lax.top_k(grp, 2)[0], axis=-1)
    _, top_g = lax.top_k(grp_score, TOPK_GROUPS)
    gmask = jnp.any(top_g[:, :, None] == jnp.arange(N_GROUPS)[None, None, :], axis=1)
    emask = jnp.repeat(gmask, N_EXPERTS // N_GROUPS, axis=1)
    _, idx = lax.top_k(jnp.where(emask, biased, -jnp.inf), TOP_K)
    w = jnp.take_along_axis(scores, idx, axis=1)
    w = w / jnp.sum(w, axis=-1, keepdims=True) * ROUTED_SCALE
    return idx, w


def routed_experts(xf, idx, wts, w_gate, w_up, w_down):
    N, D = xf.shape
    NK = N * TOP_K
    e_flat = idx.reshape(NK)
    order = jnp.argsort(e_flat)
    e_sorted = e_flat[order]
    tok_sorted = (order // TOP_K).astype(jnp.int32)
    w_sorted = wts.reshape(NK)[order]
    counts = jnp.bincount(e_flat, length=N_EXPERTS)
    padded = (counts + EXPERT_BLOCK - 1) // EXPERT_BLOCK * EXPERT_BLOCK
    pad_end = jnp.cumsum(padded)
    pad_start = pad_end - padded
    start = jnp.cumsum(counts) - counts
    dest = pad_start[e_sorted] + jnp.arange(NK) - start[e_sorted]
    n_blocks = -(-NK // EXPERT_BLOCK) + N_EXPERTS
    P = n_blocks * EXPERT_BLOCK
    buf_tok = jnp.zeros((P,), jnp.int32).at[dest].set(tok_sorted)
    buf_w = jnp.zeros((P,), wts.dtype).at[dest].set(w_sorted)
    blk_exp = jnp.minimum(jnp.searchsorted(pad_end, jnp.arange(n_blocks) * EXPERT_BLOCK, side='right'),
                          N_EXPERTS - 1)

    def one(args):
        tok, wt, e = args
        xb = xf[tok]
        hdn = jax.nn.silu(xb @ w_gate[e]) * (xb @ w_up[e])
        return (hdn @ w_down[e]) * wt[:, None].astype(xf.dtype)

    ys = lax.map(one, (buf_tok.reshape(n_blocks, EXPERT_BLOCK),
                       buf_w.reshape(n_blocks, EXPERT_BLOCK), blk_exp))
    return jnp.zeros_like(xf).at[buf_tok].add(ys.reshape(P, D))


def moe_ffn(h, w_router, r_bias, w_gate, w_up, w_down, ws_gate, ws_up, ws_down):
    B, T, D = h.shape
    xf = h.reshape(B * T, D)
    idx, wts = route(xf, w_router, r_bias)
    routed = routed_experts(xf, idx, wts, w_gate, w_up, w_down)
    shared = (jax.nn.silu(xf @ ws_gate) * (xf @ ws_up)) @ ws_down
    return (routed + shared).reshape(B, T, D)


def setup_inputs(seed: int = 0) -> dict:
    key = jax.random.key(seed)
    ks = jax.random.split(key, 26)
    L = DEPTH
    f32 = jnp.float32

    def nrm(k, shape, s):
        return jax.random.normal(k, shape, f32) * s

    return {
        'x': nrm(ks[0], (BATCH, SEQ, D_MODEL), 1.0),
        'meta_tokens': nrm(ks[1], (N_META, D_MODEL), 1.0),
        'ln_emb_g': 1.0 + nrm(ks[2], (D_MODEL,), 0.02),
        'ln_emb_b': nrm(ks[3], (D_MODEL,), 0.02),
        'w_in': nrm(ks[4], (L, D_MODEL, IN_W), D_MODEL ** -0.5),
        'conv_w': nrm(ks[5], (L, CONV_WIDTH, CONV_CH), CONV_WIDTH ** -0.5),
        'conv_b': nrm(ks[6], (L, CONV_CH), 0.02),
        'ln_conv_g': 1.0 + nrm(ks[7], (L, CONV_CH), 0.02),
        'ln_conv_b': nrm(ks[8], (L, CONV_CH), 0.02),
        'ln_kidx_g': 1.0 + nrm(ks[9], (L, IDX_DIM), 0.02),
        'ln_kidx_b': nrm(ks[10], (L, IDX_DIM), 0.02),
        'w_out': nrm(ks[11], (L, MIX_W, D_MODEL), DN_BETA * MIX_W ** -0.5),
        'ln1_g': 1.0 + nrm(ks[12], (L, D_MODEL), 0.02),
        'ln1_b': nrm(ks[13], (L, D_MODEL), 0.02),
        'w_router': nrm(ks[14], (L, D_MODEL, N_EXPERTS), D_MODEL ** -0.5),
        'router_bias': nrm(ks[15], (L, N_EXPERTS), 0.01),
        'w_gate': nrm(ks[16], (L, N_EXPERTS, D_MODEL, D_EXPERT), D_MODEL ** -0.5),
        'w_up': nrm(ks[17], (L, N_EXPERTS, D_MODEL, D_EXPERT), D_MODEL ** -0.5),
        'w_down': nrm(ks[18], (L, N_EXPERTS, D_EXPERT, D_MODEL), DN_BETA * D_EXPERT ** -0.5),
        'ws_gate': nrm(ks[19], (L, D_MODEL, D_SHARED), D_MODEL ** -0.5),
        'ws_up': nrm(ks[20], (L, D_MODEL, D_SHARED), D_MODEL ** -0.5),
        'ws_down': nrm(ks[21], (L, D_SHARED, D_MODEL), DN_BETA * D_SHARED ** -0.5),
        'ln2_g': 1.0 + nrm(ks[22], (L, D_MODEL), 0.02),
        'ln2_b': nrm(ks[23], (L, D_MODEL), 0.02),
    }


def reference(x, meta_tokens, ln_emb_g, ln_emb_b, w_in, conv_w, conv_b, ln_conv_g, ln_conv_b,
              ln_kidx_g, ln_kidx_b, w_out, ln1_g, ln1_b, w_router, router_bias, w_gate, w_up,
              w_down, ws_gate, ws_up, ws_down, ln2_g, ln2_b):
    B, L_seq, D = x.shape
    k_sel = min(INDEX_TOPK, L_seq // 4)
    meta = jnp.broadcast_to(meta_tokens[None].astype(x.dtype), (B, N_META, D))
    h = jnp.concatenate([meta, x], axis=1)
    h = layer_norm(h, ln_emb_g, ln_emb_b)
    for l in range(DEPTH):
        mix = hybrid_mixer(h, w_in[l], conv_w[l], conv_b[l], ln_conv_g[l], ln_conv_b[l],
                           ln_kidx_g[l], ln_kidx_b[l], w_out[l], k_sel)
        h = layer_norm(DN_ALPHA * h + mix, ln1_g[l], ln1_b[l])
        ffn = moe_ffn(h, w_router[l], router_bias[l], w_gate[l], w_up[l], w_down[l],
                      ws_gate[l], ws_up[l], ws_down[l])
        h = layer_norm(DN_ALPHA * h + ffn, ln2_g[l], ln2_b[l])
    return h[:, N_META:]
```

```python
import functools

import numpy as np
import jax
import jax.numpy as jnp
from jax import lax
from jax.experimental import pallas as pl
from jax.experimental.pallas import tpu as pltpu

N_META = 16
CONV_CH = 512
CONV_WIDTH = 31
N_HEADS = 8
HEAD_DIM = 64
ATTN_W = N_HEADS * HEAD_DIM
IDX_HEADS = 8
IDX_DIM = 64
INDEX_TOPK = 256
ROPE_DIM = HEAD_DIM // 4
ROPE_HALF = ROPE_DIM // 2
ROPE_THETA = 500000.0
N_EXPERTS = 64
TOP_K = 8
N_GROUPS = 8
GROUP_SIZE = N_EXPERTS // N_GROUPS
TOPK_GROUPS = 4
ROUTED_SCALE = 2.5
LN_EPS = 1e-5
DEPTH = 1
DN_ALPHA = (2.0 * DEPTH) ** 0.25

LANES = 128
Q_TILE = 128
K_CHUNK = 256
SEQ_ALIGN = 256
CONV_HALO = 32
VMEM_LIMIT = 56 * 1024 * 1024

F32 = jnp.float32
BF16 = jnp.bfloat16
NEG_INF = float("-inf")
INT_MIN = -2 ** 31
KEY_NEG_INF = -2139095041


def _dot(a, b):
    return jnp.dot(a, b, preferred_element_type=F32)


def _dot_nt(a, b):
    return lax.dot_general(a, b, (((1,), (1,)), ((), ())), preferred_element_type=F32)


def _layer_norm_rows(x, g, b):
    mu = jnp.mean(x, axis=-1, keepdims=True)
    xc = x - mu
    var = jnp.mean(xc * xc, axis=-1, keepdims=True)
    return xc * lax.rsqrt(var + LN_EPS) * g + b


def _pick_tile(n, candidates):
    for c in candidates:
        if n % c == 0:
            return c
    raise ValueError(f"no tile for {n}")


def _rope_rows(x, c_tab, s1_tab, s2_tab):
    outs = []
    for j in range(x.shape[1] // LANES):
        xs = x[:, j * LANES:(j + 1) * LANES]
        up = pltpu.roll(xs, LANES - ROPE_HALF, axis=1)
        dn = pltpu.roll(xs, ROPE_HALF, axis=1)
        outs.append(xs * c_tab + up * s1_tab + dn * s2_tab)
    return jnp.concatenate(outs, axis=1)


def _rope_cols(xt, cos_t, sin_t, heads):
    r = xt.shape[1]
    x3 = xt.reshape(heads, HEAD_DIM, r)
    x1 = x3[:, 0:ROPE_HALF, :]
    x2 = x3[:, ROPE_HALF:ROPE_DIM, :]
    n1 = x1 * cos_t - x2 * sin_t
    n2 = x2 * cos_t + x1 * sin_t
    out = jnp.concatenate([n1, n2, x3[:, ROPE_DIM:, :]], axis=1)
    return out.reshape(heads * HEAD_DIM, r)


def _inproj_kernel(h_ref, ctab_ref, s1tab_ref, s2tab_ref, cost_ref, sint_ref,
                   lng_ref, lnb_ref, wag_ref, wq_ref, wv_ref, wqi_ref, wkt_ref, wkit_ref, wwi_ref,
                   cw_ref, cb_ref, lncg_ref, lncb_ref, lnkg_ref, lnkb_ref,
                   hn_ref, conv_ref, q_ref, kt_ref, v_ref, qi_ref, kit_ref, wi_ref,
                   ubuf_ref):
    t = pl.program_id(1)
    tr = h_ref.shape[1]

    hn = _layer_norm_rows(h_ref[0], lng_ref[...], lnb_ref[...])
    hn_ref[0] = hn
    xb = hn.astype(BF16)

    ag = _dot(xb, wag_ref[...])
    u = ag[:, :CONV_CH] * jax.nn.sigmoid(ag[:, CONV_CH:])

    @pl.when(t == 0)
    def _():
        ubuf_ref[0:CONV_HALO, :] = jnp.zeros((CONV_HALO, CONV_CH), F32)

    ubuf_ref[CONV_HALO:CONV_HALO + tr, :] = u
    base = CONV_HALO - (CONV_WIDTH - 1)
    acc = jnp.zeros((tr, CONV_CH), F32)
    for j in range(CONV_WIDTH):
        acc = acc + cw_ref[j:j + 1, :] * ubuf_ref[base + j:base + j + tr, :]
    ubuf_ref[0:CONV_HALO, :] = ubuf_ref[tr:tr + CONV_HALO, :]
    c = _layer_norm_rows(acc + cb_ref[...], lncg_ref[...], lncb_ref[...])
    conv_ref[0] = (c * jax.nn.sigmoid(c)).astype(conv_ref.dtype)

    ctab, s1tab, s2tab = ctab_ref[...], s1tab_ref[...], s2tab_ref[...]
    cos_t, sin_t = cost_ref[...], sint_ref[...]

    q = _rope_rows(_dot(xb, wq_ref[...]), ctab, s1tab, s2tab)
    q_ref[0] = (q * (HEAD_DIM ** -0.5)).astype(q_ref.dtype)
    kt = _rope_cols(_dot_nt(wkt_ref[...], xb), cos_t, sin_t, N_HEADS)
    kt_ref[0] = kt.astype(kt_ref.dtype)
    v_ref[0] = _dot(xb, wv_ref[...]).astype(v_ref.dtype)

    qi = _rope_rows(_dot(xb, wqi_ref[...]), ctab, s1tab, s2tab)
    qi_ref[0] = qi.astype(qi_ref.dtype)
    kit = _dot_nt(wkit_ref[...], xb)
    mu = jnp.mean(kit, axis=0, keepdims=True)
    kc = kit - mu
    var = jnp.mean(kc * kc, axis=0, keepdims=True)
    kit = kc * lax.rsqrt(var + LN_EPS) * lnkg_ref[...] + lnkb_ref[...]
    kit = _rope_cols(kit, cos_t, sin_t, 1).astype(kit_ref.dtype)
    kit_ref[0] = jnp.concatenate([kit, kit], axis=0)
    wi_ref[0] = _dot(xb, wwi_ref[...]) * (IDX_HEADS ** -0.5)


def _inproj_call(hp, tabs, ln_g, ln_b, weights, conv_w, conv_b, lnc_g, lnc_b, lnk_g, lnk_b):
    B, tp, D = hp.shape
    tr = _pick_tile(tp, (768, 512, 256))
    nt = tp // tr
    ctab, s1tab, s2tab, cos_t, sin_t = tabs
    wag, wq, wv, wqi, wkt, wkit, wwi = weights

    def row_spec(w):
        return pl.BlockSpec((1, tr, w), lambda b, t: (b, t, 0))

    def col_spec(r):
        return pl.BlockSpec((1, r, tr), lambda b, t: (b, 0, t))

    def full(a):
        return pl.BlockSpec(a.shape, lambda b, t: (0,) * a.ndim)

    tab_row = pl.BlockSpec((tr, LANES), lambda b, t: (t, 0))
    tab_col = pl.BlockSpec((ROPE_HALF, tr), lambda b, t: (0, t))
    consts = [ln_g, ln_b, wag, wq, wv, wqi, wkt, wkit, wwi, conv_w, conv_b, lnc_g, lnc_b, lnk_g, lnk_b]
    out_shape = [
        jax.ShapeDtypeStruct((B, tp, D), F32),
        jax.ShapeDtypeStruct((B, tp, CONV_CH), BF16),
        jax.ShapeDtypeStruct((B, tp, ATTN_W), BF16),
        jax.ShapeDtypeStruct((B, ATTN_W, tp), BF16),
        jax.ShapeDtypeStruct((B, tp, ATTN_W), BF16),
        jax.ShapeDtypeStruct((B, tp, ATTN_W), BF16),
        jax.ShapeDtypeStruct((B, 2 * IDX_DIM, tp), BF16),
        jax.ShapeDtypeStruct((B, tp, LANES), F32),
    ]
    out_specs = [row_spec(D), row_spec(CONV_CH), row_spec(ATTN_W), col_spec(ATTN_W), row_spec(ATTN_W),
                 row_spec(ATTN_W), col_spec(2 * IDX_DIM), row_spec(LANES)]
    return pl.pallas_call(
        _inproj_kernel,
        grid=(B, nt),
        in_specs=[row_spec(D), tab_row, tab_row, tab_row, tab_col, tab_col] + [full(a) for a in consts],
        out_specs=out_specs,
        out_shape=out_shape,
        scratch_shapes=[pltpu.VMEM((CONV_HALO + tr, CONV_CH), F32)],
        compiler_params=pltpu.CompilerParams(
            dimension_semantics=("arbitrary", "arbitrary"), vmem_limit_bytes=VMEM_LIMIT),
        name="inproj",
    )(hp, ctab, s1tab, s2tab, cos_t, sin_t, *consts)


def _key_to_float(key):
    bits = jnp.where(key >= 0, key, key ^ jnp.int32(0x7FFFFFFF))
    f = pltpu.bitcast(bits, F32)
    return jnp.where(key < jnp.int32(KEY_NEG_INF), NEG_INF, f)


def _lane_fold(x):
    out = x[:, 0:LANES]
    for j in range(1, x.shape[1] // LANES):
        out = out + x[:, j * LANES:(j + 1) * LANES]
    return out


def _dsa_kernel(q_ref, qi_ref, wi_ref, kt_ref, kit_ref, v_ref, o_ref, sc_ref, *, k_sel):
    i = pl.program_id(1)
    tq = q_ref.shape[1]
    n_chunks = (i * tq + tq + K_CHUNK - 1) // K_CHUNK
    qpos = i * tq + lax.broadcasted_iota(jnp.int32, (tq, 1), 0)
    lane_k = lax.broadcasted_iota(jnp.int32, (1, K_CHUNK), 1)
    lane_in_pair = lax.broadcasted_iota(jnp.int32, (1, LANES), 1)
    low_half = lane_in_pair < HEAD_DIM

    def head_slab(ref, h):
        slab = ref[0, :, (h // 2) * LANES:(h // 2 + 1) * LANES]
        keep = low_half if h % 2 == 0 else jnp.logical_not(low_half)
        return jnp.where(keep, slab, jnp.zeros_like(slab))

    wi = wi_ref[0]
    qi_heads = [head_slab(qi_ref, h) for h in range(IDX_HEADS)]
    w_heads = [wi[:, h:h + 1] for h in range(IDX_HEADS)]

    def score_body(c, carry):
        k0 = pl.multiple_of(c * K_CHUNK, K_CHUNK)
        kic = kit_ref[0, :, pl.ds(k0, K_CHUNK)]
        acc = jnp.zeros((tq, K_CHUNK), F32)
        for h in range(IDX_HEADS):
            acc = acc + w_heads[h] * jnp.maximum(_dot(qi_heads[h], kic), 0.0)
        acc = acc * (IDX_DIM ** -0.5)
        sc_ref[:, pl.ds(k0, K_CHUNK)] = jnp.where(k0 + lane_k <= qpos, acc, NEG_INF)
        return carry

    lax.fori_loop(0, n_chunks, score_body, 0)

    def count_ge(tf):
        def body(c, cnt):
            k0 = pl.multiple_of(c * K_CHUNK, K_CHUNK)
            s = sc_ref[:, pl.ds(k0, K_CHUNK)]
            return cnt + _lane_fold(jnp.where(s >= tf, 1.0, 0.0))
        cnt = lax.fori_loop(0, n_chunks, body, jnp.zeros((tq, LANES), F32))
        return jnp.sum(cnt, axis=1, keepdims=True)

    def search_body(it, tkey):
        cand = tkey + lax.shift_left(jnp.int32(1), 31 - it)
        cnt = count_ge(_key_to_float(cand))
        return jnp.where(cnt >= k_sel, cand, tkey)

    tkey = lax.fori_loop(0, 32, search_body, jnp.full((tq, 1), INT_MIN, jnp.int32))
    thr = _key_to_float(tkey)

    def count2_body(c, carry):
        cge, cgt = carry
        k0 = pl.multiple_of(c * K_CHUNK, K_CHUNK)
        s = sc_ref[:, pl.ds(k0, K_CHUNK)]
        cge = cge + _lane_fold(jnp.where(s >= thr, 1.0, 0.0))
        cgt = cgt + _lane_fold(jnp.where(s > thr, 1.0, 0.0))
        return cge, cgt

    zero_cnt = jnp.zeros((tq, LANES), F32)
    cge, cgt = lax.fori_loop(0, n_chunks, count2_body, (zero_cnt, zero_cnt))
    n_ge = jnp.sum(cge, axis=1, keepdims=True)
    n_gt = jnp.sum(cgt, axis=1, keepdims=True)
    need = k_sel - n_gt
    has_ties = jnp.max(n_ge) > k_sel

    @pl.when(jnp.logical_not(has_ties))
    def _():
        def body(c, carry):
            k0 = pl.multiple_of(c * K_CHUNK, K_CHUNK)
            s = sc_ref[:, pl.ds(k0, K_CHUNK)]
            causal = k0 + lane_k <= qpos
            sc_ref[:, pl.ds(k0, K_CHUNK)] = jnp.where(
                s >= thr, jnp.where(causal, 0.0, NEG_INF), NEG_INF)
            return carry
        lax.fori_loop(0, n_chunks, body, 0)

    @pl.when(has_ties)
    def _():
        r_i = lax.broadcasted_iota(jnp.int32, (K_CHUNK, K_CHUNK), 0)
        c_i = lax.broadcasted_iota(jnp.int32, (K_CHUNK, K_CHUNK), 1)
        upper = jnp.where(r_i <= c_i, 1.0, 0.0).astype(BF16)

        def body(c, seen):
            k0 = pl.multiple_of(c * K_CHUNK, K_CHUNK)
            s = sc_ref[:, pl.ds(k0, K_CHUNK)]
            causal = k0 + lane_k <= qpos
            eq = jnp.where(s == thr, 1.0, 0.0)
            rank = _dot(eq.astype(BF16), upper) + seen
            keep_tie = jnp.where(rank <= need, eq, 0.0)
            sel = jnp.where(s > thr, 1.0, keep_tie)
            sc_ref[:, pl.ds(k0, K_CHUNK)] = jnp.where(
                sel > 0.0, jnp.where(causal, 0.0, NEG_INF), NEG_INF)
            return rank[:, K_CHUNK - 1:K_CHUNK]
        lax.fori_loop(0, n_chunks, body, jnp.zeros((tq, 1), F32))

    outs = []
    for h in range(N_HEADS):
        qh = head_slab(q_ref, h)
        pair = h // 2

        def attn_body(c, carry, qh=qh, pair=pair):
            m, l, acc = carry
            k0 = pl.multiple_of(c * K_CHUNK, K_CHUNK)
            kc = kt_ref[0, pair * LANES:(pair + 1) * LANES, pl.ds(k0, K_CHUNK)]
            lg = _dot(qh, kc) + sc_ref[:, pl.ds(k0, K_CHUNK)]
            m_new = jnp.maximum(m, jnp.max(lg, axis=1, keepdims=True))
            m_safe = jnp.where(m_new == NEG_INF, 0.0, m_new)
            p = jnp.exp(lg - m_safe)
            alpha = jnp.exp(m - m_safe)
            l = alpha * l + jnp.sum(p, axis=1, keepdims=True)
            vc = v_ref[0, pl.ds(k0, K_CHUNK), pair * LANES:(pair + 1) * LANES]
            acc = alpha * acc + _dot(p.astype(BF16), vc)
            return m_new, l, acc

        m0 = jnp.full((tq, 1), NEG_INF, F32)
        l0 = jnp.zeros((tq, 1), F32)
        a0 = jnp.zeros((tq, LANES), F32)
        _, l, acc = lax.fori_loop(0, n_chunks, attn_body, (m0, l0, a0))
        outs.append(acc / l)

    for pair in range(N_HEADS // 2):
        o_ref[0, :, pair * LANES:(pair + 1) * LANES] = jnp.where(
            low_half, outs[2 * pair], outs[2 * pair + 1]).astype(o_ref.dtype)


def _dsa_call(q, qi, wi, kt, kit, v, k_sel):
    B, tp, _ = q.shape
    nq = tp // Q_TILE

    def q_spec(w):
        return pl.BlockSpec((1, Q_TILE, w), lambda b, i: (b, i, 0))

    def per_batch(a):
        return pl.BlockSpec((1,) + a.shape[1:], lambda b, i: (b, 0, 0))

    return pl.pallas_call(
        functools.partial(_dsa_kernel, k_sel=k_sel),
        grid=(B, nq),
        in_specs=[q_spec(ATTN_W), q_spec(ATTN_W), q_spec(LANES), per_batch(kt), per_batch(kit), per_batch(v)],
        out_specs=q_spec(ATTN_W),
        out_shape=jax.ShapeDtypeStruct((B, tp, ATTN_W), BF16),
        scratch_shapes=[pltpu.VMEM((Q_TILE, tp), F32)],
        compiler_params=pltpu.CompilerParams(
            dimension_semantics=("arbitrary", "arbitrary"), vmem_limit_bytes=VMEM_LIMIT),
        name="dsa",
    )(q, qi, wi, kt, kit, v)


def _max_all(x):
    return jnp.max(jnp.max(x, axis=1, keepdims=True), axis=0, keepdims=True)


def _router_gates(logits_t, rbias):
    r = logits_t.shape[1]
    shape3 = (N_GROUPS, GROUP_SIZE, r)
    scores = jax.nn.sigmoid(logits_t).reshape(shape3)
    biased = scores + rbias.reshape(N_GROUPS, GROUP_SIZE, 1)
    in_grp = lax.broadcasted_iota(jnp.int32, shape3, 1).astype(F32)
    m1 = jnp.max(biased, axis=1, keepdims=True)
    first = jnp.min(jnp.where(biased == m1, in_grp, float(GROUP_SIZE)), axis=1, keepdims=True)
    m2 = jnp.max(jnp.where(in_grp == first, NEG_INF, biased), axis=1, keepdims=True)
    cur = m1 + m2

    grp_idx = lax.broadcasted_iota(jnp.int32, (N_GROUPS, 1, r), 0).astype(F32)
    grp_sel = jnp.zeros((N_GROUPS, 1, r), F32)
    for _ in range(TOPK_GROUPS):
        m = jnp.max(cur, axis=0, keepdims=True)
        pick = grp_idx == jnp.min(jnp.where(cur == m, grp_idx, float(N_GROUPS)), axis=0, keepdims=True)
        grp_sel = jnp.where(pick, 1.0, grp_sel)
        cur = jnp.where(pick, NEG_INF, cur)

    cur = jnp.where(jnp.broadcast_to(grp_sel, shape3) > 0.0, biased, NEG_INF)
    exp_idx = lax.broadcasted_iota(jnp.int32, shape3, 0).astype(F32) * GROUP_SIZE + in_grp
    chosen = jnp.zeros(shape3, F32)
    for _ in range(TOP_K):
        m = _max_all(cur)
        first = -_max_all(-jnp.where(cur == m, exp_idx, float(N_EXPERTS)))
        pick = exp_idx == first
        chosen = jnp.where(pick, 1.0, chosen)
        cur = jnp.where(pick, NEG_INF, cur)

    w = jnp.where(chosen > 0.0, scores, 0.0)
    denom = jnp.sum(jnp.sum(w, axis=1, keepdims=True), axis=0, keepdims=True)
    return (w / denom * ROUTED_SCALE).reshape(N_EXPERTS, r)


def _post_kernel(conv_ref, attn_ref, hn_ref, woc_ref, woa_ref, g1_ref, b1_ref,
                 wsg_ref, wsu_ref, wsd_ref, wrh_ref, wrl_ref, rb_ref,
                 xb_ref, base_ref, gates_ref):
    mix = _dot(conv_ref[...], woc_ref[...]) + _dot(attn_ref[...], woa_ref[...])
    h1 = _layer_norm_rows(DN_ALPHA * hn_ref[...] + mix, g1_ref[...], b1_ref[...])
    xb = h1.astype(BF16)
    xb_ref[...] = xb

    shared = jax.nn.silu(_dot(xb, wsg_ref[...])) * _dot(xb, wsu_ref[...])
    base_ref[...] = DN_ALPHA * h1 + _dot(shared.astype(BF16), wsd_ref[...])

    x_lo = (h1 - xb.astype(F32)).astype(BF16)
    logits_t = (_dot_nt(wrh_ref[...], xb) + _dot_nt(wrh_ref[...], x_lo) + _dot_nt(wrl_ref[...], xb))
    gates_t = _router_gates(logits_t, rb_ref[...])
    padded = jnp.concatenate([gates_t, jnp.zeros((LANES - N_EXPERTS, gates_t.shape[1]), F32)], axis=0)
    gates_ref[...] = padded.T


def _post_call(conv, attn, hn, woc, woa, g1, b1, wsg, wsu, wsd, wrh, wrl, rbias):
    n, D = hn.shape
    tr = _pick_tile(n, (768, 512, 256))

    def row_spec(w):
        return pl.BlockSpec((tr, w), lambda i: (i, 0))

    def full(a):
        return pl.BlockSpec(a.shape, lambda i: (0,) * a.ndim)

    consts = [woc, woa, g1, b1, wsg, wsu, wsd, wrh, wrl, rbias]
    return pl.pallas_call(
        _post_kernel,
        grid=(n // tr,),
        in_specs=[row_spec(CONV_CH), row_spec(ATTN_W), row_spec(D)] + [full(a) for a in consts],
        out_specs=[row_spec(D), row_spec(D), row_spec(LANES)],
        out_shape=[jax.ShapeDtypeStruct((n, D), BF16), jax.ShapeDtypeStruct((n, D), F32),
                   jax.ShapeDtypeStruct((n, LANES), F32)],
        compiler_params=pltpu.CompilerParams(
            dimension_semantics=("arbitrary",), vmem_limit_bytes=VMEM_LIMIT),
        name="post",
    )(conv, attn, hn, *consts)


def _moe_kernel(xb_ref, base_ref, gates_ref, wg_ref, wu_ref, wd_ref, g2_ref, b2_ref, o_ref, acc_ref):
    e = pl.program_id(1)

    @pl.when(e == 0)
    def _():
        acc_ref[...] = base_ref[...]

    x = xb_ref[...]
    hg = _dot(x, wg_ref[0].astype(BF16))
    hu = _dot(x, wu_ref[0].astype(BF16))
    hdn = (jax.nn.silu(hg) * hu).astype(BF16)
    lane = lax.broadcasted_iota(jnp.int32, gates_ref.shape, 1)
    gate = jnp.sum(jnp.where(lane == e, gates_ref[...], 0.0), axis=1, keepdims=True)
    acc_ref[...] += _dot(hdn, wd_ref[0].astype(BF16)) * gate

    @pl.when(e == pl.num_programs(1) - 1)
    def _():
        o_ref[...] = _layer_norm_rows(acc_ref[...], g2_ref[...], b2_ref[...])


def _moe_call(xb, base, gates, w_gate, w_up, w_down, g2, b2):
    n, D = xb.shape
    n_exp, _, d_exp = w_gate.shape
    tm = _pick_tile(n, (768, 512, 256))

    def row_spec(w):
        return pl.BlockSpec((tm, w), lambda i, e: (i, 0))

    vec = pl.BlockSpec((1, D), lambda i, e: (0, 0))
    return pl.pallas_call(
        _moe_kernel,
        grid=(n // tm, n_exp),
        in_specs=[row_spec(D), row_spec(D), row_spec(LANES),
                  pl.BlockSpec((1, D, d_exp), lambda i, e: (e, 0, 0)),
                  pl.BlockSpec((1, D, d_exp), lambda i, e: (e, 0, 0)),
                  pl.BlockSpec((1, d_exp, D), lambda i, e: (e, 0, 0)),
                  vec, vec],
        out_specs=row_spec(D),
        out_shape=jax.ShapeDtypeStruct((n, D), F32),
        scratch_shapes=[pltpu.VMEM((tm, D), F32)],
        compiler_params=pltpu.CompilerParams(
            dimension_semantics=("arbitrary", "arbitrary"), vmem_limit_bytes=VMEM_LIMIT),
        name="moe",
    )(xb, base, gates, w_gate, w_up, w_down, g2, b2)


def _rope_tables(tp):
    pos = jnp.arange(tp, dtype=F32)
    inv = jnp.power(ROPE_THETA, -2.0 * jnp.arange(ROPE_HALF, dtype=F32) / ROPE_DIM)
    ang = pos[:, None] * inv[None, :]
    cos, sin = jnp.cos(ang), jnp.sin(ang)
    zeros = jnp.zeros((tp, HEAD_DIM - ROPE_DIM), F32)
    zh = jnp.zeros((tp, ROPE_HALF), F32)
    c64 = jnp.concatenate([cos, cos, jnp.ones_like(zeros)], axis=1)
    s1_64 = jnp.concatenate([-sin, zh, zeros], axis=1)
    s2_64 = jnp.concatenate([zh, sin, zeros], axis=1)
    rep = LANES // HEAD_DIM
    return (jnp.tile(c64, (1, rep)), jnp.tile(s1_64, (1, rep)), jnp.tile(s2_64, (1, rep)),
            cos.T, sin.T)


def kernel(x, meta_tokens, ln_emb_g, ln_emb_b, w_in, conv_w, conv_b, ln_conv_g, ln_conv_b, ln_kidx_g, ln_kidx_b, w_out, ln1_g, ln1_b, w_router, router_bias, w_gate, w_up, w_down, ws_gate, ws_up, ws_down, ln2_g, ln2_b):
    B, seq, D = x.shape
    assert w_in.shape[0] == DEPTH
    k_sel = min(INDEX_TOPK, seq // 4)
    t_real = N_META + seq
    tp = -(-t_real // SEQ_ALIGN) * SEQ_ALIGN

    meta = jnp.broadcast_to(meta_tokens[None].astype(x.dtype), (B, N_META, D))
    hp = jnp.concatenate([meta, x, jnp.zeros((B, tp - t_real, D), x.dtype)], axis=1)

    def row(a):
        return a.reshape(1, -1).astype(F32)

    w = w_in[0]
    o = 0
    parts = []
    for width in (CONV_CH, CONV_CH, ATTN_W, ATTN_W, ATTN_W, IDX_HEADS * IDX_DIM, IDX_DIM, IDX_HEADS):
        parts.append(w[:, o:o + width])
        o += width
    wa, wgl, wq, wk, wv, wqi, wki, wwi = parts
    wwi_p = jnp.concatenate([wwi, jnp.zeros((D, LANES - IDX_HEADS), w.dtype)], axis=1)
    weights = (jnp.concatenate([wa, wgl], axis=1).astype(BF16), wq.astype(BF16), wv.astype(BF16),
               wqi.astype(BF16), wk.T.astype(BF16), wki.T.astype(BF16), wwi_p.astype(BF16))

    hn, conv, q, kt, v, qi, kit, wi = _inproj_call(
        hp, _rope_tables(tp), row(ln_emb_g), row(ln_emb_b), weights,
        conv_w[0].astype(F32), row(conv_b[0]), row(ln_conv_g[0]), row(ln_conv_b[0]),
        ln_kidx_g[0].reshape(-1, 1).astype(F32), ln_kidx_b[0].reshape(-1, 1).astype(F32))

    attn = _dsa_call(q, qi, wi, kt, kit, v, k_sel)

    n = B * tp
    wr_t = w_router[0].T.astype(F32)
    wr_hi = wr_t.astype(BF16)
    wr_lo = (wr_t - wr_hi.astype(F32)).astype(BF16)
    xb, base, gates = _post_call(
        conv.reshape(n, CONV_CH), attn.reshape(n, ATTN_W), hn.reshape(n, D),
        w_out[0][:CONV_CH].astype(BF16), w_out[0][CONV_CH:].astype(BF16), row(ln1_g[0]), row(ln1_b[0]),
        ws_gate[0].astype(BF16), ws_up[0].astype(BF16), ws_down[0].astype(BF16),
        wr_hi, wr_lo, router_bias[0].reshape(-1, 1).astype(F32))

    out = _moe_call(xb, base, gates, w_gate[0], w_up[0], w_down[0], row(ln2_g[0]), row(ln2_b[0]))
    return out.reshape(B, tp, D)[:, N_META:t_real]
```

```python
import functools

import numpy as np
import jax
import jax.numpy as jnp
from jax import lax
from jax.experimental import pallas as pl
from jax.experimental.pallas import tpu as pltpu

N_META = 16
CONV_CH = 512
CONV_WIDTH = 31
N_HEADS = 8
HEAD_DIM = 64
ATTN_W = N_HEADS * HEAD_DIM
IDX_HEADS = 8
IDX_DIM = 64
INDEX_TOPK = 256
ROPE_DIM = HEAD_DIM // 4
ROPE_HALF = ROPE_DIM // 2
ROPE_THETA = 500000.0
N_EXPERTS = 64
TOP_K = 8
N_GROUPS = 8
GROUP_SIZE = N_EXPERTS // N_GROUPS
TOPK_GROUPS = 4
ROUTED_SCALE = 2.5
LN_EPS = 1e-5
DEPTH = 1
DN_ALPHA = (2.0 * DEPTH) ** 0.25

LANES = 128
Q_TILE = 128
K_CHUNK = 256
SEQ_ALIGN = 256
CONV_HALO = 32
VMEM_LIMIT = 56 * 1024 * 1024

F32 = jnp.float32
BF16 = jnp.bfloat16
NEG_INF = float("-inf")
INT_MIN = -2 ** 31
KEY_NEG_INF = -2139095041
LOG2_E = 1.4426950408889634


def _dot(a, b):
    return jnp.dot(a, b, preferred_element_type=F32)


def _dot_nt(a, b):
    return lax.dot_general(a, b, (((1,), (1,)), ((), ())), preferred_element_type=F32)


def _layer_norm_rows(x, g, b):
    mu = jnp.mean(x, axis=-1, keepdims=True)
    xc = x - mu
    var = jnp.mean(xc * xc, axis=-1, keepdims=True)
    return xc * lax.rsqrt(var + LN_EPS) * g + b


def _pick_tile(n, candidates):
    for c in candidates:
        if n % c == 0:
            return c
    raise ValueError(f"no tile for {n}")


def _rope_rows(x, c_tab, s1_tab, s2_tab):
    outs = []
    for j in range(x.shape[1] // LANES):
        xs = x[:, j * LANES:(j + 1) * LANES]
        up = pltpu.roll(xs, LANES - ROPE_HALF, axis=1)
        dn = pltpu.roll(xs, ROPE_HALF, axis=1)
        outs.append(xs * c_tab + up * s1_tab + dn * s2_tab)
    return jnp.concatenate(outs, axis=1)


def _rope_cols(xt, cos_t, sin_t, heads):
    r = xt.shape[1]
    x3 = xt.reshape(heads, HEAD_DIM, r)
    x1 = x3[:, 0:ROPE_HALF, :]
    x2 = x3[:, ROPE_HALF:ROPE_DIM, :]
    n1 = x1 * cos_t - x2 * sin_t
    n2 = x2 * cos_t + x1 * sin_t
    out = jnp.concatenate([n1, n2, x3[:, ROPE_DIM:, :]], axis=1)
    return out.reshape(heads * HEAD_DIM, r)


def _inproj_kernel(h_ref, ctab_ref, s1tab_ref, s2tab_ref, cost_ref, sint_ref,
                   lng_ref, lnb_ref, wag_ref, wq_ref, wv_ref, wqi_ref, wkt_ref, wkit_ref, wwi_ref,
                   cw_ref, cb_ref, lncg_ref, lncb_ref, lnkg_ref, lnkb_ref,
                   hn_ref, conv_ref, q_ref, kt_ref, v_ref, qi_ref, kit_ref, wi_ref,
                   ubuf_ref):
    t = pl.program_id(1)
    tr = h_ref.shape[1]

    hn = _layer_norm_rows(h_ref[0], lng_ref[...], lnb_ref[...])
    hn_ref[0] = hn
    xb = hn.astype(BF16)

    ag = _dot(xb, wag_ref[...])
    u = ag[:, :CONV_CH] * jax.nn.sigmoid(ag[:, CONV_CH:])

    @pl.when(t == 0)
    def _():
        ubuf_ref[0:CONV_HALO, :] = jnp.zeros((CONV_HALO, CONV_CH), F32)

    ubuf_ref[CONV_HALO:CONV_HALO + tr, :] = u
    base = CONV_HALO - (CONV_WIDTH - 1)
    acc = jnp.zeros((tr, CONV_CH), F32)
    for j in range(CONV_WIDTH):
        acc = acc + cw_ref[j:j + 1, :] * ubuf_ref[base + j:base + j + tr, :]
    ubuf_ref[0:CONV_HALO, :] = ubuf_ref[tr:tr + CONV_HALO, :]
    c = _layer_norm_rows(acc + cb_ref[...], lncg_ref[...], lncb_ref[...])
    conv_ref[0] = (c * jax.nn.sigmoid(c)).astype(conv_ref.dtype)

    ctab, s1tab, s2tab = ctab_ref[...], s1tab_ref[...], s2tab_ref[...]
    cos_t, sin_t = cost_ref[...], sint_ref[...]

    q = _rope_rows(_dot(xb, wq_ref[...]), ctab, s1tab, s2tab)
    q_ref[0] = (q * (HEAD_DIM ** -0.5 * LOG2_E)).astype(q_ref.dtype)
    kt = _rope_cols(_dot_nt(wkt_ref[...], xb), cos_t, sin_t, N_HEADS)
    kt_ref[0] = kt.astype(kt_ref.dtype)
    v_ref[0] = _dot(xb, wv_ref[...]).astype(v_ref.dtype)

    qi = _rope_rows(_dot(xb, wqi_ref[...]), ctab, s1tab, s2tab)
    qi_ref[0] = qi.astype(qi_ref.dtype)
    kit = _dot_nt(wkit_ref[...], xb)
    mu = jnp.mean(kit, axis=0, keepdims=True)
    kc = kit - mu
    var = jnp.mean(kc * kc, axis=0, keepdims=True)
    kit = kc * lax.rsqrt(var + LN_EPS) * lnkg_ref[...] + lnkb_ref[...]
    kit = _rope_cols(kit, cos_t, sin_t, 1).astype(kit_ref.dtype)
    kit_ref[0] = jnp.concatenate([kit, kit], axis=0)
    wi_ref[0] = _dot(xb, wwi_ref[...]) * (IDX_HEADS ** -0.5)


def _inproj_call(hp, tabs, ln_g, ln_b, weights, conv_w, conv_b, lnc_g, lnc_b, lnk_g, lnk_b):
    B, tp, D = hp.shape
    tr = _pick_tile(tp, (768, 512, 256))
    nt = tp // tr
    ctab, s1tab, s2tab, cos_t, sin_t = tabs
    wag, wq, wv, wqi, wkt, wkit, wwi = weights

    def row_spec(w):
        return pl.BlockSpec((1, tr, w), lambda b, t: (b, t, 0))

    def col_spec(r):
        return pl.BlockSpec((1, r, tr), lambda b, t: (b, 0, t))

    def full(a):
        return pl.BlockSpec(a.shape, lambda b, t: (0,) * a.ndim)

    tab_row = pl.BlockSpec((tr, LANES), lambda b, t: (t, 0))
    tab_col = pl.BlockSpec((ROPE_HALF, tr), lambda b, t: (0, t))
    consts = [ln_g, ln_b, wag, wq, wv, wqi, wkt, wkit, wwi, conv_w, conv_b, lnc_g, lnc_b, lnk_g, lnk_b]
    out_shape = [
        jax.ShapeDtypeStruct((B, tp, D), F32),
        jax.ShapeDtypeStruct((B, tp, CONV_CH), BF16),
        jax.ShapeDtypeStruct((B, tp, ATTN_W), BF16),
        jax.ShapeDtypeStruct((B, ATTN_W, tp), BF16),
        jax.ShapeDtypeStruct((B, tp, ATTN_W), BF16),
        jax.ShapeDtypeStruct((B, tp, ATTN_W), BF16),
        jax.ShapeDtypeStruct((B, 2 * IDX_DIM, tp), BF16),
        jax.ShapeDtypeStruct((B, tp, LANES), F32),
    ]
    out_specs = [row_spec(D), row_spec(CONV_CH), row_spec(ATTN_W), col_spec(ATTN_W), row_spec(ATTN_W),
                 row_spec(ATTN_W), col_spec(2 * IDX_DIM), row_spec(LANES)]
    return pl.pallas_call(
        _inproj_kernel,
        grid=(B, nt),
        in_specs=[row_spec(D), tab_row, tab_row, tab_row, tab_col, tab_col] + [full(a) for a in consts],
        out_specs=out_specs,
        out_shape=out_shape,
        scratch_shapes=[pltpu.VMEM((CONV_HALO + tr, CONV_CH), F32)],
        compiler_params=pltpu.CompilerParams(
            dimension_semantics=("arbitrary", "arbitrary"), vmem_limit_bytes=VMEM_LIMIT),
        name="inproj",
    )(hp, ctab, s1tab, s2tab, cos_t, sin_t, *consts)


def _key_to_float(key):
    bits = jnp.where(key >= 0, key, key ^ jnp.int32(0x7FFFFFFF))
    f = pltpu.bitcast(bits, F32)
    return jnp.where(key < jnp.int32(KEY_NEG_INF), NEG_INF, f)


def _lane_fold(x, op=jnp.add):
    out = x[:, 0:LANES]
    for j in range(1, x.shape[1] // LANES):
        out = op(out, x[:, j * LANES:(j + 1) * LANES])
    return out


def _dsa_kernel(q_ref, qi_ref, wi_ref, kt_ref, kit_ref, v_ref, o_ref,
                sc_ref, qm_ref, qim_ref, wb_ref, mx_ref, acc_ref, *, k_sel):
    i = pl.program_id(1)
    tq = q_ref.shape[1]
    n_chunks = (i * tq + tq + K_CHUNK - 1) // K_CHUNK
    qpos = i * tq + lax.broadcasted_iota(jnp.int32, (tq, 1), 0)
    lane_k = lax.broadcasted_iota(jnp.int32, (1, K_CHUNK), 1)
    lane_in_pair = lax.broadcasted_iota(jnp.int32, (1, LANES), 1)
    low_half = lane_in_pair < HEAD_DIM

    def head_slab(ref, h):
        slab = ref[0, :, (h // 2) * LANES:(h // 2 + 1) * LANES]
        keep = low_half if h % 2 == 0 else jnp.logical_not(low_half)
        return jnp.where(keep, slab, jnp.zeros_like(slab))

    wi = wi_ref[0]
    for h in range(N_HEADS):
        qm_ref[h] = head_slab(q_ref, h)
    for h in range(IDX_HEADS):
        qim_ref[h] = head_slab(qi_ref, h)
        wb_ref[h] = jnp.broadcast_to(wi[:, h:h + 1], (tq, LANES))

    def score_body(c, carry):
        k0 = pl.multiple_of(c * K_CHUNK, K_CHUNK)
        kic = kit_ref[0, :, pl.ds(k0, K_CHUNK)]
        acc = jnp.zeros((tq, K_CHUNK), F32)
        for h in range(IDX_HEADS):
            wb = wb_ref[h]
            wb = jnp.concatenate([wb] * (K_CHUNK // LANES), axis=1)
            acc = acc + wb * jnp.maximum(_dot(qim_ref[h], kic), 0.0)
        acc = acc * (IDX_DIM ** -0.5)
        sc_ref[:, pl.ds(k0, K_CHUNK)] = jnp.where(k0 + lane_k <= qpos, acc, NEG_INF)
        return carry

    lax.fori_loop(0, n_chunks, score_body, 0)

    def count_ge(tf):
        def body(c, cnt):
            k0 = pl.multiple_of(c * K_CHUNK, K_CHUNK)
            s = sc_ref[:, pl.ds(k0, K_CHUNK)]
            return cnt + _lane_fold(jnp.where(s >= tf, 1.0, 0.0))
        cnt = lax.fori_loop(0, n_chunks, body, jnp.zeros((tq, LANES), F32))
        return jnp.sum(cnt, axis=1, keepdims=True)

    def search_body(it, tkey):
        cand = tkey + lax.shift_left(jnp.int32(1), 31 - it)
        cnt = count_ge(_key_to_float(cand))
        return jnp.where(cnt >= k_sel, cand, tkey)

    tkey = lax.fori_loop(0, 32, search_body, jnp.full((tq, 1), INT_MIN, jnp.int32))
    thr = _key_to_float(tkey)

    def count2_body(c, carry):
        cge, cgt = carry
        k0 = pl.multiple_of(c * K_CHUNK, K_CHUNK)
        s = sc_ref[:, pl.ds(k0, K_CHUNK)]
        cge = cge + _lane_fold(jnp.where(s >= thr, 1.0, 0.0))
        cgt = cgt + _lane_fold(jnp.where(s > thr, 1.0, 0.0))
        return cge, cgt

    zero_cnt = jnp.zeros((tq, LANES), F32)
    cge, cgt = lax.fori_loop(0, n_chunks, count2_body, (zero_cnt, zero_cnt))
    n_ge = jnp.sum(cge, axis=1, keepdims=True)
    n_gt = jnp.sum(cgt, axis=1, keepdims=True)
    need = k_sel - n_gt
    has_ties = jnp.max(n_ge) > k_sel

    @pl.when(jnp.logical_not(has_ties))
    def _():
        def body(c, carry):
            k0 = pl.multiple_of(c * K_CHUNK, K_CHUNK)
            s = sc_ref[:, pl.ds(k0, K_CHUNK)]
            causal = k0 + lane_k <= qpos
            sc_ref[:, pl.ds(k0, K_CHUNK)] = jnp.where(
                s >= thr, jnp.where(causal, 0.0, NEG_INF), NEG_INF)
            return carry
        lax.fori_loop(0, n_chunks, body, 0)

    @pl.when(has_ties)
    def _():
        r_i = lax.broadcasted_iota(jnp.int32, (K_CHUNK, K_CHUNK), 0)
        c_i = lax.broadcasted_iota(jnp.int32, (K_CHUNK, K_CHUNK), 1)
        upper = jnp.where(r_i <= c_i, 1.0, 0.0).astype(BF16)

        def body(c, seen):
            k0 = pl.multiple_of(c * K_CHUNK, K_CHUNK)
            s = sc_ref[:, pl.ds(k0, K_CHUNK)]
            causal = k0 + lane_k <= qpos
            eq = jnp.where(s == thr, 1.0, 0.0)
            rank = _dot(eq.astype(BF16), upper) + seen
            keep_tie = jnp.where(rank <= need, eq, 0.0)
            sel = jnp.where(s > thr, 1.0, keep_tie)
            sc_ref[:, pl.ds(k0, K_CHUNK)] = jnp.where(
                sel > 0.0, jnp.where(causal, 0.0, NEG_INF), NEG_INF)
            return rank[:, K_CHUNK - 1:K_CHUNK]
        lax.fori_loop(0, n_chunks, body, jnp.zeros((tq, 1), F32))

    def head_logits(h, k0):
        pair = h // 2
        kc = kt_ref[0, pair * LANES:(pair + 1) * LANES, pl.ds(k0, K_CHUNK)]
        return _dot(qm_ref[h], kc) + sc_ref[:, pl.ds(k0, K_CHUNK)]

    mx_ref[...] = jnp.full(mx_ref.shape, NEG_INF, F32)

    def max_body(c, carry):
        k0 = pl.multiple_of(c * K_CHUNK, K_CHUNK)
        for h in range(N_HEADS):
            mx_ref[h] = jnp.maximum(mx_ref[h], _lane_fold(head_logits(h, k0), jnp.maximum))
        return carry

    lax.fori_loop(0, n_chunks, max_body, 0)

    for h in range(N_HEADS):
        m = jnp.max(mx_ref[h], axis=1, keepdims=True)
        m = jnp.where(m == NEG_INF, 0.0, m)
        mx_ref[h] = jnp.broadcast_to(m, (tq, LANES))
    acc_ref[...] = jnp.zeros(acc_ref.shape, F32)

    ones_cols = jnp.ones((K_CHUNK, LANES), BF16)

    def pv_body(c, carry):
        k0 = pl.multiple_of(c * K_CHUNK, K_CHUNK)
        for h in range(N_HEADS):
            pair = h // 2
            mb = mx_ref[h]
            mb = jnp.concatenate([mb] * (K_CHUNK // LANES), axis=1)
            p = jnp.exp2(head_logits(h, k0) - mb).astype(BF16)
            vc = v_ref[0, pl.ds(k0, K_CHUNK), pair * LANES:(pair + 1) * LANES]
            acc_ref[h] += _dot(p, jnp.concatenate([vc, ones_cols], axis=1))
        return carry

    lax.fori_loop(0, n_chunks, pv_body, 0)

    for pair in range(N_HEADS // 2):
        outs = []
        for h in (2 * pair, 2 * pair + 1):
            a = acc_ref[h]
            outs.append(a[:, :LANES] / a[:, LANES:])
        o_ref[0, :, pair * LANES:(pair + 1) * LANES] = jnp.where(
            low_half, outs[0], outs[1]).astype(o_ref.dtype)


def _dsa_call(q, qi, wi, kt, kit, v, k_sel):
    B, tp, _ = q.shape
    nq = tp // Q_TILE

    def q_spec(w):
        return pl.BlockSpec((1, Q_TILE, w), lambda b, i: (b, i, 0))

    def per_batch(a):
        return pl.BlockSpec((1,) + a.shape[1:], lambda b, i: (b, 0, 0))

    return pl.pallas_call(
        functools.partial(_dsa_kernel, k_sel=k_sel),
        grid=(B, nq),
        in_specs=[q_spec(ATTN_W), q_spec(ATTN_W), q_spec(LANES), per_batch(kt), per_batch(kit), per_batch(v)],
        out_specs=q_spec(ATTN_W),
        out_shape=jax.ShapeDtypeStruct((B, tp, ATTN_W), BF16),
        scratch_shapes=[pltpu.VMEM((Q_TILE, tp), F32),
                        pltpu.VMEM((N_HEADS, Q_TILE, LANES), BF16),
                        pltpu.VMEM((IDX_HEADS, Q_TILE, LANES), BF16),
                        pltpu.VMEM((IDX_HEADS, Q_TILE, LANES), F32),
                        pltpu.VMEM((N_HEADS, Q_TILE, LANES), F32),
                        pltpu.VMEM((N_HEADS, Q_TILE, 2 * LANES), F32)],
        compiler_params=pltpu.CompilerParams(
            dimension_semantics=("arbitrary", "arbitrary"), vmem_limit_bytes=VMEM_LIMIT),
        name="dsa",
    )(q, qi, wi, kt, kit, v)


def _max_all(x):
    return jnp.max(jnp.max(x, axis=1, keepdims=True), axis=0, keepdims=True)


def _router_gates(logits_t, rbias):
    r = logits_t.shape[1]
    shape3 = (N_GROUPS, GROUP_SIZE, r)
    scores = jax.nn.sigmoid(logits_t).reshape(shape3)
    biased = scores + rbias.reshape(N_GROUPS, GROUP_SIZE, 1)
    in_grp = lax.broadcasted_iota(jnp.int32, shape3, 1).astype(F32)
    m1 = jnp.max(biased, axis=1, keepdims=True)
    first = jnp.min(jnp.where(biased == m1, in_grp, float(GROUP_SIZE)), axis=1, keepdims=True)
    m2 = jnp.max(jnp.where(in_grp == first, NEG_INF, biased), axis=1, keepdims=True)
    cur = m1 + m2

    grp_idx = lax.broadcasted_iota(jnp.int32, (N_GROUPS, 1, r), 0).astype(F32)
    grp_sel = jnp.zeros((N_GROUPS, 1, r), F32)
    for _ in range(TOPK_GROUPS):
        m = jnp.max(cur, axis=0, keepdims=True)
        pick = grp_idx == jnp.min(jnp.where(cur == m, grp_idx, float(N_GROUPS)), axis=0, keepdims=True)
        grp_sel = jnp.where(pick, 1.0, grp_sel)
        cur = jnp.where(pick, NEG_INF, cur)

    cur = jnp.where(jnp.broadcast_to(grp_sel, shape3) > 0.0, biased, NEG_INF)
    exp_idx = lax.broadcasted_iota(jnp.int32, shape3, 0).astype(F32) * GROUP_SIZE + in_grp
    chosen = jnp.zeros(shape3, F32)
    for _ in range(TOP_K):
        m = _max_all(cur)
        first = -_max_all(-jnp.where(cur == m, exp_idx, float(N_EXPERTS)))
        pick = exp_idx == first
        chosen = jnp.where(pick, 1.0, chosen)
        cur = jnp.where(pick, NEG_INF, cur)

    w = jnp.where(chosen > 0.0, scores, 0.0)
    denom = jnp.sum(jnp.sum(w, axis=1, keepdims=True), axis=0, keepdims=True)
    return (w / denom * ROUTED_SCALE).reshape(N_EXPERTS, r)


def _post_kernel(conv_ref, attn_ref, hn_ref, woc_ref, woa_ref, g1_ref, b1_ref,
                 wsg_ref, wsu_ref, wsd_ref, wrh_ref, wrl_ref, rb_ref,
                 xb_ref, base_ref, gates_ref):
    mix = _dot(conv_ref[...], woc_ref[...]) + _dot(attn_ref[...], woa_ref[...])
    h1 = _layer_norm_rows(DN_ALPHA * hn_ref[...] + mix, g1_ref[...], b1_ref[...])
    xb = h1.astype(BF16)
    xb_ref[...] = xb

    shared = jax.nn.silu(_dot(xb, wsg_ref[...])) * _dot(xb, wsu_ref[...])
    base_ref[...] = DN_ALPHA * h1 + _dot(shared.astype(BF16), wsd_ref[...])

    x_lo = (h1 - xb.astype(F32)).astype(BF16)
    logits_t = (_dot_nt(wrh_ref[...], xb) + _dot_nt(wrh_ref[...], x_lo) + _dot_nt(wrl_ref[...], xb))
    gates_t = _router_gates(logits_t, rb_ref[...])
    padded = jnp.concatenate([gates_t, jnp.zeros((LANES - N_EXPERTS, gates_t.shape[1]), F32)], axis=0)
    gates_ref[...] = padded.T


def _post_call(conv, attn, hn, woc, woa, g1, b1, wsg, wsu, wsd, wrh, wrl, rbias):
    n, D = hn.shape
    tr = _pick_tile(n, (768, 512, 256))

    def row_spec(w):
        return pl.BlockSpec((tr, w), lambda i: (i, 0))

    def full(a):
        return pl.BlockSpec(a.shape, lambda i: (0,) * a.ndim)

    consts = [woc, woa, g1, b1, wsg, wsu, wsd, wrh, wrl, rbias]
    return pl.pallas_call(
        _post_kernel,
        grid=(n // tr,),
        in_specs=[row_spec(CONV_CH), row_spec(ATTN_W), row_spec(D)] + [full(a) for a in consts],
        out_specs=[row_spec(D), row_spec(D), row_spec(LANES)],
        out_shape=[jax.ShapeDtypeStruct((n, D), BF16), jax.ShapeDtypeStruct((n, D), F32),
                   jax.ShapeDtypeStruct((n, LANES), F32)],
        compiler_params=pltpu.CompilerParams(
            dimension_semantics=("arbitrary",), vmem_limit_bytes=VMEM_LIMIT),
        name="post",
    )(conv, attn, hn, *consts)


def _moe_kernel(xb_ref, base_ref, gates_ref, wg_ref, wu_ref, wd_ref, g2_ref, b2_ref, o_ref, acc_ref):
    e = pl.program_id(1)

    @pl.when(e == 0)
    def _():
        acc_ref[...] = base_ref[...]

    x = xb_ref[...]
    hg = _dot(x, wg_ref[0].astype(BF16))
    hu = _dot(x, wu_ref[0].astype(BF16))
    hdn = (jax.nn.silu(hg) * hu).astype(BF16)
    lane = lax.broadcasted_iota(jnp.int32, gates_ref.shape, 1)
    gate = jnp.sum(jnp.where(lane == e, gates_ref[...], 0.0), axis=1, keepdims=True)
    acc_ref[...] += _dot(hdn, wd_ref[0].astype(BF16)) * gate

    @pl.when(e == pl.num_programs(1) - 1)
    def _():
        o_ref[...] = _layer_norm_rows(acc_ref[...], g2_ref[...], b2_ref[...])


def _moe_call(xb, base, gates, w_gate, w_up, w_down, g2, b2):
    n, D = xb.shape
    n_exp, _, d_exp = w_gate.shape
    tm = _pick_tile(n, (768, 512, 256))

    def row_spec(w):
        return pl.BlockSpec((tm, w), lambda i, e: (i, 0))

    vec = pl.BlockSpec((1, D), lambda i, e: (0, 0))
    return pl.pallas_call(
        _moe_kernel,
        grid=(n // tm, n_exp),
        in_specs=[row_spec(D), row_spec(D), row_spec(LANES),
                  pl.BlockSpec((1, D, d_exp), lambda i, e: (e, 0, 0)),
                  pl.BlockSpec((1, D, d_exp), lambda i, e: (e, 0, 0)),
                  pl.BlockSpec((1, d_exp, D), lambda i, e: (e, 0, 0)),
                  vec, vec],
        out_specs=row_spec(D),
        out_shape=jax.ShapeDtypeStruct((n, D), F32),
        scratch_shapes=[pltpu.VMEM((tm, D), F32)],
        compiler_params=pltpu.CompilerParams(
            dimension_semantics=("arbitrary", "arbitrary"), vmem_limit_bytes=VMEM_LIMIT),
        name="moe",
    )(xb, base, gates, w_gate, w_up, w_down, g2, b2)


def _rope_tables(tp):
    pos = jnp.arange(tp, dtype=F32)
    inv = jnp.power(ROPE_THETA, -2.0 * jnp.arange(ROPE_HALF, dtype=F32) / ROPE_DIM)
    ang = pos[:, None] * inv[None, :]
    cos, sin = jnp.cos(ang), jnp.sin(ang)
    zeros = jnp.zeros((tp, HEAD_DIM - ROPE_DIM), F32)
    zh = jnp.zeros((tp, ROPE_HALF), F32)
    c64 = jnp.concatenate([cos, cos, jnp.ones_like(zeros)], axis=1)
    s1_64 = jnp.concatenate([-sin, zh, zeros], axis=1)
    s2_64 = jnp.concatenate([zh, sin, zeros], axis=1)
    rep = LANES // HEAD_DIM
    return (jnp.tile(c64, (1, rep)), jnp.tile(s1_64, (1, rep)), jnp.tile(s2_64, (1, rep)),
            cos.T, sin.T)


def kernel(x, meta_tokens, ln_emb_g, ln_emb_b, w_in, conv_w, conv_b, ln_conv_g, ln_conv_b, ln_kidx_g, ln_kidx_b, w_out, ln1_g, ln1_b, w_router, router_bias, w_gate, w_up, w_down, ws_gate, ws_up, ws_down, ln2_g, ln2_b):
    B, seq, D = x.shape
    assert w_in.shape[0] == DEPTH
    k_sel = min(INDEX_TOPK, seq // 4)
    t_real = N_META + seq
    tp = -(-t_real // SEQ_ALIGN) * SEQ_ALIGN

    meta = jnp.broadcast_to(meta_tokens[None].astype(x.dtype), (B, N_META, D))
    hp = jnp.concatenate([meta, x, jnp.zeros((B, tp - t_real, D), x.dtype)], axis=1)

    def row(a):
        return a.reshape(1, -1).astype(F32)

    w = w_in[0]
    o = 0
    parts = []
    for width in (CONV_CH, CONV_CH, ATTN_W, ATTN_W, ATTN_W, IDX_HEADS * IDX_DIM, IDX_DIM, IDX_HEADS):
        parts.append(w[:, o:o + width])
        o += width
    wa, wgl, wq, wk, wv, wqi, wki, wwi = parts
    wwi_p = jnp.concatenate([wwi, jnp.zeros((D, LANES - IDX_HEADS), w.dtype)], axis=1)
    weights = (jnp.concatenate([wa, wgl], axis=1).astype(BF16), wq.astype(BF16), wv.astype(BF16),
               wqi.astype(BF16), wk.T.astype(BF16), wki.T.astype(BF16), wwi_p.astype(BF16))

    hn, conv, q, kt, v, qi, kit, wi = _inproj_call(
        hp, _rope_tables(tp), row(ln_emb_g), row(ln_emb_b), weights,
        conv_w[0].astype(F32), row(conv_b[0]), row(ln_conv_g[0]), row(ln_conv_b[0]),
        ln_kidx_g[0].reshape(-1, 1).astype(F32), ln_kidx_b[0].reshape(-1, 1).astype(F32))

    attn = _dsa_call(q, qi, wi, kt, kit, v, k_sel)

    n = B * tp
    wr_t = w_router[0].T.astype(F32)
    wr_hi = wr_t.astype(BF16)
    wr_lo = (wr_t - wr_hi.astype(F32)).astype(BF16)
    xb, base, gates = _post_call(
        conv.reshape(n, CONV_CH), attn.reshape(n, ATTN_W), hn.reshape(n, D),
        w_out[0][:CONV_CH].astype(BF16), w_out[0][CONV_CH:].astype(BF16), row(ln1_g[0]), row(ln1_b[0]),
        ws_gate[0].astype(BF16), ws_up[0].astype(BF16), ws_down[0].astype(BF16),
        wr_hi, wr_lo, router_bias[0].reshape(-1, 1).astype(F32))

    out = _moe_call(xb, base, gates, w_gate[0], w_up[0], w_down[0], row(ln2_g[0]), row(ln2_b[0]))
    return out.reshape(B, tp, D)[:, N_META:t_real]
```

```python
import functools

import numpy as np
import jax
import jax.numpy as jnp
from jax import lax
from jax.experimental import pallas as pl
from jax.experimental.pallas import tpu as pltpu

N_META = 16
CONV_CH = 512
CONV_WIDTH = 31
N_HEADS = 8
HEAD_DIM = 64
ATTN_W = N_HEADS * HEAD_DIM
IDX_HEADS = 8
IDX_DIM = 64
INDEX_TOPK = 256
ROPE_DIM = HEAD_DIM // 4
ROPE_HALF = ROPE_DIM // 2
ROPE_THETA = 500000.0
N_EXPERTS = 64
TOP_K = 8
N_GROUPS = 8
GROUP_SIZE = N_EXPERTS // N_GROUPS
TOPK_GROUPS = 4
ROUTED_SCALE = 2.5
LN_EPS = 1e-5
DEPTH = 1
DN_ALPHA = (2.0 * DEPTH) ** 0.25

LANES = 128
Q_TILE = 256
SEARCH_ROWS = 128
MOE_EXPERTS_PER_STEP = 2
K_CHUNK = 256
SEQ_ALIGN = 256
CONV_HALO = 32
VMEM_LIMIT = 56 * 1024 * 1024

F32 = jnp.float32
BF16 = jnp.bfloat16
COARSE = jnp.bfloat16
NEG_INF = float("-inf")
INT_MIN = -2 ** 31
KEY_NEG_INF = -2139095041
LOG2_E = 1.4426950408889634


def _dot(a, b):
    return jnp.dot(a, b, preferred_element_type=F32)


def _dot_nt(a, b):
    return lax.dot_general(a, b, (((1,), (1,)), ((), ())), preferred_element_type=F32)


def _layer_norm_rows(x, g, b):
    mu = jnp.mean(x, axis=-1, keepdims=True)
    xc = x - mu
    var = jnp.mean(xc * xc, axis=-1, keepdims=True)
    return xc * lax.rsqrt(var + LN_EPS) * g + b


def _pick_tile(n, candidates):
    for c in candidates:
        if n % c == 0:
            return c
    raise ValueError(f"no tile for {n}")


def _rope_rows(x, c_tab, s1_tab, s2_tab):
    outs = []
    for j in range(x.shape[1] // LANES):
        xs = x[:, j * LANES:(j + 1) * LANES]
        up = pltpu.roll(xs, LANES - ROPE_HALF, axis=1)
        dn = pltpu.roll(xs, ROPE_HALF, axis=1)
        outs.append(xs * c_tab + up * s1_tab + dn * s2_tab)
    return jnp.concatenate(outs, axis=1)


def _rope_cols(xt, cos_t, sin_t, heads):
    r = xt.shape[1]
    x3 = xt.reshape(heads, HEAD_DIM, r)
    x1 = x3[:, 0:ROPE_HALF, :]
    x2 = x3[:, ROPE_HALF:ROPE_DIM, :]
    n1 = x1 * cos_t - x2 * sin_t
    n2 = x2 * cos_t + x1 * sin_t
    out = jnp.concatenate([n1, n2, x3[:, ROPE_DIM:, :]], axis=1)
    return out.reshape(heads * HEAD_DIM, r)


def _inproj_kernel(h_ref, ctab_ref, s1tab_ref, s2tab_ref, cost_ref, sint_ref,
                   lng_ref, lnb_ref, wag_ref, wq_ref, wv_ref, wqi_ref, wkt_ref, wkit_ref, wwi_ref,
                   cw_ref, cb_ref, lncg_ref, lncb_ref, lnkg_ref, lnkb_ref,
                   hn_ref, conv_ref, q_ref, kt_ref, v_ref, qi_ref, kit_ref, wi_ref,
                   ubuf_ref):
    t = pl.program_id(1)
    tr = h_ref.shape[1]

    hn = _layer_norm_rows(h_ref[0], lng_ref[...], lnb_ref[...])
    hn_ref[0] = hn
    xb = hn.astype(BF16)

    ag = _dot(xb, wag_ref[...])
    u = ag[:, :CONV_CH] * jax.nn.sigmoid(ag[:, CONV_CH:])

    @pl.when(t == 0)
    def _():
        ubuf_ref[0:CONV_HALO, :] = jnp.zeros((CONV_HALO, CONV_CH), F32)

    ubuf_ref[CONV_HALO:CONV_HALO + tr, :] = u
    base = CONV_HALO - (CONV_WIDTH - 1)
    acc = jnp.zeros((tr, CONV_CH), F32)
    for j in range(CONV_WIDTH):
        acc = acc + cw_ref[j:j + 1, :] * ubuf_ref[base + j:base + j + tr, :]
    ubuf_ref[0:CONV_HALO, :] = ubuf_ref[tr:tr + CONV_HALO, :]
    c = _layer_norm_rows(acc + cb_ref[...], lncg_ref[...], lncb_ref[...])
    conv_ref[0] = (c * jax.nn.sigmoid(c)).astype(conv_ref.dtype)

    ctab, s1tab, s2tab = ctab_ref[...], s1tab_ref[...], s2tab_ref[...]
    cos_t, sin_t = cost_ref[...], sint_ref[...]

    q = _rope_rows(_dot(xb, wq_ref[...]), ctab, s1tab, s2tab)
    q_ref[0] = (q * (HEAD_DIM ** -0.5 * LOG2_E)).astype(q_ref.dtype)
    kt = _rope_cols(_dot_nt(wkt_ref[...], xb), cos_t, sin_t, N_HEADS)
    kt_ref[0] = kt.astype(kt_ref.dtype)
    v_ref[0] = _dot(xb, wv_ref[...]).astype(v_ref.dtype)

    qi = _rope_rows(_dot(xb, wqi_ref[...]), ctab, s1tab, s2tab)
    qi_ref[0] = qi.astype(qi_ref.dtype)
    kit = _dot_nt(wkit_ref[...], xb)
    mu = jnp.mean(kit, axis=0, keepdims=True)
    kc = kit - mu
    var = jnp.mean(kc * kc, axis=0, keepdims=True)
    kit = kc * lax.rsqrt(var + LN_EPS) * lnkg_ref[...] + lnkb_ref[...]
    kit = _rope_cols(kit, cos_t, sin_t, 1).astype(kit_ref.dtype)
    kit_ref[0] = jnp.concatenate([kit, kit], axis=0)
    wi_ref[0] = _dot(xb, wwi_ref[...]) * (IDX_HEADS ** -0.5)


def _inproj_call(hp, tabs, ln_g, ln_b, weights, conv_w, conv_b, lnc_g, lnc_b, lnk_g, lnk_b):
    B, tp, D = hp.shape
    tr = _pick_tile(tp, (768, 512, 256))
    nt = tp // tr
    ctab, s1tab, s2tab, cos_t, sin_t = tabs
    wag, wq, wv, wqi, wkt, wkit, wwi = weights

    def row_spec(w):
        return pl.BlockSpec((1, tr, w), lambda b, t: (b, t, 0))

    def col_spec(r):
        return pl.BlockSpec((1, r, tr), lambda b, t: (b, 0, t))

    def full(a):
        return pl.BlockSpec(a.shape, lambda b, t: (0,) * a.ndim)

    tab_row = pl.BlockSpec((tr, LANES), lambda b, t: (t, 0))
    tab_col = pl.BlockSpec((ROPE_HALF, tr), lambda b, t: (0, t))
    consts = [ln_g, ln_b, wag, wq, wv, wqi, wkt, wkit, wwi, conv_w, conv_b, lnc_g, lnc_b, lnk_g, lnk_b]
    out_shape = [
        jax.ShapeDtypeStruct((B, tp, D), F32),
        jax.ShapeDtypeStruct((B, tp, CONV_CH), BF16),
        jax.ShapeDtypeStruct((B, tp, ATTN_W), BF16),
        jax.ShapeDtypeStruct((B, ATTN_W, tp), BF16),
        jax.ShapeDtypeStruct((B, tp, ATTN_W), BF16),
        jax.ShapeDtypeStruct((B, tp, ATTN_W), BF16),
        jax.ShapeDtypeStruct((B, 2 * IDX_DIM, tp), BF16),
        jax.ShapeDtypeStruct((B, tp, LANES), F32),
    ]
    out_specs = [row_spec(D), row_spec(CONV_CH), row_spec(ATTN_W), col_spec(ATTN_W), row_spec(ATTN_W),
                 row_spec(ATTN_W), col_spec(2 * IDX_DIM), row_spec(LANES)]
    return pl.pallas_call(
        _inproj_kernel,
        grid=(B, nt),
        in_specs=[row_spec(D), tab_row, tab_row, tab_row, tab_col, tab_col] + [full(a) for a in consts],
        out_specs=out_specs,
        out_shape=out_shape,
        scratch_shapes=[pltpu.VMEM((CONV_HALO + tr, CONV_CH), F32)],
        compiler_params=pltpu.CompilerParams(
            dimension_semantics=("arbitrary", "arbitrary"), vmem_limit_bytes=VMEM_LIMIT),
        name="inproj",
    )(hp, ctab, s1tab, s2tab, cos_t, sin_t, *consts)


def _key_to_float(key):
    bits = jnp.where(key >= 0, key, key ^ jnp.int32(0x7FFFFFFF))
    f = pltpu.bitcast(bits, F32)
    return jnp.where(key < jnp.int32(KEY_NEG_INF), NEG_INF, f)


def _lane_fold(x, op=jnp.add):
    out = x[:, 0:LANES]
    for j in range(1, x.shape[1] // LANES):
        out = op(out, x[:, j * LANES:(j + 1) * LANES])
    return out


def _dsa_kernel(q_ref, qi_ref, wi_ref, kt_ref, kit_ref, v_ref, o_ref,
                sc_ref, sh_ref, qm_ref, qim_ref, wb_ref, mx_ref, acc_ref, *, k_sel):
    i = pl.program_id(1)
    tq = q_ref.shape[1]
    n_chunks = (i * tq + tq + K_CHUNK - 1) // K_CHUNK
    n_rep = K_CHUNK // LANES
    lane_in_pair = lax.broadcasted_iota(jnp.int32, (1, LANES), 1)
    low_half = lane_in_pair < HEAD_DIM

    def widen(x):
        return jnp.concatenate([x] * n_rep, axis=1)

    def causal_mask(k0):
        kpos = k0 + lax.broadcasted_iota(jnp.int32, (tq, K_CHUNK), 1)
        return kpos <= i * tq + lax.broadcasted_iota(jnp.int32, (tq, K_CHUNK), 0)

    def head_slab(ref, h):
        slab = ref[0, :, (h // 2) * LANES:(h // 2 + 1) * LANES]
        keep = low_half if h % 2 == 0 else jnp.logical_not(low_half)
        return jnp.where(keep, slab, jnp.zeros_like(slab))

    wi = wi_ref[0]
    for h in range(N_HEADS):
        qm_ref[h] = head_slab(q_ref, h)
    for h in range(IDX_HEADS):
        qim_ref[h] = head_slab(qi_ref, h)
        wb_ref[h] = jnp.broadcast_to(wi[:, h:h + 1], (tq, LANES))

    def score_body(c, carry):
        k0 = pl.multiple_of(c * K_CHUNK, K_CHUNK)
        kic = kit_ref[0, :, pl.ds(k0, K_CHUNK)]
        acc = jnp.zeros((tq, K_CHUNK), F32)
        for h in range(IDX_HEADS):
            acc = acc + widen(wb_ref[h]) * jnp.maximum(_dot(qim_ref[h], kic), 0.0)
        acc = acc * (IDX_DIM ** -0.5)
        acc = jnp.where(causal_mask(k0), acc, NEG_INF)
        sc_ref[:, pl.ds(k0, K_CHUNK)] = acc
        hi_bits = pltpu.bitcast(acc, jnp.int32) & jnp.int32(-65536)
        sh_ref[:, pl.ds(k0, K_CHUNK)] = pltpu.bitcast(hi_bits, F32).astype(COARSE)
        return carry

    lax.fori_loop(0, n_chunks, score_body, 0)

    row_groups = [slice(r * SEARCH_ROWS, (r + 1) * SEARCH_ROWS) for r in range(tq // SEARCH_ROWS)]
    ones_sq = jnp.ones((LANES, LANES), BF16)

    def count_rows(ref, rows, thr_rep, preds):
        dtype = ref.dtype
        one, zero = jnp.ones((), dtype), jnp.zeros((), dtype)
        tb = thr_rep[rows]

        def body(c, cnts):
            k0 = pl.multiple_of(c * K_CHUNK, K_CHUNK)
            s = ref[rows, pl.ds(k0, K_CHUNK)]
            out = []
            for cnt, p in zip(cnts, preds):
                for j in range(n_rep):
                    cnt = cnt + jnp.where(p(s[:, j * LANES:(j + 1) * LANES], tb), one, zero)
                out.append(cnt)
            return tuple(out)

        init = tuple(jnp.zeros((SEARCH_ROWS, LANES), dtype) for _ in preds)
        return lax.fori_loop(0, n_chunks, body, init)

    def count_all(ref, thr_rep, preds):
        per_group = [count_rows(ref, rows, thr_rep, preds) for rows in row_groups]
        return [_dot(jnp.concatenate([g[k] for g in per_group], axis=0).astype(BF16), ones_sq)
                for k in range(len(preds))]

    def search_body(it, tkey, coarse):
        cand = tkey + lax.shift_left(jnp.int32(1), 31 - it)
        cf = _key_to_float(cand)
        if coarse:
            cf = pltpu.bitcast(pltpu.bitcast(cf, jnp.int32) & jnp.int32(-65536), F32).astype(COARSE)
        ref = sh_ref if coarse else sc_ref
        cnt, = count_all(ref, cf, [lambda s, t: s >= t])
        return jnp.where(cnt >= k_sel, cand, tkey)

    tkey = jnp.full((tq, LANES), INT_MIN, jnp.int32)
    tkey = lax.fori_loop(0, 16, functools.partial(search_body, coarse=True), tkey)
    tkey = lax.fori_loop(16, 32, functools.partial(search_body, coarse=False), tkey)
    thr = _key_to_float(tkey)

    n_ge, n_gt = count_all(sc_ref, thr, [lambda s, t: s >= t, lambda s, t: s > t])
    need = widen(k_sel - n_gt)
    has_ties = jnp.max(n_ge) > k_sel
    thr_w = widen(thr)

    @pl.when(jnp.logical_not(has_ties))
    def _():
        def body(c, carry):
            k0 = pl.multiple_of(c * K_CHUNK, K_CHUNK)
            s = sc_ref[:, pl.ds(k0, K_CHUNK)]
            sc_ref[:, pl.ds(k0, K_CHUNK)] = jnp.where(
                s >= thr_w, jnp.where(causal_mask(k0), 0.0, NEG_INF), NEG_INF)
            return carry
        lax.fori_loop(0, n_chunks, body, 0)

    @pl.when(has_ties)
    def _():
        r_i = lax.broadcasted_iota(jnp.int32, (K_CHUNK, K_CHUNK), 0)
        c_i = lax.broadcasted_iota(jnp.int32, (K_CHUNK, K_CHUNK), 1)
        upper = jnp.where(r_i <= c_i, 1.0, 0.0).astype(BF16)

        def body(c, seen):
            k0 = pl.multiple_of(c * K_CHUNK, K_CHUNK)
            s = sc_ref[:, pl.ds(k0, K_CHUNK)]
            eq = jnp.where(s == thr_w, 1.0, 0.0)
            rank = _dot(eq.astype(BF16), upper) + seen
            keep_tie = jnp.where(rank <= need, eq, 0.0)
            sel = jnp.where(s > thr_w, 1.0, keep_tie)
            sc_ref[:, pl.ds(k0, K_CHUNK)] = jnp.where(
                sel > 0.0, jnp.where(causal_mask(k0), 0.0, NEG_INF), NEG_INF)
            return rank[:, K_CHUNK - 1:K_CHUNK]
        lax.fori_loop(0, n_chunks, body, jnp.zeros((tq, 1), F32))

    def head_logits(h, k0):
        pair = h // 2
        kc = kt_ref[0, pair * LANES:(pair + 1) * LANES, pl.ds(k0, K_CHUNK)]
        return _dot(qm_ref[h], kc) + sc_ref[:, pl.ds(k0, K_CHUNK)]

    mx_ref[...] = jnp.full(mx_ref.shape, NEG_INF, F32)

    def max_body(c, carry):
        k0 = pl.multiple_of(c * K_CHUNK, K_CHUNK)
        for h in range(N_HEADS):
            mx_ref[h] = jnp.maximum(mx_ref[h], _lane_fold(head_logits(h, k0), jnp.maximum))
        return carry

    lax.fori_loop(0, n_chunks, max_body, 0)

    for h in range(N_HEADS):
        m = jnp.max(mx_ref[h], axis=1, keepdims=True)
        m = jnp.where(m == NEG_INF, 0.0, m)
        mx_ref[h] = jnp.broadcast_to(m, (tq, LANES))
    acc_ref[...] = jnp.zeros(acc_ref.shape, F32)

    ones_cols = jnp.ones((K_CHUNK, LANES), BF16)

    def pv_body(c, carry):
        k0 = pl.multiple_of(c * K_CHUNK, K_CHUNK)
        for h in range(N_HEADS):
            pair = h // 2
            p = jnp.exp2(head_logits(h, k0) - widen(mx_ref[h])).astype(BF16)
            vc = v_ref[0, pl.ds(k0, K_CHUNK), pair * LANES:(pair + 1) * LANES]
            acc_ref[h] += _dot(p, jnp.concatenate([vc, ones_cols], axis=1))
        return carry

    lax.fori_loop(0, n_chunks, pv_body, 0)

    for pair in range(N_HEADS // 2):
        outs = []
        for h in (2 * pair, 2 * pair + 1):
            a = acc_ref[h]
            outs.append(a[:, :LANES] / a[:, LANES:])
        o_ref[0, :, pair * LANES:(pair + 1) * LANES] = jnp.where(
            low_half, outs[0], outs[1]).astype(o_ref.dtype)


def _dsa_call(q, qi, wi, kt, kit, v, k_sel):
    B, tp, _ = q.shape
    nq = tp // Q_TILE

    def q_spec(w):
        return pl.BlockSpec((1, Q_TILE, w), lambda b, i: (b, i, 0))

    def per_batch(a):
        return pl.BlockSpec((1,) + a.shape[1:], lambda b, i: (b, 0, 0), pipeline_mode=pl.Buffered(1))

    return pl.pallas_call(
        functools.partial(_dsa_kernel, k_sel=k_sel),
        grid=(B, nq),
        in_specs=[q_spec(ATTN_W), q_spec(ATTN_W), q_spec(LANES), per_batch(kt), per_batch(kit), per_batch(v)],
        out_specs=q_spec(ATTN_W),
        out_shape=jax.ShapeDtypeStruct((B, tp, ATTN_W), BF16),
        scratch_shapes=[pltpu.VMEM((Q_TILE, tp), F32),
                        pltpu.VMEM((Q_TILE, tp), COARSE),
                        pltpu.VMEM((N_HEADS, Q_TILE, LANES), BF16),
                        pltpu.VMEM((IDX_HEADS, Q_TILE, LANES), BF16),
                        pltpu.VMEM((IDX_HEADS, Q_TILE, LANES), F32),
                        pltpu.VMEM((N_HEADS, Q_TILE, LANES), F32),
                        pltpu.VMEM((N_HEADS, Q_TILE, 2 * LANES), F32)],
        compiler_params=pltpu.CompilerParams(
            dimension_semantics=("arbitrary", "arbitrary"), vmem_limit_bytes=VMEM_LIMIT),
        name="dsa",
    )(q, qi, wi, kt, kit, v)


def _max_all(x):
    return jnp.max(jnp.max(x, axis=1, keepdims=True), axis=0, keepdims=True)


def _router_gates(logits_t, rbias):
    r = logits_t.shape[1]
    shape3 = (N_GROUPS, GROUP_SIZE, r)
    scores = jax.nn.sigmoid(logits_t).reshape(shape3)
    biased = scores + rbias.reshape(N_GROUPS, GROUP_SIZE, 1)
    in_grp = lax.broadcasted_iota(jnp.int32, shape3, 1).astype(F32)
    m1 = jnp.max(biased, axis=1, keepdims=True)
    first = jnp.min(jnp.where(biased == m1, in_grp, float(GROUP_SIZE)), axis=1, keepdims=True)
    m2 = jnp.max(jnp.where(in_grp == first, NEG_INF, biased), axis=1, keepdims=True)
    cur = m1 + m2

    grp_idx = lax.broadcasted_iota(jnp.int32, (N_GROUPS, 1, r), 0).astype(F32)
    grp_sel = jnp.zeros((N_GROUPS, 1, r), F32)
    for _ in range(TOPK_GROUPS):
        m = jnp.max(cur, axis=0, keepdims=True)
        pick = grp_idx == jnp.min(jnp.where(cur == m, grp_idx, float(N_GROUPS)), axis=0, keepdims=True)
        grp_sel = jnp.where(pick, 1.0, grp_sel)
        cur = jnp.where(pick, NEG_INF, cur)

    cur = jnp.where(jnp.broadcast_to(grp_sel, shape3) > 0.0, biased, NEG_INF)
    exp_idx = lax.broadcasted_iota(jnp.int32, shape3, 0).astype(F32) * GROUP_SIZE + in_grp
    chosen = jnp.zeros(shape3, F32)
    for _ in range(TOP_K):
        m = _max_all(cur)
        first = -_max_all(-jnp.where(cur == m, exp_idx, float(N_EXPERTS)))
        pick = exp_idx == first
        chosen = jnp.where(pick, 1.0, chosen)
        cur = jnp.where(pick, NEG_INF, cur)

    w = jnp.where(chosen > 0.0, scores, 0.0)
    denom = jnp.sum(jnp.sum(w, axis=1, keepdims=True), axis=0, keepdims=True)
    return (w / denom * ROUTED_SCALE).reshape(N_EXPERTS, r)


def _post_kernel(conv_ref, attn_ref, hn_ref, woc_ref, woa_ref, g1_ref, b1_ref,
                 wsg_ref, wsu_ref, wsd_ref, wrh_ref, wrl_ref, rb_ref,
                 xb_ref, base_ref, gates_ref):
    mix = _dot(conv_ref[...], woc_ref[...]) + _dot(attn_ref[...], woa_ref[...])
    h1 = _layer_norm_rows(DN_ALPHA * hn_ref[...] + mix, g1_ref[...], b1_ref[...])
    xb = h1.astype(BF16)
    xb_ref[...] = xb

    shared = jax.nn.silu(_dot(xb, wsg_ref[...])) * _dot(xb, wsu_ref[...])
    base_ref[...] = DN_ALPHA * h1 + _dot(shared.astype(BF16), wsd_ref[...])

    x_lo = (h1 - xb.astype(F32)).astype(BF16)
    logits_t = (_dot_nt(wrh_ref[...], xb) + _dot_nt(wrh_ref[...], x_lo) + _dot_nt(wrl_ref[...], xb))
    gates_t = _router_gates(logits_t, rb_ref[...])
    padded = jnp.concatenate([gates_t, jnp.zeros((LANES - N_EXPERTS, gates_t.shape[1]), F32)], axis=0)
    gates_ref[...] = padded.T


def _post_call(conv, attn, hn, woc, woa, g1, b1, wsg, wsu, wsd, wrh, wrl, rbias):
    n, D = hn.shape
    tr = _pick_tile(n, (768, 512, 256))

    def row_spec(w):
        return pl.BlockSpec((tr, w), lambda i: (i, 0))

    def full(a):
        return pl.BlockSpec(a.shape, lambda i: (0,) * a.ndim)

    consts = [woc, woa, g1, b1, wsg, wsu, wsd, wrh, wrl, rbias]
    return pl.pallas_call(
        _post_kernel,
        grid=(n // tr,),
        in_specs=[row_spec(CONV_CH), row_spec(ATTN_W), row_spec(D)] + [full(a) for a in consts],
        out_specs=[row_spec(D), row_spec(D), row_spec(LANES)],
        out_shape=[jax.ShapeDtypeStruct((n, D), BF16), jax.ShapeDtypeStruct((n, D), F32),
                   jax.ShapeDtypeStruct((n, LANES), F32)],
        compiler_params=pltpu.CompilerParams(
            dimension_semantics=("arbitrary",), vmem_limit_bytes=VMEM_LIMIT),
        name="post",
    )(conv, attn, hn, *consts)


def _moe_kernel(xb_ref, base_ref, gates_ref, wg_ref, wu_ref, wd_ref, g2_ref, b2_ref, o_ref):
    s = pl.program_id(1)
    n_per_step, d_exp = wg_ref.shape[0], wg_ref.shape[2]

    @pl.when(s == 0)
    def _():
        o_ref[...] = base_ref[...]

    x = xb_ref[...]
    gates = gates_ref[...]
    lane = lax.broadcasted_iota(jnp.int32, gates.shape, 1)
    hidden = []
    for j in range(n_per_step):
        hg = _dot(x, wg_ref[j].astype(BF16))
        hu = _dot(x, wu_ref[j].astype(BF16))
        gate = jnp.sum(jnp.where(lane == s * n_per_step + j, gates, 0.0), axis=1, keepdims=True)
        hidden.append((jax.nn.silu(hg) * hu * gate).astype(BF16))
    wd = wd_ref[...].astype(BF16).reshape(n_per_step * d_exp, wd_ref.shape[2])
    o_ref[...] += _dot(jnp.concatenate(hidden, axis=1), wd)

    @pl.when(s == pl.num_programs(1) - 1)
    def _():
        o_ref[...] = _layer_norm_rows(o_ref[...], g2_ref[...], b2_ref[...])


def _moe_call(xb, base, gates, w_gate, w_up, w_down, g2, b2):
    n, D = xb.shape
    n_exp, _, d_exp = w_gate.shape
    tm = _pick_tile(n, (1536, 768, 512, 256))
    eps = MOE_EXPERTS_PER_STEP
    assert n_exp % eps == 0

    def row_spec(w, **kw):
        return pl.BlockSpec((tm, w), lambda i, s: (i, 0), **kw)

    once = dict(pipeline_mode=pl.Buffered(1))
    vec = pl.BlockSpec((1, D), lambda i, s: (0, 0))
    return pl.pallas_call(
        _moe_kernel,
        grid=(n // tm, n_exp // eps),
        in_specs=[row_spec(D, **once), row_spec(D, **once), row_spec(LANES, **once),
                  pl.BlockSpec((eps, D, d_exp), lambda i, s: (s, 0, 0)),
                  pl.BlockSpec((eps, D, d_exp), lambda i, s: (s, 0, 0)),
                  pl.BlockSpec((eps, d_exp, D), lambda i, s: (s, 0, 0)),
                  vec, vec],
        out_specs=row_spec(D),
        out_shape=jax.ShapeDtypeStruct((n, D), F32),
        compiler_params=pltpu.CompilerParams(
            dimension_semantics=("arbitrary", "arbitrary"), vmem_limit_bytes=VMEM_LIMIT),
        name="moe",
    )(xb, base, gates, w_gate, w_up, w_down, g2, b2)


def _rope_tables(tp):
    pos = jnp.arange(tp, dtype=F32)
    inv = jnp.power(ROPE_THETA, -2.0 * jnp.arange(ROPE_HALF, dtype=F32) / ROPE_DIM)
    ang = pos[:, None] * inv[None, :]
    cos, sin = jnp.cos(ang), jnp.sin(ang)
    zeros = jnp.zeros((tp, HEAD_DIM - ROPE_DIM), F32)
    zh = jnp.zeros((tp, ROPE_HALF), F32)
    c64 = jnp.concatenate([cos, cos, jnp.ones_like(zeros)], axis=1)
    s1_64 = jnp.concatenate([-sin, zh, zeros], axis=1)
    s2_64 = jnp.concatenate([zh, sin, zeros], axis=1)
    rep = LANES // HEAD_DIM
    return (jnp.tile(c64, (1, rep)), jnp.tile(s1_64, (1, rep)), jnp.tile(s2_64, (1, rep)),
            cos.T, sin.T)


def kernel(x, meta_tokens, ln_emb_g, ln_emb_b, w_in, conv_w, conv_b, ln_conv_g, ln_conv_b, ln_kidx_g, ln_kidx_b, w_out, ln1_g, ln1_b, w_router, router_bias, w_gate, w_up, w_down, ws_gate, ws_up, ws_down, ln2_g, ln2_b):
    B, seq, D = x.shape
    assert w_in.shape[0] == DEPTH
    k_sel = min(INDEX_TOPK, seq // 4)
    t_real = N_META + seq
    tp = -(-t_real // SEQ_ALIGN) * SEQ_ALIGN

    meta = jnp.broadcast_to(meta_tokens[None].astype(x.dtype), (B, N_META, D))
    hp = jnp.concatenate([meta, x, jnp.zeros((B, tp - t_real, D), x.dtype)], axis=1)

    def row(a):
        return a.reshape(1, -1).astype(F32)

    w = w_in[0]
    o = 0
    parts = []
    for width in (CONV_CH, CONV_CH, ATTN_W, ATTN_W, ATTN_W, IDX_HEADS * IDX_DIM, IDX_DIM, IDX_HEADS):
        parts.append(w[:, o:o + width])
        o += width
    wa, wgl, wq, wk, wv, wqi, wki, wwi = parts
    wwi_p = jnp.concatenate([wwi, jnp.zeros((D, LANES - IDX_HEADS), w.dtype)], axis=1)
    weights = (jnp.concatenate([wa, wgl], axis=1).astype(BF16), wq.astype(BF16), wv.astype(BF16),
               wqi.astype(BF16), wk.T.astype(BF16), wki.T.astype(BF16), wwi_p.astype(BF16))

    hn, conv, q, kt, v, qi, kit, wi = _inproj_call(
        hp, _rope_tables(tp), row(ln_emb_g), row(ln_emb_b), weights,
        conv_w[0].astype(F32), row(conv_b[0]), row(ln_conv_g[0]), row(ln_conv_b[0]),
        ln_kidx_g[0].reshape(-1, 1).astype(F32), ln_kidx_b[0].reshape(-1, 1).astype(F32))

    attn = _dsa_call(q, qi, wi, kt, kit, v, k_sel)

    n = B * tp
    wr_t = w_router[0].T.astype(F32)
    wr_hi = wr_t.astype(BF16)
    wr_lo = (wr_t - wr_hi.astype(F32)).astype(BF16)
    xb, base, gates = _post_call(
        conv.reshape(n, CONV_CH), attn.reshape(n, ATTN_W), hn.reshape(n, D),
        w_out[0][:CONV_CH].astype(BF16), w_out[0][CONV_CH:].astype(BF16), row(ln1_g[0]), row(ln1_b[0]),
        ws_gate[0].astype(BF16), ws_up[0].astype(BF16), ws_down[0].astype(BF16),
        wr_hi, wr_lo, router_bias[0].reshape(-1, 1).astype(F32))

    out = _moe_call(xb, base, gates, w_gate[0], w_up[0], w_down[0], row(ln2_g[0]), row(ln2_b[0]))
    return out.reshape(B, tp, D)[:, N_META:t_real]
```

```python
import functools

import numpy as np
import jax
import jax.numpy as jnp
from jax import lax
from jax.experimental import pallas as pl
from jax.experimental.pallas import tpu as pltpu

N_META = 16
CONV_CH = 512
CONV_WIDTH = 31
N_HEADS = 8
HEAD_DIM = 64
ATTN_W = N_HEADS * HEAD_DIM
IDX_HEADS = 8
IDX_DIM = 64
INDEX_TOPK = 256
ROPE_DIM = HEAD_DIM // 4
ROPE_HALF = ROPE_DIM // 2
ROPE_THETA = 500000.0
N_EXPERTS = 64
TOP_K = 8
N_GROUPS = 8
GROUP_SIZE = N_EXPERTS // N_GROUPS
TOPK_GROUPS = 4
ROUTED_SCALE = 2.5
LN_EPS = 1e-5
DEPTH = 1
DN_ALPHA = (2.0 * DEPTH) ** 0.25

LANES = 128
Q_TILE = 256
SUBLANES = 8
BF16_SUBLANES = 16
MOE_EXPERTS_PER_STEP = 2
K_CHUNK = 256
SEQ_ALIGN = 256
CONV_HALO = 32
VMEM_LIMIT = 56 * 1024 * 1024

F32 = jnp.float32
BF16 = jnp.bfloat16
COARSE = jnp.bfloat16
NEG_INF = float("-inf")
INT_MIN = -2 ** 31
KEY_NEG_INF = -2139095041
LOG2_E = 1.4426950408889634


def _dot(a, b):
    return jnp.dot(a, b, preferred_element_type=F32)


def _dot_nt(a, b):
    return lax.dot_general(a, b, (((1,), (1,)), ((), ())), preferred_element_type=F32)


def _layer_norm_rows(x, g, b):
    mu = jnp.mean(x, axis=-1, keepdims=True)
    xc = x - mu
    var = jnp.mean(xc * xc, axis=-1, keepdims=True)
    return xc * lax.rsqrt(var + LN_EPS) * g + b


def _pick_tile(n, candidates):
    for c in candidates:
        if n % c == 0:
            return c
    raise ValueError(f"no tile for {n}")


def _rope_rows(x, c_tab, s1_tab, s2_tab):
    outs = []
    for j in range(x.shape[1] // LANES):
        xs = x[:, j * LANES:(j + 1) * LANES]
        up = pltpu.roll(xs, LANES - ROPE_HALF, axis=1)
        dn = pltpu.roll(xs, ROPE_HALF, axis=1)
        outs.append(xs * c_tab + up * s1_tab + dn * s2_tab)
    return jnp.concatenate(outs, axis=1)


def _rope_cols(xt, cos_t, sin_t, heads):
    r = xt.shape[1]
    x3 = xt.reshape(heads, HEAD_DIM, r)
    x1 = x3[:, 0:ROPE_HALF, :]
    x2 = x3[:, ROPE_HALF:ROPE_DIM, :]
    n1 = x1 * cos_t - x2 * sin_t
    n2 = x2 * cos_t + x1 * sin_t
    out = jnp.concatenate([n1, n2, x3[:, ROPE_DIM:, :]], axis=1)
    return out.reshape(heads * HEAD_DIM, r)


def _inproj_kernel(h_ref, ctab_ref, s1tab_ref, s2tab_ref, cost_ref, sint_ref,
                   lng_ref, lnb_ref, wag_ref, wqt_ref, wk_ref, wvt_ref, wqit_ref, wki_ref, wwit_ref,
                   cw_ref, cb_ref, lncg_ref, lncb_ref, lnkg_ref, lnkb_ref,
                   hn_ref, conv_ref, qt_ref, k_ref, vt_ref, qit_ref, ki_ref, wit_ref,
                   ubuf_ref):
    t = pl.program_id(1)
    tr = h_ref.shape[1]

    hn = _layer_norm_rows(h_ref[0], lng_ref[...], lnb_ref[...])
    hn_ref[0] = hn
    xb = hn.astype(BF16)

    ag = _dot(xb, wag_ref[...])
    u = ag[:, :CONV_CH] * jax.nn.sigmoid(ag[:, CONV_CH:])

    @pl.when(t == 0)
    def _():
        ubuf_ref[0:CONV_HALO, :] = jnp.zeros((CONV_HALO, CONV_CH), F32)

    ubuf_ref[CONV_HALO:CONV_HALO + tr, :] = u
    base = CONV_HALO - (CONV_WIDTH - 1)
    acc = jnp.zeros((tr, CONV_CH), F32)
    for j in range(CONV_WIDTH):
        acc = acc + cw_ref[j:j + 1, :] * ubuf_ref[base + j:base + j + tr, :]
    ubuf_ref[0:CONV_HALO, :] = ubuf_ref[tr:tr + CONV_HALO, :]
    c = _layer_norm_rows(acc + cb_ref[...], lncg_ref[...], lncb_ref[...])
    conv_ref[0] = (c * jax.nn.sigmoid(c)).astype(conv_ref.dtype)

    ctab, s1tab, s2tab = ctab_ref[...], s1tab_ref[...], s2tab_ref[...]
    cos_t, sin_t = cost_ref[...], sint_ref[...]

    qt = _rope_cols(_dot_nt(wqt_ref[...], xb), cos_t, sin_t, N_HEADS)
    qt_ref[0] = (qt * (HEAD_DIM ** -0.5 * LOG2_E)).astype(qt_ref.dtype)
    k = _rope_rows(_dot(xb, wk_ref[...]), ctab, s1tab, s2tab)
    k_ref[0] = k.astype(k_ref.dtype)
    vt_ref[0] = _dot_nt(wvt_ref[...], xb).astype(vt_ref.dtype)

    qit = _rope_cols(_dot_nt(wqit_ref[...], xb), cos_t, sin_t, IDX_HEADS)
    qit_ref[0] = qit.astype(qit_ref.dtype)
    ki = _layer_norm_rows(_dot(xb, wki_ref[...]), lnkg_ref[...], lnkb_ref[...])
    ki_ref[0] = _rope_rows(ki, ctab, s1tab, s2tab).astype(ki_ref.dtype)
    wit = _dot_nt(wwit_ref[...], xb) * (IDX_HEADS ** -0.5)
    wit_ref[0] = wit[:IDX_HEADS]


def _inproj_call(hp, tabs, ln_g, ln_b, weights, conv_w, conv_b, lnc_g, lnc_b, lnk_g, lnk_b):
    B, tp, D = hp.shape
    tr = _pick_tile(tp, (768, 512, 256))
    nt = tp // tr
    ctab, s1tab, s2tab, cos_t, sin_t = tabs
    def row_spec(w):
        return pl.BlockSpec((1, tr, w), lambda b, t: (b, t, 0))

    def col_spec(r):
        return pl.BlockSpec((1, r, tr), lambda b, t: (b, 0, t))

    def full(a):
        return pl.BlockSpec(a.shape, lambda b, t: (0,) * a.ndim)

    tab_row = pl.BlockSpec((tr, LANES), lambda b, t: (t, 0))
    tab_col = pl.BlockSpec((ROPE_HALF, tr), lambda b, t: (0, t))
    consts = [ln_g, ln_b, *weights, conv_w, conv_b, lnc_g, lnc_b, lnk_g, lnk_b]
    out_shape = [
        jax.ShapeDtypeStruct((B, tp, D), F32),
        jax.ShapeDtypeStruct((B, tp, CONV_CH), BF16),
        jax.ShapeDtypeStruct((B, ATTN_W, tp), BF16),
        jax.ShapeDtypeStruct((B, tp, ATTN_W), BF16),
        jax.ShapeDtypeStruct((B, ATTN_W, tp), BF16),
        jax.ShapeDtypeStruct((B, IDX_HEADS * IDX_DIM, tp), BF16),
        jax.ShapeDtypeStruct((B, tp, 2 * IDX_DIM), BF16),
        jax.ShapeDtypeStruct((B, IDX_HEADS, tp), F32),
    ]
    out_specs = [row_spec(D), row_spec(CONV_CH), col_spec(ATTN_W), row_spec(ATTN_W), col_spec(ATTN_W),
                 col_spec(IDX_HEADS * IDX_DIM), row_spec(2 * IDX_DIM), col_spec(IDX_HEADS)]
    return pl.pallas_call(
        _inproj_kernel,
        grid=(B, nt),
        in_specs=[row_spec(D), tab_row, tab_row, tab_row, tab_col, tab_col] + [full(a) for a in consts],
        out_specs=out_specs,
        out_shape=out_shape,
        scratch_shapes=[pltpu.VMEM((CONV_HALO + tr, CONV_CH), F32)],
        compiler_params=pltpu.CompilerParams(
            dimension_semantics=("arbitrary", "arbitrary"), vmem_limit_bytes=VMEM_LIMIT),
        name="inproj",
    )(hp, ctab, s1tab, s2tab, cos_t, sin_t, *consts)


def _key_to_float(key):
    bits = jnp.where(key >= 0, key, key ^ jnp.int32(0x7FFFFFFF))
    f = pltpu.bitcast(bits, F32)
    return jnp.where(key < jnp.int32(KEY_NEG_INF), NEG_INF, f)


def _tree(parts, op):
    parts = list(parts)
    while len(parts) > 1:
        nxt = [op(parts[j], parts[j + 1]) for j in range(0, len(parts) - 1, 2)]
        if len(parts) % 2:
            nxt.append(parts[-1])
        parts = nxt
    return parts[0]


def _fold_rows(x, rows, op):
    return _tree([x[j * rows:(j + 1) * rows] for j in range(x.shape[0] // rows)], op)


def _dsa_kernel(qt_ref, qit_ref, wit_ref, k_ref, ki_ref, vt_ref, o_ref,
                sc_ref, sh_ref, qm_ref, qim_ref, m_ref, alpha_ref, lg_ref, acc_ref, *, k_sel):
    i = pl.program_id(1)
    tq = qt_ref.shape[2]
    n_chunks = (i * tq + tq + K_CHUNK - 1) // K_CHUNK
    v_rows = LANES + BF16_SUBLANES

    def causal_mask(k0):
        kpos = k0 + lax.broadcasted_iota(jnp.int32, (K_CHUNK, tq), 0)
        return kpos <= i * tq + lax.broadcasted_iota(jnp.int32, (K_CHUNK, tq), 1)

    def rows8(x):
        return jnp.broadcast_to(x, (SUBLANES, tq))

    def tiles(x):
        return x.reshape(K_CHUNK // SUBLANES, SUBLANES, tq)

    def head_slab(ref, h):
        slab = ref[0, (h // 2) * LANES:(h // 2 + 1) * LANES, :]
        zeros = jnp.zeros((HEAD_DIM, tq), slab.dtype)
        if h % 2 == 0:
            return jnp.concatenate([slab[:HEAD_DIM], zeros], axis=0)
        return jnp.concatenate([zeros, slab[HEAD_DIM:]], axis=0)

    for h in range(N_HEADS):
        qm_ref[h] = head_slab(qt_ref, h)
    for h in range(IDX_HEADS):
        qim_ref[h] = head_slab(qit_ref, h)
    wit = wit_ref[0]
    w_heads = [rows8(wit[h:h + 1]) for h in range(IDX_HEADS)]

    def score_body(c, carry):
        k0 = pl.multiple_of(c * K_CHUNK, K_CHUNK)
        kic = ki_ref[0, pl.ds(k0, K_CHUNK), :]
        acc = jnp.zeros((K_CHUNK // SUBLANES, SUBLANES, tq), F32)
        for h in range(IDX_HEADS):
            acc = acc + w_heads[h][None] * jnp.maximum(tiles(_dot(kic, qim_ref[h])), 0.0)
        acc = (acc * (IDX_DIM ** -0.5)).reshape(K_CHUNK, tq)
        acc = jnp.where(causal_mask(k0), acc, NEG_INF)
        sc_ref[pl.ds(k0, K_CHUNK), :] = acc
        hi_bits = pltpu.bitcast(acc, jnp.int32) & jnp.int32(-65536)
        sh_ref[pl.ds(k0, K_CHUNK), :] = pltpu.bitcast(hi_bits, F32).astype(COARSE)
        return carry

    lax.fori_loop(0, n_chunks, score_body, 0)

    def count_all(ref, thr_tile, preds):
        rows = thr_tile.shape[0]
        one, zero = jnp.ones((), ref.dtype), jnp.zeros((), ref.dtype)

        def body(c, cnts):
            k0 = pl.multiple_of(c * K_CHUNK, K_CHUNK)
            s = ref[pl.ds(k0, K_CHUNK), :]
            out = []
            for cnt, p in zip(cnts, preds):
                hits = [jnp.where(p(s[j * rows:(j + 1) * rows], thr_tile), one, zero)
                        for j in range(K_CHUNK // rows)]
                out.append(cnt + _tree(hits, jnp.add).astype(F32))
            return tuple(out)

        init = tuple(jnp.zeros((rows, tq), F32) for _ in preds)
        cnts = lax.fori_loop(0, n_chunks, body, init)
        return [rows8(jnp.sum(cnt, axis=0, keepdims=True)) for cnt in cnts]

    def search_body(it, tkey, coarse):
        cand = tkey + lax.shift_left(jnp.int32(1), 31 - it)
        cf = _key_to_float(cand)
        if coarse:
            cf = pltpu.bitcast(pltpu.bitcast(cf, jnp.int32) & jnp.int32(-65536), F32).astype(COARSE)
            cf = jnp.concatenate([cf] * (BF16_SUBLANES // SUBLANES), axis=0)
        cnt, = count_all(sh_ref if coarse else sc_ref, cf, [lambda s, t: s >= t])
        return jnp.where(cnt >= k_sel, cand, tkey)

    tkey = jnp.full((SUBLANES, tq), INT_MIN, jnp.int32)
    tkey = lax.fori_loop(0, 16, functools.partial(search_body, coarse=True), tkey)
    tkey = lax.fori_loop(16, 32, functools.partial(search_body, coarse=False), tkey)
    thr = _key_to_float(tkey)

    n_ge, n_gt = count_all(sc_ref, thr, [lambda s, t: s >= t, lambda s, t: s > t])
    need = k_sel - n_gt
    has_ties = jnp.max(n_ge) > k_sel

    @pl.when(jnp.logical_not(has_ties))
    def _():
        def body(c, carry):
            k0 = pl.multiple_of(c * K_CHUNK, K_CHUNK)
            s = sc_ref[pl.ds(k0, K_CHUNK), :]
            bias = jnp.where(tiles(s) >= thr[None], 0.0, NEG_INF).reshape(K_CHUNK, tq)
            sc_ref[pl.ds(k0, K_CHUNK), :] = jnp.where(causal_mask(k0), bias, NEG_INF)
            return carry
        lax.fori_loop(0, n_chunks, body, 0)

    @pl.when(has_ties)
    def _():
        r_i = lax.broadcasted_iota(jnp.int32, (K_CHUNK, K_CHUNK), 0)
        c_i = lax.broadcasted_iota(jnp.int32, (K_CHUNK, K_CHUNK), 1)
        lower = jnp.where(c_i <= r_i, 1.0, 0.0).astype(BF16)
        thr_row, need_row = thr[0:1], need[0:1]

        def body(c, seen):
            k0 = pl.multiple_of(c * K_CHUNK, K_CHUNK)
            s = sc_ref[pl.ds(k0, K_CHUNK), :]
            eq = jnp.where(s == thr_row, 1.0, 0.0)
            rank = _dot(lower, eq.astype(BF16)) + seen
            keep_tie = jnp.where(rank <= need_row, eq, 0.0)
            sel = jnp.where(s > thr_row, 1.0, keep_tie)
            sc_ref[pl.ds(k0, K_CHUNK), :] = jnp.where(
                sel > 0.0, jnp.where(causal_mask(k0), 0.0, NEG_INF), NEG_INF)
            return rank[K_CHUNK - 1:K_CHUNK, :]
        lax.fori_loop(0, n_chunks, body, jnp.zeros((1, tq), F32))

    m_ref[...] = jnp.full(m_ref.shape, NEG_INF, F32)
    acc_ref[...] = jnp.zeros(acc_ref.shape, F32)
    ones_rows = jnp.ones((BF16_SUBLANES, K_CHUNK), BF16)

    def attn_body(c, carry):
        k0 = pl.multiple_of(c * K_CHUNK, K_CHUNK)
        bias = tiles(sc_ref[pl.ds(k0, K_CHUNK), :])
        for h in range(N_HEADS):
            pair = slice((h // 2) * LANES, (h // 2 + 1) * LANES)
            lg = tiles(_dot(k_ref[0, pl.ds(k0, K_CHUNK), pair], qm_ref[h])) + bias
            lg_ref[h] = lg.reshape(K_CHUNK, tq)
            m_old = m_ref[h]
            m_new = jnp.maximum(m_old, rows8(jnp.max(jnp.max(lg, axis=0), axis=0, keepdims=True)))
            m_safe = jnp.where(m_new == NEG_INF, 0.0, m_new)
            alpha_ref[h] = jnp.exp2(m_old - m_safe)
            m_ref[h] = m_new
        for h in range(N_HEADS):
            pair = slice((h // 2) * LANES, (h // 2 + 1) * LANES)
            m_new = m_ref[h]
            m_safe = jnp.where(m_new == NEG_INF, 0.0, m_new)
            p = jnp.exp2(tiles(lg_ref[h]) - m_safe[None]).reshape(K_CHUNK, tq).astype(BF16)
            v_aug = jnp.concatenate([vt_ref[0, pair, pl.ds(k0, K_CHUNK)], ones_rows], axis=0)
            pv = _dot(v_aug, p)
            acc = acc_ref[h].reshape(v_rows // SUBLANES, SUBLANES, tq) * alpha_ref[h][None]
            acc_ref[h] = acc.reshape(v_rows, tq) + pv
        return carry

    lax.fori_loop(0, n_chunks, attn_body, 0)

    for pair in range(N_HEADS // 2):
        halves = []
        for h in (2 * pair, 2 * pair + 1):
            a = acc_ref[h]
            lo = (h % 2) * HEAD_DIM
            halves.append(a[lo:lo + HEAD_DIM] / a[LANES:LANES + 1])
        out_t = jnp.concatenate(halves, axis=0)
        o_ref[0, :, pair * LANES:(pair + 1) * LANES] = out_t.T.astype(o_ref.dtype)


def _dsa_call(qt, qit, wit, k, ki, vt, k_sel):
    B, tp, _ = k.shape
    nq = tp // Q_TILE

    def q_cols(r):
        return pl.BlockSpec((1, r, Q_TILE), lambda b, i: (b, 0, i))

    def per_batch(a):
        return pl.BlockSpec((1,) + a.shape[1:], lambda b, i: (b, 0, 0), pipeline_mode=pl.Buffered(1))

    return pl.pallas_call(
        functools.partial(_dsa_kernel, k_sel=k_sel),
        grid=(B, nq),
        in_specs=[q_cols(ATTN_W), q_cols(IDX_HEADS * IDX_DIM), q_cols(IDX_HEADS),
                  per_batch(k), per_batch(ki), per_batch(vt)],
        out_specs=pl.BlockSpec((1, Q_TILE, ATTN_W), lambda b, i: (b, i, 0)),
        out_shape=jax.ShapeDtypeStruct((B, tp, ATTN_W), BF16),
        scratch_shapes=[pltpu.VMEM((tp, Q_TILE), F32),
                        pltpu.VMEM((tp, Q_TILE), COARSE),
                        pltpu.VMEM((N_HEADS, LANES, Q_TILE), BF16),
                        pltpu.VMEM((IDX_HEADS, LANES, Q_TILE), BF16),
                        pltpu.VMEM((N_HEADS, SUBLANES, Q_TILE), F32),
                        pltpu.VMEM((N_HEADS, SUBLANES, Q_TILE), F32),
                        pltpu.VMEM((N_HEADS, K_CHUNK, Q_TILE), F32),
                        pltpu.VMEM((N_HEADS, LANES + BF16_SUBLANES, Q_TILE), F32)],
        compiler_params=pltpu.CompilerParams(
            dimension_semantics=("arbitrary", "arbitrary"), vmem_limit_bytes=VMEM_LIMIT),
        name="dsa",
    )(qt, qit, wit, k, ki, vt)


def _max_all(x):
    return jnp.max(jnp.max(x, axis=1, keepdims=True), axis=0, keepdims=True)


def _router_gates(logits_t, rbias):
    r = logits_t.shape[1]
    shape3 = (N_GROUPS, GROUP_SIZE, r)
    scores = jax.nn.sigmoid(logits_t).reshape(shape3)
    biased = scores + rbias.reshape(N_GROUPS, GROUP_SIZE, 1)
    in_grp = lax.broadcasted_iota(jnp.int32, shape3, 1).astype(F32)
    m1 = jnp.max(biased, axis=1, keepdims=True)
    first = jnp.min(jnp.where(biased == m1, in_grp, float(GROUP_SIZE)), axis=1, keepdims=True)
    m2 = jnp.max(jnp.where(in_grp == first, NEG_INF, biased), axis=1, keepdims=True)
    cur = m1 + m2

    grp_idx = lax.broadcasted_iota(jnp.int32, (N_GROUPS, 1, r), 0).astype(F32)
    grp_sel = jnp.zeros((N_GROUPS, 1, r), F32)
    for _ in range(TOPK_GROUPS):
        m = jnp.max(cur, axis=0, keepdims=True)
        pick = grp_idx == jnp.min(jnp.where(cur == m, grp_idx, float(N_GROUPS)), axis=0, keepdims=True)
        grp_sel = jnp.where(pick, 1.0, grp_sel)
        cur = jnp.where(pick, NEG_INF, cur)

    cur = jnp.where(jnp.broadcast_to(grp_sel, shape3) > 0.0, biased, NEG_INF)
    exp_idx = lax.broadcasted_iota(jnp.int32, shape3, 0).astype(F32) * GROUP_SIZE + in_grp
    chosen = jnp.zeros(shape3, F32)
    for _ in range(TOP_K):
        m = _max_all(cur)
        first = -_max_all(-jnp.where(cur == m, exp_idx, float(N_EXPERTS)))
        pick = exp_idx == first
        chosen = jnp.where(pick, 1.0, chosen)
        cur = jnp.where(pick, NEG_INF, cur)

    w = jnp.where(chosen > 0.0, scores, 0.0)
    denom = jnp.sum(jnp.sum(w, axis=1, keepdims=True), axis=0, keepdims=True)
    return (w / denom * ROUTED_SCALE).reshape(N_EXPERTS, r)


def _post_kernel(conv_ref, attn_ref, hn_ref, woc_ref, woa_ref, g1_ref, b1_ref,
                 wsg_ref, wsu_ref, wsd_ref, wrh_ref, wrl_ref, rb_ref,
                 xb_ref, base_ref, gates_ref):
    mix = _dot(conv_ref[...], woc_ref[...]) + _dot(attn_ref[...], woa_ref[...])
    h1 = _layer_norm_rows(DN_ALPHA * hn_ref[...] + mix, g1_ref[...], b1_ref[...])
    xb = h1.astype(BF16)
    xb_ref[...] = xb

    shared = jax.nn.silu(_dot(xb, wsg_ref[...])) * _dot(xb, wsu_ref[...])
    base_ref[...] = DN_ALPHA * h1 + _dot(shared.astype(BF16), wsd_ref[...])

    x_lo = (h1 - xb.astype(F32)).astype(BF16)
    logits_t = (_dot_nt(wrh_ref[...], xb) + _dot_nt(wrh_ref[...], x_lo) + _dot_nt(wrl_ref[...], xb))
    gates_t = _router_gates(logits_t, rb_ref[...])
    padded = jnp.concatenate([gates_t, jnp.zeros((LANES - N_EXPERTS, gates_t.shape[1]), F32)], axis=0)
    gates_ref[...] = padded.T


def _post_call(conv, attn, hn, woc, woa, g1, b1, wsg, wsu, wsd, wrh, wrl, rbias):
    n, D = hn.shape
    tr = _pick_tile(n, (768, 512, 256))

    def row_spec(w):
        return pl.BlockSpec((tr, w), lambda i: (i, 0))

    def full(a):
        return pl.BlockSpec(a.shape, lambda i: (0,) * a.ndim)

    consts = [woc, woa, g1, b1, wsg, wsu, wsd, wrh, wrl, rbias]
    return pl.pallas_call(
        _post_kernel,
        grid=(n // tr,),
        in_specs=[row_spec(CONV_CH), row_spec(ATTN_W), row_spec(D)] + [full(a) for a in consts],
        out_specs=[row_spec(D), row_spec(D), row_spec(LANES)],
        out_shape=[jax.ShapeDtypeStruct((n, D), BF16), jax.ShapeDtypeStruct((n, D), F32),
                   jax.ShapeDtypeStruct((n, LANES), F32)],
        compiler_params=pltpu.CompilerParams(
            dimension_semantics=("arbitrary",), vmem_limit_bytes=VMEM_LIMIT),
        name="post",
    )(conv, attn, hn, *consts)


def _moe_kernel(xb_ref, base_ref, gates_ref, wg_ref, wu_ref, wd_ref, g2_ref, b2_ref, o_ref):
    s = pl.program_id(1)
    n_per_step, d_exp = wg_ref.shape[0], wg_ref.shape[2]

    @pl.when(s == 0)
    def _():
        o_ref[...] = base_ref[...]

    x = xb_ref[...]
    gates = gates_ref[...]
    lane = lax.broadcasted_iota(jnp.int32, gates.shape, 1)
    hidden = []
    for j in range(n_per_step):
        hg = _dot(x, wg_ref[j].astype(BF16))
        hu = _dot(x, wu_ref[j].astype(BF16))
        gate = jnp.sum(jnp.where(lane == s * n_per_step + j, gates, 0.0), axis=1, keepdims=True)
        hidden.append((jax.nn.silu(hg) * hu * gate).astype(BF16))
    wd = wd_ref[...].astype(BF16).reshape(n_per_step * d_exp, wd_ref.shape[2])
    o_ref[...] += _dot(jnp.concatenate(hidden, axis=1), wd)

    @pl.when(s == pl.num_programs(1) - 1)
    def _():
        o_ref[...] = _layer_norm_rows(o_ref[...], g2_ref[...], b2_ref[...])


def _moe_call(xb, base, gates, w_gate, w_up, w_down, g2, b2):
    n, D = xb.shape
    n_exp, _, d_exp = w_gate.shape
    tm = _pick_tile(n, (1536, 768, 512, 256))
    eps = MOE_EXPERTS_PER_STEP
    assert n_exp % eps == 0

    def row_spec(w, **kw):
        return pl.BlockSpec((tm, w), lambda i, s: (i, 0), **kw)

    once = dict(pipeline_mode=pl.Buffered(1))
    vec = pl.BlockSpec((1, D), lambda i, s: (0, 0))
    return pl.pallas_call(
        _moe_kernel,
        grid=(n // tm, n_exp // eps),
        in_specs=[row_spec(D, **once), row_spec(D, **once), row_spec(LANES, **once),
                  pl.BlockSpec((eps, D, d_exp), lambda i, s: (s, 0, 0)),
                  pl.BlockSpec((eps, D, d_exp), lambda i, s: (s, 0, 0)),
                  pl.BlockSpec((eps, d_exp, D), lambda i, s: (s, 0, 0)),
                  vec, vec],
        out_specs=row_spec(D),
        out_shape=jax.ShapeDtypeStruct((n, D), F32),
        compiler_params=pltpu.CompilerParams(
            dimension_semantics=("arbitrary", "arbitrary"), vmem_limit_bytes=VMEM_LIMIT),
        name="moe",
    )(xb, base, gates, w_gate, w_up, w_down, g2, b2)


def _rope_tables(tp):
    pos = jnp.arange(tp, dtype=F32)
    inv = jnp.power(ROPE_THETA, -2.0 * jnp.arange(ROPE_HALF, dtype=F32) / ROPE_DIM)
    ang = pos[:, None] * inv[None, :]
    cos, sin = jnp.cos(ang), jnp.sin(ang)
    zeros = jnp.zeros((tp, HEAD_DIM - ROPE_DIM), F32)
    zh = jnp.zeros((tp, ROPE_HALF), F32)
    c64 = jnp.concatenate([cos, cos, jnp.ones_like(zeros)], axis=1)
    s1_64 = jnp.concatenate([-sin, zh, zeros], axis=1)
    s2_64 = jnp.concatenate([zh, sin, zeros], axis=1)
    rep = LANES // HEAD_DIM
    return (jnp.tile(c64, (1, rep)), jnp.tile(s1_64, (1, rep)), jnp.tile(s2_64, (1, rep)),
            cos.T, sin.T)


def kernel(x, meta_tokens, ln_emb_g, ln_emb_b, w_in, conv_w, conv_b, ln_conv_g, ln_conv_b, ln_kidx_g, ln_kidx_b, w_out, ln1_g, ln1_b, w_router, router_bias, w_gate, w_up, w_down, ws_gate, ws_up, ws_down, ln2_g, ln2_b):
    B, seq, D = x.shape
    assert w_in.shape[0] == DEPTH
    k_sel = min(INDEX_TOPK, seq // 4)
    t_real = N_META + seq
    tp = -(-t_real // SEQ_ALIGN) * SEQ_ALIGN

    meta = jnp.broadcast_to(meta_tokens[None].astype(x.dtype), (B, N_META, D))
    hp = jnp.concatenate([meta, x, jnp.zeros((B, tp - t_real, D), x.dtype)], axis=1)

    def row(a):
        return a.reshape(1, -1).astype(F32)

    w = w_in[0]
    o = 0
    parts = []
    for width in (CONV_CH, CONV_CH, ATTN_W, ATTN_W, ATTN_W, IDX_HEADS * IDX_DIM, IDX_DIM, IDX_HEADS):
        parts.append(w[:, o:o + width])
        o += width
    wa, wgl, wq, wk, wv, wqi, wki, wwi = parts
    wwi_t = jnp.concatenate([wwi.T, jnp.zeros((BF16_SUBLANES - IDX_HEADS, D), w.dtype)], axis=0)
    weights = (jnp.concatenate([wa, wgl], axis=1).astype(BF16), wq.T.astype(BF16), wk.astype(BF16),
               wv.T.astype(BF16), wqi.T.astype(BF16), jnp.concatenate([wki, wki], axis=1).astype(BF16),
               wwi_t.astype(BF16))

    def twice(a):
        return row(jnp.concatenate([a, a]))

    hn, conv, qt, k, vt, qit, ki, wit = _inproj_call(
        hp, _rope_tables(tp), row(ln_emb_g), row(ln_emb_b), weights,
        conv_w[0].astype(F32), row(conv_b[0]), row(ln_conv_g[0]), row(ln_conv_b[0]),
        twice(ln_kidx_g[0]), twice(ln_kidx_b[0]))

    attn = _dsa_call(qt, qit, wit, k, ki, vt, k_sel)

    n = B * tp
    wr_t = w_router[0].T.astype(F32)
    wr_hi = wr_t.astype(BF16)
    wr_lo = (wr_t - wr_hi.astype(F32)).astype(BF16)
    xb, base, gates = _post_call(
        conv.reshape(n, CONV_CH), attn.reshape(n, ATTN_W), hn.reshape(n, D),
        w_out[0][:CONV_CH].astype(BF16), w_out[0][CONV_CH:].astype(BF16), row(ln1_g[0]), row(ln1_b[0]),
        ws_gate[0].astype(BF16), ws_up[0].astype(BF16), ws_down[0].astype(BF16),
        wr_hi, wr_lo, router_bias[0].reshape(-1, 1).astype(F32))

    out = _moe_call(xb, base, gates, w_gate[0], w_up[0], w_down[0], row(ln2_g[0]), row(ln2_b[0]))
    return out.reshape(B, tp, D)[:, N_META:t_real]
```

```python
import functools

import numpy as np
import jax
import jax.numpy as jnp
from jax import lax
from jax.experimental import pallas as pl
from jax.experimental.pallas import tpu as pltpu
from jax.experimental.pallas import tpu_sc as plsc

N_META = 16
CONV_CH = 512
CONV_WIDTH = 31
N_HEADS = 8
HEAD_DIM = 64
ATTN_W = N_HEADS * HEAD_DIM
IDX_HEADS = 8
IDX_DIM = 64
INDEX_TOPK = 256
ROPE_DIM = HEAD_DIM // 4
ROPE_HALF = ROPE_DIM // 2
ROPE_THETA = 500000.0
N_EXPERTS = 64
TOP_K = 8
N_GROUPS = 8
GROUP_SIZE = N_EXPERTS // N_GROUPS
TOPK_GROUPS = 4
ROUTED_SCALE = 2.5
LN_EPS = 1e-5
DEPTH = 1
DN_ALPHA = (2.0 * DEPTH) ** 0.25

LANES = 128
Q_TILE = 256
SUBLANES = 8
BF16_SUBLANES = 16
MOE_EXPERTS_PER_STEP = 2
EXPERT_BLOCK = 512
SC_ROWS = 128
K_CHUNK = 256
K_SUB = 128
SEQ_ALIGN = 256
CONV_HALO = 32
VMEM_LIMIT = 56 * 1024 * 1024

F32 = jnp.float32
BF16 = jnp.bfloat16
COARSE = jnp.bfloat16
NEG_INF = float("-inf")
INT_MIN = -2 ** 31
KEY_NEG_INF = -2139095041
LOG2_E = 1.4426950408889634


def _dot(a, b):
    return jnp.dot(a, b, preferred_element_type=F32)


def _dot_nt(a, b):
    return lax.dot_general(a, b, (((1,), (1,)), ((), ())), preferred_element_type=F32)


def _layer_norm_rows(x, g, b):
    mu = jnp.mean(x, axis=-1, keepdims=True)
    xc = x - mu
    var = jnp.mean(xc * xc, axis=-1, keepdims=True)
    return xc * lax.rsqrt(var + LN_EPS) * g + b


def _pick_tile(n, candidates):
    for c in candidates:
        if n % c == 0:
            return c
    raise ValueError(f"no tile for {n}")


def _rope_rows(x, c_tab, s1_tab, s2_tab):
    outs = []
    for j in range(x.shape[1] // LANES):
        xs = x[:, j * LANES:(j + 1) * LANES]
        up = pltpu.roll(xs, LANES - ROPE_HALF, axis=1)
        dn = pltpu.roll(xs, ROPE_HALF, axis=1)
        outs.append(xs * c_tab + up * s1_tab + dn * s2_tab)
    return jnp.concatenate(outs, axis=1)


def _rope_cols(xt, cos_t, sin_t, heads):
    r = xt.shape[1]
    x3 = xt.reshape(heads, HEAD_DIM, r)
    x1 = x3[:, 0:ROPE_HALF, :]
    x2 = x3[:, ROPE_HALF:ROPE_DIM, :]
    n1 = x1 * cos_t - x2 * sin_t
    n2 = x2 * cos_t + x1 * sin_t
    out = jnp.concatenate([n1, n2, x3[:, ROPE_DIM:, :]], axis=1)
    return out.reshape(heads * HEAD_DIM, r)


def _inproj_kernel(h_ref, ctab_ref, s1tab_ref, s2tab_ref, cost_ref, sint_ref,
                   lng_ref, lnb_ref, wag_ref, wqt_ref, wk_ref, wvt_ref, wqit_ref, wki_ref, wwit_ref,
                   cw_ref, cb_ref, lncg_ref, lncb_ref, lnkg_ref, lnkb_ref,
                   hn_ref, conv_ref, qt_ref, k_ref, vt_ref, qit_ref, ki_ref, wit_ref,
                   ubuf_ref):
    t = pl.program_id(1)
    tr = h_ref.shape[1]

    hn = _layer_norm_rows(h_ref[0], lng_ref[...], lnb_ref[...])
    hn_ref[0] = hn
    xb = hn.astype(BF16)

    ag = _dot(xb, wag_ref[...])
    u = ag[:, :CONV_CH] * jax.nn.sigmoid(ag[:, CONV_CH:])

    @pl.when(t == 0)
    def _():
        ubuf_ref[0:CONV_HALO, :] = jnp.zeros((CONV_HALO, CONV_CH), F32)

    ubuf_ref[CONV_HALO:CONV_HALO + tr, :] = u
    base = CONV_HALO - (CONV_WIDTH - 1)
    acc = jnp.zeros((tr, CONV_CH), F32)
    for j in range(CONV_WIDTH):
        acc = acc + cw_ref[j:j + 1, :] * ubuf_ref[base + j:base + j + tr, :]
    ubuf_ref[0:CONV_HALO, :] = ubuf_ref[tr:tr + CONV_HALO, :]
    c = _layer_norm_rows(acc + cb_ref[...], lncg_ref[...], lncb_ref[...])
    conv_ref[0] = (c * jax.nn.sigmoid(c)).astype(conv_ref.dtype)

    ctab, s1tab, s2tab = ctab_ref[...], s1tab_ref[...], s2tab_ref[...]
    cos_t, sin_t = cost_ref[...], sint_ref[...]

    qt = _rope_cols(_dot_nt(wqt_ref[...], xb), cos_t, sin_t, N_HEADS)
    qt_ref[0] = (qt * (HEAD_DIM ** -0.5 * LOG2_E)).astype(qt_ref.dtype)
    k = _rope_rows(_dot(xb, wk_ref[...]), ctab, s1tab, s2tab)
    k_ref[0] = k.astype(k_ref.dtype)
    vt_ref[0] = _dot_nt(wvt_ref[...], xb).astype(vt_ref.dtype)

    qit = _rope_cols(_dot_nt(wqit_ref[...], xb), cos_t, sin_t, IDX_HEADS)
    qit_ref[0] = qit.astype(qit_ref.dtype)
    ki = _layer_norm_rows(_dot(xb, wki_ref[...]), lnkg_ref[...], lnkb_ref[...])
    ki_ref[0] = _rope_rows(ki, ctab, s1tab, s2tab).astype(ki_ref.dtype)
    wit = _dot_nt(wwit_ref[...], xb) * (IDX_HEADS ** -0.5)
    wit_ref[0] = wit[:IDX_HEADS]


def _inproj_call(hp, tabs, ln_g, ln_b, weights, conv_w, conv_b, lnc_g, lnc_b, lnk_g, lnk_b):
    B, tp, D = hp.shape
    tr = _pick_tile(tp, (768, 512, 256))
    nt = tp // tr
    ctab, s1tab, s2tab, cos_t, sin_t = tabs
    def row_spec(w):
        return pl.BlockSpec((1, tr, w), lambda b, t: (b, t, 0))

    def col_spec(r):
        return pl.BlockSpec((1, r, tr), lambda b, t: (b, 0, t))

    def full(a):
        return pl.BlockSpec(a.shape, lambda b, t: (0,) * a.ndim)

    tab_row = pl.BlockSpec((tr, LANES), lambda b, t: (t, 0))
    tab_col = pl.BlockSpec((ROPE_HALF, tr), lambda b, t: (0, t))
    consts = [ln_g, ln_b, *weights, conv_w, conv_b, lnc_g, lnc_b, lnk_g, lnk_b]
    out_shape = [
        jax.ShapeDtypeStruct((B, tp, D), F32),
        jax.ShapeDtypeStruct((B, tp, CONV_CH), BF16),
        jax.ShapeDtypeStruct((B, ATTN_W, tp), BF16),
        jax.ShapeDtypeStruct((B, tp, ATTN_W), BF16),
        jax.ShapeDtypeStruct((B, ATTN_W, tp), BF16),
        jax.ShapeDtypeStruct((B, IDX_HEADS * IDX_DIM, tp), BF16),
        jax.ShapeDtypeStruct((B, tp, 2 * IDX_DIM), BF16),
        jax.ShapeDtypeStruct((B, IDX_HEADS, tp), F32),
    ]
    out_specs = [row_spec(D), row_spec(CONV_CH), col_spec(ATTN_W), row_spec(ATTN_W), col_spec(ATTN_W),
                 col_spec(IDX_HEADS * IDX_DIM), row_spec(2 * IDX_DIM), col_spec(IDX_HEADS)]
    return pl.pallas_call(
        _inproj_kernel,
        grid=(B, nt),
        in_specs=[row_spec(D), tab_row, tab_row, tab_row, tab_col, tab_col] + [full(a) for a in consts],
        out_specs=out_specs,
        out_shape=out_shape,
        scratch_shapes=[pltpu.VMEM((CONV_HALO + tr, CONV_CH), F32)],
        compiler_params=pltpu.CompilerParams(
            dimension_semantics=("arbitrary", "arbitrary"), vmem_limit_bytes=VMEM_LIMIT),
        name="inproj",
    )(hp, ctab, s1tab, s2tab, cos_t, sin_t, *consts)


def _key_to_float(key):
    bits = jnp.where(key >= 0, key, key ^ jnp.int32(0x7FFFFFFF))
    f = pltpu.bitcast(bits, F32)
    return jnp.where(key < jnp.int32(KEY_NEG_INF), NEG_INF, f)


def _tree(parts, op):
    parts = list(parts)
    while len(parts) > 1:
        nxt = [op(parts[j], parts[j + 1]) for j in range(0, len(parts) - 1, 2)]
        if len(parts) % 2:
            nxt.append(parts[-1])
        parts = nxt
    return parts[0]


def _fold_rows(x, rows, op):
    return _tree([x[j * rows:(j + 1) * rows] for j in range(x.shape[0] // rows)], op)


def _dsa_kernel(qt_ref, qit_ref, wit_ref, k_ref, ki_ref, vt_ref, o_ref,
                sc_ref, sh_ref, qm_ref, qim_ref, m_ref, alpha_ref, lg_ref, acc_ref, *, k_sel):
    i = pl.program_id(1)
    tq = qt_ref.shape[2]
    n_chunks = (i * tq + tq + K_CHUNK - 1) // K_CHUNK
    v_rows = LANES + BF16_SUBLANES

    def causal_mask(k0, rows=K_CHUNK):
        kpos = k0 + lax.broadcasted_iota(jnp.int32, (rows, tq), 0)
        return kpos <= i * tq + lax.broadcasted_iota(jnp.int32, (rows, tq), 1)

    def rows8(x):
        return jnp.broadcast_to(x, (SUBLANES, tq))

    def tiles(x):
        return x.reshape(x.shape[0] // SUBLANES, SUBLANES, tq)

    def head_slab(ref, h):
        slab = ref[0, (h // 2) * LANES:(h // 2 + 1) * LANES, :]
        zeros = jnp.zeros((HEAD_DIM, tq), slab.dtype)
        if h % 2 == 0:
            return jnp.concatenate([slab[:HEAD_DIM], zeros], axis=0)
        return jnp.concatenate([zeros, slab[HEAD_DIM:]], axis=0)

    for h in range(N_HEADS):
        qm_ref[h] = head_slab(qt_ref, h)
    for h in range(IDX_HEADS):
        qim_ref[h] = head_slab(qit_ref, h)
    wit = wit_ref[0]
    w_heads = [rows8(wit[h:h + 1]) for h in range(IDX_HEADS)]

    def score_body(c, carry):
        for s in range(K_CHUNK // K_SUB):
            k0 = pl.multiple_of(c * K_CHUNK + s * K_SUB, K_SUB)
            kic = ki_ref[0, pl.ds(k0, K_SUB), :]
            acc = jnp.zeros((K_SUB // SUBLANES, SUBLANES, tq), F32)
            for h in range(IDX_HEADS):
                acc = acc + w_heads[h][None] * jnp.maximum(tiles(_dot(kic, qim_ref[h])), 0.0)
            acc = (acc * (IDX_DIM ** -0.5)).reshape(K_SUB, tq)
            acc = jnp.where(causal_mask(k0, K_SUB), acc, NEG_INF)
            sc_ref[pl.ds(k0, K_SUB), :] = acc
            hi_bits = pltpu.bitcast(acc, jnp.int32) & jnp.int32(-65536)
            sh_ref[pl.ds(k0, K_SUB), :] = pltpu.bitcast(hi_bits, F32).astype(COARSE)
        return carry

    lax.fori_loop(0, n_chunks, score_body, 0)

    def count_all(ref, thr_tile, preds):
        rows = thr_tile.shape[0]
        one, zero = jnp.ones((), ref.dtype), jnp.zeros((), ref.dtype)

        def body(c, cnts):
            k0 = pl.multiple_of(c * K_CHUNK, K_CHUNK)
            s = ref[pl.ds(k0, K_CHUNK), :]
            out = []
            for cnt, p in zip(cnts, preds):
                hits = [jnp.where(p(s[j * rows:(j + 1) * rows], thr_tile), one, zero)
                        for j in range(K_CHUNK // rows)]
                out.append(cnt + _tree(hits, jnp.add).astype(F32))
            return tuple(out)

        init = tuple(jnp.zeros((rows, tq), F32) for _ in preds)
        cnts = lax.fori_loop(0, n_chunks, body, init)
        return [rows8(jnp.sum(cnt, axis=0, keepdims=True)) for cnt in cnts]

    def search_body(it, tkey, coarse):
        cand = tkey + lax.shift_left(jnp.int32(1), 31 - it)
        cf = _key_to_float(cand)
        if coarse:
            cf = pltpu.bitcast(pltpu.bitcast(cf, jnp.int32) & jnp.int32(-65536), F32).astype(COARSE)
            cf = jnp.concatenate([cf] * (BF16_SUBLANES // SUBLANES), axis=0)
        cnt, = count_all(sh_ref if coarse else sc_ref, cf, [lambda s, t: s >= t])
        return jnp.where(cnt >= k_sel, cand, tkey)

    tkey = jnp.full((SUBLANES, tq), INT_MIN, jnp.int32)
    tkey = lax.fori_loop(0, 16, functools.partial(search_body, coarse=True), tkey)
    tkey = lax.fori_loop(16, 32, functools.partial(search_body, coarse=False), tkey)
    thr = _key_to_float(tkey)

    n_ge, n_gt = count_all(sc_ref, thr, [lambda s, t: s >= t, lambda s, t: s > t])
    need = k_sel - n_gt
    has_ties = jnp.max(n_ge) > k_sel

    @pl.when(jnp.logical_not(has_ties))
    def _():
        def body(c, carry):
            k0 = pl.multiple_of(c * K_CHUNK, K_CHUNK)
            s = sc_ref[pl.ds(k0, K_CHUNK), :]
            bias = jnp.where(tiles(s) >= thr[None], 0.0, NEG_INF).reshape(K_CHUNK, tq)
            sc_ref[pl.ds(k0, K_CHUNK), :] = jnp.where(causal_mask(k0), bias, NEG_INF)
            return carry
        lax.fori_loop(0, n_chunks, body, 0)

    @pl.when(has_ties)
    def _():
        r_i = lax.broadcasted_iota(jnp.int32, (K_CHUNK, K_CHUNK), 0)
        c_i = lax.broadcasted_iota(jnp.int32, (K_CHUNK, K_CHUNK), 1)
        lower = jnp.where(c_i <= r_i, 1.0, 0.0).astype(BF16)
        thr_row, need_row = thr[0:1], need[0:1]

        def body(c, seen):
            k0 = pl.multiple_of(c * K_CHUNK, K_CHUNK)
            s = sc_ref[pl.ds(k0, K_CHUNK), :]
            eq = jnp.where(s == thr_row, 1.0, 0.0)
            rank = _dot(lower, eq.astype(BF16)) + seen
            keep_tie = jnp.where(rank <= need_row, eq, 0.0)
            sel = jnp.where(s > thr_row, 1.0, keep_tie)
            sc_ref[pl.ds(k0, K_CHUNK), :] = jnp.where(
                sel > 0.0, jnp.where(causal_mask(k0), 0.0, NEG_INF), NEG_INF)
            return rank[K_CHUNK - 1:K_CHUNK, :]
        lax.fori_loop(0, n_chunks, body, jnp.zeros((1, tq), F32))

    m_ref[...] = jnp.full(m_ref.shape, NEG_INF, F32)
    acc_ref[...] = jnp.zeros(acc_ref.shape, F32)
    ones_rows = jnp.ones((BF16_SUBLANES, K_CHUNK), BF16)

    def attn_body(c, carry):
        k0 = pl.multiple_of(c * K_CHUNK, K_CHUNK)
        for h in range(N_HEADS):
            pair = slice((h // 2) * LANES, (h // 2 + 1) * LANES)
            cmax = []
            for s in range(K_CHUNK // K_SUB):
                rows = pl.ds(pl.multiple_of(k0 + s * K_SUB, K_SUB), K_SUB)
                lg = tiles(_dot(k_ref[0, rows, pair], qm_ref[h])) + tiles(sc_ref[rows, :])
                lg_ref[h, s * K_SUB:(s + 1) * K_SUB, :] = lg.reshape(K_SUB, tq)
                cmax.append(jnp.max(lg, axis=0))
            m_old = m_ref[h]
            m_new = jnp.maximum(m_old, rows8(jnp.max(_tree(cmax, jnp.maximum), axis=0, keepdims=True)))
            m_safe = jnp.where(m_new == NEG_INF, 0.0, m_new)
            alpha_ref[h] = jnp.exp2(m_old - m_safe)
            m_ref[h] = m_new
        for h in range(N_HEADS):
            pair = slice((h // 2) * LANES, (h // 2 + 1) * LANES)
            m_new = m_ref[h]
            m_safe = jnp.where(m_new == NEG_INF, 0.0, m_new)
            p = jnp.exp2(tiles(lg_ref[h]) - m_safe[None]).reshape(K_CHUNK, tq).astype(BF16)
            v_aug = jnp.concatenate([vt_ref[0, pair, pl.ds(k0, K_CHUNK)], ones_rows], axis=0)
            pv = _dot(v_aug, p)
            acc = acc_ref[h].reshape(v_rows // SUBLANES, SUBLANES, tq) * alpha_ref[h][None]
            acc_ref[h] = acc.reshape(v_rows, tq) + pv
        return carry

    lax.fori_loop(0, n_chunks, attn_body, 0)

    for pair in range(N_HEADS // 2):
        halves = []
        for h in (2 * pair, 2 * pair + 1):
            a = acc_ref[h]
            lo = (h % 2) * HEAD_DIM
            halves.append(a[lo:lo + HEAD_DIM] / a[LANES:LANES + 1])
        out_t = jnp.concatenate(halves, axis=0)
        o_ref[0, :, pair * LANES:(pair + 1) * LANES] = out_t.T.astype(o_ref.dtype)


def _dsa_call(qt, qit, wit, k, ki, vt, k_sel):
    B, tp, _ = k.shape
    nq = tp // Q_TILE

    def q_cols(r):
        return pl.BlockSpec((1, r, Q_TILE), lambda b, i: (b, 0, i))

    def per_batch(a):
        return pl.BlockSpec((1,) + a.shape[1:], lambda b, i: (b, 0, 0), pipeline_mode=pl.Buffered(1))

    return pl.pallas_call(
        functools.partial(_dsa_kernel, k_sel=k_sel),
        grid=(B, nq),
        in_specs=[q_cols(ATTN_W), q_cols(IDX_HEADS * IDX_DIM), q_cols(IDX_HEADS),
                  per_batch(k), per_batch(ki), per_batch(vt)],
        out_specs=pl.BlockSpec((1, Q_TILE, ATTN_W), lambda b, i: (b, i, 0)),
        out_shape=jax.ShapeDtypeStruct((B, tp, ATTN_W), BF16),
        scratch_shapes=[pltpu.VMEM((tp, Q_TILE), F32),
                        pltpu.VMEM((tp, Q_TILE), COARSE),
                        pltpu.VMEM((N_HEADS, LANES, Q_TILE), BF16),
                        pltpu.VMEM((IDX_HEADS, LANES, Q_TILE), BF16),
                        pltpu.VMEM((N_HEADS, SUBLANES, Q_TILE), F32),
                        pltpu.VMEM((N_HEADS, SUBLANES, Q_TILE), F32),
                        pltpu.VMEM((N_HEADS, K_CHUNK, Q_TILE), F32),
                        pltpu.VMEM((N_HEADS, LANES + BF16_SUBLANES, Q_TILE), F32)],
        compiler_params=pltpu.CompilerParams(
            dimension_semantics=("arbitrary", "arbitrary"), vmem_limit_bytes=VMEM_LIMIT),
        name="dsa",
    )(qt, qit, wit, k, ki, vt)


def _max_all(x):
    return jnp.max(jnp.max(x, axis=1, keepdims=True), axis=0, keepdims=True)


def _router_gates(logits_t, rbias):
    r = logits_t.shape[1]
    shape3 = (N_GROUPS, GROUP_SIZE, r)
    scores = jax.nn.sigmoid(logits_t).reshape(shape3)
    biased = scores + rbias.reshape(N_GROUPS, GROUP_SIZE, 1)
    in_grp = lax.broadcasted_iota(jnp.int32, shape3, 1).astype(F32)
    m1 = jnp.max(biased, axis=1, keepdims=True)
    first = jnp.min(jnp.where(biased == m1, in_grp, float(GROUP_SIZE)), axis=1, keepdims=True)
    m2 = jnp.max(jnp.where(in_grp == first, NEG_INF, biased), axis=1, keepdims=True)
    cur = m1 + m2

    grp_idx = lax.broadcasted_iota(jnp.int32, (N_GROUPS, 1, r), 0).astype(F32)
    grp_sel = jnp.zeros((N_GROUPS, 1, r), F32)
    for _ in range(TOPK_GROUPS):
        m = jnp.max(cur, axis=0, keepdims=True)
        pick = grp_idx == jnp.min(jnp.where(cur == m, grp_idx, float(N_GROUPS)), axis=0, keepdims=True)
        grp_sel = jnp.where(pick, 1.0, grp_sel)
        cur = jnp.where(pick, NEG_INF, cur)

    cur = jnp.where(jnp.broadcast_to(grp_sel, shape3) > 0.0, biased, NEG_INF)
    exp_idx = lax.broadcasted_iota(jnp.int32, shape3, 0).astype(F32) * GROUP_SIZE + in_grp
    chosen = jnp.zeros(shape3, F32)
    for _ in range(TOP_K):
        m = _max_all(cur)
        first = -_max_all(-jnp.where(cur == m, exp_idx, float(N_EXPERTS)))
        pick = exp_idx == first
        chosen = jnp.where(pick, 1.0, chosen)
        cur = jnp.where(pick, NEG_INF, cur)

    w = jnp.where(chosen > 0.0, scores, 0.0)
    denom = jnp.sum(jnp.sum(w, axis=1, keepdims=True), axis=0, keepdims=True)
    return (w / denom * ROUTED_SCALE).reshape(N_EXPERTS, r)


def _post_kernel(conv_ref, attn_ref, hn_ref, woc_ref, woa_ref, g1_ref, b1_ref,
                 wsg_ref, wsu_ref, wsd_ref, wrh_ref, wrl_ref, rb_ref,
                 xb_ref, base_ref, gates_ref):
    mix = _dot(conv_ref[...], woc_ref[...]) + _dot(attn_ref[...], woa_ref[...])
    h1 = _layer_norm_rows(DN_ALPHA * hn_ref[...] + mix, g1_ref[...], b1_ref[...])
    xb = h1.astype(BF16)
    xb_ref[...] = xb

    shared = jax.nn.silu(_dot(xb, wsg_ref[...])) * _dot(xb, wsu_ref[...])
    base_ref[...] = DN_ALPHA * h1 + _dot(shared.astype(BF16), wsd_ref[...])

    x_lo = (h1 - xb.astype(F32)).astype(BF16)
    logits_t = (_dot_nt(wrh_ref[...], xb) + _dot_nt(wrh_ref[...], x_lo) + _dot_nt(wrl_ref[...], xb))
    gates_t = _router_gates(logits_t, rb_ref[...])
    padded = jnp.concatenate([gates_t, jnp.zeros((LANES - N_EXPERTS, gates_t.shape[1]), F32)], axis=0)
    gates_ref[...] = padded.T


def _post_call(conv, attn, hn, woc, woa, g1, b1, wsg, wsu, wsd, wrh, wrl, rbias):
    n, D = hn.shape
    tr = _pick_tile(n, (768, 512, 256))

    def row_spec(w):
        return pl.BlockSpec((tr, w), lambda i: (i, 0))

    def full(a):
        return pl.BlockSpec(a.shape, lambda i: (0,) * a.ndim)

    consts = [woc, woa, g1, b1, wsg, wsu, wsd, wrh, wrl, rbias]
    return pl.pallas_call(
        _post_kernel,
        grid=(n // tr,),
        in_specs=[row_spec(CONV_CH), row_spec(ATTN_W), row_spec(D)] + [full(a) for a in consts],
        out_specs=[row_spec(D), row_spec(D), row_spec(LANES)],
        out_shape=[jax.ShapeDtypeStruct((n, D), BF16), jax.ShapeDtypeStruct((n, D), F32),
                   jax.ShapeDtypeStruct((n, LANES), F32)],
        compiler_params=pltpu.CompilerParams(
            dimension_semantics=("arbitrary",), vmem_limit_bytes=VMEM_LIMIT),
        name="post",
    )(conv, attn, hn, *consts)


def _moe_kernel(xb_ref, base_ref, gates_ref, wg_ref, wu_ref, wd_ref, g2_ref, b2_ref, o_ref):
    s = pl.program_id(1)
    n_per_step, d_exp = wg_ref.shape[0], wg_ref.shape[2]

    @pl.when(s == 0)
    def _():
        o_ref[...] = base_ref[...]

    x = xb_ref[...]
    gates = gates_ref[...]
    lane = lax.broadcasted_iota(jnp.int32, gates.shape, 1)
    hidden = []
    for j in range(n_per_step):
        hg = _dot(x, wg_ref[j].astype(BF16))
        hu = _dot(x, wu_ref[j].astype(BF16))
        gate = jnp.sum(jnp.where(lane == s * n_per_step + j, gates, 0.0), axis=1, keepdims=True)
        hidden.append((jax.nn.silu(hg) * hu * gate).astype(BF16))
    wd = wd_ref[...].astype(BF16).reshape(n_per_step * d_exp, wd_ref.shape[2])
    o_ref[...] += _dot(jnp.concatenate(hidden, axis=1), wd)

    @pl.when(s == pl.num_programs(1) - 1)
    def _():
        o_ref[...] = _layer_norm_rows(o_ref[...], g2_ref[...], b2_ref[...])


def _moe_call(xb, base, gates, w_gate, w_up, w_down, g2, b2):
    n, D = xb.shape
    n_exp, _, d_exp = w_gate.shape
    tm = _pick_tile(n, (1536, 768, 512, 256))
    eps = MOE_EXPERTS_PER_STEP
    assert n_exp % eps == 0

    def row_spec(w, **kw):
        return pl.BlockSpec((tm, w), lambda i, s: (i, 0), **kw)

    once = dict(pipeline_mode=pl.Buffered(1))
    vec = pl.BlockSpec((1, D), lambda i, s: (0, 0))
    return pl.pallas_call(
        _moe_kernel,
        grid=(n // tm, n_exp // eps),
        in_specs=[row_spec(D, **once), row_spec(D, **once), row_spec(LANES, **once),
                  pl.BlockSpec((eps, D, d_exp), lambda i, s: (s, 0, 0)),
                  pl.BlockSpec((eps, D, d_exp), lambda i, s: (s, 0, 0)),
                  pl.BlockSpec((eps, d_exp, D), lambda i, s: (s, 0, 0)),
                  vec, vec],
        out_specs=row_spec(D),
        out_shape=jax.ShapeDtypeStruct((n, D), F32),
        compiler_params=pltpu.CompilerParams(
            dimension_semantics=("arbitrary", "arbitrary"), vmem_limit_bytes=VMEM_LIMIT),
        name="moe",
    )(xb, base, gates, w_gate, w_up, w_down, g2, b2)


def _sum_all(x):
    return jnp.sum(jnp.sum(x, axis=1, keepdims=True), axis=0, keepdims=True)


def _pack_bf16_pairs(x):
    w = x.shape[1] // 2
    lo = pltpu.bitcast(x[:, :w].astype(BF16).astype(F32), jnp.int32)
    hi = pltpu.bitcast(x[:, w:].astype(BF16).astype(F32), jnp.int32)
    return lax.shift_right_logical(lo, 16) | (hi & jnp.int32(-65536))


def _unpack_bf16_pairs(p):
    lo = pltpu.bitcast(lax.shift_left(p, 16), F32)
    hi = pltpu.bitcast(p & jnp.int32(-65536), F32)
    return jnp.concatenate([lo, hi], axis=1)


def _router_topk(logits_t, rbias):
    r = logits_t.shape[1]
    shape3 = (N_GROUPS, GROUP_SIZE, r)
    scores = jax.nn.sigmoid(logits_t).reshape(shape3)
    biased = scores + rbias.reshape(N_GROUPS, GROUP_SIZE, 1)
    in_grp = lax.broadcasted_iota(jnp.int32, shape3, 1).astype(F32)
    m1 = jnp.max(biased, axis=1, keepdims=True)
    first = jnp.min(jnp.where(biased == m1, in_grp, float(GROUP_SIZE)), axis=1, keepdims=True)
    m2 = jnp.max(jnp.where(in_grp == first, NEG_INF, biased), axis=1, keepdims=True)
    cur = m1 + m2

    grp_idx = lax.broadcasted_iota(jnp.int32, (N_GROUPS, 1, r), 0).astype(F32)
    grp_sel = jnp.zeros((N_GROUPS, 1, r), F32)
    for _ in range(TOPK_GROUPS):
        m = jnp.max(cur, axis=0, keepdims=True)
        pick = grp_idx == jnp.min(jnp.where(cur == m, grp_idx, float(N_GROUPS)), axis=0, keepdims=True)
        grp_sel = jnp.where(pick, 1.0, grp_sel)
        cur = jnp.where(pick, NEG_INF, cur)

    cur = jnp.where(jnp.broadcast_to(grp_sel, shape3) > 0.0, biased, NEG_INF)
    exp_idx = lax.broadcasted_iota(jnp.int32, shape3, 0).astype(F32) * GROUP_SIZE + in_grp
    chosen = jnp.zeros(shape3, F32)
    ids, wts = [], []
    for _ in range(TOP_K):
        m = _max_all(cur)
        first = -_max_all(-jnp.where(cur == m, exp_idx, float(N_EXPERTS)))
        pick = exp_idx == first
        chosen = jnp.where(pick, 1.0, chosen)
        cur = jnp.where(pick, NEG_INF, cur)
        ids.append(first.reshape(1, r))
        wts.append(_sum_all(jnp.where(pick, scores, 0.0)).reshape(1, r))
    ids = jnp.concatenate(ids, axis=0)
    wts = jnp.concatenate(wts, axis=0)
    gates = wts / jnp.sum(wts, axis=0, keepdims=True) * ROUTED_SCALE
    return ids, gates, chosen.reshape(N_EXPERTS, r), exp_idx


def _route_kernel(conv_ref, attn_ref, hn_ref, woc_ref, woa_ref, g1_ref, b1_ref,
                  wsg_ref, wsu_ref, wsd_ref, wrh_ref, wrl_ref, rb_ref,
                  xp_ref, base_ref, gates_ref, ek_ref, rk_ref, cnt_ref):
    step = pl.program_id(0)
    tr = hn_ref.shape[0]
    mix = _dot(conv_ref[...], woc_ref[...]) + _dot(attn_ref[...], woa_ref[...])
    h1 = _layer_norm_rows(DN_ALPHA * hn_ref[...] + mix, g1_ref[...], b1_ref[...])
    xb = h1.astype(BF16)
    xp_ref[...] = _pack_bf16_pairs(h1)

    shared = jax.nn.silu(_dot(xb, wsg_ref[...])) * _dot(xb, wsu_ref[...])
    base_ref[...] = DN_ALPHA * h1 + _dot(shared.astype(BF16), wsd_ref[...])

    x_lo = (h1 - xb.astype(F32)).astype(BF16)
    logits_t = (_dot_nt(wrh_ref[...], xb) + _dot_nt(wrh_ref[...], x_lo) + _dot_nt(wrl_ref[...], xb))
    ids, gates, chosen, exp_idx = _router_topk(logits_t, rb_ref[...])
    padded = jnp.concatenate([gates, jnp.zeros((LANES - TOP_K, tr), F32)], axis=0)
    gates_ref[...] = padded.T
    ek_ref[...] = ids.astype(jnp.int32)

    @pl.when(step == 0)
    def _():
        cnt_ref[...] = jnp.zeros(cnt_ref.shape, F32)

    t_i = lax.broadcasted_iota(jnp.int32, (tr, tr), 0)
    t_j = lax.broadcasted_iota(jnp.int32, (tr, tr), 1)
    before = jnp.where(t_i < t_j, 1.0, 0.0).astype(BF16)
    chosen_b = chosen.astype(BF16)
    running = cnt_ref[...]
    rank = _dot(chosen_b, before) + jnp.concatenate([running] * (tr // LANES), axis=1)
    rank3 = rank.reshape(N_GROUPS, GROUP_SIZE, tr)
    rk = [_sum_all(jnp.where(exp_idx == ids[k:k + 1].reshape(1, 1, tr), rank3, 0.0)).reshape(1, tr)
          for k in range(TOP_K)]
    rk_ref[...] = jnp.concatenate(rk, axis=0).astype(jnp.int32)
    cnt_ref[...] = running + _dot(chosen_b, jnp.ones((tr, LANES), BF16))


def _route_call(conv, attn, hn, woc, woa, g1, b1, wsg, wsu, wsd, wrh, wrl, rbias):
    n, D = hn.shape
    tr = _pick_tile(n, (768, 512, 256))

    def row_spec(w):
        return pl.BlockSpec((tr, w), lambda i: (i, 0))

    def col_spec(r):
        return pl.BlockSpec((r, tr), lambda i: (0, i))

    def full(a):
        return pl.BlockSpec(a.shape, lambda i: (0,) * a.ndim)

    consts = [woc, woa, g1, b1, wsg, wsu, wsd, wrh, wrl, rbias]
    return pl.pallas_call(
        _route_kernel,
        grid=(n // tr,),
        in_specs=[row_spec(CONV_CH), row_spec(ATTN_W), row_spec(D)] + [full(a) for a in consts],
        out_specs=[row_spec(D // 2), row_spec(D), row_spec(LANES), col_spec(TOP_K), col_spec(TOP_K),
                   pl.BlockSpec((N_EXPERTS, LANES), lambda i: (0, 0))],
        out_shape=[jax.ShapeDtypeStruct((n, D // 2), jnp.int32),
                   jax.ShapeDtypeStruct((n, D), F32),
                   jax.ShapeDtypeStruct((n, LANES), F32),
                   jax.ShapeDtypeStruct((TOP_K, n), jnp.int32),
                   jax.ShapeDtypeStruct((TOP_K, n), jnp.int32),
                   jax.ShapeDtypeStruct((N_EXPERTS, LANES), F32)],
        compiler_params=pltpu.CompilerParams(
            dimension_semantics=("arbitrary",), vmem_limit_bytes=VMEM_LIMIT),
        name="route",
    )(conv, attn, hn, *consts)


def _plan_kernel(cnt_ref, ek_ref, rk_ref, slot_ref, blk_ref):
    tr = ek_ref.shape[1]
    counts = cnt_ref[...]
    padded = jnp.ceil(counts / EXPERT_BLOCK) * EXPERT_BLOCK
    starts = [jnp.zeros((1, LANES), F32)]
    for e in range(1, N_EXPERTS):
        starts.append(starts[-1] + padded[e - 1:e])
    ek = ek_ref[...]
    seg = jnp.zeros(ek.shape, F32)
    for e in range(N_EXPERTS):
        seg = jnp.where(ek == e, jnp.concatenate([starts[e]] * (tr // LANES), axis=1), seg)
    slot_ref[...] = seg.astype(jnp.int32) + rk_ref[...]

    ends = jnp.concatenate(starts, axis=0) + padded
    w = blk_ref.shape[1]
    blk_start = lax.broadcasted_iota(jnp.int32, (N_EXPERTS, w), 1).astype(F32) * EXPERT_BLOCK
    done = jnp.where(jnp.concatenate([ends] * (w // LANES), axis=1) <= blk_start, 1.0, 0.0)
    owner = jnp.minimum(jnp.sum(done, axis=0, keepdims=True), N_EXPERTS - 1.0)
    blk_ref[...] = jnp.broadcast_to(owner, blk_ref.shape).astype(jnp.int32)


def _plan_call(cnt, ek, rk, n_blocks):
    k, n = ek.shape
    tr = _pick_tile(n, (768, 512, 256))
    w = -(-n_blocks // LANES) * LANES
    col = pl.BlockSpec((k, tr), lambda i: (0, i))
    return pl.pallas_call(
        _plan_kernel,
        grid=(n // tr,),
        in_specs=[pl.BlockSpec(cnt.shape, lambda i: (0, 0)), col, col],
        out_specs=[col, pl.BlockSpec((SUBLANES, w), lambda i: (0, 0))],
        out_shape=[jax.ShapeDtypeStruct((k, n), jnp.int32), jax.ShapeDtypeStruct((SUBLANES, w), jnp.int32)],
        compiler_params=pltpu.CompilerParams(dimension_semantics=("arbitrary",)),
        name="plan",
    )(cnt, ek, rk)


def _sc_workers():
    info = plsc.get_sparse_core_info()
    return info.num_cores, info.num_subcores


def _sc_scatter_rows(src, slots, n_out):
    n_src, d = src.shape
    units = slots.shape[0]
    n_chunks = n_src // SC_ROWS
    n_cores, n_sub = _sc_workers()
    per_worker = units // (n_cores * n_sub)
    assert per_worker * n_cores * n_sub == units and slots.shape[1] == SC_ROWS

    def body(src_hbm, slot_hbm, out_hbm, idx_v, rows_v):
        wid = lax.axis_index("s") * n_cores + lax.axis_index("c")

        @pl.loop(0, per_worker)
        def _(j):
            u = wid * per_worker + j
            pltpu.sync_copy(slot_hbm.at[pl.ds(u, 1)], idx_v)
            pltpu.sync_copy(src_hbm.at[pl.ds((u % n_chunks) * SC_ROWS, SC_ROWS)], rows_v)
            pltpu.sync_copy(rows_v, out_hbm.at[idx_v.at[0]])

    return pl.kernel(
        body, out_type=jax.ShapeDtypeStruct((n_out, d), src.dtype),
        mesh=plsc.VectorSubcoreMesh(core_axis_name="c", subcore_axis_name="s"),
        scratch_types=[pltpu.VMEM((1, SC_ROWS), jnp.int32), pltpu.VMEM((SC_ROWS, d), src.dtype)],
        name="dispatch_rows",
    )(src, slots)


def _sc_gather_rows(table, slots):
    units = slots.shape[0]
    d = table.shape[1]
    n_cores, n_sub = _sc_workers()
    per_worker = units // (n_cores * n_sub)
    assert per_worker * n_cores * n_sub == units and slots.shape[1] == SC_ROWS

    def body(table_hbm, slot_hbm, out_hbm, idx_v, rows_v):
        wid = lax.axis_index("s") * n_cores + lax.axis_index("c")

        @pl.loop(0, per_worker)
        def _(j):
            u = wid * per_worker + j
            pltpu.sync_copy(slot_hbm.at[pl.ds(u, 1)], idx_v)
            pltpu.sync_copy(table_hbm.at[idx_v.at[0]], rows_v)
            pltpu.sync_copy(rows_v, out_hbm.at[pl.ds(u * SC_ROWS, SC_ROWS)])

    return pl.kernel(
        body, out_type=jax.ShapeDtypeStruct((units * SC_ROWS, d), table.dtype),
        mesh=plsc.VectorSubcoreMesh(core_axis_name="c", subcore_axis_name="s"),
        scratch_types=[pltpu.VMEM((1, SC_ROWS), jnp.int32), pltpu.VMEM((SC_ROWS, d), table.dtype)],
        name="collect_rows",
    )(table, slots)


def _expert_kernel(blk_ref, xs_ref, wg_ref, wu_ref, wd_ref, ys_ref):
    del blk_ref
    x = _unpack_bf16_pairs(xs_ref[...]).astype(BF16)
    hg = _dot(x, wg_ref[0].astype(BF16))
    hu = _dot(x, wu_ref[0].astype(BF16))
    hdn = (jax.nn.silu(hg) * hu).astype(BF16)
    ys_ref[...] = _pack_bf16_pairs(_dot(hdn, wd_ref[0].astype(BF16)))


def _expert_call(blk_exp, xs, w_gate, w_up, w_down):
    p, half = xs.shape
    _, D, d_exp = w_gate.shape
    rows = pl.BlockSpec((EXPERT_BLOCK, half), lambda b, blk: (b, 0))
    grid_spec = pltpu.PrefetchScalarGridSpec(
        num_scalar_prefetch=1,
        grid=(p // EXPERT_BLOCK,),
        in_specs=[rows,
                  pl.BlockSpec((1, D, d_exp), lambda b, blk: (blk[b], 0, 0)),
                  pl.BlockSpec((1, D, d_exp), lambda b, blk: (blk[b], 0, 0)),
                  pl.BlockSpec((1, d_exp, D), lambda b, blk: (blk[b], 0, 0))],
        out_specs=rows,
    )
    return pl.pallas_call(
        _expert_kernel,
        grid_spec=grid_spec,
        out_shape=jax.ShapeDtypeStruct((p, half), jnp.int32),
        compiler_params=pltpu.CompilerParams(
            dimension_semantics=("arbitrary",), vmem_limit_bytes=VMEM_LIMIT),
        name="experts",
    )(blk_exp, xs, w_gate, w_up, w_down)


def _combine_kernel(g_ref, gates_ref, base_ref, g2_ref, b2_ref, o_ref):
    gates = gates_ref[...]
    acc = base_ref[...]
    for k in range(g_ref.shape[0]):
        acc = acc + _unpack_bf16_pairs(g_ref[k]) * gates[:, k:k + 1]
    o_ref[...] = _layer_norm_rows(acc, g2_ref[...], b2_ref[...])


def _combine_call(g, gates, base, g2, b2):
    k, n, half = g.shape
    D = base.shape[1]
    tr = _pick_tile(n, (768, 512, 256))

    def row_spec(w):
        return pl.BlockSpec((tr, w), lambda i: (i, 0))

    vec = pl.BlockSpec((1, D), lambda i: (0, 0))
    return pl.pallas_call(
        _combine_kernel,
        grid=(n // tr,),
        in_specs=[pl.BlockSpec((k, tr, half), lambda i: (0, i, 0)), row_spec(LANES), row_spec(D), vec, vec],
        out_specs=row_spec(D),
        out_shape=jax.ShapeDtypeStruct((n, D), F32),
        compiler_params=pltpu.CompilerParams(
            dimension_semantics=("arbitrary",), vmem_limit_bytes=VMEM_LIMIT),
        name="combine",
    )(g, gates, base, g2, b2)


def _rope_tables(tp):
    pos = jnp.arange(tp, dtype=F32)
    inv = jnp.power(ROPE_THETA, -2.0 * jnp.arange(ROPE_HALF, dtype=F32) / ROPE_DIM)
    ang = pos[:, None] * inv[None, :]
    cos, sin = jnp.cos(ang), jnp.sin(ang)
    zeros = jnp.zeros((tp, HEAD_DIM - ROPE_DIM), F32)
    zh = jnp.zeros((tp, ROPE_HALF), F32)
    c64 = jnp.concatenate([cos, cos, jnp.ones_like(zeros)], axis=1)
    s1_64 = jnp.concatenate([-sin, zh, zeros], axis=1)
    s2_64 = jnp.concatenate([zh, sin, zeros], axis=1)
    rep = LANES // HEAD_DIM
    return (jnp.tile(c64, (1, rep)), jnp.tile(s1_64, (1, rep)), jnp.tile(s2_64, (1, rep)),
            cos.T, sin.T)


def kernel(x, meta_tokens, ln_emb_g, ln_emb_b, w_in, conv_w, conv_b, ln_conv_g, ln_conv_b, ln_kidx_g, ln_kidx_b, w_out, ln1_g, ln1_b, w_router, router_bias, w_gate, w_up, w_down, ws_gate, ws_up, ws_down, ln2_g, ln2_b):
    B, seq, D = x.shape
    assert w_in.shape[0] == DEPTH
    k_sel = min(INDEX_TOPK, seq // 4)
    t_real = N_META + seq
    tp = -(-t_real // SEQ_ALIGN) * SEQ_ALIGN

    meta = jnp.broadcast_to(meta_tokens[None].astype(x.dtype), (B, N_META, D))
    hp = jnp.concatenate([meta, x, jnp.zeros((B, tp - t_real, D), x.dtype)], axis=1)

    def row(a):
        return a.reshape(1, -1).astype(F32)

    w = w_in[0]
    o = 0
    parts = []
    for width in (CONV_CH, CONV_CH, ATTN_W, ATTN_W, ATTN_W, IDX_HEADS * IDX_DIM, IDX_DIM, IDX_HEADS):
        parts.append(w[:, o:o + width])
        o += width
    wa, wgl, wq, wk, wv, wqi, wki, wwi = parts
    wwi_t = jnp.concatenate([wwi.T, jnp.zeros((BF16_SUBLANES - IDX_HEADS, D), w.dtype)], axis=0)
    weights = (jnp.concatenate([wa, wgl], axis=1).astype(BF16), wq.T.astype(BF16), wk.astype(BF16),
               wv.T.astype(BF16), wqi.T.astype(BF16), jnp.concatenate([wki, wki], axis=1).astype(BF16),
               wwi_t.astype(BF16))

    def twice(a):
        return row(jnp.concatenate([a, a]))

    hn, conv, qt, k, vt, qit, ki, wit = _inproj_call(
        hp, _rope_tables(tp), row(ln_emb_g), row(ln_emb_b), weights,
        conv_w[0].astype(F32), row(conv_b[0]), row(ln_conv_g[0]), row(ln_conv_b[0]),
        twice(ln_kidx_g[0]), twice(ln_kidx_b[0]))

    attn = _dsa_call(qt, qit, wit, k, ki, vt, k_sel)

    n = B * tp
    wr_t = w_router[0].T.astype(F32)
    wr_hi = wr_t.astype(BF16)
    wr_lo = (wr_t - wr_hi.astype(F32)).astype(BF16)
    xp, base, gates, ek, rk, cnt = _route_call(
        conv.reshape(n, CONV_CH), attn.reshape(n, ATTN_W), hn.reshape(n, D),
        w_out[0][:CONV_CH].astype(BF16), w_out[0][CONV_CH:].astype(BF16), row(ln1_g[0]), row(ln1_b[0]),
        ws_gate[0].astype(BF16), ws_up[0].astype(BF16), ws_down[0].astype(BF16),
        wr_hi, wr_lo, router_bias[0].reshape(-1, 1).astype(F32))

    n_blocks = n * TOP_K // EXPERT_BLOCK + N_EXPERTS
    slot, blk = _plan_call(cnt, ek, rk, n_blocks)
    slots = slot.reshape(TOP_K * n // SC_ROWS, SC_ROWS)
    xs = _sc_scatter_rows(xp, slots, n_blocks * EXPERT_BLOCK)
    ys = _expert_call(blk[0, :n_blocks], xs, w_gate[0], w_up[0], w_down[0])
    picked = _sc_gather_rows(ys, slots).reshape(TOP_K, n, D // 2)
    out = _combine_call(picked, gates, base, row(ln2_g[0]), row(ln2_b[0]))
    return out.reshape(B, tp, D)[:, N_META:t_real]
```

```python
import functools

import numpy as np
import jax
import jax.numpy as jnp
from jax import lax
from jax.experimental import pallas as pl
from jax.experimental.pallas import tpu as pltpu
from jax.experimental.pallas import tpu_sc as plsc

N_META = 16
CONV_CH = 512
CONV_WIDTH = 31
N_HEADS = 8
HEAD_DIM = 64
ATTN_W = N_HEADS * HEAD_DIM
IDX_HEADS = 8
IDX_DIM = 64
INDEX_TOPK = 256
ROPE_DIM = HEAD_DIM // 4
ROPE_HALF = ROPE_DIM // 2
ROPE_THETA = 500000.0
N_EXPERTS = 64
TOP_K = 8
N_GROUPS = 8
GROUP_SIZE = N_EXPERTS // N_GROUPS
TOPK_GROUPS = 4
ROUTED_SCALE = 2.5
LN_EPS = 1e-5
DEPTH = 1
DN_ALPHA = (2.0 * DEPTH) ** 0.25

LANES = 128
Q_TILE = 256
SUBLANES = 8
BF16_SUBLANES = 16
MOE_EXPERTS_PER_STEP = 2
EXPERT_BLOCK = 512
SC_ROWS = 128
K_CHUNK = 256
K_SUB = 128
SEQ_ALIGN = 256
CONV_HALO = 32
VMEM_LIMIT = 56 * 1024 * 1024

F32 = jnp.float32
BF16 = jnp.bfloat16
COARSE = jnp.bfloat16
NEG_INF = float("-inf")
INT_MIN = -2 ** 31
KEY_NEG_INF = -2139095041
LOG2_E = 1.4426950408889634


def _dot(a, b):
    return jnp.dot(a, b, preferred_element_type=F32)


def _dot_nt(a, b):
    return lax.dot_general(a, b, (((1,), (1,)), ((), ())), preferred_element_type=F32)


def _layer_norm_rows(x, g, b):
    mu = jnp.mean(x, axis=-1, keepdims=True)
    xc = x - mu
    var = jnp.mean(xc * xc, axis=-1, keepdims=True)
    return xc * lax.rsqrt(var + LN_EPS) * g + b


def _pick_tile(n, candidates):
    for c in candidates:
        if n % c == 0:
            return c
    raise ValueError(f"no tile for {n}")


def _rope_rows(x, c_tab, s1_tab, s2_tab):
    outs = []
    for j in range(x.shape[1] // LANES):
        xs = x[:, j * LANES:(j + 1) * LANES]
        up = pltpu.roll(xs, LANES - ROPE_HALF, axis=1)
        dn = pltpu.roll(xs, ROPE_HALF, axis=1)
        outs.append(xs * c_tab + up * s1_tab + dn * s2_tab)
    return jnp.concatenate(outs, axis=1)


def _rope_cols(xt, cos_t, sin_t, heads):
    r = xt.shape[1]
    x3 = xt.reshape(heads, HEAD_DIM, r)
    x1 = x3[:, 0:ROPE_HALF, :]
    x2 = x3[:, ROPE_HALF:ROPE_DIM, :]
    n1 = x1 * cos_t - x2 * sin_t
    n2 = x2 * cos_t + x1 * sin_t
    out = jnp.concatenate([n1, n2, x3[:, ROPE_DIM:, :]], axis=1)
    return out.reshape(heads * HEAD_DIM, r)


def _inproj_kernel(h_ref, ctab_ref, s1tab_ref, s2tab_ref, cost_ref, sint_ref,
                   lng_ref, lnb_ref, wag_ref, wqt_ref, wk_ref, wvt_ref, wqit_ref, wki_ref, wwit_ref,
                   cw_ref, cb_ref, lncg_ref, lncb_ref, lnkg_ref, lnkb_ref,
                   hn_ref, conv_ref, qt_ref, k_ref, vt_ref, qit_ref, ki_ref, wit_ref,
                   ubuf_ref):
    t = pl.program_id(1)
    tr = h_ref.shape[1]

    hn = _layer_norm_rows(h_ref[0], lng_ref[...], lnb_ref[...])
    hn_ref[0] = hn
    xb = hn.astype(BF16)

    ag = _dot(xb, wag_ref[...])
    u = ag[:, :CONV_CH] * jax.nn.sigmoid(ag[:, CONV_CH:])

    @pl.when(t == 0)
    def _():
        ubuf_ref[0:CONV_HALO, :] = jnp.zeros((CONV_HALO, CONV_CH), F32)

    ubuf_ref[CONV_HALO:CONV_HALO + tr, :] = u
    base = CONV_HALO - (CONV_WIDTH - 1)
    acc = jnp.zeros((tr, CONV_CH), F32)
    for j in range(CONV_WIDTH):
        acc = acc + cw_ref[j:j + 1, :] * ubuf_ref[base + j:base + j + tr, :]
    ubuf_ref[0:CONV_HALO, :] = ubuf_ref[tr:tr + CONV_HALO, :]
    c = _layer_norm_rows(acc + cb_ref[...], lncg_ref[...], lncb_ref[...])
    conv_ref[0] = (c * jax.nn.sigmoid(c)).astype(conv_ref.dtype)

    ctab, s1tab, s2tab = ctab_ref[...], s1tab_ref[...], s2tab_ref[...]
    cos_t, sin_t = cost_ref[...], sint_ref[...]

    qt = _rope_cols(_dot_nt(wqt_ref[...], xb), cos_t, sin_t, N_HEADS)
    qt_ref[0] = (qt * (HEAD_DIM ** -0.5 * LOG2_E)).astype(qt_ref.dtype)
    k = _rope_rows(_dot(xb, wk_ref[...]), ctab, s1tab, s2tab)
    k_ref[0] = k.astype(k_ref.dtype)
    vt_ref[0] = _dot_nt(wvt_ref[...], xb).astype(vt_ref.dtype)

    qit = _rope_cols(_dot_nt(wqit_ref[...], xb), cos_t, sin_t, IDX_HEADS)
    qit_ref[0] = qit.astype(qit_ref.dtype)
    ki = _layer_norm_rows(_dot(xb, wki_ref[...]), lnkg_ref[...], lnkb_ref[...])
    ki_ref[0] = _rope_rows(ki, ctab, s1tab, s2tab).astype(ki_ref.dtype)
    wit = _dot_nt(wwit_ref[...], xb) * (IDX_HEADS ** -0.5)
    wit_ref[0] = wit[:IDX_HEADS]


def _inproj_call(hp, tabs, ln_g, ln_b, weights, conv_w, conv_b, lnc_g, lnc_b, lnk_g, lnk_b):
    B, tp, D = hp.shape
    tr = _pick_tile(tp, (768, 512, 256))
    nt = tp // tr
    ctab, s1tab, s2tab, cos_t, sin_t = tabs
    def row_spec(w):
        return pl.BlockSpec((1, tr, w), lambda b, t: (b, t, 0))

    def col_spec(r):
        return pl.BlockSpec((1, r, tr), lambda b, t: (b, 0, t))

    def full(a):
        return pl.BlockSpec(a.shape, lambda b, t: (0,) * a.ndim)

    tab_row = pl.BlockSpec((tr, LANES), lambda b, t: (t, 0))
    tab_col = pl.BlockSpec((ROPE_HALF, tr), lambda b, t: (0, t))
    consts = [ln_g, ln_b, *weights, conv_w, conv_b, lnc_g, lnc_b, lnk_g, lnk_b]
    out_shape = [
        jax.ShapeDtypeStruct((B, tp, D), F32),
        jax.ShapeDtypeStruct((B, tp, CONV_CH), BF16),
        jax.ShapeDtypeStruct((B, ATTN_W, tp), BF16),
        jax.ShapeDtypeStruct((B, tp, ATTN_W), BF16),
        jax.ShapeDtypeStruct((B, ATTN_W, tp), BF16),
        jax.ShapeDtypeStruct((B, IDX_HEADS * IDX_DIM, tp), BF16),
        jax.ShapeDtypeStruct((B, tp, 2 * IDX_DIM), BF16),
        jax.ShapeDtypeStruct((B, IDX_HEADS, tp), F32),
    ]
    out_specs = [row_spec(D), row_spec(CONV_CH), col_spec(ATTN_W), row_spec(ATTN_W), col_spec(ATTN_W),
                 col_spec(IDX_HEADS * IDX_DIM), row_spec(2 * IDX_DIM), col_spec(IDX_HEADS)]
    return pl.pallas_call(
        _inproj_kernel,
        grid=(B, nt),
        in_specs=[row_spec(D), tab_row, tab_row, tab_row, tab_col, tab_col] + [full(a) for a in consts],
        out_specs=out_specs,
        out_shape=out_shape,
        scratch_shapes=[pltpu.VMEM((CONV_HALO + tr, CONV_CH), F32)],
        compiler_params=pltpu.CompilerParams(
            dimension_semantics=("arbitrary", "arbitrary"), vmem_limit_bytes=VMEM_LIMIT),
        name="inproj",
    )(hp, ctab, s1tab, s2tab, cos_t, sin_t, *consts)


def _key_to_float(key):
    bits = jnp.where(key >= 0, key, key ^ jnp.int32(0x7FFFFFFF))
    f = pltpu.bitcast(bits, F32)
    return jnp.where(key < jnp.int32(KEY_NEG_INF), NEG_INF, f)


def _tree(parts, op):
    parts = list(parts)
    while len(parts) > 1:
        nxt = [op(parts[j], parts[j + 1]) for j in range(0, len(parts) - 1, 2)]
        if len(parts) % 2:
            nxt.append(parts[-1])
        parts = nxt
    return parts[0]


def _fold_rows(x, rows, op):
    return _tree([x[j * rows:(j + 1) * rows] for j in range(x.shape[0] // rows)], op)


def _dsa_kernel(qt_ref, qit_ref, wit_ref, k_ref, ki_ref, vt_ref, o_ref,
                sc_ref, sh_ref, qm_ref, qim_ref, m_ref, alpha_ref, lg_ref, acc_ref, *, k_sel):
    i = pl.program_id(1)
    tq = qt_ref.shape[2]
    n_chunks = (i * tq + tq + K_CHUNK - 1) // K_CHUNK
    v_rows = LANES + BF16_SUBLANES

    def causal_mask(k0, rows=K_CHUNK):
        kpos = k0 + lax.broadcasted_iota(jnp.int32, (rows, tq), 0)
        return kpos <= i * tq + lax.broadcasted_iota(jnp.int32, (rows, tq), 1)

    def rows8(x):
        return jnp.broadcast_to(x, (SUBLANES, tq))

    def tiles(x):
        return x.reshape(x.shape[0] // SUBLANES, SUBLANES, tq)

    def head_slab(ref, h):
        slab = ref[0, (h // 2) * LANES:(h // 2 + 1) * LANES, :]
        zeros = jnp.zeros((HEAD_DIM, tq), slab.dtype)
        if h % 2 == 0:
            return jnp.concatenate([slab[:HEAD_DIM], zeros], axis=0)
        return jnp.concatenate([zeros, slab[HEAD_DIM:]], axis=0)

    for h in range(N_HEADS):
        qm_ref[h] = head_slab(qt_ref, h)
    for h in range(IDX_HEADS):
        qim_ref[h] = head_slab(qit_ref, h)
    wit = wit_ref[0]
    w_heads = [rows8(wit[h:h + 1]) for h in range(IDX_HEADS)]

    def score_body(c, carry):
        for s in range(K_CHUNK // K_SUB):
            k0 = pl.multiple_of(c * K_CHUNK + s * K_SUB, K_SUB)
            kic = ki_ref[0, pl.ds(k0, K_SUB), :]
            acc = jnp.zeros((K_SUB // SUBLANES, SUBLANES, tq), F32)
            for h in range(IDX_HEADS):
                acc = acc + w_heads[h][None] * jnp.maximum(tiles(_dot(kic, qim_ref[h])), 0.0)
            acc = (acc * (IDX_DIM ** -0.5)).reshape(K_SUB, tq)
            acc = jnp.where(causal_mask(k0, K_SUB), acc, NEG_INF)
            sc_ref[pl.ds(k0, K_SUB), :] = acc
            hi_bits = pltpu.bitcast(acc, jnp.int32) & jnp.int32(-65536)
            sh_ref[pl.ds(k0, K_SUB), :] = pltpu.bitcast(hi_bits, F32).astype(COARSE)
        return carry

    lax.fori_loop(0, n_chunks, score_body, 0)

    def count_all(ref, thr_tile, preds):
        rows = thr_tile.shape[0]
        one, zero = jnp.ones((), ref.dtype), jnp.zeros((), ref.dtype)

        def body(c, cnts):
            k0 = pl.multiple_of(c * K_CHUNK, K_CHUNK)
            s = ref[pl.ds(k0, K_CHUNK), :]
            out = []
            for cnt, p in zip(cnts, preds):
                hits = [jnp.where(p(s[j * rows:(j + 1) * rows], thr_tile), one, zero)
                        for j in range(K_CHUNK // rows)]
                out.append(cnt + _tree(hits, jnp.add).astype(F32))
            return tuple(out)

        init = tuple(jnp.zeros((rows, tq), F32) for _ in preds)
        cnts = lax.fori_loop(0, n_chunks, body, init)
        return [rows8(jnp.sum(cnt, axis=0, keepdims=True)) for cnt in cnts]

    def search_body(it, tkey, coarse):
        cand = tkey + lax.shift_left(jnp.int32(1), 31 - it)
        cf = _key_to_float(cand)
        if coarse:
            cf = pltpu.bitcast(pltpu.bitcast(cf, jnp.int32) & jnp.int32(-65536), F32).astype(COARSE)
            cf = jnp.concatenate([cf] * (BF16_SUBLANES // SUBLANES), axis=0)
        cnt, = count_all(sh_ref if coarse else sc_ref, cf, [lambda s, t: s >= t])
        return jnp.where(cnt >= k_sel, cand, tkey)

    tkey = jnp.full((SUBLANES, tq), INT_MIN, jnp.int32)
    tkey = lax.fori_loop(0, 16, functools.partial(search_body, coarse=True), tkey)
    tkey = lax.fori_loop(16, 32, functools.partial(search_body, coarse=False), tkey)
    thr = _key_to_float(tkey)

    n_ge, n_gt = count_all(sc_ref, thr, [lambda s, t: s >= t, lambda s, t: s > t])
    need = k_sel - n_gt
    has_ties = jnp.max(n_ge) > k_sel

    @pl.when(jnp.logical_not(has_ties))
    def _():
        def body(c, carry):
            k0 = pl.multiple_of(c * K_CHUNK, K_CHUNK)
            s = sc_ref[pl.ds(k0, K_CHUNK), :]
            bias = jnp.where(tiles(s) >= thr[None], 0.0, NEG_INF).reshape(K_CHUNK, tq)
            sc_ref[pl.ds(k0, K_CHUNK), :] = jnp.where(causal_mask(k0), bias, NEG_INF)
            return carry
        lax.fori_loop(0, n_chunks, body, 0)

    @pl.when(has_ties)
    def _():
        r_i = lax.broadcasted_iota(jnp.int32, (K_CHUNK, K_CHUNK), 0)
        c_i = lax.broadcasted_iota(jnp.int32, (K_CHUNK, K_CHUNK), 1)
        lower = jnp.where(c_i <= r_i, 1.0, 0.0).astype(BF16)
        thr_row, need_row = thr[0:1], need[0:1]

        def body(c, seen):
            k0 = pl.multiple_of(c * K_CHUNK, K_CHUNK)
            s = sc_ref[pl.ds(k0, K_CHUNK), :]
            eq = jnp.where(s == thr_row, 1.0, 0.0)
            rank = _dot(lower, eq.astype(BF16)) + seen
            keep_tie = jnp.where(rank <= need_row, eq, 0.0)
            sel = jnp.where(s > thr_row, 1.0, keep_tie)
            sc_ref[pl.ds(k0, K_CHUNK), :] = jnp.where(
                sel > 0.0, jnp.where(causal_mask(k0), 0.0, NEG_INF), NEG_INF)
            return rank[K_CHUNK - 1:K_CHUNK, :]
        lax.fori_loop(0, n_chunks, body, jnp.zeros((1, tq), F32))

    m_ref[...] = jnp.full(m_ref.shape, NEG_INF, F32)
    acc_ref[...] = jnp.zeros(acc_ref.shape, F32)
    ones_rows = jnp.ones((BF16_SUBLANES, K_CHUNK), BF16)

    def attn_body(c, carry):
        k0 = pl.multiple_of(c * K_CHUNK, K_CHUNK)
        for h in range(N_HEADS):
            pair = slice((h // 2) * LANES, (h // 2 + 1) * LANES)
            cmax = []
            for s in range(K_CHUNK // K_SUB):
                rows = pl.ds(pl.multiple_of(k0 + s * K_SUB, K_SUB), K_SUB)
                lg = tiles(_dot(k_ref[0, rows, pair], qm_ref[h])) + tiles(sc_ref[rows, :])
                lg_ref[h, s * K_SUB:(s + 1) * K_SUB, :] = lg.reshape(K_SUB, tq)
                cmax.append(jnp.max(lg, axis=0))
            m_old = m_ref[h]
            m_new = jnp.maximum(m_old, rows8(jnp.max(_tree(cmax, jnp.maximum), axis=0, keepdims=True)))
            m_safe = jnp.where(m_new == NEG_INF, 0.0, m_new)
            alpha_ref[h] = jnp.exp2(m_old - m_safe)
            m_ref[h] = m_new
        for h in range(N_HEADS):
            pair = slice((h // 2) * LANES, (h // 2 + 1) * LANES)
            m_new = m_ref[h]
            m_safe = jnp.where(m_new == NEG_INF, 0.0, m_new)
            p = jnp.exp2(tiles(lg_ref[h]) - m_safe[None]).reshape(K_CHUNK, tq).astype(BF16)
            v_aug = jnp.concatenate([vt_ref[0, pair, pl.ds(k0, K_CHUNK)], ones_rows], axis=0)
            pv = _dot(v_aug, p)
            acc = acc_ref[h].reshape(v_rows // SUBLANES, SUBLANES, tq) * alpha_ref[h][None]
            acc_ref[h] = acc.reshape(v_rows, tq) + pv
        return carry

    lax.fori_loop(0, n_chunks, attn_body, 0)

    for pair in range(N_HEADS // 2):
        halves = []
        for h in (2 * pair, 2 * pair + 1):
            a = acc_ref[h]
            lo = (h % 2) * HEAD_DIM
            halves.append(a[lo:lo + HEAD_DIM] / a[LANES:LANES + 1])
        out_t = jnp.concatenate(halves, axis=0)
        o_ref[0, :, pair * LANES:(pair + 1) * LANES] = out_t.T.astype(o_ref.dtype)


def _dsa_call(qt, qit, wit, k, ki, vt, k_sel):
    B, tp, _ = k.shape
    nq = tp // Q_TILE

    def q_cols(r):
        return pl.BlockSpec((1, r, Q_TILE), lambda b, i: (b, 0, i))

    def per_batch(a):
        return pl.BlockSpec((1,) + a.shape[1:], lambda b, i: (b, 0, 0), pipeline_mode=pl.Buffered(1))

    return pl.pallas_call(
        functools.partial(_dsa_kernel, k_sel=k_sel),
        grid=(B, nq),
        in_specs=[q_cols(ATTN_W), q_cols(IDX_HEADS * IDX_DIM), q_cols(IDX_HEADS),
                  per_batch(k), per_batch(ki), per_batch(vt)],
        out_specs=pl.BlockSpec((1, Q_TILE, ATTN_W), lambda b, i: (b, i, 0)),
        out_shape=jax.ShapeDtypeStruct((B, tp, ATTN_W), BF16),
        scratch_shapes=[pltpu.VMEM((tp, Q_TILE), F32),
                        pltpu.VMEM((tp, Q_TILE), COARSE),
                        pltpu.VMEM((N_HEADS, LANES, Q_TILE), BF16),
                        pltpu.VMEM((IDX_HEADS, LANES, Q_TILE), BF16),
                        pltpu.VMEM((N_HEADS, SUBLANES, Q_TILE), F32),
                        pltpu.VMEM((N_HEADS, SUBLANES, Q_TILE), F32),
                        pltpu.VMEM((N_HEADS, K_CHUNK, Q_TILE), F32),
                        pltpu.VMEM((N_HEADS, LANES + BF16_SUBLANES, Q_TILE), F32)],
        compiler_params=pltpu.CompilerParams(
            dimension_semantics=("arbitrary", "arbitrary"), vmem_limit_bytes=VMEM_LIMIT),
        name="dsa",
    )(qt, qit, wit, k, ki, vt)


def _max_all(x):
    return jnp.max(jnp.max(x, axis=1, keepdims=True), axis=0, keepdims=True)


def _router_gates(logits_t, rbias):
    r = logits_t.shape[1]
    shape3 = (N_GROUPS, GROUP_SIZE, r)
    scores = jax.nn.sigmoid(logits_t).reshape(shape3)
    biased = scores + rbias.reshape(N_GROUPS, GROUP_SIZE, 1)
    in_grp = lax.broadcasted_iota(jnp.int32, shape3, 1).astype(F32)
    m1 = jnp.max(biased, axis=1, keepdims=True)
    first = jnp.min(jnp.where(biased == m1, in_grp, float(GROUP_SIZE)), axis=1, keepdims=True)
    m2 = jnp.max(jnp.where(in_grp == first, NEG_INF, biased), axis=1, keepdims=True)
    cur = m1 + m2

    grp_idx = lax.broadcasted_iota(jnp.int32, (N_GROUPS, 1, r), 0).astype(F32)
    grp_sel = jnp.zeros((N_GROUPS, 1, r), F32)
    for _ in range(TOPK_GROUPS):
        m = jnp.max(cur, axis=0, keepdims=True)
        pick = grp_idx == jnp.min(jnp.where(cur == m, grp_idx, float(N_GROUPS)), axis=0, keepdims=True)
        grp_sel = jnp.where(pick, 1.0, grp_sel)
        cur = jnp.where(pick, NEG_INF, cur)

    cur = jnp.where(jnp.broadcast_to(grp_sel, shape3) > 0.0, biased, NEG_INF)
    exp_idx = lax.broadcasted_iota(jnp.int32, shape3, 0).astype(F32) * GROUP_SIZE + in_grp
    chosen = jnp.zeros(shape3, F32)
    for _ in range(TOP_K):
        m = _max_all(cur)
        first = -_max_all(-jnp.where(cur == m, exp_idx, float(N_EXPERTS)))
        pick = exp_idx == first
        chosen = jnp.where(pick, 1.0, chosen)
        cur = jnp.where(pick, NEG_INF, cur)

    w = jnp.where(chosen > 0.0, scores, 0.0)
    denom = jnp.sum(jnp.sum(w, axis=1, keepdims=True), axis=0, keepdims=True)
    return (w / denom * ROUTED_SCALE).reshape(N_EXPERTS, r)


def _post_kernel(conv_ref, attn_ref, hn_ref, woc_ref, woa_ref, g1_ref, b1_ref,
                 wsg_ref, wsu_ref, wsd_ref, wrh_ref, wrl_ref, rb_ref,
                 xb_ref, base_ref, gates_ref):
    mix = _dot(conv_ref[...], woc_ref[...]) + _dot(attn_ref[...], woa_ref[...])
    h1 = _layer_norm_rows(DN_ALPHA * hn_ref[...] + mix, g1_ref[...], b1_ref[...])
    xb = h1.astype(BF16)
    xb_ref[...] = xb

    shared = jax.nn.silu(_dot(xb, wsg_ref[...])) * _dot(xb, wsu_ref[...])
    base_ref[...] = DN_ALPHA * h1 + _dot(shared.astype(BF16), wsd_ref[...])

    x_lo = (h1 - xb.astype(F32)).astype(BF16)
    logits_t = (_dot_nt(wrh_ref[...], xb) + _dot_nt(wrh_ref[...], x_lo) + _dot_nt(wrl_ref[...], xb))
    gates_t = _router_gates(logits_t, rb_ref[...])
    padded = jnp.concatenate([gates_t, jnp.zeros((LANES - N_EXPERTS, gates_t.shape[1]), F32)], axis=0)
    gates_ref[...] = padded.T


def _post_call(conv, attn, hn, woc, woa, g1, b1, wsg, wsu, wsd, wrh, wrl, rbias):
    n, D = hn.shape
    tr = _pick_tile(n, (768, 512, 256))

    def row_spec(w):
        return pl.BlockSpec((tr, w), lambda i: (i, 0))

    def full(a):
        return pl.BlockSpec(a.shape, lambda i: (0,) * a.ndim)

    consts = [woc, woa, g1, b1, wsg, wsu, wsd, wrh, wrl, rbias]
    return pl.pallas_call(
        _post_kernel,
        grid=(n // tr,),
        in_specs=[row_spec(CONV_CH), row_spec(ATTN_W), row_spec(D)] + [full(a) for a in consts],
        out_specs=[row_spec(D), row_spec(D), row_spec(LANES)],
        out_shape=[jax.ShapeDtypeStruct((n, D), BF16), jax.ShapeDtypeStruct((n, D), F32),
                   jax.ShapeDtypeStruct((n, LANES), F32)],
        compiler_params=pltpu.CompilerParams(
            dimension_semantics=("arbitrary",), vmem_limit_bytes=VMEM_LIMIT),
        name="post",
    )(conv, attn, hn, *consts)


def _moe_kernel(xb_ref, base_ref, gates_ref, wg_ref, wu_ref, wd_ref, g2_ref, b2_ref, o_ref):
    s = pl.program_id(1)
    n_per_step, d_exp = wg_ref.shape[0], wg_ref.shape[2]

    @pl.when(s == 0)
    def _():
        o_ref[...] = base_ref[...]

    x = xb_ref[...]
    gates = gates_ref[...]
    lane = lax.broadcasted_iota(jnp.int32, gates.shape, 1)
    hidden = []
    for j in range(n_per_step):
        hg = _dot(x, wg_ref[j].astype(BF16))
        hu = _dot(x, wu_ref[j].astype(BF16))
        gate = jnp.sum(jnp.where(lane == s * n_per_step + j, gates, 0.0), axis=1, keepdims=True)
        hidden.append((jax.nn.silu(hg) * hu * gate).astype(BF16))
    wd = wd_ref[...].astype(BF16).reshape(n_per_step * d_exp, wd_ref.shape[2])
    o_ref[...] += _dot(jnp.concatenate(hidden, axis=1), wd)

    @pl.when(s == pl.num_programs(1) - 1)
    def _():
        o_ref[...] = _layer_norm_rows(o_ref[...], g2_ref[...], b2_ref[...])


def _moe_call(xb, base, gates, w_gate, w_up, w_down, g2, b2):
    n, D = xb.shape
    n_exp, _, d_exp = w_gate.shape
    tm = _pick_tile(n, (1536, 768, 512, 256))
    eps = MOE_EXPERTS_PER_STEP
    assert n_exp % eps == 0

    def row_spec(w, **kw):
        return pl.BlockSpec((tm, w), lambda i, s: (i, 0), **kw)

    once = dict(pipeline_mode=pl.Buffered(1))
    vec = pl.BlockSpec((1, D), lambda i, s: (0, 0))
    return pl.pallas_call(
        _moe_kernel,
        grid=(n // tm, n_exp // eps),
        in_specs=[row_spec(D, **once), row_spec(D, **once), row_spec(LANES, **once),
                  pl.BlockSpec((eps, D, d_exp), lambda i, s: (s, 0, 0)),
                  pl.BlockSpec((eps, D, d_exp), lambda i, s: (s, 0, 0)),
                  pl.BlockSpec((eps, d_exp, D), lambda i, s: (s, 0, 0)),
                  vec, vec],
        out_specs=row_spec(D),
        out_shape=jax.ShapeDtypeStruct((n, D), F32),
        compiler_params=pltpu.CompilerParams(
            dimension_semantics=("arbitrary", "arbitrary"), vmem_limit_bytes=VMEM_LIMIT),
        name="moe",
    )(xb, base, gates, w_gate, w_up, w_down, g2, b2)


def _sum_all(x):
    return jnp.sum(jnp.sum(x, axis=1, keepdims=True), axis=0, keepdims=True)


def _pack_bf16_pairs(x):
    w = x.shape[1] // 2
    lo = pltpu.bitcast(x[:, :w].astype(BF16).astype(F32), jnp.int32)
    hi = pltpu.bitcast(x[:, w:].astype(BF16).astype(F32), jnp.int32)
    return lax.shift_right_logical(lo, 16) | (hi & jnp.int32(-65536))


def _unpack_bf16_pairs(p):
    lo = pltpu.bitcast(lax.shift_left(p, 16), F32)
    hi = pltpu.bitcast(p & jnp.int32(-65536), F32)
    return jnp.concatenate([lo, hi], axis=1)


def _router_topk(logits_t, rbias):
    r = logits_t.shape[1]
    shape3 = (N_GROUPS, GROUP_SIZE, r)
    scores = jax.nn.sigmoid(logits_t).reshape(shape3)
    biased = scores + rbias.reshape(N_GROUPS, GROUP_SIZE, 1)
    in_grp = lax.broadcasted_iota(jnp.int32, shape3, 1).astype(F32)
    m1 = jnp.max(biased, axis=1, keepdims=True)
    first = jnp.min(jnp.where(biased == m1, in_grp, float(GROUP_SIZE)), axis=1, keepdims=True)
    m2 = jnp.max(jnp.where(in_grp == first, NEG_INF, biased), axis=1, keepdims=True)
    cur = m1 + m2

    grp_idx = lax.broadcasted_iota(jnp.int32, (N_GROUPS, 1, r), 0).astype(F32)
    grp_sel = jnp.zeros((N_GROUPS, 1, r), F32)
    for _ in range(TOPK_GROUPS):
        m = jnp.max(cur, axis=0, keepdims=True)
        pick = grp_idx == jnp.min(jnp.where(cur == m, grp_idx, float(N_GROUPS)), axis=0, keepdims=True)
        grp_sel = jnp.where(pick, 1.0, grp_sel)
        cur = jnp.where(pick, NEG_INF, cur)

    cur = jnp.where(jnp.broadcast_to(grp_sel, shape3) > 0.0, biased, NEG_INF)
    exp_idx = lax.broadcasted_iota(jnp.int32, shape3, 0).astype(F32) * GROUP_SIZE + in_grp
    chosen = jnp.zeros(shape3, F32)
    ids, wts = [], []
    for _ in range(TOP_K):
        m = _max_all(cur)
        first = -_max_all(-jnp.where(cur == m, exp_idx, float(N_EXPERTS)))
        pick = exp_idx == first
        chosen = jnp.where(pick, 1.0, chosen)
        cur = jnp.where(pick, NEG_INF, cur)
        ids.append(first.reshape(1, r))
        wts.append(_sum_all(jnp.where(pick, scores, 0.0)).reshape(1, r))
    ids = jnp.concatenate(ids, axis=0)
    wts = jnp.concatenate(wts, axis=0)
    gates = wts / jnp.sum(wts, axis=0, keepdims=True) * ROUTED_SCALE
    return ids, gates, chosen.reshape(N_EXPERTS, r), exp_idx


def _route_kernel(conv_ref, attn_ref, hn_ref, woc_ref, woa_ref, g1_ref, b1_ref,
                  wsg_ref, wsu_ref, wsd_ref, wrh_ref, wrl_ref, rb_ref,
                  xp_ref, base_ref, gates_ref, ek_ref, rk_ref, cnt_ref):
    step = pl.program_id(0)
    tr = hn_ref.shape[0]
    mix = _dot(conv_ref[...], woc_ref[...]) + _dot(attn_ref[...], woa_ref[...])
    h1 = _layer_norm_rows(DN_ALPHA * hn_ref[...] + mix, g1_ref[...], b1_ref[...])
    xb = h1.astype(BF16)
    xp_ref[...] = _pack_bf16_pairs(h1)

    shared = jax.nn.silu(_dot(xb, wsg_ref[...])) * _dot(xb, wsu_ref[...])
    base_ref[...] = DN_ALPHA * h1 + _dot(shared.astype(BF16), wsd_ref[...])

    x_lo = (h1 - xb.astype(F32)).astype(BF16)
    logits_t = (_dot_nt(wrh_ref[...], xb) + _dot_nt(wrh_ref[...], x_lo) + _dot_nt(wrl_ref[...], xb))
    ids, gates, chosen, exp_idx = _router_topk(logits_t, rb_ref[...])
    padded = jnp.concatenate([gates, jnp.zeros((LANES - TOP_K, tr), F32)], axis=0)
    gates_ref[...] = padded.T
    ek_ref[...] = ids.astype(jnp.int32)

    @pl.when(step == 0)
    def _():
        cnt_ref[...] = jnp.zeros(cnt_ref.shape, F32)

    t_i = lax.broadcasted_iota(jnp.int32, (tr, tr), 0)
    t_j = lax.broadcasted_iota(jnp.int32, (tr, tr), 1)
    before = jnp.where(t_i < t_j, 1.0, 0.0).astype(BF16)
    chosen_b = chosen.astype(BF16)
    running = cnt_ref[...]
    rank = _dot(chosen_b, before) + jnp.concatenate([running] * (tr // LANES), axis=1)
    rank3 = rank.reshape(N_GROUPS, GROUP_SIZE, tr)
    rk = [_sum_all(jnp.where(exp_idx == ids[k:k + 1].reshape(1, 1, tr), rank3, 0.0)).reshape(1, tr)
          for k in range(TOP_K)]
    rk_ref[...] = jnp.concatenate(rk, axis=0).astype(jnp.int32)
    cnt_ref[...] = running + _dot(chosen_b, jnp.ones((tr, LANES), BF16))


def _route_call(conv, attn, hn, woc, woa, g1, b1, wsg, wsu, wsd, wrh, wrl, rbias):
    n, D = hn.shape
    tr = _pick_tile(n, (768, 512, 256))

    def row_spec(w):
        return pl.BlockSpec((tr, w), lambda i: (i, 0))

    def col_spec(r):
        return pl.BlockSpec((r, tr), lambda i: (0, i))

    def full(a):
        return pl.BlockSpec(a.shape, lambda i: (0,) * a.ndim)

    consts = [woc, woa, g1, b1, wsg, wsu, wsd, wrh, wrl, rbias]
    return pl.pallas_call(
        _route_kernel,
        grid=(n // tr,),
        in_specs=[row_spec(CONV_CH), row_spec(ATTN_W), row_spec(D)] + [full(a) for a in consts],
        out_specs=[row_spec(D // 2), row_spec(D), row_spec(LANES), col_spec(TOP_K), col_spec(TOP_K),
                   pl.BlockSpec((N_EXPERTS, LANES), lambda i: (0, 0))],
        out_shape=[jax.ShapeDtypeStruct((n, D // 2), jnp.int32),
                   jax.ShapeDtypeStruct((n, D), F32),
                   jax.ShapeDtypeStruct((n, LANES), F32),
                   jax.ShapeDtypeStruct((TOP_K, n), jnp.int32),
                   jax.ShapeDtypeStruct((TOP_K, n), jnp.int32),
                   jax.ShapeDtypeStruct((N_EXPERTS, LANES), F32)],
        compiler_params=pltpu.CompilerParams(
            dimension_semantics=("arbitrary",), vmem_limit_bytes=VMEM_LIMIT),
        name="route",
    )(conv, attn, hn, *consts)


def _plan_kernel(cnt_ref, ek_ref, rk_ref, slot_ref, blk_ref):
    tr = ek_ref.shape[1]
    counts = cnt_ref[...]
    padded = jnp.ceil(counts / EXPERT_BLOCK) * EXPERT_BLOCK
    starts = [jnp.zeros((1, LANES), F32)]
    for e in range(1, N_EXPERTS):
        starts.append(starts[-1] + padded[e - 1:e])
    ek = ek_ref[...]
    seg = jnp.zeros(ek.shape, F32)
    for e in range(N_EXPERTS):
        seg = jnp.where(ek == e, jnp.concatenate([starts[e]] * (tr // LANES), axis=1), seg)
    slot_ref[...] = seg.astype(jnp.int32) + rk_ref[...]

    ends = jnp.concatenate(starts, axis=0) + padded
    w = blk_ref.shape[1]
    blk_start = lax.broadcasted_iota(jnp.int32, (N_EXPERTS, w), 1).astype(F32) * EXPERT_BLOCK
    done = jnp.where(jnp.concatenate([ends] * (w // LANES), axis=1) <= blk_start, 1.0, 0.0)
    owner = jnp.minimum(jnp.sum(done, axis=0, keepdims=True), N_EXPERTS - 1.0)
    blk_ref[...] = jnp.broadcast_to(owner, blk_ref.shape).astype(jnp.int32)


def _plan_call(cnt, ek, rk, n_blocks):
    k, n = ek.shape
    tr = _pick_tile(n, (768, 512, 256))
    w = -(-n_blocks // LANES) * LANES
    col = pl.BlockSpec((k, tr), lambda i: (0, i))
    return pl.pallas_call(
        _plan_kernel,
        grid=(n // tr,),
        in_specs=[pl.BlockSpec(cnt.shape, lambda i: (0, 0)), col, col],
        out_specs=[col, pl.BlockSpec((SUBLANES, w), lambda i: (0, 0))],
        out_shape=[jax.ShapeDtypeStruct((k, n), jnp.int32), jax.ShapeDtypeStruct((SUBLANES, w), jnp.int32)],
        compiler_params=pltpu.CompilerParams(dimension_semantics=("arbitrary",)),
        name="plan",
    )(cnt, ek, rk)


def _sc_workers():
    info = plsc.get_sparse_core_info()
    return info.num_cores, info.num_subcores


def _sc_scatter_rows(src, slots, n_out):
    n_src, d = src.shape
    units = slots.shape[0]
    n_chunks = n_src // SC_ROWS
    n_cores, n_sub = _sc_workers()
    n_workers = n_cores * n_sub
    assert slots.shape[1] == SC_ROWS

    def body(src_hbm, slot_hbm, out_hbm, idx_v, rows_v):
        wid = lax.axis_index("s") * n_cores + lax.axis_index("c")

        @pl.loop(0, -(-units // n_workers))
        def _(j):
            u = j * n_workers + wid

            @pl.when(u < units)
            def _():
                pltpu.sync_copy(slot_hbm.at[pl.ds(u, 1)], idx_v)
                pltpu.sync_copy(src_hbm.at[pl.ds((u % n_chunks) * SC_ROWS, SC_ROWS)], rows_v)
                pltpu.sync_copy(rows_v, out_hbm.at[idx_v.at[0]])

    return pl.kernel(
        body, out_type=jax.ShapeDtypeStruct((n_out, d), src.dtype),
        mesh=plsc.VectorSubcoreMesh(core_axis_name="c", subcore_axis_name="s"),
        scratch_types=[pltpu.VMEM((1, SC_ROWS), jnp.int32), pltpu.VMEM((SC_ROWS, d), src.dtype)],
        name="dispatch_rows",
    )(src, slots)


def _sc_gather_rows(table, slots):
    units = slots.shape[0]
    d = table.shape[1]
    n_cores, n_sub = _sc_workers()
    n_workers = n_cores * n_sub
    assert slots.shape[1] == SC_ROWS

    def body(table_hbm, slot_hbm, out_hbm, idx_v, rows_v):
        wid = lax.axis_index("s") * n_cores + lax.axis_index("c")

        @pl.loop(0, -(-units // n_workers))
        def _(j):
            u = j * n_workers + wid

            @pl.when(u < units)
            def _():
                pltpu.sync_copy(slot_hbm.at[pl.ds(u, 1)], idx_v)
                pltpu.sync_copy(table_hbm.at[idx_v.at[0]], rows_v)
                pltpu.sync_copy(rows_v, out_hbm.at[pl.ds(u * SC_ROWS, SC_ROWS)])

    return pl.kernel(
        body, out_type=jax.ShapeDtypeStruct((units * SC_ROWS, d), table.dtype),
        mesh=plsc.VectorSubcoreMesh(core_axis_name="c", subcore_axis_name="s"),
        scratch_types=[pltpu.VMEM((1, SC_ROWS), jnp.int32), pltpu.VMEM((SC_ROWS, d), table.dtype)],
        name="collect_rows",
    )(table, slots)


def _expert_kernel(blk_ref, xs_ref, wg_ref, wu_ref, wd_ref, ys_ref):
    del blk_ref
    x = _unpack_bf16_pairs(xs_ref[...]).astype(BF16)
    hg = _dot(x, wg_ref[0].astype(BF16))
    hu = _dot(x, wu_ref[0].astype(BF16))
    hdn = (jax.nn.silu(hg) * hu).astype(BF16)
    ys_ref[...] = _pack_bf16_pairs(_dot(hdn, wd_ref[0].astype(BF16)))


def _expert_call(blk_exp, xs, w_gate, w_up, w_down):
    p, half = xs.shape
    _, D, d_exp = w_gate.shape
    rows = pl.BlockSpec((EXPERT_BLOCK, half), lambda b, blk: (b, 0))
    grid_spec = pltpu.PrefetchScalarGridSpec(
        num_scalar_prefetch=1,
        grid=(p // EXPERT_BLOCK,),
        in_specs=[rows,
                  pl.BlockSpec((1, D, d_exp), lambda b, blk: (blk[b], 0, 0)),
                  pl.BlockSpec((1, D, d_exp), lambda b, blk: (blk[b], 0, 0)),
                  pl.BlockSpec((1, d_exp, D), lambda b, blk: (blk[b], 0, 0))],
        out_specs=rows,
    )
    return pl.pallas_call(
        _expert_kernel,
        grid_spec=grid_spec,
        out_shape=jax.ShapeDtypeStruct((p, half), jnp.int32),
        compiler_params=pltpu.CompilerParams(
            dimension_semantics=("arbitrary",), vmem_limit_bytes=VMEM_LIMIT),
        name="experts",
    )(blk_exp, xs, w_gate, w_up, w_down)


def _combine_kernel(g_ref, gates_ref, base_ref, g2_ref, b2_ref, o_ref):
    gates = gates_ref[...]
    acc = base_ref[...]
    for k in range(g_ref.shape[0]):
        acc = acc + _unpack_bf16_pairs(g_ref[k]) * gates[:, k:k + 1]
    o_ref[...] = _layer_norm_rows(acc, g2_ref[...], b2_ref[...])


def _combine_call(g, gates, base, g2, b2):
    k, n, half = g.shape
    D = base.shape[1]
    tr = _pick_tile(n, (768, 512, 256))

    def row_spec(w):
        return pl.BlockSpec((tr, w), lambda i: (i, 0))

    vec = pl.BlockSpec((1, D), lambda i: (0, 0))
    return pl.pallas_call(
        _combine_kernel,
        grid=(n // tr,),
        in_specs=[pl.BlockSpec((k, tr, half), lambda i: (0, i, 0)), row_spec(LANES), row_spec(D), vec, vec],
        out_specs=row_spec(D),
        out_shape=jax.ShapeDtypeStruct((n, D), F32),
        compiler_params=pltpu.CompilerParams(
            dimension_semantics=("arbitrary",), vmem_limit_bytes=VMEM_LIMIT),
        name="combine",
    )(g, gates, base, g2, b2)


def _rope_tables(tp):
    pos = jnp.arange(tp, dtype=F32)
    inv = jnp.power(ROPE_THETA, -2.0 * jnp.arange(ROPE_HALF, dtype=F32) / ROPE_DIM)
    ang = pos[:, None] * inv[None, :]
    cos, sin = jnp.cos(ang), jnp.sin(ang)
    zeros = jnp.zeros((tp, HEAD_DIM - ROPE_DIM), F32)
    zh = jnp.zeros((tp, ROPE_HALF), F32)
    c64 = jnp.concatenate([cos, cos, jnp.ones_like(zeros)], axis=1)
    s1_64 = jnp.concatenate([-sin, zh, zeros], axis=1)
    s2_64 = jnp.concatenate([zh, sin, zeros], axis=1)
    rep = LANES // HEAD_DIM
    return (jnp.tile(c64, (1, rep)), jnp.tile(s1_64, (1, rep)), jnp.tile(s2_64, (1, rep)),
            cos.T, sin.T)


def kernel(x, meta_tokens, ln_emb_g, ln_emb_b, w_in, conv_w, conv_b, ln_conv_g, ln_conv_b, ln_kidx_g, ln_kidx_b, w_out, ln1_g, ln1_b, w_router, router_bias, w_gate, w_up, w_down, ws_gate, ws_up, ws_down, ln2_g, ln2_b):
    B, seq, D = x.shape
    assert w_in.shape[0] == DEPTH
    k_sel = min(INDEX_TOPK, seq // 4)
    t_real = N_META + seq
    tp = -(-t_real // SEQ_ALIGN) * SEQ_ALIGN

    def row(a):
        return a.reshape(1, -1).astype(F32)

    w = w_in[0]
    o = 0
    parts = []
    for width in (CONV_CH, CONV_CH, ATTN_W, ATTN_W, ATTN_W, IDX_HEADS * IDX_DIM, IDX_DIM, IDX_HEADS):
        parts.append(w[:, o:o + width])
        o += width
    wa, wgl, wq, wk, wv, wqi, wki, wwi = parts
    wwi_t = jnp.concatenate([wwi.T, jnp.zeros((BF16_SUBLANES - IDX_HEADS, D), w.dtype)], axis=0)
    weights = (jnp.concatenate([wa, wgl], axis=1).astype(BF16), wq.T.astype(BF16), wk.astype(BF16),
               wv.T.astype(BF16), wqi.T.astype(BF16), jnp.concatenate([wki, wki], axis=1).astype(BF16),
               wwi_t.astype(BF16))

    def twice(a):
        return row(jnp.concatenate([a, a]))

    tabs = _rope_tables(tp)
    inproj_consts = (row(ln_emb_g), row(ln_emb_b), weights, conv_w[0].astype(F32), row(conv_b[0]),
                     row(ln_conv_g[0]), row(ln_conv_b[0]), twice(ln_kidx_g[0]), twice(ln_kidx_b[0]))
    wr_t = w_router[0].T.astype(F32)
    wr_hi = wr_t.astype(BF16)
    wr_lo = (wr_t - wr_hi.astype(F32)).astype(BF16)
    route_consts = (w_out[0][:CONV_CH].astype(BF16), w_out[0][CONV_CH:].astype(BF16), row(ln1_g[0]),
                    row(ln1_b[0]), ws_gate[0].astype(BF16), ws_up[0].astype(BF16), ws_down[0].astype(BF16),
                    wr_hi, wr_lo, router_bias[0].reshape(-1, 1).astype(F32))
    meta = meta_tokens.astype(x.dtype)
    tail = jnp.zeros((tp - t_real, D), x.dtype)
    n_blocks = tp * TOP_K // EXPERT_BLOCK + N_EXPERTS

    outs = []
    for b in range(B):
        hp = jnp.concatenate([meta, x[b], tail], axis=0)[None]
        hn, conv, qt, k, vt, qit, ki, wit = _inproj_call(hp, tabs, *inproj_consts)
        attn = _dsa_call(qt, qit, wit, k, ki, vt, k_sel)
        xp, base, gates, ek, rk, cnt = _route_call(conv[0], attn[0], hn[0], *route_consts)

        slot, blk = _plan_call(cnt, ek, rk, n_blocks)
        slots = slot.reshape(TOP_K * tp // SC_ROWS, SC_ROWS)
        xs = _sc_scatter_rows(xp, slots, n_blocks * EXPERT_BLOCK)
        ys = _expert_call(blk[0, :n_blocks], xs, w_gate[0], w_up[0], w_down[0])
        picked = _sc_gather_rows(ys, slots).reshape(TOP_K, tp, D // 2)
        out = _combine_call(picked, gates, base, row(ln2_g[0]), row(ln2_b[0]))
        outs.append(out[N_META:t_real])
    return jnp.stack(outs)
```

```python
import functools

import numpy as np
import jax
import jax.numpy as jnp
from jax import lax
from jax.experimental import pallas as pl
from jax.experimental.pallas import tpu as pltpu
from jax.experimental.pallas import tpu_sc as plsc

N_META = 16
CONV_CH = 512
CONV_WIDTH = 31
N_HEADS = 8
HEAD_DIM = 64
ATTN_W = N_HEADS * HEAD_DIM
IDX_HEADS = 8
IDX_DIM = 64
INDEX_TOPK = 256
ROPE_DIM = HEAD_DIM // 4
ROPE_HALF = ROPE_DIM // 2
ROPE_THETA = 500000.0
N_EXPERTS = 64
TOP_K = 8
N_GROUPS = 8
GROUP_SIZE = N_EXPERTS // N_GROUPS
TOPK_GROUPS = 4
ROUTED_SCALE = 2.5
LN_EPS = 1e-5
DEPTH = 1
DN_ALPHA = (2.0 * DEPTH) ** 0.25

LANES = 128
Q_TILE = 256
SUBLANES = 8
BF16_SUBLANES = 16
MOE_EXPERTS_PER_STEP = 2
EXPERT_BLOCK = 512
SC_ROWS = 128
K_CHUNK = 256
K_SUB = 128
SEQ_ALIGN = 256
CONV_HALO = 32
VMEM_LIMIT = 56 * 1024 * 1024

F32 = jnp.float32
BF16 = jnp.bfloat16
COARSE = jnp.bfloat16
NEG_INF = float("-inf")
INT_MIN = -2 ** 31
KEY_NEG_INF = -2139095041
LOG2_E = 1.4426950408889634


def _dot(a, b):
    return jnp.dot(a, b, preferred_element_type=F32)


def _dot_nt(a, b):
    return lax.dot_general(a, b, (((1,), (1,)), ((), ())), preferred_element_type=F32)


def _layer_norm_rows(x, g, b):
    mu = jnp.mean(x, axis=-1, keepdims=True)
    xc = x - mu
    var = jnp.mean(xc * xc, axis=-1, keepdims=True)
    return xc * lax.rsqrt(var + LN_EPS) * g + b


def _pick_tile(n, candidates):
    for c in candidates:
        if n % c == 0:
            return c
    raise ValueError(f"no tile for {n}")


def _rope_rows(x, c_tab, s1_tab, s2_tab):
    outs = []
    for j in range(x.shape[1] // LANES):
        xs = x[:, j * LANES:(j + 1) * LANES]
        up = pltpu.roll(xs, LANES - ROPE_HALF, axis=1)
        dn = pltpu.roll(xs, ROPE_HALF, axis=1)
        outs.append(xs * c_tab + up * s1_tab + dn * s2_tab)
    return jnp.concatenate(outs, axis=1)


def _rope_cols(xt, cos_t, sin_t, heads):
    r = xt.shape[1]
    x3 = xt.reshape(heads, HEAD_DIM, r)
    x1 = x3[:, 0:ROPE_HALF, :]
    x2 = x3[:, ROPE_HALF:ROPE_DIM, :]
    n1 = x1 * cos_t - x2 * sin_t
    n2 = x2 * cos_t + x1 * sin_t
    out = jnp.concatenate([n1, n2, x3[:, ROPE_DIM:, :]], axis=1)
    return out.reshape(heads * HEAD_DIM, r)


def _inproj_kernel(h_ref, ctab_ref, s1tab_ref, s2tab_ref, cost_ref, sint_ref,
                   lng_ref, lnb_ref, wag_ref, wqt_ref, wk_ref, wvt_ref, wqit_ref, wki_ref, wwit_ref,
                   cw_ref, cb_ref, lncg_ref, lncb_ref, lnkg_ref, lnkb_ref,
                   hn_ref, conv_ref, qt_ref, k_ref, vt_ref, qit_ref, ki_ref, wit_ref,
                   ubuf_ref):
    t = pl.program_id(1)
    tr = h_ref.shape[1]

    hn = _layer_norm_rows(h_ref[0], lng_ref[...], lnb_ref[...])
    hn_ref[0] = hn
    xb = hn.astype(BF16)

    ag = _dot(xb, wag_ref[...])
    u = ag[:, :CONV_CH] * jax.nn.sigmoid(ag[:, CONV_CH:])

    @pl.when(t == 0)
    def _():
        ubuf_ref[0:CONV_HALO, :] = jnp.zeros((CONV_HALO, CONV_CH), F32)

    ubuf_ref[CONV_HALO:CONV_HALO + tr, :] = u
    base = CONV_HALO - (CONV_WIDTH - 1)
    acc = jnp.zeros((tr, CONV_CH), F32)
    for j in range(CONV_WIDTH):
        acc = acc + cw_ref[j:j + 1, :] * ubuf_ref[base + j:base + j + tr, :]
    ubuf_ref[0:CONV_HALO, :] = ubuf_ref[tr:tr + CONV_HALO, :]
    c = _layer_norm_rows(acc + cb_ref[...], lncg_ref[...], lncb_ref[...])
    conv_ref[0] = (c * jax.nn.sigmoid(c)).astype(conv_ref.dtype)

    ctab, s1tab, s2tab = ctab_ref[...], s1tab_ref[...], s2tab_ref[...]
    cos_t, sin_t = cost_ref[...], sint_ref[...]

    qt = _rope_cols(_dot_nt(wqt_ref[...], xb), cos_t, sin_t, N_HEADS)
    qt_ref[0] = (qt * (HEAD_DIM ** -0.5 * LOG2_E)).astype(qt_ref.dtype)
    k = _rope_rows(_dot(xb, wk_ref[...]), ctab, s1tab, s2tab)
    k_ref[0] = k.astype(k_ref.dtype)
    vt_ref[0] = _dot_nt(wvt_ref[...], xb).astype(vt_ref.dtype)

    qit = _rope_cols(_dot_nt(wqit_ref[...], xb), cos_t, sin_t, IDX_HEADS)
    qit_ref[0] = qit.astype(qit_ref.dtype)
    ki = _layer_norm_rows(_dot(xb, wki_ref[...]), lnkg_ref[...], lnkb_ref[...])
    ki_ref[0] = _rope_rows(ki, ctab, s1tab, s2tab).astype(ki_ref.dtype)
    wit = _dot_nt(wwit_ref[...], xb) * (IDX_HEADS ** -0.5)
    wit_ref[0] = wit[:IDX_HEADS]


def _inproj_call(hp, tabs, ln_g, ln_b, weights, conv_w, conv_b, lnc_g, lnc_b, lnk_g, lnk_b):
    B, tp, D = hp.shape
    tr = _pick_tile(tp, (768, 512, 256))
    nt = tp // tr
    ctab, s1tab, s2tab, cos_t, sin_t = tabs
    def row_spec(w):
        return pl.BlockSpec((1, tr, w), lambda b, t: (b, t, 0))

    def col_spec(r):
        return pl.BlockSpec((1, r, tr), lambda b, t: (b, 0, t))

    def full(a):
        return pl.BlockSpec(a.shape, lambda b, t: (0,) * a.ndim)

    tab_row = pl.BlockSpec((tr, LANES), lambda b, t: (t, 0))
    tab_col = pl.BlockSpec((ROPE_HALF, tr), lambda b, t: (0, t))
    consts = [ln_g, ln_b, *weights, conv_w, conv_b, lnc_g, lnc_b, lnk_g, lnk_b]
    out_shape = [
        jax.ShapeDtypeStruct((B, tp, D), F32),
        jax.ShapeDtypeStruct((B, tp, CONV_CH), BF16),
        jax.ShapeDtypeStruct((B, ATTN_W, tp), BF16),
        jax.ShapeDtypeStruct((B, tp, ATTN_W), BF16),
        jax.ShapeDtypeStruct((B, ATTN_W, tp), BF16),
        jax.ShapeDtypeStruct((B, IDX_HEADS * IDX_DIM, tp), BF16),
        jax.ShapeDtypeStruct((B, tp, 2 * IDX_DIM), BF16),
        jax.ShapeDtypeStruct((B, IDX_HEADS, tp), F32),
    ]
    out_specs = [row_spec(D), row_spec(CONV_CH), col_spec(ATTN_W), row_spec(ATTN_W), col_spec(ATTN_W),
                 col_spec(IDX_HEADS * IDX_DIM), row_spec(2 * IDX_DIM), col_spec(IDX_HEADS)]
    return pl.pallas_call(
        _inproj_kernel,
        grid=(B, nt),
        in_specs=[row_spec(D), tab_row, tab_row, tab_row, tab_col, tab_col] + [full(a) for a in consts],
        out_specs=out_specs,
        out_shape=out_shape,
        scratch_shapes=[pltpu.VMEM((CONV_HALO + tr, CONV_CH), F32)],
        compiler_params=pltpu.CompilerParams(
            dimension_semantics=("arbitrary", "arbitrary"), vmem_limit_bytes=VMEM_LIMIT),
        name="inproj",
    )(hp, ctab, s1tab, s2tab, cos_t, sin_t, *consts)


def _key_to_float(key):
    bits = jnp.where(key >= 0, key, key ^ jnp.int32(0x7FFFFFFF))
    f = pltpu.bitcast(bits, F32)
    return jnp.where(key < jnp.int32(KEY_NEG_INF), NEG_INF, f)


def _tree(parts, op):
    parts = list(parts)
    while len(parts) > 1:
        nxt = [op(parts[j], parts[j + 1]) for j in range(0, len(parts) - 1, 2)]
        if len(parts) % 2:
            nxt.append(parts[-1])
        parts = nxt
    return parts[0]


def _fold_rows(x, rows, op):
    return _tree([x[j * rows:(j + 1) * rows] for j in range(x.shape[0] // rows)], op)


def _dsa_kernel(qt_ref, qit_ref, wit_ref, k_ref, ki_ref, vt_ref, o_ref,
                sc_ref, sh_ref, qm_ref, qim_ref, m_ref, alpha_ref, lg_ref, acc_ref, *, k_sel):
    i = pl.program_id(1)
    tq = qt_ref.shape[2]
    n_chunks = (i * tq + tq + K_CHUNK - 1) // K_CHUNK
    v_rows = LANES + BF16_SUBLANES

    def causal_mask(k0, rows=K_CHUNK):
        kpos = k0 + lax.broadcasted_iota(jnp.int32, (rows, tq), 0)
        return kpos <= i * tq + lax.broadcasted_iota(jnp.int32, (rows, tq), 1)

    def rows8(x):
        return jnp.broadcast_to(x, (SUBLANES, tq))

    def tiles(x):
        return x.reshape(x.shape[0] // SUBLANES, SUBLANES, tq)

    def head_slab(ref, h):
        slab = ref[0, (h // 2) * LANES:(h // 2 + 1) * LANES, :]
        zeros = jnp.zeros((HEAD_DIM, tq), slab.dtype)
        if h % 2 == 0:
            return jnp.concatenate([slab[:HEAD_DIM], zeros], axis=0)
        return jnp.concatenate([zeros, slab[HEAD_DIM:]], axis=0)

    for h in range(N_HEADS):
        qm_ref[h] = head_slab(qt_ref, h)
    for h in range(IDX_HEADS):
        qim_ref[h] = head_slab(qit_ref, h)
    wit = wit_ref[0]
    w_heads = [rows8(wit[h:h + 1]) for h in range(IDX_HEADS)]

    def score_body(c, carry):
        for s in range(K_CHUNK // K_SUB):
            k0 = pl.multiple_of(c * K_CHUNK + s * K_SUB, K_SUB)
            kic = ki_ref[0, pl.ds(k0, K_SUB), :]
            acc = jnp.zeros((K_SUB // SUBLANES, SUBLANES, tq), F32)
            for h in range(IDX_HEADS):
                acc = acc + w_heads[h][None] * jnp.maximum(tiles(_dot(kic, qim_ref[h])), 0.0)
            acc = (acc * (IDX_DIM ** -0.5)).reshape(K_SUB, tq)
            acc = jnp.where(causal_mask(k0, K_SUB), acc, NEG_INF)
            sc_ref[pl.ds(k0, K_SUB), :] = acc
            hi_bits = pltpu.bitcast(acc, jnp.int32) & jnp.int32(-65536)
            sh_ref[pl.ds(k0, K_SUB), :] = pltpu.bitcast(hi_bits, F32).astype(COARSE)
        return carry

    lax.fori_loop(0, n_chunks, score_body, 0)

    def count_all(ref, thr_tile, preds):
        rows = thr_tile.shape[0]
        one, zero = jnp.ones((), ref.dtype), jnp.zeros((), ref.dtype)

        def body(c, cnts):
            k0 = pl.multiple_of(c * K_CHUNK, K_CHUNK)
            s = ref[pl.ds(k0, K_CHUNK), :]
            out = []
            for cnt, p in zip(cnts, preds):
                hits = [jnp.where(p(s[j * rows:(j + 1) * rows], thr_tile), one, zero)
                        for j in range(K_CHUNK // rows)]
                out.append(cnt + _tree(hits, jnp.add).astype(F32))
            return tuple(out)

        init = tuple(jnp.zeros((rows, tq), F32) for _ in preds)
        cnts = lax.fori_loop(0, n_chunks, body, init)
        return [rows8(jnp.sum(cnt, axis=0, keepdims=True)) for cnt in cnts]

    def search_body(it, tkey, coarse):
        cand = tkey + lax.shift_left(jnp.int32(1), 31 - it)
        cf = _key_to_float(cand)
        if coarse:
            cf = pltpu.bitcast(pltpu.bitcast(cf, jnp.int32) & jnp.int32(-65536), F32).astype(COARSE)
            cf = jnp.concatenate([cf] * (BF16_SUBLANES // SUBLANES), axis=0)
        cnt, = count_all(sh_ref if coarse else sc_ref, cf, [lambda s, t: s >= t])
        return jnp.where(cnt >= k_sel, cand, tkey)

    tkey = jnp.full((SUBLANES, tq), INT_MIN, jnp.int32)
    tkey = lax.fori_loop(0, 16, functools.partial(search_body, coarse=True), tkey)
    tkey = lax.fori_loop(16, 32, functools.partial(search_body, coarse=False), tkey)
    thr = _key_to_float(tkey)

    n_ge, n_gt = count_all(sc_ref, thr, [lambda s, t: s >= t, lambda s, t: s > t])
    need = k_sel - n_gt
    has_ties = jnp.max(n_ge) > k_sel

    @pl.when(jnp.logical_not(has_ties))
    def _():
        def body(c, carry):
            k0 = pl.multiple_of(c * K_CHUNK, K_CHUNK)
            s = sc_ref[pl.ds(k0, K_CHUNK), :]
            bias = jnp.where(tiles(s) >= thr[None], 0.0, NEG_INF).reshape(K_CHUNK, tq)
            sc_ref[pl.ds(k0, K_CHUNK), :] = jnp.where(causal_mask(k0), bias, NEG_INF)
            return carry
        lax.fori_loop(0, n_chunks, body, 0)

    @pl.when(has_ties)
    def _():
        r_i = lax.broadcasted_iota(jnp.int32, (K_CHUNK, K_CHUNK), 0)
        c_i = lax.broadcasted_iota(jnp.int32, (K_CHUNK, K_CHUNK), 1)
        lower = jnp.where(c_i <= r_i, 1.0, 0.0).astype(BF16)
        thr_row, need_row = thr[0:1], need[0:1]

        def body(c, seen):
            k0 = pl.multiple_of(c * K_CHUNK, K_CHUNK)
            s = sc_ref[pl.ds(k0, K_CHUNK), :]
            eq = jnp.where(s == thr_row, 1.0, 0.0)
            rank = _dot(lower, eq.astype(BF16)) + seen
            keep_tie = jnp.where(rank <= need_row, eq, 0.0)
            sel = jnp.where(s > thr_row, 1.0, keep_tie)
            sc_ref[pl.ds(k0, K_CHUNK), :] = jnp.where(
                sel > 0.0, jnp.where(causal_mask(k0), 0.0, NEG_INF), NEG_INF)
            return rank[K_CHUNK - 1:K_CHUNK, :]
        lax.fori_loop(0, n_chunks, body, jnp.zeros((1, tq), F32))

    m_ref[...] = jnp.full(m_ref.shape, NEG_INF, F32)
    acc_ref[...] = jnp.zeros(acc_ref.shape, F32)
    ones_rows = jnp.ones((BF16_SUBLANES, K_CHUNK), BF16)

    def attn_body(c, carry):
        k0 = pl.multiple_of(c * K_CHUNK, K_CHUNK)
        for h in range(N_HEADS):
            pair = slice((h // 2) * LANES, (h // 2 + 1) * LANES)
            cmax = []
            for s in range(K_CHUNK // K_SUB):
                rows = pl.ds(pl.multiple_of(k0 + s * K_SUB, K_SUB), K_SUB)
                lg = tiles(_dot(k_ref[0, rows, pair], qm_ref[h])) + tiles(sc_ref[rows, :])
                lg_ref[h, s * K_SUB:(s + 1) * K_SUB, :] = lg.reshape(K_SUB, tq)
                cmax.append(jnp.max(lg, axis=0))
            m_old = m_ref[h]
            m_new = jnp.maximum(m_old, rows8(jnp.max(_tree(cmax, jnp.maximum), axis=0, keepdims=True)))
            m_safe = jnp.where(m_new == NEG_INF, 0.0, m_new)
            alpha_ref[h] = jnp.exp2(m_old - m_safe)
            m_ref[h] = m_new
        for h in range(N_HEADS):
            pair = slice((h // 2) * LANES, (h // 2 + 1) * LANES)
            m_new = m_ref[h]
            m_safe = jnp.where(m_new == NEG_INF, 0.0, m_new)
            p = jnp.exp2(tiles(lg_ref[h]) - m_safe[None]).reshape(K_CHUNK, tq).astype(BF16)
            v_aug = jnp.concatenate([vt_ref[0, pair, pl.ds(k0, K_CHUNK)], ones_rows], axis=0)
            pv = _dot(v_aug, p)
            acc = acc_ref[h].reshape(v_rows // SUBLANES, SUBLANES, tq) * alpha_ref[h][None]
            acc_ref[h] = acc.reshape(v_rows, tq) + pv
        return carry

    lax.fori_loop(0, n_chunks, attn_body, 0)

    for pair in range(N_HEADS // 2):
        halves = []
        for h in (2 * pair, 2 * pair + 1):
            a = acc_ref[h]
            lo = (h % 2) * HEAD_DIM
            halves.append(a[lo:lo + HEAD_DIM] / a[LANES:LANES + 1])
        out_t = jnp.concatenate(halves, axis=0)
        o_ref[0, :, pair * LANES:(pair + 1) * LANES] = out_t.T.astype(o_ref.dtype)


def _dsa_call(qt, qit, wit, k, ki, vt, k_sel):
    B, tp, _ = k.shape
    nq = tp // Q_TILE

    def q_cols(r):
        return pl.BlockSpec((1, r, Q_TILE), lambda b, i: (b, 0, i))

    def per_batch(a):
        return pl.BlockSpec((1,) + a.shape[1:], lambda b, i: (b, 0, 0), pipeline_mode=pl.Buffered(1))

    return pl.pallas_call(
        functools.partial(_dsa_kernel, k_sel=k_sel),
        grid=(B, nq),
        in_specs=[q_cols(ATTN_W), q_cols(IDX_HEADS * IDX_DIM), q_cols(IDX_HEADS),
                  per_batch(k), per_batch(ki), per_batch(vt)],
        out_specs=pl.BlockSpec((1, Q_TILE, ATTN_W), lambda b, i: (b, i, 0)),
        out_shape=jax.ShapeDtypeStruct((B, tp, ATTN_W), BF16),
        scratch_shapes=[pltpu.VMEM((tp, Q_TILE), F32),
                        pltpu.VMEM((tp, Q_TILE), COARSE),
                        pltpu.VMEM((N_HEADS, LANES, Q_TILE), BF16),
                        pltpu.VMEM((IDX_HEADS, LANES, Q_TILE), BF16),
                        pltpu.VMEM((N_HEADS, SUBLANES, Q_TILE), F32),
                        pltpu.VMEM((N_HEADS, SUBLANES, Q_TILE), F32),
                        pltpu.VMEM((N_HEADS, K_CHUNK, Q_TILE), F32),
                        pltpu.VMEM((N_HEADS, LANES + BF16_SUBLANES, Q_TILE), F32)],
        compiler_params=pltpu.CompilerParams(
            dimension_semantics=("arbitrary", "arbitrary"), vmem_limit_bytes=VMEM_LIMIT),
        name="dsa",
    )(qt, qit, wit, k, ki, vt)


def _max_all(x):
    return jnp.max(jnp.max(x, axis=1, keepdims=True), axis=0, keepdims=True)


def _router_gates(logits_t, rbias):
    r = logits_t.shape[1]
    shape3 = (N_GROUPS, GROUP_SIZE, r)
    scores = jax.nn.sigmoid(logits_t).reshape(shape3)
    biased = scores + rbias.reshape(N_GROUPS, GROUP_SIZE, 1)
    in_grp = lax.broadcasted_iota(jnp.int32, shape3, 1).astype(F32)
    m1 = jnp.max(biased, axis=1, keepdims=True)
    first = jnp.min(jnp.where(biased == m1, in_grp, float(GROUP_SIZE)), axis=1, keepdims=True)
    m2 = jnp.max(jnp.where(in_grp == first, NEG_INF, biased), axis=1, keepdims=True)
    cur = m1 + m2

    grp_idx = lax.broadcasted_iota(jnp.int32, (N_GROUPS, 1, r), 0).astype(F32)
    grp_sel = jnp.zeros((N_GROUPS, 1, r), F32)
    for _ in range(TOPK_GROUPS):
        m = jnp.max(cur, axis=0, keepdims=True)
        pick = grp_idx == jnp.min(jnp.where(cur == m, grp_idx, float(N_GROUPS)), axis=0, keepdims=True)
        grp_sel = jnp.where(pick, 1.0, grp_sel)
        cur = jnp.where(pick, NEG_INF, cur)

    cur = jnp.where(jnp.broadcast_to(grp_sel, shape3) > 0.0, biased, NEG_INF)
    exp_idx = lax.broadcasted_iota(jnp.int32, shape3, 0).astype(F32) * GROUP_SIZE + in_grp
    chosen = jnp.zeros(shape3, F32)
    for _ in range(TOP_K):
        m = _max_all(cur)
        first = -_max_all(-jnp.where(cur == m, exp_idx, float(N_EXPERTS)))
        pick = exp_idx == first
        chosen = jnp.where(pick, 1.0, chosen)
        cur = jnp.where(pick, NEG_INF, cur)

    w = jnp.where(chosen > 0.0, scores, 0.0)
    denom = jnp.sum(jnp.sum(w, axis=1, keepdims=True), axis=0, keepdims=True)
    return (w / denom * ROUTED_SCALE).reshape(N_EXPERTS, r)


def _post_kernel(conv_ref, attn_ref, hn_ref, woc_ref, woa_ref, g1_ref, b1_ref,
                 wsg_ref, wsu_ref, wsd_ref, wrh_ref, wrl_ref, rb_ref,
                 xb_ref, base_ref, gates_ref):
    mix = _dot(conv_ref[...], woc_ref[...]) + _dot(attn_ref[...], woa_ref[...])
    h1 = _layer_norm_rows(DN_ALPHA * hn_ref[...] + mix, g1_ref[...], b1_ref[...])
    xb = h1.astype(BF16)
    xb_ref[...] = xb

    shared = jax.nn.silu(_dot(xb, wsg_ref[...])) * _dot(xb, wsu_ref[...])
    base_ref[...] = DN_ALPHA * h1 + _dot(shared.astype(BF16), wsd_ref[...])

    x_lo = (h1 - xb.astype(F32)).astype(BF16)
    logits_t = (_dot_nt(wrh_ref[...], xb) + _dot_nt(wrh_ref[...], x_lo) + _dot_nt(wrl_ref[...], xb))
    gates_t = _router_gates(logits_t, rb_ref[...])
    padded = jnp.concatenate([gates_t, jnp.zeros((LANES - N_EXPERTS, gates_t.shape[1]), F32)], axis=0)
    gates_ref[...] = padded.T


def _post_call(conv, attn, hn, woc, woa, g1, b1, wsg, wsu, wsd, wrh, wrl, rbias):
    n, D = hn.shape
    tr = _pick_tile(n, (768, 512, 256))

    def row_spec(w):
        return pl.BlockSpec((tr, w), lambda i: (i, 0))

    def full(a):
        return pl.BlockSpec(a.shape, lambda i: (0,) * a.ndim)

    consts = [woc, woa, g1, b1, wsg, wsu, wsd, wrh, wrl, rbias]
    return pl.pallas_call(
        _post_kernel,
        grid=(n // tr,),
        in_specs=[row_spec(CONV_CH), row_spec(ATTN_W), row_spec(D)] + [full(a) for a in consts],
        out_specs=[row_spec(D), row_spec(D), row_spec(LANES)],
        out_shape=[jax.ShapeDtypeStruct((n, D), BF16), jax.ShapeDtypeStruct((n, D), F32),
                   jax.ShapeDtypeStruct((n, LANES), F32)],
        compiler_params=pltpu.CompilerParams(
            dimension_semantics=("arbitrary",), vmem_limit_bytes=VMEM_LIMIT),
        name="post",
    )(conv, attn, hn, *consts)


def _moe_kernel(xb_ref, base_ref, gates_ref, wg_ref, wu_ref, wd_ref, g2_ref, b2_ref, o_ref):
    s = pl.program_id(1)
    n_per_step, d_exp = wg_ref.shape[0], wg_ref.shape[2]

    @pl.when(s == 0)
    def _():
        o_ref[...] = base_ref[...]

    x = xb_ref[...]
    gates = gates_ref[...]
    lane = lax.broadcasted_iota(jnp.int32, gates.shape, 1)
    hidden = []
    for j in range(n_per_step):
        hg = _dot(x, wg_ref[j].astype(BF16))
        hu = _dot(x, wu_ref[j].astype(BF16))
        gate = jnp.sum(jnp.where(lane == s * n_per_step + j, gates, 0.0), axis=1, keepdims=True)
        hidden.append((jax.nn.silu(hg) * hu * gate).astype(BF16))
    wd = wd_ref[...].astype(BF16).reshape(n_per_step * d_exp, wd_ref.shape[2])
    o_ref[...] += _dot(jnp.concatenate(hidden, axis=1), wd)

    @pl.when(s == pl.num_programs(1) - 1)
    def _():
        o_ref[...] = _layer_norm_rows(o_ref[...], g2_ref[...], b2_ref[...])


def _moe_call(xb, base, gates, w_gate, w_up, w_down, g2, b2):
    n, D = xb.shape
    n_exp, _, d_exp = w_gate.shape
    tm = _pick_tile(n, (1536, 768, 512, 256))
    eps = MOE_EXPERTS_PER_STEP
    assert n_exp % eps == 0

    def row_spec(w, **kw):
        return pl.BlockSpec((tm, w), lambda i, s: (i, 0), **kw)

    once = dict(pipeline_mode=pl.Buffered(1))
    vec = pl.BlockSpec((1, D), lambda i, s: (0, 0))
    return pl.pallas_call(
        _moe_kernel,
        grid=(n // tm, n_exp // eps),
        in_specs=[row_spec(D, **once), row_spec(D, **once), row_spec(LANES, **once),
                  pl.BlockSpec((eps, D, d_exp), lambda i, s: (s, 0, 0)),
                  pl.BlockSpec((eps, D, d_exp), lambda i, s: (s, 0, 0)),
                  pl.BlockSpec((eps, d_exp, D), lambda i, s: (s, 0, 0)),
                  vec, vec],
        out_specs=row_spec(D),
        out_shape=jax.ShapeDtypeStruct((n, D), F32),
        compiler_params=pltpu.CompilerParams(
            dimension_semantics=("arbitrary", "arbitrary"), vmem_limit_bytes=VMEM_LIMIT),
        name="moe",
    )(xb, base, gates, w_gate, w_up, w_down, g2, b2)


def _sum_all(x):
    return jnp.sum(jnp.sum(x, axis=1, keepdims=True), axis=0, keepdims=True)


def _pack_bf16_pairs(x):
    w = x.shape[1] // 2
    lo = pltpu.bitcast(x[:, :w].astype(BF16).astype(F32), jnp.int32)
    hi = pltpu.bitcast(x[:, w:].astype(BF16).astype(F32), jnp.int32)
    return lax.shift_right_logical(lo, 16) | (hi & jnp.int32(-65536))


def _unpack_bf16_pairs(p):
    lo = pltpu.bitcast(lax.shift_left(p, 16), F32)
    hi = pltpu.bitcast(p & jnp.int32(-65536), F32)
    return jnp.concatenate([lo, hi], axis=1)


def _router_topk(logits_t, rbias):
    r = logits_t.shape[1]
    shape3 = (N_GROUPS, GROUP_SIZE, r)
    scores = jax.nn.sigmoid(logits_t).reshape(shape3)
    biased = scores + rbias.reshape(N_GROUPS, GROUP_SIZE, 1)
    in_grp = lax.broadcasted_iota(jnp.int32, shape3, 1).astype(F32)
    m1 = jnp.max(biased, axis=1, keepdims=True)
    first = jnp.min(jnp.where(biased == m1, in_grp, float(GROUP_SIZE)), axis=1, keepdims=True)
    m2 = jnp.max(jnp.where(in_grp == first, NEG_INF, biased), axis=1, keepdims=True)
    cur = m1 + m2

    grp_idx = lax.broadcasted_iota(jnp.int32, (N_GROUPS, 1, r), 0).astype(F32)
    grp_sel = jnp.zeros((N_GROUPS, 1, r), F32)
    for _ in range(TOPK_GROUPS):
        m = jnp.max(cur, axis=0, keepdims=True)
        pick = grp_idx == jnp.min(jnp.where(cur == m, grp_idx, float(N_GROUPS)), axis=0, keepdims=True)
        grp_sel = jnp.where(pick, 1.0, grp_sel)
        cur = jnp.where(pick, NEG_INF, cur)

    cur = jnp.where(jnp.broadcast_to(grp_sel, shape3) > 0.0, biased, NEG_INF)
    exp_idx = lax.broadcasted_iota(jnp.int32, shape3, 0).astype(F32) * GROUP_SIZE + in_grp
    chosen = jnp.zeros(shape3, F32)
    ids, wts = [], []
    for _ in range(TOP_K):
        m = _max_all(cur)
        first = -_max_all(-jnp.where(cur == m, exp_idx, float(N_EXPERTS)))
        pick = exp_idx == first
        chosen = jnp.where(pick, 1.0, chosen)
        cur = jnp.where(pick, NEG_INF, cur)
        ids.append(first.reshape(1, r))
        wts.append(_sum_all(jnp.where(pick, scores, 0.0)).reshape(1, r))
    ids = jnp.concatenate(ids, axis=0)
    wts = jnp.concatenate(wts, axis=0)
    gates = wts / jnp.sum(wts, axis=0, keepdims=True) * ROUTED_SCALE
    return ids, gates, chosen.reshape(N_EXPERTS, r), exp_idx


def _route_kernel(conv_ref, attn_ref, hn_ref, woc_ref, woa_ref, g1_ref, b1_ref,
                  wsg_ref, wsu_ref, wsd_ref, wrh_ref, wrl_ref, rb_ref,
                  xp_ref, base_ref, gates_ref, ek_ref, rk_ref, cnt_ref):
    step = pl.program_id(0)
    tr = hn_ref.shape[0]
    mix = _dot(conv_ref[...], woc_ref[...]) + _dot(attn_ref[...], woa_ref[...])
    h1 = _layer_norm_rows(DN_ALPHA * hn_ref[...] + mix, g1_ref[...], b1_ref[...])
    xb = h1.astype(BF16)
    xp_ref[...] = _pack_bf16_pairs(h1)

    shared = jax.nn.silu(_dot(xb, wsg_ref[...])) * _dot(xb, wsu_ref[...])
    base_ref[...] = DN_ALPHA * h1 + _dot(shared.astype(BF16), wsd_ref[...])

    x_lo = (h1 - xb.astype(F32)).astype(BF16)
    logits_t = (_dot_nt(wrh_ref[...], xb) + _dot_nt(wrh_ref[...], x_lo) + _dot_nt(wrl_ref[...], xb))
    ids, gates, chosen, exp_idx = _router_topk(logits_t, rb_ref[...])
    padded = jnp.concatenate([gates, jnp.zeros((LANES - TOP_K, tr), F32)], axis=0)
    gates_ref[...] = padded.T
    ek_ref[...] = ids.astype(jnp.int32)

    @pl.when(step == 0)
    def _():
        cnt_ref[...] = jnp.zeros(cnt_ref.shape, F32)

    t_i = lax.broadcasted_iota(jnp.int32, (tr, tr), 0)
    t_j = lax.broadcasted_iota(jnp.int32, (tr, tr), 1)
    before = jnp.where(t_i < t_j, 1.0, 0.0).astype(BF16)
    chosen_b = chosen.astype(BF16)
    running = cnt_ref[...]
    rank = _dot(chosen_b, before) + jnp.concatenate([running] * (tr // LANES), axis=1)
    rank3 = rank.reshape(N_GROUPS, GROUP_SIZE, tr)
    rk = [_sum_all(jnp.where(exp_idx == ids[k:k + 1].reshape(1, 1, tr), rank3, 0.0)).reshape(1, tr)
          for k in range(TOP_K)]
    rk_ref[...] = jnp.concatenate(rk, axis=0).astype(jnp.int32)
    cnt_ref[...] = running + _dot(chosen_b, jnp.ones((tr, LANES), BF16))


def _route_call(conv, attn, hn, woc, woa, g1, b1, wsg, wsu, wsd, wrh, wrl, rbias):
    n, D = hn.shape
    tr = _pick_tile(n, (768, 512, 256))

    def row_spec(w):
        return pl.BlockSpec((tr, w), lambda i: (i, 0))

    def col_spec(r):
        return pl.BlockSpec((r, tr), lambda i: (0, i))

    def full(a):
        return pl.BlockSpec(a.shape, lambda i: (0,) * a.ndim)

    consts = [woc, woa, g1, b1, wsg, wsu, wsd, wrh, wrl, rbias]
    return pl.pallas_call(
        _route_kernel,
        grid=(n // tr,),
        in_specs=[row_spec(CONV_CH), row_spec(ATTN_W), row_spec(D)] + [full(a) for a in consts],
        out_specs=[row_spec(D // 2), row_spec(D), row_spec(LANES), col_spec(TOP_K), col_spec(TOP_K),
                   pl.BlockSpec((N_EXPERTS, LANES), lambda i: (0, 0))],
        out_shape=[jax.ShapeDtypeStruct((n, D // 2), jnp.int32),
                   jax.ShapeDtypeStruct((n, D), F32),
                   jax.ShapeDtypeStruct((n, LANES), F32),
                   jax.ShapeDtypeStruct((TOP_K, n), jnp.int32),
                   jax.ShapeDtypeStruct((TOP_K, n), jnp.int32),
                   jax.ShapeDtypeStruct((N_EXPERTS, LANES), F32)],
        compiler_params=pltpu.CompilerParams(
            dimension_semantics=("arbitrary",), vmem_limit_bytes=VMEM_LIMIT),
        name="route",
    )(conv, attn, hn, *consts)


def _plan_kernel(cnt_ref, ek_ref, rk_ref, slot_ref, blk_ref, *, n_blocks):
    tr = ek_ref.shape[1]
    counts = cnt_ref[...]
    padded = jnp.ceil(counts / EXPERT_BLOCK) * EXPERT_BLOCK
    starts = [jnp.zeros((1, LANES), F32)]
    for e in range(1, N_EXPERTS):
        starts.append(starts[-1] + padded[e - 1:e])
    ek = ek_ref[...]
    seg = jnp.zeros(ek.shape, F32)
    for e in range(N_EXPERTS):
        seg = jnp.where(ek == e, jnp.concatenate([starts[e]] * (tr // LANES), axis=1), seg)
    slot_ref[...] = seg.astype(jnp.int32) + rk_ref[...]

    ends = jnp.concatenate(starts, axis=0) + padded
    w = blk_ref.shape[1]
    blk_start = lax.broadcasted_iota(jnp.int32, (N_EXPERTS, w), 1).astype(F32) * EXPERT_BLOCK
    done = jnp.where(jnp.concatenate([ends] * (w // LANES), axis=1) <= blk_start, 1.0, 0.0)
    owner = jnp.minimum(jnp.sum(done, axis=0, keepdims=True), N_EXPERTS - 1.0)
    used = jnp.concatenate([ends[N_EXPERTS - 1:] * (1.0 / EXPERT_BLOCK)] * (w // LANES), axis=1)
    lane = lax.broadcasted_iota(jnp.int32, (1, w), 1)
    table = jnp.where(lane == n_blocks, used, owner)
    blk_ref[...] = jnp.broadcast_to(table, blk_ref.shape).astype(jnp.int32)


def _plan_call(cnt, ek, rk, n_blocks):
    k, n = ek.shape
    tr = _pick_tile(n, (768, 512, 256))
    w = -(-(n_blocks + 1) // LANES) * LANES
    col = pl.BlockSpec((k, tr), lambda i: (0, i))
    return pl.pallas_call(
        functools.partial(_plan_kernel, n_blocks=n_blocks),
        grid=(n // tr,),
        in_specs=[pl.BlockSpec(cnt.shape, lambda i: (0, 0)), col, col],
        out_specs=[col, pl.BlockSpec((SUBLANES, w), lambda i: (0, 0))],
        out_shape=[jax.ShapeDtypeStruct((k, n), jnp.int32), jax.ShapeDtypeStruct((SUBLANES, w), jnp.int32)],
        compiler_params=pltpu.CompilerParams(dimension_semantics=("arbitrary",)),
        name="plan",
    )(cnt, ek, rk)


def _sc_workers():
    info = plsc.get_sparse_core_info()
    return info.num_cores, info.num_subcores


def _sc_scatter_rows(src, slots, n_out):
    n_src, d = src.shape
    units = slots.shape[0]
    n_chunks = n_src // SC_ROWS
    n_cores, n_sub = _sc_workers()
    n_workers = n_cores * n_sub
    assert slots.shape[1] == SC_ROWS

    def body(src_hbm, slot_hbm, out_hbm, idx_v, rows_v):
        wid = lax.axis_index("s") * n_cores + lax.axis_index("c")

        @pl.loop(0, -(-units // n_workers))
        def _(j):
            u = j * n_workers + wid

            @pl.when(u < units)
            def _():
                pltpu.sync_copy(slot_hbm.at[pl.ds(u, 1)], idx_v)
                pltpu.sync_copy(src_hbm.at[pl.ds((u % n_chunks) * SC_ROWS, SC_ROWS)], rows_v)
                pltpu.sync_copy(rows_v, out_hbm.at[idx_v.at[0]])

    return pl.kernel(
        body, out_type=jax.ShapeDtypeStruct((n_out, d), src.dtype),
        mesh=plsc.VectorSubcoreMesh(core_axis_name="c", subcore_axis_name="s"),
        scratch_types=[pltpu.VMEM((1, SC_ROWS), jnp.int32), pltpu.VMEM((SC_ROWS, d), src.dtype)],
        name="dispatch_rows",
    )(src, slots)


def _sc_gather_rows(table, slots):
    units = slots.shape[0]
    d = table.shape[1]
    n_cores, n_sub = _sc_workers()
    n_workers = n_cores * n_sub
    assert slots.shape[1] == SC_ROWS

    def body(table_hbm, slot_hbm, out_hbm, idx_v, rows_v):
        wid = lax.axis_index("s") * n_cores + lax.axis_index("c")

        @pl.loop(0, -(-units // n_workers))
        def _(j):
            u = j * n_workers + wid

            @pl.when(u < units)
            def _():
                pltpu.sync_copy(slot_hbm.at[pl.ds(u, 1)], idx_v)
                pltpu.sync_copy(table_hbm.at[idx_v.at[0]], rows_v)
                pltpu.sync_copy(rows_v, out_hbm.at[pl.ds(u * SC_ROWS, SC_ROWS)])

    return pl.kernel(
        body, out_type=jax.ShapeDtypeStruct((units * SC_ROWS, d), table.dtype),
        mesh=plsc.VectorSubcoreMesh(core_axis_name="c", subcore_axis_name="s"),
        scratch_types=[pltpu.VMEM((1, SC_ROWS), jnp.int32), pltpu.VMEM((SC_ROWS, d), table.dtype)],
        name="collect_rows",
    )(table, slots)


def _expert_kernel(blk_ref, xs_ref, wg_ref, wu_ref, wd_ref, ys_ref, wg_s, wu_s, wd_s):
    b = pl.program_id(0)
    n_blocks = pl.num_programs(0)
    in_use = b < blk_ref[n_blocks]

    @pl.when(jnp.logical_and(in_use, jnp.logical_or(b == 0, blk_ref[b] != blk_ref[jnp.maximum(b - 1, 0)])))
    def _():
        wg_s[...] = wg_ref[0].astype(BF16)
        wu_s[...] = wu_ref[0].astype(BF16)
        wd_s[...] = wd_ref[0].astype(BF16)

    @pl.when(in_use)
    def _():
        x = _unpack_bf16_pairs(xs_ref[...]).astype(BF16)
        hdn = (jax.nn.silu(_dot(x, wg_s[...])) * _dot(x, wu_s[...])).astype(BF16)
        ys_ref[...] = _pack_bf16_pairs(_dot(hdn, wd_s[...]))


def _expert_call(blk_exp, xs, w_gate, w_up, w_down):
    p, half = xs.shape
    _, D, d_exp = w_gate.shape
    n_blocks = p // EXPERT_BLOCK
    assert blk_exp.shape == (n_blocks + 1,)

    def row_block(b, blk):
        return (jnp.minimum(b, blk[n_blocks] - 1), 0)

    rows = pl.BlockSpec((EXPERT_BLOCK, half), row_block)
    grid_spec = pltpu.PrefetchScalarGridSpec(
        num_scalar_prefetch=1,
        grid=(n_blocks,),
        in_specs=[rows,
                  pl.BlockSpec((1, D, d_exp), lambda b, blk: (blk[b], 0, 0)),
                  pl.BlockSpec((1, D, d_exp), lambda b, blk: (blk[b], 0, 0)),
                  pl.BlockSpec((1, d_exp, D), lambda b, blk: (blk[b], 0, 0))],
        out_specs=rows,
        scratch_shapes=[pltpu.VMEM((D, d_exp), BF16), pltpu.VMEM((D, d_exp), BF16),
                        pltpu.VMEM((d_exp, D), BF16)],
    )
    return pl.pallas_call(
        _expert_kernel,
        grid_spec=grid_spec,
        out_shape=jax.ShapeDtypeStruct((p, half), jnp.int32),
        compiler_params=pltpu.CompilerParams(
            dimension_semantics=("arbitrary",), vmem_limit_bytes=VMEM_LIMIT),
        name="experts",
    )(blk_exp, xs, w_gate, w_up, w_down)


def _combine_kernel(g_ref, gates_ref, base_ref, g2_ref, b2_ref, o_ref):
    gates = gates_ref[...]
    acc = base_ref[...]
    for k in range(g_ref.shape[0]):
        acc = acc + _unpack_bf16_pairs(g_ref[k]) * gates[:, k:k + 1]
    o_ref[...] = _layer_norm_rows(acc, g2_ref[...], b2_ref[...])


def _combine_call(g, gates, base, g2, b2):
    k, n, half = g.shape
    D = base.shape[1]
    tr = _pick_tile(n, (768, 512, 256))

    def row_spec(w):
        return pl.BlockSpec((tr, w), lambda i: (i, 0))

    vec = pl.BlockSpec((1, D), lambda i: (0, 0))
    return pl.pallas_call(
        _combine_kernel,
        grid=(n // tr,),
        in_specs=[pl.BlockSpec((k, tr, half), lambda i: (0, i, 0)), row_spec(LANES), row_spec(D), vec, vec],
        out_specs=row_spec(D),
        out_shape=jax.ShapeDtypeStruct((n, D), F32),
        compiler_params=pltpu.CompilerParams(
            dimension_semantics=("arbitrary",), vmem_limit_bytes=VMEM_LIMIT),
        name="combine",
    )(g, gates, base, g2, b2)


def _rope_tables(tp):
    pos = jnp.arange(tp, dtype=F32)
    inv = jnp.power(ROPE_THETA, -2.0 * jnp.arange(ROPE_HALF, dtype=F32) / ROPE_DIM)
    ang = pos[:, None] * inv[None, :]
    cos, sin = jnp.cos(ang), jnp.sin(ang)
    zeros = jnp.zeros((tp, HEAD_DIM - ROPE_DIM), F32)
    zh = jnp.zeros((tp, ROPE_HALF), F32)
    c64 = jnp.concatenate([cos, cos, jnp.ones_like(zeros)], axis=1)
    s1_64 = jnp.concatenate([-sin, zh, zeros], axis=1)
    s2_64 = jnp.concatenate([zh, sin, zeros], axis=1)
    rep = LANES // HEAD_DIM
    return (jnp.tile(c64, (1, rep)), jnp.tile(s1_64, (1, rep)), jnp.tile(s2_64, (1, rep)),
            cos.T, sin.T)


def kernel(x, meta_tokens, ln_emb_g, ln_emb_b, w_in, conv_w, conv_b, ln_conv_g, ln_conv_b, ln_kidx_g, ln_kidx_b, w_out, ln1_g, ln1_b, w_router, router_bias, w_gate, w_up, w_down, ws_gate, ws_up, ws_down, ln2_g, ln2_b):
    B, seq, D = x.shape
    assert w_in.shape[0] == DEPTH
    k_sel = min(INDEX_TOPK, seq // 4)
    t_real = N_META + seq
    tp = -(-t_real // SEQ_ALIGN) * SEQ_ALIGN

    def row(a):
        return a.reshape(1, -1).astype(F32)

    w = w_in[0]
    o = 0
    parts = []
    for width in (CONV_CH, CONV_CH, ATTN_W, ATTN_W, ATTN_W, IDX_HEADS * IDX_DIM, IDX_DIM, IDX_HEADS):
        parts.append(w[:, o:o + width])
        o += width
    wa, wgl, wq, wk, wv, wqi, wki, wwi = parts
    wwi_t = jnp.concatenate([wwi.T, jnp.zeros((BF16_SUBLANES - IDX_HEADS, D), w.dtype)], axis=0)
    weights = (jnp.concatenate([wa, wgl], axis=1).astype(BF16), wq.T.astype(BF16), wk.astype(BF16),
               wv.T.astype(BF16), wqi.T.astype(BF16), jnp.concatenate([wki, wki], axis=1).astype(BF16),
               wwi_t.astype(BF16))

    def twice(a):
        return row(jnp.concatenate([a, a]))

    tabs = _rope_tables(tp)
    inproj_consts = (row(ln_emb_g), row(ln_emb_b), weights, conv_w[0].astype(F32), row(conv_b[0]),
                     row(ln_conv_g[0]), row(ln_conv_b[0]), twice(ln_kidx_g[0]), twice(ln_kidx_b[0]))
    wr_t = w_router[0].T.astype(F32)
    wr_hi = wr_t.astype(BF16)
    wr_lo = (wr_t - wr_hi.astype(F32)).astype(BF16)
    route_consts = (w_out[0][:CONV_CH].astype(BF16), w_out[0][CONV_CH:].astype(BF16), row(ln1_g[0]),
                    row(ln1_b[0]), ws_gate[0].astype(BF16), ws_up[0].astype(BF16), ws_down[0].astype(BF16),
                    wr_hi, wr_lo, router_bias[0].reshape(-1, 1).astype(F32))
    meta = meta_tokens.astype(x.dtype)
    tail = jnp.zeros((tp - t_real, D), x.dtype)
    n_blocks = tp * TOP_K // EXPERT_BLOCK + N_EXPERTS

    outs = []
    for b in range(B):
        hp = jnp.concatenate([meta, x[b], tail], axis=0)[None]
        hn, conv, qt, k, vt, qit, ki, wit = _inproj_call(hp, tabs, *inproj_consts)
        attn = _dsa_call(qt, qit, wit, k, ki, vt, k_sel)
        xp, base, gates, ek, rk, cnt = _route_call(conv[0], attn[0], hn[0], *route_consts)

        slot, blk = _plan_call(cnt, ek, rk, n_blocks)
        slots = slot.reshape(TOP_K * tp // SC_ROWS, SC_ROWS)
        xs = _sc_scatter_rows(xp, slots, n_blocks * EXPERT_BLOCK)
        ys = _expert_call(blk[0, :n_blocks + 1], xs, w_gate[0], w_up[0], w_down[0])
        picked = _sc_gather_rows(ys, slots).reshape(TOP_K, tp, D // 2)
        out = _combine_call(picked, gates, base, row(ln2_g[0]), row(ln2_b[0]))
        outs.append(out[N_META:t_real])
    return jnp.stack(outs)
```

```python
import functools

import numpy as np
import jax
import jax.numpy as jnp
from jax import lax
from jax.experimental import pallas as pl
from jax.experimental.pallas import tpu as pltpu
from jax.experimental.pallas import tpu_sc as plsc

N_META = 16
CONV_CH = 512
CONV_WIDTH = 31
N_HEADS = 8
HEAD_DIM = 64
ATTN_W = N_HEADS * HEAD_DIM
IDX_HEADS = 8
IDX_DIM = 64
INDEX_TOPK = 256
ROPE_DIM = HEAD_DIM // 4
ROPE_HALF = ROPE_DIM // 2
ROPE_THETA = 500000.0
N_EXPERTS = 64
TOP_K = 8
N_GROUPS = 8
GROUP_SIZE = N_EXPERTS // N_GROUPS
TOPK_GROUPS = 4
ROUTED_SCALE = 2.5
LN_EPS = 1e-5
DEPTH = 1
DN_ALPHA = (2.0 * DEPTH) ** 0.25

LANES = 128
Q_TILE = 256
SUBLANES = 8
BF16_SUBLANES = 16
MOE_EXPERTS_PER_STEP = 2
EXPERT_BLOCK = 512
SC_ROWS = 128
K_CHUNK = 256
K_SUB = 128
SEQ_ALIGN = 256
CONV_HALO = 32
VMEM_LIMIT = 56 * 1024 * 1024

F32 = jnp.float32
BF16 = jnp.bfloat16
COARSE = jnp.bfloat16
NEG_INF = float("-inf")
INT_MIN = -2 ** 31
KEY_NEG_INF = -2139095041
LOG2_E = 1.4426950408889634


def _dot(a, b):
    return jnp.dot(a, b, preferred_element_type=F32)


def _dot_nt(a, b):
    return lax.dot_general(a, b, (((1,), (1,)), ((), ())), preferred_element_type=F32)


def _layer_norm_rows(x, g, b):
    mu = jnp.mean(x, axis=-1, keepdims=True)
    xc = x - mu
    var = jnp.mean(xc * xc, axis=-1, keepdims=True)
    return xc * lax.rsqrt(var + LN_EPS) * g + b


def _pick_tile(n, candidates):
    for c in candidates:
        if n % c == 0:
            return c
    raise ValueError(f"no tile for {n}")


def _rope_rows(x, c_tab, s1_tab, s2_tab):
    outs = []
    for j in range(x.shape[1] // LANES):
        xs = x[:, j * LANES:(j + 1) * LANES]
        up = pltpu.roll(xs, LANES - ROPE_HALF, axis=1)
        dn = pltpu.roll(xs, ROPE_HALF, axis=1)
        outs.append(xs * c_tab + up * s1_tab + dn * s2_tab)
    return jnp.concatenate(outs, axis=1)


def _rope_cols(xt, cos_t, sin_t, heads):
    r = xt.shape[1]
    x3 = xt.reshape(heads, HEAD_DIM, r)
    x1 = x3[:, 0:ROPE_HALF, :]
    x2 = x3[:, ROPE_HALF:ROPE_DIM, :]
    n1 = x1 * cos_t - x2 * sin_t
    n2 = x2 * cos_t + x1 * sin_t
    out = jnp.concatenate([n1, n2, x3[:, ROPE_DIM:, :]], axis=1)
    return out.reshape(heads * HEAD_DIM, r)


def _inproj_kernel(h_ref, ctab_ref, s1tab_ref, s2tab_ref, cost_ref, sint_ref,
                   lng_ref, lnb_ref, wag_ref, wqt_ref, wk_ref, wvt_ref, wqit_ref, wki_ref, wwit_ref,
                   cw_ref, cb_ref, lncg_ref, lncb_ref, lnkg_ref, lnkb_ref,
                   hn_ref, conv_ref, qt_ref, k_ref, vt_ref, qit_ref, ki_ref, wit_ref,
                   ubuf_ref):
    t = pl.program_id(1)
    tr = h_ref.shape[1]

    hn = _layer_norm_rows(h_ref[0], lng_ref[...], lnb_ref[...])
    hn_ref[0] = hn
    xb = hn.astype(BF16)

    ag = _dot(xb, wag_ref[...])
    u = ag[:, :CONV_CH] * jax.nn.sigmoid(ag[:, CONV_CH:])

    @pl.when(t == 0)
    def _():
        ubuf_ref[0:CONV_HALO, :] = jnp.zeros((CONV_HALO, CONV_CH), F32)

    ubuf_ref[CONV_HALO:CONV_HALO + tr, :] = u
    base = CONV_HALO - (CONV_WIDTH - 1)
    acc = jnp.zeros((tr, CONV_CH), F32)
    for j in range(CONV_WIDTH):
        acc = acc + cw_ref[j:j + 1, :] * ubuf_ref[base + j:base + j + tr, :]
    ubuf_ref[0:CONV_HALO, :] = ubuf_ref[tr:tr + CONV_HALO, :]
    c = _layer_norm_rows(acc + cb_ref[...], lncg_ref[...], lncb_ref[...])
    conv_ref[0] = (c * jax.nn.sigmoid(c)).astype(conv_ref.dtype)

    ctab, s1tab, s2tab = ctab_ref[...], s1tab_ref[...], s2tab_ref[...]
    cos_t, sin_t = cost_ref[...], sint_ref[...]

    qt = _rope_cols(_dot_nt(wqt_ref[...], xb), cos_t, sin_t, N_HEADS)
    qt_ref[0] = (qt * (HEAD_DIM ** -0.5 * LOG2_E)).astype(qt_ref.dtype)
    k = _rope_rows(_dot(xb, wk_ref[...]), ctab, s1tab, s2tab)
    k_ref[0] = k.astype(k_ref.dtype)
    vt_ref[0] = _dot_nt(wvt_ref[...], xb).astype(vt_ref.dtype)

    qit = _rope_cols(_dot_nt(wqit_ref[...], xb), cos_t, sin_t, IDX_HEADS)
    qit_ref[0] = qit.astype(qit_ref.dtype)
    ki = _layer_norm_rows(_dot(xb, wki_ref[...]), lnkg_ref[...], lnkb_ref[...])
    ki_ref[0] = _rope_rows(ki, ctab, s1tab, s2tab).astype(ki_ref.dtype)
    wit = _dot_nt(wwit_ref[...], xb) * (IDX_HEADS ** -0.5)
    wit_ref[0] = wit[:IDX_HEADS]


def _inproj_call(hp, tabs, ln_g, ln_b, weights, conv_w, conv_b, lnc_g, lnc_b, lnk_g, lnk_b):
    B, tp, D = hp.shape
    tr = _pick_tile(tp, (768, 512, 256))
    nt = tp // tr
    ctab, s1tab, s2tab, cos_t, sin_t = tabs
    def row_spec(w):
        return pl.BlockSpec((1, tr, w), lambda b, t: (b, t, 0))

    def col_spec(r):
        return pl.BlockSpec((1, r, tr), lambda b, t: (b, 0, t))

    def full(a):
        return pl.BlockSpec(a.shape, lambda b, t: (0,) * a.ndim)

    tab_row = pl.BlockSpec((tr, LANES), lambda b, t: (t, 0))
    tab_col = pl.BlockSpec((ROPE_HALF, tr), lambda b, t: (0, t))
    consts = [ln_g, ln_b, *weights, conv_w, conv_b, lnc_g, lnc_b, lnk_g, lnk_b]
    out_shape = [
        jax.ShapeDtypeStruct((B, tp, D), F32),
        jax.ShapeDtypeStruct((B, tp, CONV_CH), BF16),
        jax.ShapeDtypeStruct((B, ATTN_W, tp), BF16),
        jax.ShapeDtypeStruct((B, tp, ATTN_W), BF16),
        jax.ShapeDtypeStruct((B, ATTN_W, tp), BF16),
        jax.ShapeDtypeStruct((B, IDX_HEADS * IDX_DIM, tp), BF16),
        jax.ShapeDtypeStruct((B, tp, 2 * IDX_DIM), BF16),
        jax.ShapeDtypeStruct((B, IDX_HEADS, tp), F32),
    ]
    out_specs = [row_spec(D), row_spec(CONV_CH), col_spec(ATTN_W), row_spec(ATTN_W), col_spec(ATTN_W),
                 col_spec(IDX_HEADS * IDX_DIM), row_spec(2 * IDX_DIM), col_spec(IDX_HEADS)]
    return pl.pallas_call(
        _inproj_kernel,
        grid=(B, nt),
        in_specs=[row_spec(D), tab_row, tab_row, tab_row, tab_col, tab_col] + [full(a) for a in consts],
        out_specs=out_specs,
        out_shape=out_shape,
        scratch_shapes=[pltpu.VMEM((CONV_HALO + tr, CONV_CH), F32)],
        compiler_params=pltpu.CompilerParams(
            dimension_semantics=("arbitrary", "arbitrary"), vmem_limit_bytes=VMEM_LIMIT),
        name="inproj",
    )(hp, ctab, s1tab, s2tab, cos_t, sin_t, *consts)


def _key_to_float(key):
    bits = jnp.where(key >= 0, key, key ^ jnp.int32(0x7FFFFFFF))
    f = pltpu.bitcast(bits, F32)
    return jnp.where(key < jnp.int32(KEY_NEG_INF), NEG_INF, f)


def _tree(parts, op):
    parts = list(parts)
    while len(parts) > 1:
        nxt = [op(parts[j], parts[j + 1]) for j in range(0, len(parts) - 1, 2)]
        if len(parts) % 2:
            nxt.append(parts[-1])
        parts = nxt
    return parts[0]


def _fold_rows(x, rows, op):
    return _tree([x[j * rows:(j + 1) * rows] for j in range(x.shape[0] // rows)], op)


def _dsa_kernel(qt_ref, qit_ref, wit_ref, k_ref, ki_ref, vt_ref, o_ref,
                sc_ref, sh_ref, qm_ref, qim_ref, m_ref, alpha_ref, shift_ref, lg_ref, acc_ref, *, k_sel):
    i = pl.program_id(1)
    tq = qt_ref.shape[2]
    n_chunks = (i * tq + tq + K_CHUNK - 1) // K_CHUNK
    v_rows = HEAD_DIM + BF16_SUBLANES

    def causal_mask(k0, rows=K_CHUNK):
        kpos = k0 + lax.broadcasted_iota(jnp.int32, (rows, tq), 0)
        return kpos <= i * tq + lax.broadcasted_iota(jnp.int32, (rows, tq), 1)

    def rows8(x):
        return jnp.broadcast_to(x, (SUBLANES, tq))

    def tiles(x):
        return x.reshape(x.shape[0] // SUBLANES, SUBLANES, tq)

    def head_slab(ref, h):
        slab = ref[0, (h // 2) * LANES:(h // 2 + 1) * LANES, :]
        zeros = jnp.zeros((HEAD_DIM, tq), slab.dtype)
        if h % 2 == 0:
            return jnp.concatenate([slab[:HEAD_DIM], zeros], axis=0)
        return jnp.concatenate([zeros, slab[HEAD_DIM:]], axis=0)

    for h in range(N_HEADS):
        qm_ref[h] = head_slab(qt_ref, h)
    for h in range(IDX_HEADS):
        qim_ref[h] = head_slab(qit_ref, h)
    wit = wit_ref[0]
    w_heads = [rows8(wit[h:h + 1]) for h in range(IDX_HEADS)]

    def score_body(c, carry):
        for s in range(K_CHUNK // K_SUB):
            k0 = pl.multiple_of(c * K_CHUNK + s * K_SUB, K_SUB)
            kic = ki_ref[0, pl.ds(k0, K_SUB), :]
            acc = jnp.zeros((K_SUB // SUBLANES, SUBLANES, tq), F32)
            for h in range(IDX_HEADS):
                acc = acc + w_heads[h][None] * jnp.maximum(tiles(_dot(kic, qim_ref[h])), 0.0)
            acc = (acc * (IDX_DIM ** -0.5)).reshape(K_SUB, tq)
            acc = jnp.where(causal_mask(k0, K_SUB), acc, NEG_INF)
            sc_ref[pl.ds(k0, K_SUB), :] = acc
            hi_bits = pltpu.bitcast(acc, jnp.int32) & jnp.int32(-65536)
            sh_ref[pl.ds(k0, K_SUB), :] = pltpu.bitcast(hi_bits, F32).astype(COARSE)
        return carry

    lax.fori_loop(0, n_chunks, score_body, 0)

    def count_all(ref, thr_tile, preds):
        rows = thr_tile.shape[0]
        one, zero = jnp.ones((), ref.dtype), jnp.zeros((), ref.dtype)

        def body(c, cnts):
            k0 = pl.multiple_of(c * K_CHUNK, K_CHUNK)
            s = ref[pl.ds(k0, K_CHUNK), :]
            out = []
            for cnt, p in zip(cnts, preds):
                hits = [jnp.where(p(s[j * rows:(j + 1) * rows], thr_tile), one, zero)
                        for j in range(K_CHUNK // rows)]
                out.append(cnt + _tree(hits, jnp.add).astype(F32))
            return tuple(out)

        init = tuple(jnp.zeros((rows, tq), F32) for _ in preds)
        cnts = lax.fori_loop(0, n_chunks, body, init)
        return [rows8(jnp.sum(cnt, axis=0, keepdims=True)) for cnt in cnts]

    def search_body(it, tkey, coarse):
        cand = tkey + lax.shift_left(jnp.int32(1), 31 - it)
        cf = _key_to_float(cand)
        if coarse:
            cf = pltpu.bitcast(pltpu.bitcast(cf, jnp.int32) & jnp.int32(-65536), F32).astype(COARSE)
            cf = jnp.concatenate([cf] * (BF16_SUBLANES // SUBLANES), axis=0)
        cnt, = count_all(sh_ref if coarse else sc_ref, cf, [lambda s, t: s >= t])
        return jnp.where(cnt >= k_sel, cand, tkey)

    tkey = jnp.full((SUBLANES, tq), INT_MIN, jnp.int32)
    tkey = lax.fori_loop(0, 16, functools.partial(search_body, coarse=True), tkey)
    tkey = lax.fori_loop(16, 32, functools.partial(search_body, coarse=False), tkey)
    thr = _key_to_float(tkey)

    n_ge, n_gt = count_all(sc_ref, thr, [lambda s, t: s >= t, lambda s, t: s > t])
    need = k_sel - n_gt
    has_ties = jnp.max(n_ge) > k_sel

    @pl.when(jnp.logical_not(has_ties))
    def _():
        def body(c, carry):
            k0 = pl.multiple_of(c * K_CHUNK, K_CHUNK)
            s = sc_ref[pl.ds(k0, K_CHUNK), :]
            bias = jnp.where(tiles(s) >= thr[None], 0.0, NEG_INF).reshape(K_CHUNK, tq)
            sh_ref[pl.ds(k0, K_CHUNK), :] = jnp.where(causal_mask(k0), bias, NEG_INF).astype(COARSE)
            return carry
        lax.fori_loop(0, n_chunks, body, 0)

    @pl.when(has_ties)
    def _():
        r_i = lax.broadcasted_iota(jnp.int32, (K_CHUNK, K_CHUNK), 0)
        c_i = lax.broadcasted_iota(jnp.int32, (K_CHUNK, K_CHUNK), 1)
        lower = jnp.where(c_i <= r_i, 1.0, 0.0).astype(BF16)
        thr_row, need_row = thr[0:1], need[0:1]

        def body(c, seen):
            k0 = pl.multiple_of(c * K_CHUNK, K_CHUNK)
            s = sc_ref[pl.ds(k0, K_CHUNK), :]
            eq = jnp.where(s == thr_row, 1.0, 0.0)
            rank = _dot(lower, eq.astype(BF16)) + seen
            keep_tie = jnp.where(rank <= need_row, eq, 0.0)
            sel = jnp.where(s > thr_row, 1.0, keep_tie)
            sh_ref[pl.ds(k0, K_CHUNK), :] = jnp.where(
                sel > 0.0, jnp.where(causal_mask(k0), 0.0, NEG_INF), NEG_INF).astype(COARSE)
            return rank[K_CHUNK - 1:K_CHUNK, :]
        lax.fori_loop(0, n_chunks, body, jnp.zeros((1, tq), F32))

    m_ref[...] = jnp.full(m_ref.shape, NEG_INF, F32)
    acc_ref[...] = jnp.zeros(acc_ref.shape, F32)
    ones_rows = jnp.ones((BF16_SUBLANES, K_CHUNK), BF16)

    def logits_stage(c, slot):
        k0 = pl.multiple_of(c * K_CHUNK, K_CHUNK)
        for h in range(N_HEADS):
            pair = slice((h // 2) * LANES, (h // 2 + 1) * LANES)
            cmax = []
            for s in range(K_CHUNK // K_SUB):
                rows = pl.ds(pl.multiple_of(k0 + s * K_SUB, K_SUB), K_SUB)
                lg = _dot(k_ref[0, rows, pair], qm_ref[h]).astype(BF16) + sh_ref[rows, :].astype(BF16)
                lg_ref[slot, h, s * K_SUB:(s + 1) * K_SUB, :] = lg
                cmax.append(_fold_rows(lg, BF16_SUBLANES, jnp.maximum))
            cmax = _tree(cmax, jnp.maximum).astype(F32)
            m_old = m_ref[h]
            m_new = jnp.maximum(m_old, rows8(jnp.max(cmax, axis=0, keepdims=True)))
            m_safe = jnp.where(m_new == NEG_INF, 0.0, m_new)
            alpha_ref[slot, h] = jnp.exp2(m_old - m_safe)
            shift_ref[slot, h] = m_safe
            m_ref[h] = m_new

    def values_stage(c, slot):
        k0 = pl.multiple_of(c * K_CHUNK, K_CHUNK)
        for h in range(N_HEADS):
            m_safe = shift_ref[slot, h].astype(BF16)
            m_tile = jnp.concatenate([m_safe] * (BF16_SUBLANES // SUBLANES), axis=0)
            lg = lg_ref[slot, h].reshape(K_CHUNK // BF16_SUBLANES, BF16_SUBLANES, tq)
            p = jnp.exp2(lg - m_tile[None]).reshape(K_CHUNK, tq)
            v_aug = jnp.concatenate(
                [vt_ref[0, h * HEAD_DIM:(h + 1) * HEAD_DIM, pl.ds(k0, K_CHUNK)], ones_rows], axis=0)
            pv = _dot(v_aug, p)
            acc = acc_ref[h].reshape(v_rows // SUBLANES, SUBLANES, tq) * alpha_ref[slot, h][None]
            acc_ref[h] = acc.reshape(v_rows, tq) + pv

    def attn_body(c, carry):
        logits_stage(c, 0)
        values_stage(c, 0)
        return carry

    lax.fori_loop(0, n_chunks, attn_body, 0)

    for pair in range(N_HEADS // 2):
        halves = []
        for h in (2 * pair, 2 * pair + 1):
            a = acc_ref[h]
            halves.append(a[:HEAD_DIM] / a[HEAD_DIM:HEAD_DIM + 1])
        out_t = jnp.concatenate(halves, axis=0)
        o_ref[0, :, pair * LANES:(pair + 1) * LANES] = out_t.T.astype(o_ref.dtype)


def _dsa_call(qt, qit, wit, k, ki, vt, k_sel):
    B, tp, _ = k.shape
    nq = tp // Q_TILE

    def q_cols(r):
        return pl.BlockSpec((1, r, Q_TILE), lambda b, i: (b, 0, i))

    def per_batch(a):
        return pl.BlockSpec((1,) + a.shape[1:], lambda b, i: (b, 0, 0), pipeline_mode=pl.Buffered(1))

    return pl.pallas_call(
        functools.partial(_dsa_kernel, k_sel=k_sel),
        grid=(B, nq),
        in_specs=[q_cols(ATTN_W), q_cols(IDX_HEADS * IDX_DIM), q_cols(IDX_HEADS),
                  per_batch(k), per_batch(ki), per_batch(vt)],
        out_specs=pl.BlockSpec((1, Q_TILE, ATTN_W), lambda b, i: (b, i, 0)),
        out_shape=jax.ShapeDtypeStruct((B, tp, ATTN_W), BF16),
        scratch_shapes=[pltpu.VMEM((tp, Q_TILE), F32),
                        pltpu.VMEM((tp, Q_TILE), COARSE),
                        pltpu.VMEM((N_HEADS, LANES, Q_TILE), BF16),
                        pltpu.VMEM((IDX_HEADS, LANES, Q_TILE), BF16),
                        pltpu.VMEM((N_HEADS, SUBLANES, Q_TILE), F32),
                        pltpu.VMEM((1, N_HEADS, SUBLANES, Q_TILE), F32),
                        pltpu.VMEM((1, N_HEADS, SUBLANES, Q_TILE), F32),
                        pltpu.VMEM((1, N_HEADS, K_CHUNK, Q_TILE), BF16),
                        pltpu.VMEM((N_HEADS, HEAD_DIM + BF16_SUBLANES, Q_TILE), F32)],
        compiler_params=pltpu.CompilerParams(
            dimension_semantics=("arbitrary", "arbitrary"), vmem_limit_bytes=VMEM_LIMIT),
        name="dsa",
    )(qt, qit, wit, k, ki, vt)


def _max_all(x):
    return jnp.max(jnp.max(x, axis=1, keepdims=True), axis=0, keepdims=True)


def _router_gates(logits_t, rbias):
    r = logits_t.shape[1]
    shape3 = (N_GROUPS, GROUP_SIZE, r)
    scores = jax.nn.sigmoid(logits_t).reshape(shape3)
    biased = scores + rbias.reshape(N_GROUPS, GROUP_SIZE, 1)
    in_grp = lax.broadcasted_iota(jnp.int32, shape3, 1).astype(F32)
    m1 = jnp.max(biased, axis=1, keepdims=True)
    first = jnp.min(jnp.where(biased == m1, in_grp, float(GROUP_SIZE)), axis=1, keepdims=True)
    m2 = jnp.max(jnp.where(in_grp == first, NEG_INF, biased), axis=1, keepdims=True)
    cur = m1 + m2

    grp_idx = lax.broadcasted_iota(jnp.int32, (N_GROUPS, 1, r), 0).astype(F32)
    grp_sel = jnp.zeros((N_GROUPS, 1, r), F32)
    for _ in range(TOPK_GROUPS):
        m = jnp.max(cur, axis=0, keepdims=True)
        pick = grp_idx == jnp.min(jnp.where(cur == m, grp_idx, float(N_GROUPS)), axis=0, keepdims=True)
        grp_sel = jnp.where(pick, 1.0, grp_sel)
        cur = jnp.where(pick, NEG_INF, cur)

    cur = jnp.where(jnp.broadcast_to(grp_sel, shape3) > 0.0, biased, NEG_INF)
    exp_idx = lax.broadcasted_iota(jnp.int32, shape3, 0).astype(F32) * GROUP_SIZE + in_grp
    chosen = jnp.zeros(shape3, F32)
    for _ in range(TOP_K):
        m = _max_all(cur)
        first = -_max_all(-jnp.where(cur == m, exp_idx, float(N_EXPERTS)))
        pick = exp_idx == first
        chosen = jnp.where(pick, 1.0, chosen)
        cur = jnp.where(pick, NEG_INF, cur)

    w = jnp.where(chosen > 0.0, scores, 0.0)
    denom = jnp.sum(jnp.sum(w, axis=1, keepdims=True), axis=0, keepdims=True)
    return (w / denom * ROUTED_SCALE).reshape(N_EXPERTS, r)


def _post_kernel(conv_ref, attn_ref, hn_ref, woc_ref, woa_ref, g1_ref, b1_ref,
                 wsg_ref, wsu_ref, wsd_ref, wrh_ref, wrl_ref, rb_ref,
                 xb_ref, base_ref, gates_ref):
    mix = _dot(conv_ref[...], woc_ref[...]) + _dot(attn_ref[...], woa_ref[...])
    h1 = _layer_norm_rows(DN_ALPHA * hn_ref[...] + mix, g1_ref[...], b1_ref[...])
    xb = h1.astype(BF16)
    xb_ref[...] = xb

    shared = jax.nn.silu(_dot(xb, wsg_ref[...])) * _dot(xb, wsu_ref[...])
    base_ref[...] = DN_ALPHA * h1 + _dot(shared.astype(BF16), wsd_ref[...])

    x_lo = (h1 - xb.astype(F32)).astype(BF16)
    logits_t = (_dot_nt(wrh_ref[...], xb) + _dot_nt(wrh_ref[...], x_lo) + _dot_nt(wrl_ref[...], xb))
    gates_t = _router_gates(logits_t, rb_ref[...])
    padded = jnp.concatenate([gates_t, jnp.zeros((LANES - N_EXPERTS, gates_t.shape[1]), F32)], axis=0)
    gates_ref[...] = padded.T


def _post_call(conv, attn, hn, woc, woa, g1, b1, wsg, wsu, wsd, wrh, wrl, rbias):
    n, D = hn.shape
    tr = _pick_tile(n, (768, 512, 256))

    def row_spec(w):
        return pl.BlockSpec((tr, w), lambda i: (i, 0))

    def full(a):
        return pl.BlockSpec(a.shape, lambda i: (0,) * a.ndim)

    consts = [woc, woa, g1, b1, wsg, wsu, wsd, wrh, wrl, rbias]
    return pl.pallas_call(
        _post_kernel,
        grid=(n // tr,),
        in_specs=[row_spec(CONV_CH), row_spec(ATTN_W), row_spec(D)] + [full(a) for a in consts],
        out_specs=[row_spec(D), row_spec(D), row_spec(LANES)],
        out_shape=[jax.ShapeDtypeStruct((n, D), BF16), jax.ShapeDtypeStruct((n, D), F32),
                   jax.ShapeDtypeStruct((n, LANES), F32)],
        compiler_params=pltpu.CompilerParams(
            dimension_semantics=("arbitrary",), vmem_limit_bytes=VMEM_LIMIT),
        name="post",
    )(conv, attn, hn, *consts)


def _moe_kernel(xb_ref, base_ref, gates_ref, wg_ref, wu_ref, wd_ref, g2_ref, b2_ref, o_ref):
    s = pl.program_id(1)
    n_per_step, d_exp = wg_ref.shape[0], wg_ref.shape[2]

    @pl.when(s == 0)
    def _():
        o_ref[...] = base_ref[...]

    x = xb_ref[...]
    gates = gates_ref[...]
    lane = lax.broadcasted_iota(jnp.int32, gates.shape, 1)
    hidden = []
    for j in range(n_per_step):
        hg = _dot(x, wg_ref[j].astype(BF16))
        hu = _dot(x, wu_ref[j].astype(BF16))
        gate = jnp.sum(jnp.where(lane == s * n_per_step + j, gates, 0.0), axis=1, keepdims=True)
        hidden.append((jax.nn.silu(hg) * hu * gate).astype(BF16))
    wd = wd_ref[...].astype(BF16).reshape(n_per_step * d_exp, wd_ref.shape[2])
    o_ref[...] += _dot(jnp.concatenate(hidden, axis=1), wd)

    @pl.when(s == pl.num_programs(1) - 1)
    def _():
        o_ref[...] = _layer_norm_rows(o_ref[...], g2_ref[...], b2_ref[...])


def _moe_call(xb, base, gates, w_gate, w_up, w_down, g2, b2):
    n, D = xb.shape
    n_exp, _, d_exp = w_gate.shape
    tm = _pick_tile(n, (1536, 768, 512, 256))
    eps = MOE_EXPERTS_PER_STEP
    assert n_exp % eps == 0

    def row_spec(w, **kw):
        return pl.BlockSpec((tm, w), lambda i, s: (i, 0), **kw)

    once = dict(pipeline_mode=pl.Buffered(1))
    vec = pl.BlockSpec((1, D), lambda i, s: (0, 0))
    return pl.pallas_call(
        _moe_kernel,
        grid=(n // tm, n_exp // eps),
        in_specs=[row_spec(D, **once), row_spec(D, **once), row_spec(LANES, **once),
                  pl.BlockSpec((eps, D, d_exp), lambda i, s: (s, 0, 0)),
                  pl.BlockSpec((eps, D, d_exp), lambda i, s: (s, 0, 0)),
                  pl.BlockSpec((eps, d_exp, D), lambda i, s: (s, 0, 0)),
                  vec, vec],
        out_specs=row_spec(D),
        out_shape=jax.ShapeDtypeStruct((n, D), F32),
        compiler_params=pltpu.CompilerParams(
            dimension_semantics=("arbitrary", "arbitrary"), vmem_limit_bytes=VMEM_LIMIT),
        name="moe",
    )(xb, base, gates, w_gate, w_up, w_down, g2, b2)


def _sum_all(x):
    return jnp.sum(jnp.sum(x, axis=1, keepdims=True), axis=0, keepdims=True)


def _pack_bf16_pairs(x):
    w = x.shape[1] // 2
    lo = pltpu.bitcast(x[:, :w].astype(BF16).astype(F32), jnp.int32)
    hi = pltpu.bitcast(x[:, w:].astype(BF16).astype(F32), jnp.int32)
    return lax.shift_right_logical(lo, 16) | (hi & jnp.int32(-65536))


def _unpack_bf16_pairs(p):
    lo = pltpu.bitcast(lax.shift_left(p, 16), F32)
    hi = pltpu.bitcast(p & jnp.int32(-65536), F32)
    return jnp.concatenate([lo, hi], axis=1)


def _router_topk(logits_t, rbias):
    r = logits_t.shape[1]
    shape3 = (N_GROUPS, GROUP_SIZE, r)
    scores = jax.nn.sigmoid(logits_t).reshape(shape3)
    biased = scores + rbias.reshape(N_GROUPS, GROUP_SIZE, 1)
    in_grp = lax.broadcasted_iota(jnp.int32, shape3, 1).astype(F32)
    m1 = jnp.max(biased, axis=1, keepdims=True)
    first = jnp.min(jnp.where(biased == m1, in_grp, float(GROUP_SIZE)), axis=1, keepdims=True)
    m2 = jnp.max(jnp.where(in_grp == first, NEG_INF, biased), axis=1, keepdims=True)
    cur = m1 + m2

    grp_idx = lax.broadcasted_iota(jnp.int32, (N_GROUPS, 1, r), 0).astype(F32)
    grp_sel = jnp.zeros((N_GROUPS, 1, r), F32)
    for _ in range(TOPK_GROUPS):
        m = jnp.max(cur, axis=0, keepdims=True)
        pick = grp_idx == jnp.min(jnp.where(cur == m, grp_idx, float(N_GROUPS)), axis=0, keepdims=True)
        grp_sel = jnp.where(pick, 1.0, grp_sel)
        cur = jnp.where(pick, NEG_INF, cur)

    cur = jnp.where(jnp.broadcast_to(grp_sel, shape3) > 0.0, biased, NEG_INF)
    exp_idx = lax.broadcasted_iota(jnp.int32, shape3, 0).astype(F32) * GROUP_SIZE + in_grp
    chosen = jnp.zeros(shape3, F32)
    ids, wts = [], []
    for _ in range(TOP_K):
        m = _max_all(cur)
        first = -_max_all(-jnp.where(cur == m, exp_idx, float(N_EXPERTS)))
        pick = exp_idx == first
        chosen = jnp.where(pick, 1.0, chosen)
        cur = jnp.where(pick, NEG_INF, cur)
        ids.append(first.reshape(1, r))
        wts.append(_sum_all(jnp.where(pick, scores, 0.0)).reshape(1, r))
    ids = jnp.concatenate(ids, axis=0)
    wts = jnp.concatenate(wts, axis=0)
    gates = wts / jnp.sum(wts, axis=0, keepdims=True) * ROUTED_SCALE
    return ids, gates, chosen.reshape(N_EXPERTS, r), exp_idx


def _route_kernel(conv_ref, attn_ref, hn_ref, woc_ref, woa_ref, g1_ref, b1_ref,
                  wsg_ref, wsu_ref, wsd_ref, wrh_ref, wrl_ref, rb_ref,
                  xp_ref, base_ref, gates_ref, ek_ref, rk_ref, cnt_ref):
    step = pl.program_id(0)
    tr = hn_ref.shape[0]
    mix = _dot(conv_ref[...], woc_ref[...]) + _dot(attn_ref[...], woa_ref[...])
    h1 = _layer_norm_rows(DN_ALPHA * hn_ref[...] + mix, g1_ref[...], b1_ref[...])
    xb = h1.astype(BF16)
    xp_ref[...] = _pack_bf16_pairs(h1)

    shared = jax.nn.silu(_dot(xb, wsg_ref[...])) * _dot(xb, wsu_ref[...])
    base_ref[...] = DN_ALPHA * h1 + _dot(shared.astype(BF16), wsd_ref[...])

    x_lo = (h1 - xb.astype(F32)).astype(BF16)
    logits_t = (_dot_nt(wrh_ref[...], xb) + _dot_nt(wrh_ref[...], x_lo) + _dot_nt(wrl_ref[...], xb))
    ids, gates, chosen, exp_idx = _router_topk(logits_t, rb_ref[...])
    padded = jnp.concatenate([gates, jnp.zeros((LANES - TOP_K, tr), F32)], axis=0)
    gates_ref[...] = padded.T
    ek_ref[...] = ids.astype(jnp.int32)

    @pl.when(step == 0)
    def _():
        cnt_ref[...] = jnp.zeros(cnt_ref.shape, F32)

    t_i = lax.broadcasted_iota(jnp.int32, (tr, tr), 0)
    t_j = lax.broadcasted_iota(jnp.int32, (tr, tr), 1)
    before = jnp.where(t_i < t_j, 1.0, 0.0).astype(BF16)
    chosen_b = chosen.astype(BF16)
    running = cnt_ref[...]
    rank = _dot(chosen_b, before) + jnp.concatenate([running] * (tr // LANES), axis=1)
    rank3 = rank.reshape(N_GROUPS, GROUP_SIZE, tr)
    rk = [_sum_all(jnp.where(exp_idx == ids[k:k + 1].reshape(1, 1, tr), rank3, 0.0)).reshape(1, tr)
          for k in range(TOP_K)]
    rk_ref[...] = jnp.concatenate(rk, axis=0).astype(jnp.int32)
    cnt_ref[...] = running + _dot(chosen_b, jnp.ones((tr, LANES), BF16))


def _route_call(conv, attn, hn, woc, woa, g1, b1, wsg, wsu, wsd, wrh, wrl, rbias):
    n, D = hn.shape
    tr = _pick_tile(n, (768, 512, 256))

    def row_spec(w):
        return pl.BlockSpec((tr, w), lambda i: (i, 0))

    def col_spec(r):
        return pl.BlockSpec((r, tr), lambda i: (0, i))

    def full(a):
        return pl.BlockSpec(a.shape, lambda i: (0,) * a.ndim)

    consts = [woc, woa, g1, b1, wsg, wsu, wsd, wrh, wrl, rbias]
    return pl.pallas_call(
        _route_kernel,
        grid=(n // tr,),
        in_specs=[row_spec(CONV_CH), row_spec(ATTN_W), row_spec(D)] + [full(a) for a in consts],
        out_specs=[row_spec(D // 2), row_spec(D), row_spec(LANES), col_spec(TOP_K), col_spec(TOP_K),
                   pl.BlockSpec((N_EXPERTS, LANES), lambda i: (0, 0))],
        out_shape=[jax.ShapeDtypeStruct((n, D // 2), jnp.int32),
                   jax.ShapeDtypeStruct((n, D), F32),
                   jax.ShapeDtypeStruct((n, LANES), F32),
                   jax.ShapeDtypeStruct((TOP_K, n), jnp.int32),
                   jax.ShapeDtypeStruct((TOP_K, n), jnp.int32),
                   jax.ShapeDtypeStruct((N_EXPERTS, LANES), F32)],
        compiler_params=pltpu.CompilerParams(
            dimension_semantics=("arbitrary",), vmem_limit_bytes=VMEM_LIMIT),
        name="route",
    )(conv, attn, hn, *consts)


def _plan_kernel(cnt_ref, ek_ref, rk_ref, slot_ref, blk_ref, *, n_blocks):
    tr = ek_ref.shape[1]
    counts = cnt_ref[...]
    padded = jnp.ceil(counts / EXPERT_BLOCK) * EXPERT_BLOCK
    starts = [jnp.zeros((1, LANES), F32)]
    for e in range(1, N_EXPERTS):
        starts.append(starts[-1] + padded[e - 1:e])
    ek = ek_ref[...]
    seg = jnp.zeros(ek.shape, F32)
    for e in range(N_EXPERTS):
        seg = jnp.where(ek == e, jnp.concatenate([starts[e]] * (tr // LANES), axis=1), seg)
    slot_ref[...] = seg.astype(jnp.int32) + rk_ref[...]

    ends = jnp.concatenate(starts, axis=0) + padded
    w = blk_ref.shape[1]
    blk_start = lax.broadcasted_iota(jnp.int32, (N_EXPERTS, w), 1).astype(F32) * EXPERT_BLOCK
    done = jnp.where(jnp.concatenate([ends] * (w // LANES), axis=1) <= blk_start, 1.0, 0.0)
    owner = jnp.minimum(jnp.sum(done, axis=0, keepdims=True), N_EXPERTS - 1.0)
    used = jnp.concatenate([ends[N_EXPERTS - 1:] * (1.0 / EXPERT_BLOCK)] * (w // LANES), axis=1)
    lane = lax.broadcasted_iota(jnp.int32, (1, w), 1)
    table = jnp.where(lane == n_blocks, used, owner)
    blk_ref[...] = jnp.broadcast_to(table, blk_ref.shape).astype(jnp.int32)


def _plan_call(cnt, ek, rk, n_blocks):
    k, n = ek.shape
    tr = _pick_tile(n, (768, 512, 256))
    w = -(-(n_blocks + 1) // LANES) * LANES
    col = pl.BlockSpec((k, tr), lambda i: (0, i))
    return pl.pallas_call(
        functools.partial(_plan_kernel, n_blocks=n_blocks),
        grid=(n // tr,),
        in_specs=[pl.BlockSpec(cnt.shape, lambda i: (0, 0)), col, col],
        out_specs=[col, pl.BlockSpec((SUBLANES, w), lambda i: (0, 0))],
        out_shape=[jax.ShapeDtypeStruct((k, n), jnp.int32), jax.ShapeDtypeStruct((SUBLANES, w), jnp.int32)],
        compiler_params=pltpu.CompilerParams(dimension_semantics=("arbitrary",)),
        name="plan",
    )(cnt, ek, rk)


def _sc_workers():
    info = plsc.get_sparse_core_info()
    return info.num_cores, info.num_subcores


def _sc_scatter_rows(src, slots, n_out):
    n_src, d = src.shape
    units = slots.shape[0]
    n_chunks = n_src // SC_ROWS
    n_cores, n_sub = _sc_workers()
    n_workers = n_cores * n_sub
    assert slots.shape[1] == SC_ROWS

    def body(src_hbm, slot_hbm, out_hbm, idx_v, rows_v):
        wid = lax.axis_index("s") * n_cores + lax.axis_index("c")

        @pl.loop(0, -(-units // n_workers))
        def _(j):
            u = j * n_workers + wid

            @pl.when(u < units)
            def _():
                pltpu.sync_copy(slot_hbm.at[pl.ds(u, 1)], idx_v)
                pltpu.sync_copy(src_hbm.at[pl.ds((u % n_chunks) * SC_ROWS, SC_ROWS)], rows_v)
                pltpu.sync_copy(rows_v, out_hbm.at[idx_v.at[0]])

    return pl.kernel(
        body, out_type=jax.ShapeDtypeStruct((n_out, d), src.dtype),
        mesh=plsc.VectorSubcoreMesh(core_axis_name="c", subcore_axis_name="s"),
        scratch_types=[pltpu.VMEM((1, SC_ROWS), jnp.int32), pltpu.VMEM((SC_ROWS, d), src.dtype)],
        name="dispatch_rows",
    )(src, slots)


def _sc_gather_rows(table, slots):
    units = slots.shape[0]
    d = table.shape[1]
    n_cores, n_sub = _sc_workers()
    n_workers = n_cores * n_sub
    assert slots.shape[1] == SC_ROWS

    def body(table_hbm, slot_hbm, out_hbm, idx_v, rows_v):
        wid = lax.axis_index("s") * n_cores + lax.axis_index("c")

        @pl.loop(0, -(-units // n_workers))
        def _(j):
            u = j * n_workers + wid

            @pl.when(u < units)
            def _():
                pltpu.sync_copy(slot_hbm.at[pl.ds(u, 1)], idx_v)
                pltpu.sync_copy(table_hbm.at[idx_v.at[0]], rows_v)
                pltpu.sync_copy(rows_v, out_hbm.at[pl.ds(u * SC_ROWS, SC_ROWS)])

    return pl.kernel(
        body, out_type=jax.ShapeDtypeStruct((units * SC_ROWS, d), table.dtype),
        mesh=plsc.VectorSubcoreMesh(core_axis_name="c", subcore_axis_name="s"),
        scratch_types=[pltpu.VMEM((1, SC_ROWS), jnp.int32), pltpu.VMEM((SC_ROWS, d), table.dtype)],
        name="collect_rows",
    )(table, slots)


def _expert_kernel(blk_ref, xs_ref, wg_ref, wu_ref, wd_ref, ys_ref, wg_s, wu_s, wd_s):
    b = pl.program_id(0)
    n_blocks = pl.num_programs(0)
    in_use = b < blk_ref[n_blocks]

    @pl.when(jnp.logical_and(in_use, jnp.logical_or(b == 0, blk_ref[b] != blk_ref[jnp.maximum(b - 1, 0)])))
    def _():
        wg_s[...] = wg_ref[0].astype(BF16)
        wu_s[...] = wu_ref[0].astype(BF16)
        wd_s[...] = wd_ref[0].astype(BF16)

    @pl.when(in_use)
    def _():
        x = _unpack_bf16_pairs(xs_ref[...]).astype(BF16)
        hdn = (jax.nn.silu(_dot(x, wg_s[...])) * _dot(x, wu_s[...])).astype(BF16)
        ys_ref[...] = _pack_bf16_pairs(_dot(hdn, wd_s[...]))


def _expert_call(blk_exp, xs, w_gate, w_up, w_down):
    p, half = xs.shape
    _, D, d_exp = w_gate.shape
    n_blocks = p // EXPERT_BLOCK
    assert blk_exp.shape == (n_blocks + 1,)

    def row_block(b, blk):
        return (jnp.minimum(b, blk[n_blocks] - 1), 0)

    rows = pl.BlockSpec((EXPERT_BLOCK, half), row_block)
    grid_spec = pltpu.PrefetchScalarGridSpec(
        num_scalar_prefetch=1,
        grid=(n_blocks,),
        in_specs=[rows,
                  pl.BlockSpec((1, D, d_exp), lambda b, blk: (blk[b], 0, 0)),
                  pl.BlockSpec((1, D, d_exp), lambda b, blk: (blk[b], 0, 0)),
                  pl.BlockSpec((1, d_exp, D), lambda b, blk: (blk[b], 0, 0))],
        out_specs=rows,
        scratch_shapes=[pltpu.VMEM((D, d_exp), BF16), pltpu.VMEM((D, d_exp), BF16),
                        pltpu.VMEM((d_exp, D), BF16)],
    )
    return pl.pallas_call(
        _expert_kernel,
        grid_spec=grid_spec,
        out_shape=jax.ShapeDtypeStruct((p, half), jnp.int32),
        compiler_params=pltpu.CompilerParams(
            dimension_semantics=("arbitrary",), vmem_limit_bytes=VMEM_LIMIT),
        name="experts",
    )(blk_exp, xs, w_gate, w_up, w_down)


def _combine_kernel(g_ref, gates_ref, base_ref, g2_ref, b2_ref, o_ref):
    gates = gates_ref[...]
    acc = base_ref[...]
    for k in range(g_ref.shape[0]):
        acc = acc + _unpack_bf16_pairs(g_ref[k]) * gates[:, k:k + 1]
    o_ref[...] = _layer_norm_rows(acc, g2_ref[...], b2_ref[...])


def _combine_call(g, gates, base, g2, b2):
    k, n, half = g.shape
    D = base.shape[1]
    tr = _pick_tile(n, (768, 512, 256))

    def row_spec(w):
        return pl.BlockSpec((tr, w), lambda i: (i, 0))

    vec = pl.BlockSpec((1, D), lambda i: (0, 0))
    return pl.pallas_call(
        _combine_kernel,
        grid=(n // tr,),
        in_specs=[pl.BlockSpec((k, tr, half), lambda i: (0, i, 0)), row_spec(LANES), row_spec(D), vec, vec],
        out_specs=row_spec(D),
        out_shape=jax.ShapeDtypeStruct((n, D), F32),
        compiler_params=pltpu.CompilerParams(
            dimension_semantics=("arbitrary",), vmem_limit_bytes=VMEM_LIMIT),
        name="combine",
    )(g, gates, base, g2, b2)


def _rope_tables(tp):
    pos = jnp.arange(tp, dtype=F32)
    inv = jnp.power(ROPE_THETA, -2.0 * jnp.arange(ROPE_HALF, dtype=F32) / ROPE_DIM)
    ang = pos[:, None] * inv[None, :]
    cos, sin = jnp.cos(ang), jnp.sin(ang)
    zeros = jnp.zeros((tp, HEAD_DIM - ROPE_DIM), F32)
    zh = jnp.zeros((tp, ROPE_HALF), F32)
    c64 = jnp.concatenate([cos, cos, jnp.ones_like(zeros)], axis=1)
    s1_64 = jnp.concatenate([-sin, zh, zeros], axis=1)
    s2_64 = jnp.concatenate([zh, sin, zeros], axis=1)
    rep = LANES // HEAD_DIM
    return (jnp.tile(c64, (1, rep)), jnp.tile(s1_64, (1, rep)), jnp.tile(s2_64, (1, rep)),
            cos.T, sin.T)


def kernel(x, meta_tokens, ln_emb_g, ln_emb_b, w_in, conv_w, conv_b, ln_conv_g, ln_conv_b, ln_kidx_g, ln_kidx_b, w_out, ln1_g, ln1_b, w_router, router_bias, w_gate, w_up, w_down, ws_gate, ws_up, ws_down, ln2_g, ln2_b):
    B, seq, D = x.shape
    assert w_in.shape[0] == DEPTH
    k_sel = min(INDEX_TOPK, seq // 4)
    t_real = N_META + seq
    tp = -(-t_real // SEQ_ALIGN) * SEQ_ALIGN

    def row(a):
        return a.reshape(1, -1).astype(F32)

    w = w_in[0]
    o = 0
    parts = []
    for width in (CONV_CH, CONV_CH, ATTN_W, ATTN_W, ATTN_W, IDX_HEADS * IDX_DIM, IDX_DIM, IDX_HEADS):
        parts.append(w[:, o:o + width])
        o += width
    wa, wgl, wq, wk, wv, wqi, wki, wwi = parts
    wwi_t = jnp.concatenate([wwi.T, jnp.zeros((BF16_SUBLANES - IDX_HEADS, D), w.dtype)], axis=0)
    weights = (jnp.concatenate([wa, wgl], axis=1).astype(BF16), wq.T.astype(BF16), wk.astype(BF16),
               wv.T.astype(BF16), wqi.T.astype(BF16), jnp.concatenate([wki, wki], axis=1).astype(BF16),
               wwi_t.astype(BF16))

    def twice(a):
        return row(jnp.concatenate([a, a]))

    tabs = _rope_tables(tp)
    inproj_consts = (row(ln_emb_g), row(ln_emb_b), weights, conv_w[0].astype(F32), row(conv_b[0]),
                     row(ln_conv_g[0]), row(ln_conv_b[0]), twice(ln_kidx_g[0]), twice(ln_kidx_b[0]))
    wr_t = w_router[0].T.astype(F32)
    wr_hi = wr_t.astype(BF16)
    wr_lo = (wr_t - wr_hi.astype(F32)).astype(BF16)
    route_consts = (w_out[0][:CONV_CH].astype(BF16), w_out[0][CONV_CH:].astype(BF16), row(ln1_g[0]),
                    row(ln1_b[0]), ws_gate[0].astype(BF16), ws_up[0].astype(BF16), ws_down[0].astype(BF16),
                    wr_hi, wr_lo, router_bias[0].reshape(-1, 1).astype(F32))
    meta = meta_tokens.astype(x.dtype)
    tail = jnp.zeros((tp - t_real, D), x.dtype)
    n_blocks = tp * TOP_K // EXPERT_BLOCK + N_EXPERTS

    outs = []
    for b in range(B):
        hp = jnp.concatenate([meta, x[b], tail], axis=0)[None]
        hn, conv, qt, k, vt, qit, ki, wit = _inproj_call(hp, tabs, *inproj_consts)
        attn = _dsa_call(qt, qit, wit, k, ki, vt, k_sel)
        xp, base, gates, ek, rk, cnt = _route_call(conv[0], attn[0], hn[0], *route_consts)

        slot, blk = _plan_call(cnt, ek, rk, n_blocks)
        slots = slot.reshape(TOP_K * tp // SC_ROWS, SC_ROWS)
        xs = _sc_scatter_rows(xp, slots, n_blocks * EXPERT_BLOCK)
        ys = _expert_call(blk[0, :n_blocks + 1], xs, w_gate[0], w_up[0], w_down[0])
        picked = _sc_gather_rows(ys, slots).reshape(TOP_K, tp, D // 2)
        out = _combine_call(picked, gates, base, row(ln2_g[0]), row(ln2_b[0]))
        outs.append(out[N_META:t_real])
    return jnp.stack(outs)
```

```python
import functools

import numpy as np
import jax
import jax.numpy as jnp
from jax import lax
from jax.experimental import pallas as pl
from jax.experimental.pallas import tpu as pltpu
from jax.experimental.pallas import tpu_sc as plsc

N_META = 16
CONV_CH = 512
CONV_WIDTH = 31
N_HEADS = 8
HEAD_DIM = 64
ATTN_W = N_HEADS * HEAD_DIM
IDX_HEADS = 8
IDX_DIM = 64
INDEX_TOPK = 256
ROPE_DIM = HEAD_DIM // 4
ROPE_HALF = ROPE_DIM // 2
ROPE_THETA = 500000.0
N_EXPERTS = 64
TOP_K = 8
N_GROUPS = 8
GROUP_SIZE = N_EXPERTS // N_GROUPS
TOPK_GROUPS = 4
ROUTED_SCALE = 2.5
LN_EPS = 1e-5
DEPTH = 1
DN_ALPHA = (2.0 * DEPTH) ** 0.25

LANES = 128
Q_TILE = 256
SUBLANES = 8
BF16_SUBLANES = 16
MOE_EXPERTS_PER_STEP = 2
EXPERT_BLOCK = 512
SC_ROWS = 128
K_CHUNK = 256
K_SUB = 128
SEQ_ALIGN = 256
CONV_HALO = 32
VMEM_LIMIT = 56 * 1024 * 1024

F32 = jnp.float32
BF16 = jnp.bfloat16
COARSE = jnp.bfloat16
NEG_INF = float("-inf")
INT_MIN = -2 ** 31
KEY_NEG_INF = -2139095041
LOG2_E = 1.4426950408889634


def _dot(a, b):
    return jnp.dot(a, b, preferred_element_type=F32)


def _dot_nt(a, b):
    return lax.dot_general(a, b, (((1,), (1,)), ((), ())), preferred_element_type=F32)


def _layer_norm_rows(x, g, b):
    mu = jnp.mean(x, axis=-1, keepdims=True)
    xc = x - mu
    var = jnp.mean(xc * xc, axis=-1, keepdims=True)
    return xc * lax.rsqrt(var + LN_EPS) * g + b


def _pick_tile(n, candidates):
    for c in candidates:
        if n % c == 0:
            return c
    raise ValueError(f"no tile for {n}")


def _rope_rows(x, c_tab, s1_tab, s2_tab):
    outs = []
    for j in range(x.shape[1] // LANES):
        xs = x[:, j * LANES:(j + 1) * LANES]
        up = pltpu.roll(xs, LANES - ROPE_HALF, axis=1)
        dn = pltpu.roll(xs, ROPE_HALF, axis=1)
        outs.append(xs * c_tab + up * s1_tab + dn * s2_tab)
    return jnp.concatenate(outs, axis=1)


def _rope_cols(xt, cos_t, sin_t, heads):
    r = xt.shape[1]
    x3 = xt.reshape(heads, HEAD_DIM, r)
    x1 = x3[:, 0:ROPE_HALF, :]
    x2 = x3[:, ROPE_HALF:ROPE_DIM, :]
    n1 = x1 * cos_t - x2 * sin_t
    n2 = x2 * cos_t + x1 * sin_t
    out = jnp.concatenate([n1, n2, x3[:, ROPE_DIM:, :]], axis=1)
    return out.reshape(heads * HEAD_DIM, r)


def _inproj_kernel(x_ref, meta_ref, ctab_ref, s1tab_ref, s2tab_ref, cost_ref, sint_ref,
                   lng_ref, lnb_ref, wag_ref, wqt_ref, wk_ref, wvt_ref, wqit_ref, wki_ref, wwit_ref,
                   cw_ref, cb_ref, lncg_ref, lncb_ref, lnkg_ref, lnkb_ref,
                   hn_ref, conv_ref, qt_ref, k_ref, vt_ref, qit_ref, ki_ref, wit_ref,
                   ubuf_ref):
    t = pl.program_id(1)
    tr = x_ref.shape[1]
    n_meta = meta_ref.shape[0]
    first = jnp.concatenate([jnp.zeros((tr - n_meta, x_ref.shape[2]), F32), meta_ref[...]], axis=0)
    h = jnp.where(t == 0, first, x_ref[0])

    hn = _layer_norm_rows(h, lng_ref[...], lnb_ref[...])
    hn_ref[0] = hn
    xb = hn.astype(BF16)

    ag = _dot(xb, wag_ref[...])
    u = ag[:, :CONV_CH] * jax.nn.sigmoid(ag[:, CONV_CH:])
    row = lax.broadcasted_iota(jnp.int32, (tr, CONV_CH), 0)
    u = jnp.where(jnp.logical_or(t > 0, row >= tr - n_meta), u, 0.0)

    @pl.when(t == 0)
    def _():
        ubuf_ref[0:CONV_HALO, :] = jnp.zeros((CONV_HALO, CONV_CH), F32)

    ubuf_ref[CONV_HALO:CONV_HALO + tr, :] = u
    base = CONV_HALO - (CONV_WIDTH - 1)
    acc = jnp.zeros((tr, CONV_CH), F32)
    for j in range(CONV_WIDTH):
        acc = acc + cw_ref[j:j + 1, :] * ubuf_ref[base + j:base + j + tr, :]
    ubuf_ref[0:CONV_HALO, :] = ubuf_ref[tr:tr + CONV_HALO, :]
    c = _layer_norm_rows(acc + cb_ref[...], lncg_ref[...], lncb_ref[...])
    conv_ref[0] = (c * jax.nn.sigmoid(c)).astype(conv_ref.dtype)

    ctab, s1tab, s2tab = ctab_ref[...], s1tab_ref[...], s2tab_ref[...]
    cos_t, sin_t = cost_ref[...], sint_ref[...]

    qt = _rope_cols(_dot_nt(wqt_ref[...], xb), cos_t, sin_t, N_HEADS)
    qt_ref[0] = (qt * (HEAD_DIM ** -0.5 * LOG2_E)).astype(qt_ref.dtype)
    k = _rope_rows(_dot(xb, wk_ref[...]), ctab, s1tab, s2tab)
    k_ref[0] = k.astype(k_ref.dtype)
    vt_ref[0] = _dot_nt(wvt_ref[...], xb).astype(vt_ref.dtype)

    qit = _rope_cols(_dot_nt(wqit_ref[...], xb), cos_t, sin_t, IDX_HEADS)
    qit_ref[0] = qit.astype(qit_ref.dtype)
    ki = _layer_norm_rows(_dot(xb, wki_ref[...]), lnkg_ref[...], lnkb_ref[...])
    ki_ref[0] = _rope_rows(ki, ctab, s1tab, s2tab).astype(ki_ref.dtype)
    wit = _dot_nt(wwit_ref[...], xb) * (IDX_HEADS ** -0.5)
    wit_ref[0] = wit[:IDX_HEADS]


def _inproj_call(x, batch, meta, tabs, ln_g, ln_b, weights, conv_w, conv_b, lnc_g, lnc_b, lnk_g, lnk_b):
    _, seq, D = x.shape
    B = 1
    tr = SEQ_ALIGN
    tp = seq + tr
    nt = tp // tr
    ctab, s1tab, s2tab, cos_t, sin_t = tabs

    def row_spec(w):
        return pl.BlockSpec((1, tr, w), lambda b, t: (b, t, 0))

    def col_spec(r):
        return pl.BlockSpec((1, r, tr), lambda b, t: (b, 0, t))

    def full(a):
        return pl.BlockSpec(a.shape, lambda b, t: (0,) * a.ndim)

    tab_row = pl.BlockSpec((tr, LANES), lambda b, t: (t, 0))
    tab_col = pl.BlockSpec((ROPE_HALF, tr), lambda b, t: (0, t))
    consts = [ln_g, ln_b, *weights, conv_w, conv_b, lnc_g, lnc_b, lnk_g, lnk_b]
    out_shape = [
        jax.ShapeDtypeStruct((B, tp, D), F32),
        jax.ShapeDtypeStruct((B, tp, CONV_CH), BF16),
        jax.ShapeDtypeStruct((B, ATTN_W, tp), BF16),
        jax.ShapeDtypeStruct((B, tp, ATTN_W), BF16),
        jax.ShapeDtypeStruct((B, ATTN_W, tp), BF16),
        jax.ShapeDtypeStruct((B, IDX_HEADS * IDX_DIM, tp), BF16),
        jax.ShapeDtypeStruct((B, tp, 2 * IDX_DIM), BF16),
        jax.ShapeDtypeStruct((B, IDX_HEADS, tp), F32),
    ]
    out_specs = [row_spec(D), row_spec(CONV_CH), col_spec(ATTN_W), row_spec(ATTN_W), col_spec(ATTN_W),
                 col_spec(IDX_HEADS * IDX_DIM), row_spec(2 * IDX_DIM), col_spec(IDX_HEADS)]
    return pl.pallas_call(
        _inproj_kernel,
        grid=(B, nt),
        in_specs=[pl.BlockSpec((1, tr, D), lambda b, t: (batch, jnp.maximum(t - 1, 0), 0)), full(meta),
                  tab_row, tab_row, tab_row, tab_col, tab_col] + [full(a) for a in consts],
        out_specs=out_specs,
        out_shape=out_shape,
        scratch_shapes=[pltpu.VMEM((CONV_HALO + tr, CONV_CH), F32)],
        compiler_params=pltpu.CompilerParams(
            dimension_semantics=("arbitrary", "arbitrary"), vmem_limit_bytes=VMEM_LIMIT),
        name="inproj",
    )(x, meta, ctab, s1tab, s2tab, cos_t, sin_t, *consts)


def _key_to_float(key):
    bits = jnp.where(key >= 0, key, key ^ jnp.int32(0x7FFFFFFF))
    f = pltpu.bitcast(bits, F32)
    return jnp.where(key < jnp.int32(KEY_NEG_INF), NEG_INF, f)


def _tree(parts, op):
    parts = list(parts)
    while len(parts) > 1:
        nxt = [op(parts[j], parts[j + 1]) for j in range(0, len(parts) - 1, 2)]
        if len(parts) % 2:
            nxt.append(parts[-1])
        parts = nxt
    return parts[0]


def _fold_rows(x, rows, op):
    return _tree([x[j * rows:(j + 1) * rows] for j in range(x.shape[0] // rows)], op)


def _dsa_kernel(qt_ref, qit_ref, wit_ref, k_ref, ki_ref, vt_ref, o_ref,
                sc_ref, sh_ref, qm_ref, qim_ref, m_ref, alpha_ref, shift_ref, lg_ref, acc_ref, *, k_sel, n_pad):
    i = pl.program_id(1)
    tq = qt_ref.shape[2]
    n_chunks = (i * tq + tq + K_CHUNK - 1) // K_CHUNK
    v_rows = HEAD_DIM + BF16_SUBLANES

    def causal_mask(k0, rows=K_CHUNK):
        kpos = k0 + lax.broadcasted_iota(jnp.int32, (rows, tq), 0)
        return kpos <= i * tq + lax.broadcasted_iota(jnp.int32, (rows, tq), 1)

    def rows8(x):
        return jnp.broadcast_to(x, (SUBLANES, tq))

    def tiles(x):
        return x.reshape(x.shape[0] // SUBLANES, SUBLANES, tq)

    def head_slab(ref, h):
        slab = ref[0, (h // 2) * LANES:(h // 2 + 1) * LANES, :]
        zeros = jnp.zeros((HEAD_DIM, tq), slab.dtype)
        if h % 2 == 0:
            return jnp.concatenate([slab[:HEAD_DIM], zeros], axis=0)
        return jnp.concatenate([zeros, slab[HEAD_DIM:]], axis=0)

    for h in range(N_HEADS):
        qm_ref[h] = head_slab(qt_ref, h)
    for h in range(IDX_HEADS):
        qim_ref[h] = head_slab(qit_ref, h)
    wit = wit_ref[0]
    w_heads = [rows8(wit[h:h + 1]) for h in range(IDX_HEADS)]

    def score_body(c, carry):
        for s in range(K_CHUNK // K_SUB):
            k0 = pl.multiple_of(c * K_CHUNK + s * K_SUB, K_SUB)
            kic = ki_ref[0, pl.ds(k0, K_SUB), :]
            acc = jnp.zeros((K_SUB // SUBLANES, SUBLANES, tq), F32)
            for h in range(IDX_HEADS):
                acc = acc + w_heads[h][None] * jnp.maximum(tiles(_dot(kic, qim_ref[h])), 0.0)
            acc = (acc * (IDX_DIM ** -0.5)).reshape(K_SUB, tq)
            acc = jnp.where(causal_mask(k0, K_SUB), acc, NEG_INF)
            sc_ref[pl.ds(k0, K_SUB), :] = acc
            hi_bits = pltpu.bitcast(acc, jnp.int32) & jnp.int32(-65536)
            sh_ref[pl.ds(k0, K_SUB), :] = pltpu.bitcast(hi_bits, F32).astype(COARSE)
        return carry

    lax.fori_loop(0, n_chunks, score_body, 0)

    def drop_padding_keys():
        sh_ref[0:n_pad, :] = jnp.full((n_pad, tq), NEG_INF, COARSE)

    sc_ref[0:n_pad, :] = jnp.full((n_pad, tq), NEG_INF, F32)
    drop_padding_keys()

    def count_all(ref, thr_tile, preds):
        rows = thr_tile.shape[0]
        one, zero = jnp.ones((), ref.dtype), jnp.zeros((), ref.dtype)

        def body(c, cnts):
            k0 = pl.multiple_of(c * K_CHUNK, K_CHUNK)
            s = ref[pl.ds(k0, K_CHUNK), :]
            out = []
            for cnt, p in zip(cnts, preds):
                hits = [jnp.where(p(s[j * rows:(j + 1) * rows], thr_tile), one, zero)
                        for j in range(K_CHUNK // rows)]
                out.append(cnt + _tree(hits, jnp.add).astype(F32))
            return tuple(out)

        init = tuple(jnp.zeros((rows, tq), F32) for _ in preds)
        cnts = lax.fori_loop(0, n_chunks, body, init)
        return [rows8(jnp.sum(cnt, axis=0, keepdims=True)) for cnt in cnts]

    def search_body(it, tkey, coarse):
        cand = tkey + lax.shift_left(jnp.int32(1), 31 - it)
        cf = _key_to_float(cand)
        if coarse:
            cf = pltpu.bitcast(pltpu.bitcast(cf, jnp.int32) & jnp.int32(-65536), F32).astype(COARSE)
            cf = jnp.concatenate([cf] * (BF16_SUBLANES // SUBLANES), axis=0)
        cnt, = count_all(sh_ref if coarse else sc_ref, cf, [lambda s, t: s >= t])
        return jnp.where(cnt >= k_sel, cand, tkey)

    tkey = jnp.full((SUBLANES, tq), INT_MIN, jnp.int32)
    tkey = lax.fori_loop(0, 16, functools.partial(search_body, coarse=True), tkey)
    tkey = lax.fori_loop(16, 32, functools.partial(search_body, coarse=False), tkey)
    thr = _key_to_float(tkey)

    n_ge, n_gt = count_all(sc_ref, thr, [lambda s, t: s >= t, lambda s, t: s > t])
    need = k_sel - n_gt
    has_ties = jnp.max(n_ge) > k_sel

    @pl.when(jnp.logical_not(has_ties))
    def _():
        def body(c, carry):
            k0 = pl.multiple_of(c * K_CHUNK, K_CHUNK)
            s = sc_ref[pl.ds(k0, K_CHUNK), :]
            bias = jnp.where(tiles(s) >= thr[None], 0.0, NEG_INF).reshape(K_CHUNK, tq)
            sh_ref[pl.ds(k0, K_CHUNK), :] = jnp.where(causal_mask(k0), bias, NEG_INF).astype(COARSE)
            return carry
        lax.fori_loop(0, n_chunks, body, 0)

    @pl.when(has_ties)
    def _():
        r_i = lax.broadcasted_iota(jnp.int32, (K_CHUNK, K_CHUNK), 0)
        c_i = lax.broadcasted_iota(jnp.int32, (K_CHUNK, K_CHUNK), 1)
        lower = jnp.where(c_i <= r_i, 1.0, 0.0).astype(BF16)
        thr_row, need_row = thr[0:1], need[0:1]

        def body(c, seen):
            k0 = pl.multiple_of(c * K_CHUNK, K_CHUNK)
            s = sc_ref[pl.ds(k0, K_CHUNK), :]
            eq = jnp.where(s == thr_row, 1.0, 0.0)
            rank = _dot(lower, eq.astype(BF16)) + seen
            keep_tie = jnp.where(rank <= need_row, eq, 0.0)
            sel = jnp.where(s > thr_row, 1.0, keep_tie)
            sh_ref[pl.ds(k0, K_CHUNK), :] = jnp.where(
                sel > 0.0, jnp.where(causal_mask(k0), 0.0, NEG_INF), NEG_INF).astype(COARSE)
            return rank[K_CHUNK - 1:K_CHUNK, :]
        lax.fori_loop(0, n_chunks, body, jnp.zeros((1, tq), F32))

    drop_padding_keys()

    m_ref[...] = jnp.full(m_ref.shape, NEG_INF, F32)
    acc_ref[...] = jnp.zeros(acc_ref.shape, F32)
    ones_rows = jnp.ones((BF16_SUBLANES, K_CHUNK), BF16)

    def logits_stage(c, slot):
        k0 = pl.multiple_of(c * K_CHUNK, K_CHUNK)
        for h in range(N_HEADS):
            pair = slice((h // 2) * LANES, (h // 2 + 1) * LANES)
            cmax = []
            for s in range(K_CHUNK // K_SUB):
                rows = pl.ds(pl.multiple_of(k0 + s * K_SUB, K_SUB), K_SUB)
                lg = _dot(k_ref[0, rows, pair], qm_ref[h]).astype(BF16) + sh_ref[rows, :].astype(BF16)
                lg_ref[slot, h, s * K_SUB:(s + 1) * K_SUB, :] = lg
                cmax.append(_fold_rows(lg, BF16_SUBLANES, jnp.maximum))
            cmax = _tree(cmax, jnp.maximum).astype(F32)
            m_old = m_ref[h]
            m_new = jnp.maximum(m_old, rows8(jnp.max(cmax, axis=0, keepdims=True)))
            m_safe = jnp.where(m_new == NEG_INF, 0.0, m_new)
            alpha_ref[slot, h] = jnp.exp2(m_old - m_safe)
            shift_ref[slot, h] = m_safe
            m_ref[h] = m_new

    def values_stage(c, slot):
        k0 = pl.multiple_of(c * K_CHUNK, K_CHUNK)
        for h in range(N_HEADS):
            m_safe = shift_ref[slot, h].astype(BF16)
            m_tile = jnp.concatenate([m_safe] * (BF16_SUBLANES // SUBLANES), axis=0)
            lg = lg_ref[slot, h].reshape(K_CHUNK // BF16_SUBLANES, BF16_SUBLANES, tq)
            p = jnp.exp2(lg - m_tile[None]).reshape(K_CHUNK, tq)
            v_aug = jnp.concatenate(
                [vt_ref[0, h * HEAD_DIM:(h + 1) * HEAD_DIM, pl.ds(k0, K_CHUNK)], ones_rows], axis=0)
            pv = _dot(v_aug, p)
            acc = acc_ref[h].reshape(v_rows // SUBLANES, SUBLANES, tq) * alpha_ref[slot, h][None]
            acc_ref[h] = acc.reshape(v_rows, tq) + pv

    def attn_body(c, carry):
        logits_stage(c, 0)
        values_stage(c, 0)
        return carry

    lax.fori_loop(0, n_chunks, attn_body, 0)

    for pair in range(N_HEADS // 2):
        halves = []
        for h in (2 * pair, 2 * pair + 1):
            a = acc_ref[h]
            total = a[HEAD_DIM:HEAD_DIM + 1]
            halves.append(a[:HEAD_DIM] / jnp.where(total == 0.0, 1.0, total))
        out_t = jnp.concatenate(halves, axis=0)
        o_ref[0, :, pair * LANES:(pair + 1) * LANES] = out_t.T.astype(o_ref.dtype)


def _dsa_call(qt, qit, wit, k, ki, vt, k_sel, n_pad):
    B, tp, _ = k.shape
    nq = tp // Q_TILE

    def q_cols(r):
        return pl.BlockSpec((1, r, Q_TILE), lambda b, i: (b, 0, i))

    def per_batch(a):
        return pl.BlockSpec((1,) + a.shape[1:], lambda b, i: (b, 0, 0), pipeline_mode=pl.Buffered(1))

    return pl.pallas_call(
        functools.partial(_dsa_kernel, k_sel=k_sel, n_pad=n_pad),
        grid=(B, nq),
        in_specs=[q_cols(ATTN_W), q_cols(IDX_HEADS * IDX_DIM), q_cols(IDX_HEADS),
                  per_batch(k), per_batch(ki), per_batch(vt)],
        out_specs=pl.BlockSpec((1, Q_TILE, ATTN_W), lambda b, i: (b, i, 0)),
        out_shape=jax.ShapeDtypeStruct((B, tp, ATTN_W), BF16),
        scratch_shapes=[pltpu.VMEM((tp, Q_TILE), F32),
                        pltpu.VMEM((tp, Q_TILE), COARSE),
                        pltpu.VMEM((N_HEADS, LANES, Q_TILE), BF16),
                        pltpu.VMEM((IDX_HEADS, LANES, Q_TILE), BF16),
                        pltpu.VMEM((N_HEADS, SUBLANES, Q_TILE), F32),
                        pltpu.VMEM((1, N_HEADS, SUBLANES, Q_TILE), F32),
                        pltpu.VMEM((1, N_HEADS, SUBLANES, Q_TILE), F32),
                        pltpu.VMEM((1, N_HEADS, K_CHUNK, Q_TILE), BF16),
                        pltpu.VMEM((N_HEADS, HEAD_DIM + BF16_SUBLANES, Q_TILE), F32)],
        compiler_params=pltpu.CompilerParams(
            dimension_semantics=("arbitrary", "arbitrary"), vmem_limit_bytes=VMEM_LIMIT),
        name="dsa",
    )(qt, qit, wit, k, ki, vt)


def _max_all(x):
    return jnp.max(jnp.max(x, axis=1, keepdims=True), axis=0, keepdims=True)


def _router_gates(logits_t, rbias):
    r = logits_t.shape[1]
    shape3 = (N_GROUPS, GROUP_SIZE, r)
    scores = jax.nn.sigmoid(logits_t).reshape(shape3)
    biased = scores + rbias.reshape(N_GROUPS, GROUP_SIZE, 1)
    in_grp = lax.broadcasted_iota(jnp.int32, shape3, 1).astype(F32)
    m1 = jnp.max(biased, axis=1, keepdims=True)
    first = jnp.min(jnp.where(biased == m1, in_grp, float(GROUP_SIZE)), axis=1, keepdims=True)
    m2 = jnp.max(jnp.where(in_grp == first, NEG_INF, biased), axis=1, keepdims=True)
    cur = m1 + m2

    grp_idx = lax.broadcasted_iota(jnp.int32, (N_GROUPS, 1, r), 0).astype(F32)
    grp_sel = jnp.zeros((N_GROUPS, 1, r), F32)
    for _ in range(TOPK_GROUPS):
        m = jnp.max(cur, axis=0, keepdims=True)
        pick = grp_idx == jnp.min(jnp.where(cur == m, grp_idx, float(N_GROUPS)), axis=0, keepdims=True)
        grp_sel = jnp.where(pick, 1.0, grp_sel)
        cur = jnp.where(pick, NEG_INF, cur)

    cur = jnp.where(jnp.broadcast_to(grp_sel, shape3) > 0.0, biased, NEG_INF)
    exp_idx = lax.broadcasted_iota(jnp.int32, shape3, 0).astype(F32) * GROUP_SIZE + in_grp
    chosen = jnp.zeros(shape3, F32)
    for _ in range(TOP_K):
        m = _max_all(cur)
        first = -_max_all(-jnp.where(cur == m, exp_idx, float(N_EXPERTS)))
        pick = exp_idx == first
        chosen = jnp.where(pick, 1.0, chosen)
        cur = jnp.where(pick, NEG_INF, cur)

    w = jnp.where(chosen > 0.0, scores, 0.0)
    denom = jnp.sum(jnp.sum(w, axis=1, keepdims=True), axis=0, keepdims=True)
    return (w / denom * ROUTED_SCALE).reshape(N_EXPERTS, r)


def _post_kernel(conv_ref, attn_ref, hn_ref, woc_ref, woa_ref, g1_ref, b1_ref,
                 wsg_ref, wsu_ref, wsd_ref, wrh_ref, wrl_ref, rb_ref,
                 xb_ref, base_ref, gates_ref):
    mix = _dot(conv_ref[...], woc_ref[...]) + _dot(attn_ref[...], woa_ref[...])
    h1 = _layer_norm_rows(DN_ALPHA * hn_ref[...] + mix, g1_ref[...], b1_ref[...])
    xb = h1.astype(BF16)
    xb_ref[...] = xb

    shared = jax.nn.silu(_dot(xb, wsg_ref[...])) * _dot(xb, wsu_ref[...])
    base_ref[...] = DN_ALPHA * h1 + _dot(shared.astype(BF16), wsd_ref[...])

    x_lo = (h1 - xb.astype(F32)).astype(BF16)
    logits_t = (_dot_nt(wrh_ref[...], xb) + _dot_nt(wrh_ref[...], x_lo) + _dot_nt(wrl_ref[...], xb))
    gates_t = _router_gates(logits_t, rb_ref[...])
    padded = jnp.concatenate([gates_t, jnp.zeros((LANES - N_EXPERTS, gates_t.shape[1]), F32)], axis=0)
    gates_ref[...] = padded.T


def _post_call(conv, attn, hn, woc, woa, g1, b1, wsg, wsu, wsd, wrh, wrl, rbias):
    n, D = hn.shape
    tr = _pick_tile(n, (768, 512, 256))

    def row_spec(w):
        return pl.BlockSpec((tr, w), lambda i: (i, 0))

    def full(a):
        return pl.BlockSpec(a.shape, lambda i: (0,) * a.ndim)

    consts = [woc, woa, g1, b1, wsg, wsu, wsd, wrh, wrl, rbias]
    return pl.pallas_call(
        _post_kernel,
        grid=(n // tr,),
        in_specs=[row_spec(CONV_CH), row_spec(ATTN_W), row_spec(D)] + [full(a) for a in consts],
        out_specs=[row_spec(D), row_spec(D), row_spec(LANES)],
        out_shape=[jax.ShapeDtypeStruct((n, D), BF16), jax.ShapeDtypeStruct((n, D), F32),
                   jax.ShapeDtypeStruct((n, LANES), F32)],
        compiler_params=pltpu.CompilerParams(
            dimension_semantics=("arbitrary",), vmem_limit_bytes=VMEM_LIMIT),
        name="post",
    )(conv, attn, hn, *consts)


def _moe_kernel(xb_ref, base_ref, gates_ref, wg_ref, wu_ref, wd_ref, g2_ref, b2_ref, o_ref):
    s = pl.program_id(1)
    n_per_step, d_exp = wg_ref.shape[0], wg_ref.shape[2]

    @pl.when(s == 0)
    def _():
        o_ref[...] = base_ref[...]

    x = xb_ref[...]
    gates = gates_ref[...]
    lane = lax.broadcasted_iota(jnp.int32, gates.shape, 1)
    hidden = []
    for j in range(n_per_step):
        hg = _dot(x, wg_ref[j].astype(BF16))
        hu = _dot(x, wu_ref[j].astype(BF16))
        gate = jnp.sum(jnp.where(lane == s * n_per_step + j, gates, 0.0), axis=1, keepdims=True)
        hidden.append((jax.nn.silu(hg) * hu * gate).astype(BF16))
    wd = wd_ref[...].astype(BF16).reshape(n_per_step * d_exp, wd_ref.shape[2])
    o_ref[...] += _dot(jnp.concatenate(hidden, axis=1), wd)

    @pl.when(s == pl.num_programs(1) - 1)
    def _():
        o_ref[...] = _layer_norm_rows(o_ref[...], g2_ref[...], b2_ref[...])


def _moe_call(xb, base, gates, w_gate, w_up, w_down, g2, b2):
    n, D = xb.shape
    n_exp, _, d_exp = w_gate.shape
    tm = _pick_tile(n, (1536, 768, 512, 256))
    eps = MOE_EXPERTS_PER_STEP
    assert n_exp % eps == 0

    def row_spec(w, **kw):
        return pl.BlockSpec((tm, w), lambda i, s: (i, 0), **kw)

    once = dict(pipeline_mode=pl.Buffered(1))
    vec = pl.BlockSpec((1, D), lambda i, s: (0, 0))
    return pl.pallas_call(
        _moe_kernel,
        grid=(n // tm, n_exp // eps),
        in_specs=[row_spec(D, **once), row_spec(D, **once), row_spec(LANES, **once),
                  pl.BlockSpec((eps, D, d_exp), lambda i, s: (s, 0, 0)),
                  pl.BlockSpec((eps, D, d_exp), lambda i, s: (s, 0, 0)),
                  pl.BlockSpec((eps, d_exp, D), lambda i, s: (s, 0, 0)),
                  vec, vec],
        out_specs=row_spec(D),
        out_shape=jax.ShapeDtypeStruct((n, D), F32),
        compiler_params=pltpu.CompilerParams(
            dimension_semantics=("arbitrary", "arbitrary"), vmem_limit_bytes=VMEM_LIMIT),
        name="moe",
    )(xb, base, gates, w_gate, w_up, w_down, g2, b2)


def _sum_all(x):
    return jnp.sum(jnp.sum(x, axis=1, keepdims=True), axis=0, keepdims=True)


def _pack_bf16_pairs(x):
    w = x.shape[1] // 2
    lo = pltpu.bitcast(x[:, :w].astype(BF16).astype(F32), jnp.int32)
    hi = pltpu.bitcast(x[:, w:].astype(BF16).astype(F32), jnp.int32)
    return lax.shift_right_logical(lo, 16) | (hi & jnp.int32(-65536))


def _unpack_bf16_pairs(p):
    lo = pltpu.bitcast(lax.shift_left(p, 16), F32)
    hi = pltpu.bitcast(p & jnp.int32(-65536), F32)
    return jnp.concatenate([lo, hi], axis=1)


def _router_topk(logits_t, rbias):
    r = logits_t.shape[1]
    shape3 = (N_GROUPS, GROUP_SIZE, r)
    scores = jax.nn.sigmoid(logits_t).reshape(shape3)
    biased = scores + rbias.reshape(N_GROUPS, GROUP_SIZE, 1)
    in_grp = lax.broadcasted_iota(jnp.int32, shape3, 1).astype(F32)
    m1 = jnp.max(biased, axis=1, keepdims=True)
    first = jnp.min(jnp.where(biased == m1, in_grp, float(GROUP_SIZE)), axis=1, keepdims=True)
    m2 = jnp.max(jnp.where(in_grp == first, NEG_INF, biased), axis=1, keepdims=True)
    cur = m1 + m2

    grp_idx = lax.broadcasted_iota(jnp.int32, (N_GROUPS, 1, r), 0).astype(F32)
    grp_sel = jnp.zeros((N_GROUPS, 1, r), F32)
    for _ in range(TOPK_GROUPS):
        m = jnp.max(cur, axis=0, keepdims=True)
        pick = grp_idx == jnp.min(jnp.where(cur == m, grp_idx, float(N_GROUPS)), axis=0, keepdims=True)
        grp_sel = jnp.where(pick, 1.0, grp_sel)
        cur = jnp.where(pick, NEG_INF, cur)

    cur = jnp.where(jnp.broadcast_to(grp_sel, shape3) > 0.0, biased, NEG_INF)
    exp_idx = lax.broadcasted_iota(jnp.int32, shape3, 0).astype(F32) * GROUP_SIZE + in_grp
    chosen = jnp.zeros(shape3, F32)
    ids, wts = [], []
    for _ in range(TOP_K):
        m = _max_all(cur)
        first = -_max_all(-jnp.where(cur == m, exp_idx, float(N_EXPERTS)))
        pick = exp_idx == first
        chosen = jnp.where(pick, 1.0, chosen)
        cur = jnp.where(pick, NEG_INF, cur)
        ids.append(first.reshape(1, r))
        wts.append(_sum_all(jnp.where(pick, scores, 0.0)).reshape(1, r))
    ids = jnp.concatenate(ids, axis=0)
    wts = jnp.concatenate(wts, axis=0)
    gates = wts / jnp.sum(wts, axis=0, keepdims=True) * ROUTED_SCALE
    return ids, gates, chosen.reshape(N_EXPERTS, r), exp_idx


def _route_kernel(conv_ref, attn_ref, hn_ref, woc_ref, woa_ref, g1_ref, b1_ref,
                  wsg_ref, wsu_ref, wsd_ref, wrh_ref, wrl_ref, rb_ref,
                  xp_ref, base_ref, gates_ref, ek_ref, rk_ref, cnt_ref):
    step = pl.program_id(0)
    tr = hn_ref.shape[0]
    mix = _dot(conv_ref[...], woc_ref[...]) + _dot(attn_ref[...], woa_ref[...])
    h1 = _layer_norm_rows(DN_ALPHA * hn_ref[...] + mix, g1_ref[...], b1_ref[...])
    xb = h1.astype(BF16)
    xp_ref[...] = _pack_bf16_pairs(h1)

    shared = jax.nn.silu(_dot(xb, wsg_ref[...])) * _dot(xb, wsu_ref[...])
    base_ref[...] = DN_ALPHA * h1 + _dot(shared.astype(BF16), wsd_ref[...])

    x_lo = (h1 - xb.astype(F32)).astype(BF16)
    logits_t = (_dot_nt(wrh_ref[...], xb) + _dot_nt(wrh_ref[...], x_lo) + _dot_nt(wrl_ref[...], xb))
    ids, gates, chosen, exp_idx = _router_topk(logits_t, rb_ref[...])
    padded = jnp.concatenate([gates, jnp.zeros((LANES - TOP_K, tr), F32)], axis=0)
    gates_ref[...] = padded.T
    ek_ref[...] = ids.astype(jnp.int32)

    @pl.when(step == 0)
    def _():
        cnt_ref[...] = jnp.zeros(cnt_ref.shape, F32)

    t_i = lax.broadcasted_iota(jnp.int32, (tr, tr), 0)
    t_j = lax.broadcasted_iota(jnp.int32, (tr, tr), 1)
    before = jnp.where(t_i < t_j, 1.0, 0.0).astype(BF16)
    chosen_b = chosen.astype(BF16)
    running = cnt_ref[...]
    rank = _dot(chosen_b, before) + jnp.concatenate([running] * (tr // LANES), axis=1)
    rank3 = rank.reshape(N_GROUPS, GROUP_SIZE, tr)
    rk = [_sum_all(jnp.where(exp_idx == ids[k:k + 1].reshape(1, 1, tr), rank3, 0.0)).reshape(1, tr)
          for k in range(TOP_K)]
    rk_ref[...] = jnp.concatenate(rk, axis=0).astype(jnp.int32)
    cnt_ref[...] = running + _dot(chosen_b, jnp.ones((tr, LANES), BF16))


def _route_call(conv, attn, hn, woc, woa, g1, b1, wsg, wsu, wsd, wrh, wrl, rbias):
    n, D = hn.shape
    tr = _pick_tile(n, (768, 512, 256))

    def row_spec(w):
        return pl.BlockSpec((tr, w), lambda i: (i, 0))

    def col_spec(r):
        return pl.BlockSpec((r, tr), lambda i: (0, i))

    def full(a):
        return pl.BlockSpec(a.shape, lambda i: (0,) * a.ndim)

    consts = [woc, woa, g1, b1, wsg, wsu, wsd, wrh, wrl, rbias]
    return pl.pallas_call(
        _route_kernel,
        grid=(n // tr,),
        in_specs=[row_spec(CONV_CH), row_spec(ATTN_W), row_spec(D)] + [full(a) for a in consts],
        out_specs=[row_spec(D // 2), row_spec(D), row_spec(LANES), col_spec(TOP_K), col_spec(TOP_K),
                   pl.BlockSpec((N_EXPERTS, LANES), lambda i: (0, 0))],
        out_shape=[jax.ShapeDtypeStruct((n, D // 2), jnp.int32),
                   jax.ShapeDtypeStruct((n, D), F32),
                   jax.ShapeDtypeStruct((n, LANES), F32),
                   jax.ShapeDtypeStruct((TOP_K, n), jnp.int32),
                   jax.ShapeDtypeStruct((TOP_K, n), jnp.int32),
                   jax.ShapeDtypeStruct((N_EXPERTS, LANES), F32)],
        compiler_params=pltpu.CompilerParams(
            dimension_semantics=("arbitrary",), vmem_limit_bytes=VMEM_LIMIT),
        name="route",
    )(conv, attn, hn, *consts)


def _plan_kernel(cnt_ref, ek_ref, rk_ref, slot_ref, blk_ref, *, n_blocks):
    tr = ek_ref.shape[1]
    counts = cnt_ref[...]
    padded = jnp.ceil(counts / EXPERT_BLOCK) * EXPERT_BLOCK
    starts = [jnp.zeros((1, LANES), F32)]
    for e in range(1, N_EXPERTS):
        starts.append(starts[-1] + padded[e - 1:e])
    ek = ek_ref[...]
    seg = jnp.zeros(ek.shape, F32)
    for e in range(N_EXPERTS):
        seg = jnp.where(ek == e, jnp.concatenate([starts[e]] * (tr // LANES), axis=1), seg)
    slot_ref[...] = seg.astype(jnp.int32) + rk_ref[...]

    ends = jnp.concatenate(starts, axis=0) + padded
    w = blk_ref.shape[1]
    blk_start = lax.broadcasted_iota(jnp.int32, (N_EXPERTS, w), 1).astype(F32) * EXPERT_BLOCK
    done = jnp.where(jnp.concatenate([ends] * (w // LANES), axis=1) <= blk_start, 1.0, 0.0)
    owner = jnp.minimum(jnp.sum(done, axis=0, keepdims=True), N_EXPERTS - 1.0)
    used = jnp.concatenate([ends[N_EXPERTS - 1:] * (1.0 / EXPERT_BLOCK)] * (w // LANES), axis=1)
    lane = lax.broadcasted_iota(jnp.int32, (1, w), 1)
    table = jnp.where(lane == n_blocks, used, owner)
    blk_ref[...] = jnp.broadcast_to(table, blk_ref.shape).astype(jnp.int32)


def _plan_call(cnt, ek, rk, n_blocks):
    k, n = ek.shape
    tr = _pick_tile(n, (768, 512, 256))
    w = -(-(n_blocks + 1) // LANES) * LANES
    col = pl.BlockSpec((k, tr), lambda i: (0, i))
    return pl.pallas_call(
        functools.partial(_plan_kernel, n_blocks=n_blocks),
        grid=(n // tr,),
        in_specs=[pl.BlockSpec(cnt.shape, lambda i: (0, 0)), col, col],
        out_specs=[col, pl.BlockSpec((SUBLANES, w), lambda i: (0, 0))],
        out_shape=[jax.ShapeDtypeStruct((k, n), jnp.int32), jax.ShapeDtypeStruct((SUBLANES, w), jnp.int32)],
        compiler_params=pltpu.CompilerParams(dimension_semantics=("arbitrary",)),
        name="plan",
    )(cnt, ek, rk)


def _sc_workers():
    info = plsc.get_sparse_core_info()
    return info.num_cores, info.num_subcores


def _sc_scatter_rows(src, slots, n_out):
    n_src, d = src.shape
    units = slots.shape[0]
    n_chunks = n_src // SC_ROWS
    n_cores, n_sub = _sc_workers()
    n_workers = n_cores * n_sub
    assert slots.shape[1] == SC_ROWS

    def body(src_hbm, slot_hbm, out_hbm, idx_v, rows_v):
        wid = lax.axis_index("s") * n_cores + lax.axis_index("c")

        @pl.loop(0, -(-units // n_workers))
        def _(j):
            u = j * n_workers + wid

            @pl.when(u < units)
            def _():
                pltpu.sync_copy(slot_hbm.at[pl.ds(u, 1)], idx_v)
                pltpu.sync_copy(src_hbm.at[pl.ds((u % n_chunks) * SC_ROWS, SC_ROWS)], rows_v)
                pltpu.sync_copy(rows_v, out_hbm.at[idx_v.at[0]])

    return pl.kernel(
        body, out_type=jax.ShapeDtypeStruct((n_out, d), src.dtype),
        mesh=plsc.VectorSubcoreMesh(core_axis_name="c", subcore_axis_name="s"),
        scratch_types=[pltpu.VMEM((1, SC_ROWS), jnp.int32), pltpu.VMEM((SC_ROWS, d), src.dtype)],
        name="dispatch_rows",
    )(src, slots)


def _sc_gather_rows(table, slots):
    units = slots.shape[0]
    d = table.shape[1]
    n_cores, n_sub = _sc_workers()
    n_workers = n_cores * n_sub
    assert slots.shape[1] == SC_ROWS

    def body(table_hbm, slot_hbm, out_hbm, idx_v, rows_v):
        wid = lax.axis_index("s") * n_cores + lax.axis_index("c")

        @pl.loop(0, -(-units // n_workers))
        def _(j):
            u = j * n_workers + wid

            @pl.when(u < units)
            def _():
                pltpu.sync_copy(slot_hbm.at[pl.ds(u, 1)], idx_v)
                pltpu.sync_copy(table_hbm.at[idx_v.at[0]], rows_v)
                pltpu.sync_copy(rows_v, out_hbm.at[pl.ds(u * SC_ROWS, SC_ROWS)])

    return pl.kernel(
        body, out_type=jax.ShapeDtypeStruct((units * SC_ROWS, d), table.dtype),
        mesh=plsc.VectorSubcoreMesh(core_axis_name="c", subcore_axis_name="s"),
        scratch_types=[pltpu.VMEM((1, SC_ROWS), jnp.int32), pltpu.VMEM((SC_ROWS, d), table.dtype)],
        name="collect_rows",
    )(table, slots)


def _expert_kernel(blk_ref, xs_ref, wg_ref, wu_ref, wd_ref, ys_ref, wg_s, wu_s, wd_s):
    b = pl.program_id(0)
    n_blocks = pl.num_programs(0)
    in_use = b < blk_ref[n_blocks]

    @pl.when(jnp.logical_and(in_use, jnp.logical_or(b == 0, blk_ref[b] != blk_ref[jnp.maximum(b - 1, 0)])))
    def _():
        wg_s[...] = wg_ref[0].astype(BF16)
        wu_s[...] = wu_ref[0].astype(BF16)
        wd_s[...] = wd_ref[0].astype(BF16)

    @pl.when(in_use)
    def _():
        x = _unpack_bf16_pairs(xs_ref[...]).astype(BF16)
        hdn = (jax.nn.silu(_dot(x, wg_s[...])) * _dot(x, wu_s[...])).astype(BF16)
        ys_ref[...] = _pack_bf16_pairs(_dot(hdn, wd_s[...]))


def _expert_call(blk_exp, xs, w_gate, w_up, w_down):
    p, half = xs.shape
    _, D, d_exp = w_gate.shape
    n_blocks = p // EXPERT_BLOCK
    assert blk_exp.shape == (n_blocks + 1,)

    def row_block(b, blk):
        return (jnp.minimum(b, blk[n_blocks] - 1), 0)

    rows = pl.BlockSpec((EXPERT_BLOCK, half), row_block)
    grid_spec = pltpu.PrefetchScalarGridSpec(
        num_scalar_prefetch=1,
        grid=(n_blocks,),
        in_specs=[rows,
                  pl.BlockSpec((1, D, d_exp), lambda b, blk: (blk[b], 0, 0)),
                  pl.BlockSpec((1, D, d_exp), lambda b, blk: (blk[b], 0, 0)),
                  pl.BlockSpec((1, d_exp, D), lambda b, blk: (blk[b], 0, 0))],
        out_specs=rows,
        scratch_shapes=[pltpu.VMEM((D, d_exp), BF16), pltpu.VMEM((D, d_exp), BF16),
                        pltpu.VMEM((d_exp, D), BF16)],
    )
    return pl.pallas_call(
        _expert_kernel,
        grid_spec=grid_spec,
        out_shape=jax.ShapeDtypeStruct((p, half), jnp.int32),
        compiler_params=pltpu.CompilerParams(
            dimension_semantics=("arbitrary",), vmem_limit_bytes=VMEM_LIMIT),
        name="experts",
    )(blk_exp, xs, w_gate, w_up, w_down)


def _combine_kernel(g_ref, gates_ref, base_ref, g2_ref, b2_ref, *rest):
    o_ref = rest[-1]
    gates = gates_ref[...]
    acc = base_ref[...]
    for k in range(g_ref.shape[0]):
        acc = acc + _unpack_bf16_pairs(g_ref[k]) * gates[:, k:k + 1]
    o_ref[0] = _layer_norm_rows(acc, g2_ref[...], b2_ref[...])


def _combine_call(g, gates, base, g2, b2, result, batch, n_batch):
    k, n, half = g.shape
    D = base.shape[1]
    tr = SEQ_ALIGN
    seq = n - tr

    def row_spec(w):
        return pl.BlockSpec((tr, w), lambda i: (i, 0))

    vec = pl.BlockSpec((1, D), lambda i: (0, 0))
    in_specs = [pl.BlockSpec((k, tr, half), lambda i: (0, i, 0)), row_spec(LANES), row_spec(D), vec, vec]
    args = [g, gates, base, g2, b2]
    aliases = {}
    if result is not None:
        in_specs.append(pl.BlockSpec(memory_space=pl.ANY))
        args.append(result)
        aliases = {len(args) - 1: 0}
    return pl.pallas_call(
        _combine_kernel,
        grid=(n // tr,),
        in_specs=in_specs,
        out_specs=pl.BlockSpec((1, tr, D), lambda i: (batch, jnp.maximum(i - 1, 0), 0)),
        out_shape=jax.ShapeDtypeStruct((n_batch, seq, D), F32),
        input_output_aliases=aliases,
        compiler_params=pltpu.CompilerParams(
            dimension_semantics=("arbitrary",), vmem_limit_bytes=VMEM_LIMIT),
        name="combine",
    )(*args)


def _rope_tables(tp, lead):
    pos = jnp.arange(tp, dtype=F32) - lead
    inv = jnp.power(ROPE_THETA, -2.0 * jnp.arange(ROPE_HALF, dtype=F32) / ROPE_DIM)
    ang = pos[:, None] * inv[None, :]
    cos, sin = jnp.cos(ang), jnp.sin(ang)
    zeros = jnp.zeros((tp, HEAD_DIM - ROPE_DIM), F32)
    zh = jnp.zeros((tp, ROPE_HALF), F32)
    c64 = jnp.concatenate([cos, cos, jnp.ones_like(zeros)], axis=1)
    s1_64 = jnp.concatenate([-sin, zh, zeros], axis=1)
    s2_64 = jnp.concatenate([zh, sin, zeros], axis=1)
    rep = LANES // HEAD_DIM
    return (jnp.tile(c64, (1, rep)), jnp.tile(s1_64, (1, rep)), jnp.tile(s2_64, (1, rep)),
            cos.T, sin.T)


def kernel(x, meta_tokens, ln_emb_g, ln_emb_b, w_in, conv_w, conv_b, ln_conv_g, ln_conv_b, ln_kidx_g, ln_kidx_b, w_out, ln1_g, ln1_b, w_router, router_bias, w_gate, w_up, w_down, ws_gate, ws_up, ws_down, ln2_g, ln2_b):
    B, seq, D = x.shape
    assert w_in.shape[0] == DEPTH
    assert seq % SEQ_ALIGN == 0 and meta_tokens.shape[0] == N_META <= SEQ_ALIGN
    k_sel = min(INDEX_TOPK, seq // 4)
    tp = seq + SEQ_ALIGN
    n_pad = SEQ_ALIGN - N_META

    def row(a):
        return a.reshape(1, -1).astype(F32)

    w = w_in[0]
    o = 0
    parts = []
    for width in (CONV_CH, CONV_CH, ATTN_W, ATTN_W, ATTN_W, IDX_HEADS * IDX_DIM, IDX_DIM, IDX_HEADS):
        parts.append(w[:, o:o + width])
        o += width
    wa, wgl, wq, wk, wv, wqi, wki, wwi = parts
    wwi_t = jnp.concatenate([wwi.T, jnp.zeros((BF16_SUBLANES - IDX_HEADS, D), w.dtype)], axis=0)
    weights = (jnp.concatenate([wa, wgl], axis=1).astype(BF16), wq.T.astype(BF16), wk.astype(BF16),
               wv.T.astype(BF16), wqi.T.astype(BF16), jnp.concatenate([wki, wki], axis=1).astype(BF16),
               wwi_t.astype(BF16))

    def twice(a):
        return row(jnp.concatenate([a, a]))

    tabs = _rope_tables(tp, n_pad)
    inproj_consts = (row(ln_emb_g), row(ln_emb_b), weights, conv_w[0].astype(F32), row(conv_b[0]),
                     row(ln_conv_g[0]), row(ln_conv_b[0]), twice(ln_kidx_g[0]), twice(ln_kidx_b[0]))
    wr_t = w_router[0].T.astype(F32)
    wr_hi = wr_t.astype(BF16)
    wr_lo = (wr_t - wr_hi.astype(F32)).astype(BF16)
    route_consts = (w_out[0][:CONV_CH].astype(BF16), w_out[0][CONV_CH:].astype(BF16), row(ln1_g[0]),
                    row(ln1_b[0]), ws_gate[0].astype(BF16), ws_up[0].astype(BF16), ws_down[0].astype(BF16),
                    wr_hi, wr_lo, router_bias[0].reshape(-1, 1).astype(F32))
    meta = meta_tokens.astype(F32)
    n_blocks = tp * TOP_K // EXPERT_BLOCK + N_EXPERTS

    result = None
    for b in range(B):
        hn, conv, qt, k, vt, qit, ki, wit = _inproj_call(x, b, meta, tabs, *inproj_consts)
        attn = _dsa_call(qt, qit, wit, k, ki, vt, k_sel, n_pad)
        xp, base, gates, ek, rk, cnt = _route_call(conv[0], attn[0], hn[0], *route_consts)

        slot, blk = _plan_call(cnt, ek, rk, n_blocks)
        slots = slot.reshape(TOP_K * tp // SC_ROWS, SC_ROWS)
        xs = _sc_scatter_rows(xp, slots, n_blocks * EXPERT_BLOCK)
        ys = _expert_call(blk[0, :n_blocks + 1], xs, w_gate[0], w_up[0], w_down[0])
        picked = _sc_gather_rows(ys, slots).reshape(TOP_K, tp, D // 2)
        result = _combine_call(picked, gates, base, row(ln2_g[0]), row(ln2_b[0]), result, b, B)
    return result
```

```python
import functools

import numpy as np
import jax
import jax.numpy as jnp
from jax import lax
from jax.experimental import pallas as pl
from jax.experimental.pallas import tpu as pltpu
from jax.experimental.pallas import tpu_sc as plsc

N_META = 16
CONV_CH = 512
CONV_WIDTH = 31
N_HEADS = 8
HEAD_DIM = 64
ATTN_W = N_HEADS * HEAD_DIM
IDX_HEADS = 8
IDX_DIM = 64
INDEX_TOPK = 256
ROPE_DIM = HEAD_DIM // 4
ROPE_HALF = ROPE_DIM // 2
ROPE_THETA = 500000.0
N_EXPERTS = 64
TOP_K = 8
N_GROUPS = 8
GROUP_SIZE = N_EXPERTS // N_GROUPS
TOPK_GROUPS = 4
ROUTED_SCALE = 2.5
LN_EPS = 1e-5
DEPTH = 1
DN_ALPHA = (2.0 * DEPTH) ** 0.25

LANES = 128
Q_TILE = 256
SUBLANES = 8
BF16_SUBLANES = 16
EXPERT_BLOCK = 512
SC_ROWS = 128
K_CHUNK = 256
K_SUB = 128
COUNT_UNROLL = 4
SEQ_ALIGN = 256
CONV_HALO = 32
VMEM_LIMIT = 56 * 1024 * 1024

F32 = jnp.float32
BF16 = jnp.bfloat16
COARSE = jnp.bfloat16
NEG_INF = float("-inf")
INT_MIN = -2 ** 31
KEY_NEG_INF = -2139095041
LOG2_E = 1.4426950408889634


def _dot(a, b):
    return jnp.dot(a, b, preferred_element_type=F32)


def _dot_nt(a, b):
    return lax.dot_general(a, b, (((1,), (1,)), ((), ())), preferred_element_type=F32)


def _layer_norm_rows(x, g, b):
    mu = jnp.mean(x, axis=-1, keepdims=True)
    xc = x - mu
    var = jnp.mean(xc * xc, axis=-1, keepdims=True)
    return xc * lax.rsqrt(var + LN_EPS) * g + b


def _pick_tile(n, candidates):
    for c in candidates:
        if n % c == 0:
            return c
    raise ValueError(f"no tile for {n}")


def _rope_rows(x, c_tab, s1_tab, s2_tab):
    outs = []
    for j in range(x.shape[1] // LANES):
        xs = x[:, j * LANES:(j + 1) * LANES]
        up = pltpu.roll(xs, LANES - ROPE_HALF, axis=1)
        dn = pltpu.roll(xs, ROPE_HALF, axis=1)
        outs.append(xs * c_tab + up * s1_tab + dn * s2_tab)
    return jnp.concatenate(outs, axis=1)


def _rope_cols(xt, cos_t, sin_t, heads):
    r = xt.shape[1]
    x3 = xt.reshape(heads, HEAD_DIM, r)
    x1 = x3[:, 0:ROPE_HALF, :]
    x2 = x3[:, ROPE_HALF:ROPE_DIM, :]
    n1 = x1 * cos_t - x2 * sin_t
    n2 = x2 * cos_t + x1 * sin_t
    out = jnp.concatenate([n1, n2, x3[:, ROPE_DIM:, :]], axis=1)
    return out.reshape(heads * HEAD_DIM, r)


def _inproj_kernel(x_ref, meta_ref, ctab_ref, s1tab_ref, s2tab_ref, cost_ref, sint_ref,
                   lng_ref, lnb_ref, wag_ref, wqt_ref, wk_ref, wvt_ref, wqit_ref, wki_ref, wwit_ref,
                   cw_ref, cb_ref, lncg_ref, lncb_ref, lnkg_ref, lnkb_ref,
                   hn_ref, conv_ref, qt_ref, k_ref, vt_ref, qit_ref, ki_ref, wit_ref,
                   ubuf_ref, wbuf_ref):
    t = pl.program_id(1)
    tr = x_ref.shape[1]
    n_meta = meta_ref.shape[0]
    first = jnp.concatenate([jnp.zeros((tr - n_meta, x_ref.shape[2]), F32), meta_ref[...]], axis=0)
    h = jnp.where(t == 0, first, x_ref[0])

    hn = _layer_norm_rows(h, lng_ref[...], lnb_ref[...])
    hn_ref[0] = hn
    xb = hn.astype(BF16)

    ag = _dot(xb, wag_ref[...])
    u = ag[:, :CONV_CH] * jax.nn.sigmoid(ag[:, CONV_CH:])
    row = lax.broadcasted_iota(jnp.int32, (tr, CONV_CH), 0)
    u = jnp.where(jnp.logical_or(t > 0, row >= tr - n_meta), u, 0.0)

    @pl.when(t == 0)
    def _():
        ubuf_ref[0:CONV_HALO, :] = jnp.zeros((CONV_HALO, CONV_CH), F32)

    ubuf_ref[CONV_HALO:CONV_HALO + tr, :] = u
    base = CONV_HALO - (CONV_WIDTH - 1)
    acc = jnp.zeros((tr, CONV_CH), F32)
    for r in range(SUBLANES):
        offsets = [o for o in range(base, base + CONV_WIDTH) if o % SUBLANES == r]
        rows = max(offsets) - r + tr
        wbuf_ref[0:rows, :] = ubuf_ref[r:r + rows, :]
        for o in offsets:
            acc = acc + cw_ref[o - base:o - base + 1, :] * wbuf_ref[o - r:o - r + tr, :]
    ubuf_ref[0:CONV_HALO, :] = ubuf_ref[tr:tr + CONV_HALO, :]
    c = _layer_norm_rows(acc + cb_ref[...], lncg_ref[...], lncb_ref[...])
    conv_ref[0] = (c * jax.nn.sigmoid(c)).astype(conv_ref.dtype)

    ctab, s1tab, s2tab = ctab_ref[...], s1tab_ref[...], s2tab_ref[...]
    cos_t, sin_t = cost_ref[...], sint_ref[...]

    qt = _rope_cols(_dot_nt(wqt_ref[...], xb), cos_t, sin_t, N_HEADS)
    qt_ref[0] = (qt * (HEAD_DIM ** -0.5 * LOG2_E)).astype(qt_ref.dtype)
    k = _rope_rows(_dot(xb, wk_ref[...]), ctab, s1tab, s2tab)
    k_ref[0] = k.astype(k_ref.dtype)
    vt_ref[0] = _dot_nt(wvt_ref[...], xb).astype(vt_ref.dtype)

    qit = _rope_cols(_dot_nt(wqit_ref[...], xb), cos_t, sin_t, IDX_HEADS)
    qit_ref[0] = qit.astype(qit_ref.dtype)
    ki = _layer_norm_rows(_dot(xb, wki_ref[...]), lnkg_ref[...], lnkb_ref[...])
    ki_ref[0] = _rope_rows(ki, ctab, s1tab, s2tab).astype(ki_ref.dtype)
    wit = _dot_nt(wwit_ref[...], xb) * (IDX_HEADS ** -0.5)
    wit_ref[0] = wit[:IDX_HEADS]


def _inproj_call(x, batch, meta, tabs, ln_g, ln_b, weights, conv_w, conv_b, lnc_g, lnc_b, lnk_g, lnk_b):
    _, seq, D = x.shape
    B = 1
    tr = SEQ_ALIGN
    tp = seq + tr
    nt = tp // tr
    ctab, s1tab, s2tab, cos_t, sin_t = tabs

    def row_spec(w):
        return pl.BlockSpec((1, tr, w), lambda b, t: (b, t, 0))

    def col_spec(r):
        return pl.BlockSpec((1, r, tr), lambda b, t: (b, 0, t))

    def full(a):
        return pl.BlockSpec(a.shape, lambda b, t: (0,) * a.ndim)

    tab_row = pl.BlockSpec((tr, LANES), lambda b, t: (t, 0))
    tab_col = pl.BlockSpec((ROPE_HALF, tr), lambda b, t: (0, t))
    consts = [ln_g, ln_b, *weights, conv_w, conv_b, lnc_g, lnc_b, lnk_g, lnk_b]
    out_shape = [
        jax.ShapeDtypeStruct((B, tp, D), F32),
        jax.ShapeDtypeStruct((B, tp, CONV_CH), BF16),
        jax.ShapeDtypeStruct((B, ATTN_W, tp), BF16),
        jax.ShapeDtypeStruct((B, tp, ATTN_W), BF16),
        jax.ShapeDtypeStruct((B, ATTN_W, tp), BF16),
        jax.ShapeDtypeStruct((B, IDX_HEADS * IDX_DIM, tp), BF16),
        jax.ShapeDtypeStruct((B, tp, 2 * IDX_DIM), BF16),
        jax.ShapeDtypeStruct((B, IDX_HEADS, tp), F32),
    ]
    out_specs = [row_spec(D), row_spec(CONV_CH), col_spec(ATTN_W), row_spec(ATTN_W), col_spec(ATTN_W),
                 col_spec(IDX_HEADS * IDX_DIM), row_spec(2 * IDX_DIM), col_spec(IDX_HEADS)]
    return pl.pallas_call(
        _inproj_kernel,
        grid=(B, nt),
        in_specs=[pl.BlockSpec((1, tr, D), lambda b, t: (batch, jnp.maximum(t - 1, 0), 0)), full(meta),
                  tab_row, tab_row, tab_row, tab_col, tab_col] + [full(a) for a in consts],
        out_specs=out_specs,
        out_shape=out_shape,
        scratch_shapes=[pltpu.VMEM((CONV_HALO + tr, CONV_CH), F32),
                        pltpu.VMEM((CONV_HALO + tr, CONV_CH), F32)],
        compiler_params=pltpu.CompilerParams(
            dimension_semantics=("arbitrary", "arbitrary"), vmem_limit_bytes=VMEM_LIMIT),
        name="inproj",
    )(x, meta, ctab, s1tab, s2tab, cos_t, sin_t, *consts)


def _key_to_float(key):
    bits = jnp.where(key >= 0, key, key ^ jnp.int32(0x7FFFFFFF))
    f = pltpu.bitcast(bits, F32)
    return jnp.where(key < jnp.int32(KEY_NEG_INF), NEG_INF, f)


def _tree(parts, op):
    parts = list(parts)
    while len(parts) > 1:
        nxt = [op(parts[j], parts[j + 1]) for j in range(0, len(parts) - 1, 2)]
        if len(parts) % 2:
            nxt.append(parts[-1])
        parts = nxt
    return parts[0]


def _fold_rows(x, rows, op):
    return _tree([x[j * rows:(j + 1) * rows] for j in range(x.shape[0] // rows)], op)


def _dsa_kernel(qt_ref, qit_ref, wit_ref, k_ref, ki_ref, vt_ref, o_ref,
                sc_ref, sh_ref, qm_ref, qim_ref, m_ref, alpha_ref, shift_ref, lg_ref, acc_ref, *, k_sel, n_pad):
    i = pl.program_id(1)
    tq = qt_ref.shape[2]
    n_chunks = (i * tq + tq + K_CHUNK - 1) // K_CHUNK
    v_rows = HEAD_DIM + BF16_SUBLANES

    def causal_mask(k0, rows=K_CHUNK):
        kpos = k0 + lax.broadcasted_iota(jnp.int32, (rows, tq), 0)
        return kpos <= i * tq + lax.broadcasted_iota(jnp.int32, (rows, tq), 1)

    def rows8(x):
        return jnp.broadcast_to(x, (SUBLANES, tq))

    def tiles(x):
        return x.reshape(x.shape[0] // SUBLANES, SUBLANES, tq)

    def head_slab(ref, h):
        slab = ref[0, (h // 2) * LANES:(h // 2 + 1) * LANES, :]
        zeros = jnp.zeros((HEAD_DIM, tq), slab.dtype)
        if h % 2 == 0:
            return jnp.concatenate([slab[:HEAD_DIM], zeros], axis=0)
        return jnp.concatenate([zeros, slab[HEAD_DIM:]], axis=0)

    for h in range(N_HEADS):
        qm_ref[h] = head_slab(qt_ref, h)
    for h in range(IDX_HEADS):
        qim_ref[h] = head_slab(qit_ref, h)
    wit = wit_ref[0]
    w_heads = [rows8(wit[h:h + 1]) for h in range(IDX_HEADS)]

    def for_each_chunk(chunk_fn, unroll):
        def group(j, carry):
            for u in range(unroll):
                chunk_fn(unroll * j + u, u % 2)
            return carry

        def single(c, carry):
            chunk_fn(c, 0)
            return carry

        n_groups = n_chunks // unroll
        lax.fori_loop(0, n_groups, group, 0)
        lax.fori_loop(n_groups * unroll, n_chunks, single, 0)

    def score_chunk(c, slot):
        del slot
        for s in range(K_CHUNK // K_SUB):
            k0 = pl.multiple_of(c * K_CHUNK + s * K_SUB, K_SUB)
            kic = ki_ref[0, pl.ds(k0, K_SUB), :]
            acc = jnp.zeros((K_SUB // SUBLANES, SUBLANES, tq), F32)
            for h in range(IDX_HEADS):
                acc = acc + w_heads[h][None] * jnp.maximum(tiles(_dot(kic, qim_ref[h])), 0.0)
            acc = (acc * (IDX_DIM ** -0.5)).reshape(K_SUB, tq)
            acc = jnp.where(causal_mask(k0, K_SUB), acc, NEG_INF)
            sc_ref[pl.ds(k0, K_SUB), :] = acc
            hi_bits = pltpu.bitcast(acc, jnp.int32) & jnp.int32(-65536)
            sh_ref[pl.ds(k0, K_SUB), :] = pltpu.bitcast(hi_bits, F32).astype(COARSE)

    for_each_chunk(score_chunk, 4)

    def drop_padding_keys():
        sh_ref[0:n_pad, :] = jnp.full((n_pad, tq), NEG_INF, COARSE)

    sc_ref[0:n_pad, :] = jnp.full((n_pad, tq), NEG_INF, F32)
    drop_padding_keys()

    def count_all(ref, thr_tile, preds):
        rows = thr_tile.shape[0]
        one, zero = jnp.ones((), ref.dtype), jnp.zeros((), ref.dtype)

        def body(c, cnts):
            k0 = pl.multiple_of(c * K_CHUNK, K_CHUNK)
            s = ref[pl.ds(k0, K_CHUNK), :]
            out = []
            for cnt, p in zip(cnts, preds):
                hits = [jnp.where(p(s[j * rows:(j + 1) * rows], thr_tile), one, zero)
                        for j in range(K_CHUNK // rows)]
                out.append(cnt + _tree(hits, jnp.add).astype(F32))
            return tuple(out)

        def group(j, cnts):
            for u in range(COUNT_UNROLL):
                cnts = body(COUNT_UNROLL * j + u, cnts)
            return cnts

        init = tuple(jnp.zeros((rows, tq), F32) for _ in preds)
        n_groups = n_chunks // COUNT_UNROLL
        cnts = lax.fori_loop(0, n_groups, group, init)
        cnts = lax.fori_loop(n_groups * COUNT_UNROLL, n_chunks, body, cnts)
        return [rows8(jnp.sum(cnt, axis=0, keepdims=True)) for cnt in cnts]

    def search_body(it, tkey, coarse):
        cand = tkey + lax.shift_left(jnp.int32(1), 31 - it)
        cf = _key_to_float(cand)
        if coarse:
            cf = pltpu.bitcast(pltpu.bitcast(cf, jnp.int32) & jnp.int32(-65536), F32).astype(COARSE)
            cf = jnp.concatenate([cf] * (BF16_SUBLANES // SUBLANES), axis=0)
        cnt, = count_all(sh_ref if coarse else sc_ref, cf, [lambda s, t: s >= t])
        return jnp.where(cnt >= k_sel, cand, tkey)

    tkey = jnp.full((SUBLANES, tq), INT_MIN, jnp.int32)
    tkey = lax.fori_loop(0, 16, functools.partial(search_body, coarse=True), tkey)
    tkey = lax.fori_loop(16, 32, functools.partial(search_body, coarse=False), tkey)
    thr = _key_to_float(tkey)

    n_ge, n_gt = count_all(sc_ref, thr, [lambda s, t: s >= t, lambda s, t: s > t])
    need = k_sel - n_gt
    has_ties = jnp.max(n_ge) > k_sel

    @pl.when(jnp.logical_not(has_ties))
    def _():
        def body(c, carry):
            k0 = pl.multiple_of(c * K_CHUNK, K_CHUNK)
            s = sc_ref[pl.ds(k0, K_CHUNK), :]
            bias = jnp.where(tiles(s) >= thr[None], 0.0, NEG_INF).reshape(K_CHUNK, tq)
            sh_ref[pl.ds(k0, K_CHUNK), :] = jnp.where(causal_mask(k0), bias, NEG_INF).astype(COARSE)
            return carry
        lax.fori_loop(0, n_chunks, body, 0)

    @pl.when(has_ties)
    def _():
        r_i = lax.broadcasted_iota(jnp.int32, (K_CHUNK, K_CHUNK), 0)
        c_i = lax.broadcasted_iota(jnp.int32, (K_CHUNK, K_CHUNK), 1)
        lower = jnp.where(c_i <= r_i, 1.0, 0.0).astype(BF16)
        thr_row, need_row = thr[0:1], need[0:1]

        def body(c, seen):
            k0 = pl.multiple_of(c * K_CHUNK, K_CHUNK)
            s = sc_ref[pl.ds(k0, K_CHUNK), :]
            eq = jnp.where(s == thr_row, 1.0, 0.0)
            rank = _dot(lower, eq.astype(BF16)) + seen
            keep_tie = jnp.where(rank <= need_row, eq, 0.0)
            sel = jnp.where(s > thr_row, 1.0, keep_tie)
            sh_ref[pl.ds(k0, K_CHUNK), :] = jnp.where(
                sel > 0.0, jnp.where(causal_mask(k0), 0.0, NEG_INF), NEG_INF).astype(COARSE)
            return rank[K_CHUNK - 1:K_CHUNK, :]
        lax.fori_loop(0, n_chunks, body, jnp.zeros((1, tq), F32))

    drop_padding_keys()

    m_ref[...] = jnp.full(m_ref.shape, NEG_INF, F32)
    acc_ref[...] = jnp.zeros(acc_ref.shape, F32)
    ones_rows = jnp.ones((BF16_SUBLANES, K_CHUNK), BF16)

    def logits_stage(c, slot):
        k0 = pl.multiple_of(c * K_CHUNK, K_CHUNK)
        for h in range(N_HEADS):
            pair = slice((h // 2) * LANES, (h // 2 + 1) * LANES)
            cmax = []
            for s in range(K_CHUNK // K_SUB):
                rows = pl.ds(pl.multiple_of(k0 + s * K_SUB, K_SUB), K_SUB)
                lg = _dot(k_ref[0, rows, pair], qm_ref[h]).astype(BF16) + sh_ref[rows, :].astype(BF16)
                lg_ref[slot, h, s * K_SUB:(s + 1) * K_SUB, :] = lg
                cmax.append(_fold_rows(lg, BF16_SUBLANES, jnp.maximum))
            cmax = _tree(cmax, jnp.maximum).astype(F32)
            m_old = m_ref[h]
            m_new = jnp.maximum(m_old, rows8(jnp.max(cmax, axis=0, keepdims=True)))
            m_safe = jnp.where(m_new == NEG_INF, 0.0, m_new)
            alpha_ref[slot, h] = jnp.exp2(m_old - m_safe)
            shift_ref[slot, h] = m_safe
            m_ref[h] = m_new

    def values_stage(c, slot):
        k0 = pl.multiple_of(c * K_CHUNK, K_CHUNK)
        for h in range(N_HEADS):
            m_safe = shift_ref[slot, h].astype(BF16)
            m_tile = jnp.concatenate([m_safe] * (BF16_SUBLANES // SUBLANES), axis=0)
            lg = lg_ref[slot, h].reshape(K_CHUNK // BF16_SUBLANES, BF16_SUBLANES, tq)
            p = jnp.exp2(lg - m_tile[None]).reshape(K_CHUNK, tq)
            v_aug = jnp.concatenate(
                [vt_ref[0, h * HEAD_DIM:(h + 1) * HEAD_DIM, pl.ds(k0, K_CHUNK)], ones_rows], axis=0)
            pv = _dot(v_aug, p)
            acc = acc_ref[h].reshape(v_rows // SUBLANES, SUBLANES, tq) * alpha_ref[slot, h][None]
            acc_ref[h] = acc.reshape(v_rows, tq) + pv

    def attn_chunk(c, slot):
        logits_stage(c, slot)
        values_stage(c, slot)

    for_each_chunk(attn_chunk, 4)

    for pair in range(N_HEADS // 2):
        halves = []
        for h in (2 * pair, 2 * pair + 1):
            a = acc_ref[h]
            total = a[HEAD_DIM:HEAD_DIM + 1]
            halves.append(a[:HEAD_DIM] / jnp.where(total == 0.0, 1.0, total))
        out_t = jnp.concatenate(halves, axis=0)
        o_ref[0, :, pair * LANES:(pair + 1) * LANES] = out_t.T.astype(o_ref.dtype)


def _dsa_call(qt, qit, wit, k, ki, vt, k_sel, n_pad):
    B, tp, _ = k.shape
    nq = tp // Q_TILE

    def q_cols(r):
        return pl.BlockSpec((1, r, Q_TILE), lambda b, i: (b, 0, i))

    def per_batch(a):
        return pl.BlockSpec((1,) + a.shape[1:], lambda b, i: (b, 0, 0), pipeline_mode=pl.Buffered(1))

    return pl.pallas_call(
        functools.partial(_dsa_kernel, k_sel=k_sel, n_pad=n_pad),
        grid=(B, nq),
        in_specs=[q_cols(ATTN_W), q_cols(IDX_HEADS * IDX_DIM), q_cols(IDX_HEADS),
                  per_batch(k), per_batch(ki), per_batch(vt)],
        out_specs=pl.BlockSpec((1, Q_TILE, ATTN_W), lambda b, i: (b, i, 0)),
        out_shape=jax.ShapeDtypeStruct((B, tp, ATTN_W), BF16),
        scratch_shapes=[pltpu.VMEM((tp, Q_TILE), F32),
                        pltpu.VMEM((tp, Q_TILE), COARSE),
                        pltpu.VMEM((N_HEADS, LANES, Q_TILE), BF16),
                        pltpu.VMEM((IDX_HEADS, LANES, Q_TILE), BF16),
                        pltpu.VMEM((N_HEADS, SUBLANES, Q_TILE), F32),
                        pltpu.VMEM((2, N_HEADS, SUBLANES, Q_TILE), F32),
                        pltpu.VMEM((2, N_HEADS, SUBLANES, Q_TILE), F32),
                        pltpu.VMEM((2, N_HEADS, K_CHUNK, Q_TILE), BF16),
                        pltpu.VMEM((N_HEADS, HEAD_DIM + BF16_SUBLANES, Q_TILE), F32)],
        compiler_params=pltpu.CompilerParams(
            dimension_semantics=("arbitrary", "arbitrary"), vmem_limit_bytes=VMEM_LIMIT),
        name="dsa",
    )(qt, qit, wit, k, ki, vt)


def _max_all(x):
    return jnp.max(jnp.max(x, axis=1, keepdims=True), axis=0, keepdims=True)


def _sum_all(x):
    return jnp.sum(jnp.sum(x, axis=1, keepdims=True), axis=0, keepdims=True)


def _pack_bf16_pairs(x):
    w = x.shape[1] // 2
    lo = pltpu.bitcast(x[:, :w].astype(BF16).astype(F32), jnp.int32)
    hi = pltpu.bitcast(x[:, w:].astype(BF16).astype(F32), jnp.int32)
    return lax.shift_right_logical(lo, 16) | (hi & jnp.int32(-65536))


def _unpack_bf16_pairs(p):
    lo = pltpu.bitcast(lax.shift_left(p, 16), F32)
    hi = pltpu.bitcast(p & jnp.int32(-65536), F32)
    return jnp.concatenate([lo, hi], axis=1)


def _router_topk(logits_t, rbias):
    r = logits_t.shape[1]
    shape3 = (N_GROUPS, GROUP_SIZE, r)
    scores = jax.nn.sigmoid(logits_t).reshape(shape3)
    biased = scores + rbias.reshape(N_GROUPS, GROUP_SIZE, 1)
    in_grp = lax.broadcasted_iota(jnp.int32, shape3, 1).astype(F32)
    m1 = jnp.max(biased, axis=1, keepdims=True)
    first = jnp.min(jnp.where(biased == m1, in_grp, float(GROUP_SIZE)), axis=1, keepdims=True)
    m2 = jnp.max(jnp.where(in_grp == first, NEG_INF, biased), axis=1, keepdims=True)
    cur = m1 + m2

    grp_idx = lax.broadcasted_iota(jnp.int32, (N_GROUPS, 1, r), 0).astype(F32)
    grp_sel = jnp.zeros((N_GROUPS, 1, r), F32)
    for _ in range(TOPK_GROUPS):
        m = jnp.max(cur, axis=0, keepdims=True)
        pick = grp_idx == jnp.min(jnp.where(cur == m, grp_idx, float(N_GROUPS)), axis=0, keepdims=True)
        grp_sel = jnp.where(pick, 1.0, grp_sel)
        cur = jnp.where(pick, NEG_INF, cur)

    cur = jnp.where(jnp.broadcast_to(grp_sel, shape3) > 0.0, biased, NEG_INF)
    exp_idx = lax.broadcasted_iota(jnp.int32, shape3, 0).astype(F32) * GROUP_SIZE + in_grp
    chosen = jnp.zeros(shape3, F32)
    ids, wts = [], []
    for _ in range(TOP_K):
        m = _max_all(cur)
        first = -_max_all(-jnp.where(cur == m, exp_idx, float(N_EXPERTS)))
        pick = exp_idx == first
        chosen = jnp.where(pick, 1.0, chosen)
        cur = jnp.where(pick, NEG_INF, cur)
        ids.append(first.reshape(1, r))
        wts.append(_sum_all(jnp.where(pick, scores, 0.0)).reshape(1, r))
    ids = jnp.concatenate(ids, axis=0)
    wts = jnp.concatenate(wts, axis=0)
    gates = wts / jnp.sum(wts, axis=0, keepdims=True) * ROUTED_SCALE
    return ids, gates, chosen.reshape(N_EXPERTS, r), exp_idx


def _route_kernel(conv_ref, attn_ref, hn_ref, woc_ref, woa_ref, g1_ref, b1_ref,
                  wsg_ref, wsu_ref, wsd_ref, wrh_ref, wrl_ref, rb_ref,
                  xp_ref, base_ref, gates_ref, ek_ref, rk_ref, cnt_ref):
    step = pl.program_id(0)
    tr = hn_ref.shape[0]
    mix = _dot(conv_ref[...], woc_ref[...]) + _dot(attn_ref[...], woa_ref[...])
    h1 = _layer_norm_rows(DN_ALPHA * hn_ref[...] + mix, g1_ref[...], b1_ref[...])
    xb = h1.astype(BF16)
    xp_ref[...] = _pack_bf16_pairs(h1)

    shared = jax.nn.silu(_dot(xb, wsg_ref[...])) * _dot(xb, wsu_ref[...])
    base_ref[...] = DN_ALPHA * h1 + _dot(shared.astype(BF16), wsd_ref[...])

    x_lo = (h1 - xb.astype(F32)).astype(BF16)
    logits_t = (_dot_nt(wrh_ref[...], xb) + _dot_nt(wrh_ref[...], x_lo) + _dot_nt(wrl_ref[...], xb))
    ids, gates, chosen, exp_idx = _router_topk(logits_t, rb_ref[...])
    padded = jnp.concatenate([gates, jnp.zeros((LANES - TOP_K, tr), F32)], axis=0)
    gates_ref[...] = padded.T
    ek_ref[...] = ids.astype(jnp.int32)

    @pl.when(step == 0)
    def _():
        cnt_ref[...] = jnp.zeros(cnt_ref.shape, F32)

    t_i = lax.broadcasted_iota(jnp.int32, (tr, tr), 0)
    t_j = lax.broadcasted_iota(jnp.int32, (tr, tr), 1)
    before = jnp.where(t_i < t_j, 1.0, 0.0).astype(BF16)
    chosen_b = chosen.astype(BF16)
    running = cnt_ref[...]
    rank = _dot(chosen_b, before) + jnp.concatenate([running] * (tr // LANES), axis=1)
    rank3 = rank.reshape(N_GROUPS, GROUP_SIZE, tr)
    rk = [_sum_all(jnp.where(exp_idx == ids[k:k + 1].reshape(1, 1, tr), rank3, 0.0)).reshape(1, tr)
          for k in range(TOP_K)]
    rk_ref[...] = jnp.concatenate(rk, axis=0).astype(jnp.int32)
    cnt_ref[...] = running + _dot(chosen_b, jnp.ones((tr, LANES), BF16))


def _route_call(conv, attn, hn, woc, woa, g1, b1, wsg, wsu, wsd, wrh, wrl, rbias):
    n, D = hn.shape
    tr = _pick_tile(n, (768, 512, 256))

    def row_spec(w):
        return pl.BlockSpec((tr, w), lambda i: (i, 0))

    def col_spec(r):
        return pl.BlockSpec((r, tr), lambda i: (0, i))

    def full(a):
        return pl.BlockSpec(a.shape, lambda i: (0,) * a.ndim)

    consts = [woc, woa, g1, b1, wsg, wsu, wsd, wrh, wrl, rbias]
    return pl.pallas_call(
        _route_kernel,
        grid=(n // tr,),
        in_specs=[row_spec(CONV_CH), row_spec(ATTN_W), row_spec(D)] + [full(a) for a in consts],
        out_specs=[row_spec(D // 2), row_spec(D), row_spec(LANES), col_spec(TOP_K), col_spec(TOP_K),
                   pl.BlockSpec((N_EXPERTS, LANES), lambda i: (0, 0))],
        out_shape=[jax.ShapeDtypeStruct((n, D // 2), jnp.int32),
                   jax.ShapeDtypeStruct((n, D), F32),
                   jax.ShapeDtypeStruct((n, LANES), F32),
                   jax.ShapeDtypeStruct((TOP_K, n), jnp.int32),
                   jax.ShapeDtypeStruct((TOP_K, n), jnp.int32),
                   jax.ShapeDtypeStruct((N_EXPERTS, LANES), F32)],
        compiler_params=pltpu.CompilerParams(
            dimension_semantics=("arbitrary",), vmem_limit_bytes=VMEM_LIMIT),
        name="route",
    )(conv, attn, hn, *consts)


def _plan_kernel(cnt_ref, ek_ref, rk_ref, slot_ref, blk_ref, *, n_blocks):
    tr = ek_ref.shape[1]
    counts = cnt_ref[...]
    padded = jnp.ceil(counts / EXPERT_BLOCK) * EXPERT_BLOCK
    starts = [jnp.zeros((1, LANES), F32)]
    for e in range(1, N_EXPERTS):
        starts.append(starts[-1] + padded[e - 1:e])
    ek = ek_ref[...]
    seg = jnp.zeros(ek.shape, F32)
    for e in range(N_EXPERTS):
        seg = jnp.where(ek == e, jnp.concatenate([starts[e]] * (tr // LANES), axis=1), seg)
    slot_ref[...] = seg.astype(jnp.int32) + rk_ref[...]

    ends = jnp.concatenate(starts, axis=0) + padded
    w = blk_ref.shape[1]
    blk_start = lax.broadcasted_iota(jnp.int32, (N_EXPERTS, w), 1).astype(F32) * EXPERT_BLOCK
    done = jnp.where(jnp.concatenate([ends] * (w // LANES), axis=1) <= blk_start, 1.0, 0.0)
    owner = jnp.minimum(jnp.sum(done, axis=0, keepdims=True), N_EXPERTS - 1.0)
    used = jnp.concatenate([ends[N_EXPERTS - 1:] * (1.0 / EXPERT_BLOCK)] * (w // LANES), axis=1)
    lane = lax.broadcasted_iota(jnp.int32, (1, w), 1)
    table = jnp.where(lane == n_blocks, used, owner)
    blk_ref[...] = jnp.broadcast_to(table, blk_ref.shape).astype(jnp.int32)


def _plan_call(cnt, ek, rk, n_blocks):
    k, n = ek.shape
    tr = _pick_tile(n, (768, 512, 256))
    w = -(-(n_blocks + 1) // LANES) * LANES
    col = pl.BlockSpec((k, tr), lambda i: (0, i))
    return pl.pallas_call(
        functools.partial(_plan_kernel, n_blocks=n_blocks),
        grid=(n // tr,),
        in_specs=[pl.BlockSpec(cnt.shape, lambda i: (0, 0)), col, col],
        out_specs=[col, pl.BlockSpec((SUBLANES, w), lambda i: (0, 0))],
        out_shape=[jax.ShapeDtypeStruct((k, n), jnp.int32), jax.ShapeDtypeStruct((SUBLANES, w), jnp.int32)],
        compiler_params=pltpu.CompilerParams(dimension_semantics=("arbitrary",)),
        name="plan",
    )(cnt, ek, rk)


def _sc_workers():
    info = plsc.get_sparse_core_info()
    return info.num_cores, info.num_subcores


def _sc_scatter_rows(src, slots, n_out):
    n_src, d = src.shape
    units = slots.shape[0]
    n_chunks = n_src // SC_ROWS
    n_cores, n_sub = _sc_workers()
    n_workers = n_cores * n_sub
    assert slots.shape[1] == SC_ROWS

    def body(src_hbm, slot_hbm, out_hbm, idx_v, rows_v):
        wid = lax.axis_index("s") * n_cores + lax.axis_index("c")

        @pl.loop(0, -(-units // n_workers))
        def _(j):
            u = j * n_workers + wid

            @pl.when(u < units)
            def _():
                pltpu.sync_copy(slot_hbm.at[pl.ds(u, 1)], idx_v)
                pltpu.sync_copy(src_hbm.at[pl.ds((u % n_chunks) * SC_ROWS, SC_ROWS)], rows_v)
                pltpu.sync_copy(rows_v, out_hbm.at[idx_v.at[0]])

    return pl.kernel(
        body, out_type=jax.ShapeDtypeStruct((n_out, d), src.dtype),
        mesh=plsc.VectorSubcoreMesh(core_axis_name="c", subcore_axis_name="s"),
        scratch_types=[pltpu.VMEM((1, SC_ROWS), jnp.int32), pltpu.VMEM((SC_ROWS, d), src.dtype)],
        name="dispatch_rows",
    )(src, slots)


def _sc_gather_rows(table, slots):
    units = slots.shape[0]
    d = table.shape[1]
    n_cores, n_sub = _sc_workers()
    n_workers = n_cores * n_sub
    assert slots.shape[1] == SC_ROWS

    def body(table_hbm, slot_hbm, out_hbm, idx_v, rows_v):
        wid = lax.axis_index("s") * n_cores + lax.axis_index("c")

        @pl.loop(0, -(-units // n_workers))
        def _(j):
            u = j * n_workers + wid

            @pl.when(u < units)
            def _():
                pltpu.sync_copy(slot_hbm.at[pl.ds(u, 1)], idx_v)
                pltpu.sync_copy(table_hbm.at[idx_v.at[0]], rows_v)
                pltpu.sync_copy(rows_v, out_hbm.at[pl.ds(u * SC_ROWS, SC_ROWS)])

    return pl.kernel(
        body, out_type=jax.ShapeDtypeStruct((units * SC_ROWS, d), table.dtype),
        mesh=plsc.VectorSubcoreMesh(core_axis_name="c", subcore_axis_name="s"),
        scratch_types=[pltpu.VMEM((1, SC_ROWS), jnp.int32), pltpu.VMEM((SC_ROWS, d), table.dtype)],
        name="collect_rows",
    )(table, slots)


def _expert_kernel(blk_ref, xs_ref, wg_ref, wu_ref, wd_ref, ys_ref, wg_s, wu_s, wd_s):
    b = pl.program_id(0)
    n_blocks = pl.num_programs(0)
    in_use = b < blk_ref[n_blocks]

    @pl.when(jnp.logical_and(in_use, jnp.logical_or(b == 0, blk_ref[b] != blk_ref[jnp.maximum(b - 1, 0)])))
    def _():
        wg_s[...] = wg_ref[0].astype(BF16)
        wu_s[...] = wu_ref[0].astype(BF16)
        wd_s[...] = wd_ref[0].astype(BF16)

    @pl.when(in_use)
    def _():
        x = _unpack_bf16_pairs(xs_ref[...]).astype(BF16)
        hdn = (jax.nn.silu(_dot(x, wg_s[...])) * _dot(x, wu_s[...])).astype(BF16)
        ys_ref[...] = _pack_bf16_pairs(_dot(hdn, wd_s[...]))


def _expert_call(blk_exp, xs, w_gate, w_up, w_down):
    p, half = xs.shape
    _, D, d_exp = w_gate.shape
    n_blocks = p // EXPERT_BLOCK
    assert blk_exp.shape == (n_blocks + 1,)

    def row_block(b, blk):
        return (jnp.minimum(b, blk[n_blocks] - 1), 0)

    rows = pl.BlockSpec((EXPERT_BLOCK, half), row_block)
    grid_spec = pltpu.PrefetchScalarGridSpec(
        num_scalar_prefetch=1,
        grid=(n_blocks,),
        in_specs=[rows,
                  pl.BlockSpec((1, D, d_exp), lambda b, blk: (blk[b], 0, 0)),
                  pl.BlockSpec((1, D, d_exp), lambda b, blk: (blk[b], 0, 0)),
                  pl.BlockSpec((1, d_exp, D), lambda b, blk: (blk[b], 0, 0))],
        out_specs=rows,
        scratch_shapes=[pltpu.VMEM((D, d_exp), BF16), pltpu.VMEM((D, d_exp), BF16),
                        pltpu.VMEM((d_exp, D), BF16)],
    )
    return pl.pallas_call(
        _expert_kernel,
        grid_spec=grid_spec,
        out_shape=jax.ShapeDtypeStruct((p, half), jnp.int32),
        compiler_params=pltpu.CompilerParams(
            dimension_semantics=("arbitrary",), vmem_limit_bytes=VMEM_LIMIT),
        name="experts",
    )(blk_exp, xs, w_gate, w_up, w_down)


def _combine_kernel(g_ref, gates_ref, base_ref, g2_ref, b2_ref, *rest):
    o_ref = rest[-1]
    gates = gates_ref[...]
    acc = base_ref[...]
    for k in range(g_ref.shape[0]):
        acc = acc + _unpack_bf16_pairs(g_ref[k]) * gates[:, k:k + 1]
    o_ref[0] = _layer_norm_rows(acc, g2_ref[...], b2_ref[...])


def _combine_call(g, gates, base, g2, b2, result, batch, n_batch):
    k, n, half = g.shape
    D = base.shape[1]
    tr = SEQ_ALIGN
    seq = n - tr

    def row_spec(w):
        return pl.BlockSpec((tr, w), lambda i: (i, 0))

    vec = pl.BlockSpec((1, D), lambda i: (0, 0))
    in_specs = [pl.BlockSpec((k, tr, half), lambda i: (0, i, 0)), row_spec(LANES), row_spec(D), vec, vec]
    args = [g, gates, base, g2, b2]
    aliases = {}
    if result is not None:
        in_specs.append(pl.BlockSpec(memory_space=pl.ANY))
        args.append(result)
        aliases = {len(args) - 1: 0}
    return pl.pallas_call(
        _combine_kernel,
        grid=(n // tr,),
        in_specs=in_specs,
        out_specs=pl.BlockSpec((1, tr, D), lambda i: (batch, jnp.maximum(i - 1, 0), 0)),
        out_shape=jax.ShapeDtypeStruct((n_batch, seq, D), F32),
        input_output_aliases=aliases,
        compiler_params=pltpu.CompilerParams(
            dimension_semantics=("arbitrary",), vmem_limit_bytes=VMEM_LIMIT),
        name="combine",
    )(*args)


def _rope_tables(tp, lead):
    pos = jnp.arange(tp, dtype=F32) - lead
    inv = jnp.power(ROPE_THETA, -2.0 * jnp.arange(ROPE_HALF, dtype=F32) / ROPE_DIM)
    ang = pos[:, None] * inv[None, :]
    cos, sin = jnp.cos(ang), jnp.sin(ang)
    zeros = jnp.zeros((tp, HEAD_DIM - ROPE_DIM), F32)
    zh = jnp.zeros((tp, ROPE_HALF), F32)
    c64 = jnp.concatenate([cos, cos, jnp.ones_like(zeros)], axis=1)
    s1_64 = jnp.concatenate([-sin, zh, zeros], axis=1)
    s2_64 = jnp.concatenate([zh, sin, zeros], axis=1)
    rep = LANES // HEAD_DIM
    return (jnp.tile(c64, (1, rep)), jnp.tile(s1_64, (1, rep)), jnp.tile(s2_64, (1, rep)),
            cos.T, sin.T)


def kernel(x, meta_tokens, ln_emb_g, ln_emb_b, w_in, conv_w, conv_b, ln_conv_g, ln_conv_b, ln_kidx_g, ln_kidx_b, w_out, ln1_g, ln1_b, w_router, router_bias, w_gate, w_up, w_down, ws_gate, ws_up, ws_down, ln2_g, ln2_b):
    B, seq, D = x.shape
    assert w_in.shape[0] == DEPTH
    assert seq % SEQ_ALIGN == 0 and meta_tokens.shape[0] == N_META <= SEQ_ALIGN
    k_sel = min(INDEX_TOPK, seq // 4)
    tp = seq + SEQ_ALIGN
    n_pad = SEQ_ALIGN - N_META

    def row(a):
        return a.reshape(1, -1).astype(F32)

    w = w_in[0]
    o = 0
    parts = []
    for width in (CONV_CH, CONV_CH, ATTN_W, ATTN_W, ATTN_W, IDX_HEADS * IDX_DIM, IDX_DIM, IDX_HEADS):
        parts.append(w[:, o:o + width])
        o += width
    wa, wgl, wq, wk, wv, wqi, wki, wwi = parts
    wwi_t = jnp.concatenate([wwi.T, jnp.zeros((BF16_SUBLANES - IDX_HEADS, D), w.dtype)], axis=0)
    weights = (jnp.concatenate([wa, wgl], axis=1).astype(BF16), wq.T.astype(BF16), wk.astype(BF16),
               wv.T.astype(BF16), wqi.T.astype(BF16), jnp.concatenate([wki, wki], axis=1).astype(BF16),
               wwi_t.astype(BF16))

    def twice(a):
        return row(jnp.concatenate([a, a]))

    tabs = _rope_tables(tp, n_pad)
    inproj_consts = (row(ln_emb_g), row(ln_emb_b), weights, conv_w[0].astype(F32), row(conv_b[0]),
                     row(ln_conv_g[0]), row(ln_conv_b[0]), twice(ln_kidx_g[0]), twice(ln_kidx_b[0]))
    wr_t = w_router[0].T.astype(F32)
    wr_hi = wr_t.astype(BF16)
    wr_lo = (wr_t - wr_hi.astype(F32)).astype(BF16)
    route_consts = (w_out[0][:CONV_CH].astype(BF16), w_out[0][CONV_CH:].astype(BF16), row(ln1_g[0]),
                    row(ln1_b[0]), ws_gate[0].astype(BF16), ws_up[0].astype(BF16), ws_down[0].astype(BF16),
                    wr_hi, wr_lo, router_bias[0].reshape(-1, 1).astype(F32))
    meta = meta_tokens.astype(F32)
    n_blocks = tp * TOP_K // EXPERT_BLOCK + N_EXPERTS

    result = None
    for b in range(B):
        hn, conv, qt, k, vt, qit, ki, wit = _inproj_call(x, b, meta, tabs, *inproj_consts)
        attn = _dsa_call(qt, qit, wit, k, ki, vt, k_sel, n_pad)
        xp, base, gates, ek, rk, cnt = _route_call(conv[0], attn[0], hn[0], *route_consts)

        slot, blk = _plan_call(cnt, ek, rk, n_blocks)
        slots = slot.reshape(TOP_K * tp // SC_ROWS, SC_ROWS)
        xs = _sc_scatter_rows(xp, slots, n_blocks * EXPERT_BLOCK)
        ys = _expert_call(blk[0, :n_blocks + 1], xs, w_gate[0], w_up[0], w_down[0])
        picked = _sc_gather_rows(ys, slots).reshape(TOP_K, tp, D // 2)
        result = _combine_call(picked, gates, base, row(ln2_g[0]), row(ln2_b[0]), result, b, B)
    return result
```

```python
import functools

import numpy as np
import jax
import jax.numpy as jnp
from jax import lax
from jax.experimental import pallas as pl
from jax.experimental.pallas import tpu as pltpu
from jax.experimental.pallas import tpu_sc as plsc

N_META = 16
CONV_CH = 512
CONV_WIDTH = 31
N_HEADS = 8
HEAD_DIM = 64
ATTN_W = N_HEADS * HEAD_DIM
IDX_HEADS = 8
IDX_DIM = 64
INDEX_TOPK = 256
ROPE_DIM = HEAD_DIM // 4
ROPE_HALF = ROPE_DIM // 2
ROPE_THETA = 500000.0
N_EXPERTS = 64
TOP_K = 8
N_GROUPS = 8
GROUP_SIZE = N_EXPERTS // N_GROUPS
TOPK_GROUPS = 4
ROUTED_SCALE = 2.5
LN_EPS = 1e-5
DEPTH = 1
DN_ALPHA = (2.0 * DEPTH) ** 0.25

LANES = 128
Q_TILE = 256
SUBLANES = 8
BF16_SUBLANES = 16
EXPERT_BLOCK = 512
SC_ROWS = 128
K_CHUNK = 256
K_SUB = 128
COUNT_UNROLL = 4
SEQ_ALIGN = 256
CONV_HALO = 32
VMEM_LIMIT = 56 * 1024 * 1024

F32 = jnp.float32
BF16 = jnp.bfloat16
COARSE = jnp.bfloat16
NEG_INF = float("-inf")
INT_MIN = -2 ** 31
KEY_NEG_INF = -2139095041
LOG2_E = 1.4426950408889634


def _dot(a, b):
    return jnp.dot(a, b, preferred_element_type=F32)


def _dot_nt(a, b):
    return lax.dot_general(a, b, (((1,), (1,)), ((), ())), preferred_element_type=F32)


def _layer_norm_rows(x, g, b):
    mu = jnp.mean(x, axis=-1, keepdims=True)
    xc = x - mu
    var = jnp.mean(xc * xc, axis=-1, keepdims=True)
    return xc * lax.rsqrt(var + LN_EPS) * g + b


def _pick_tile(n, candidates):
    for c in candidates:
        if n % c == 0:
            return c
    raise ValueError(f"no tile for {n}")


def _rope_rows(x, c_tab, s1_tab, s2_tab):
    outs = []
    for j in range(x.shape[1] // LANES):
        xs = x[:, j * LANES:(j + 1) * LANES]
        up = pltpu.roll(xs, LANES - ROPE_HALF, axis=1)
        dn = pltpu.roll(xs, ROPE_HALF, axis=1)
        outs.append(xs * c_tab + up * s1_tab + dn * s2_tab)
    return jnp.concatenate(outs, axis=1)


def _rope_cols(xt, cos_t, sin_t, heads):
    r = xt.shape[1]
    x3 = xt.reshape(heads, HEAD_DIM, r)
    x1 = x3[:, 0:ROPE_HALF, :]
    x2 = x3[:, ROPE_HALF:ROPE_DIM, :]
    n1 = x1 * cos_t - x2 * sin_t
    n2 = x2 * cos_t + x1 * sin_t
    out = jnp.concatenate([n1, n2, x3[:, ROPE_DIM:, :]], axis=1)
    return out.reshape(heads * HEAD_DIM, r)


def _inproj_kernel(x_ref, meta_ref, ctab_ref, s1tab_ref, s2tab_ref, cost_ref, sint_ref,
                   lng_ref, lnb_ref, wag_ref, wqt_ref, wk_ref, wvt_ref, wqit_ref, wki_ref, wwit_ref,
                   cw_ref, cb_ref, lncg_ref, lncb_ref, lnkg_ref, lnkb_ref,
                   hn_ref, conv_ref, qt_ref, k_ref, vt_ref, qit_ref, ki_ref, wit_ref,
                   ubuf_ref, wbuf_ref):
    t = pl.program_id(1)
    tr = x_ref.shape[1]
    n_meta = meta_ref.shape[0]
    first = jnp.concatenate([jnp.zeros((tr - n_meta, x_ref.shape[2]), F32), meta_ref[...]], axis=0)
    h = jnp.where(t == 0, first, x_ref[0])

    hn = _layer_norm_rows(h, lng_ref[...], lnb_ref[...])
    hn_ref[0] = hn
    xb = hn.astype(BF16)

    ag = _dot(xb, wag_ref[...])
    u = ag[:, :CONV_CH] * jax.nn.sigmoid(ag[:, CONV_CH:])
    row = lax.broadcasted_iota(jnp.int32, (tr, CONV_CH), 0)
    u = jnp.where(jnp.logical_or(t > 0, row >= tr - n_meta), u, 0.0)

    @pl.when(t == 0)
    def _():
        ubuf_ref[0:CONV_HALO, :] = jnp.zeros((CONV_HALO, CONV_CH), F32)

    ubuf_ref[CONV_HALO:CONV_HALO + tr, :] = u
    base = CONV_HALO - (CONV_WIDTH - 1)
    acc = jnp.zeros((tr, CONV_CH), F32)
    for r in range(SUBLANES):
        offsets = [o for o in range(base, base + CONV_WIDTH) if o % SUBLANES == r]
        rows = max(offsets) - r + tr
        wbuf_ref[0:rows, :] = ubuf_ref[r:r + rows, :]
        for o in offsets:
            acc = acc + cw_ref[o - base:o - base + 1, :] * wbuf_ref[o - r:o - r + tr, :]
    ubuf_ref[0:CONV_HALO, :] = ubuf_ref[tr:tr + CONV_HALO, :]
    c = _layer_norm_rows(acc + cb_ref[...], lncg_ref[...], lncb_ref[...])
    conv_ref[0] = (c * jax.nn.sigmoid(c)).astype(conv_ref.dtype)

    ctab, s1tab, s2tab = ctab_ref[...], s1tab_ref[...], s2tab_ref[...]
    cos_t, sin_t = cost_ref[...], sint_ref[...]

    qt = _rope_cols(_dot_nt(wqt_ref[...], xb), cos_t, sin_t, N_HEADS)
    qt_ref[0] = (qt * (HEAD_DIM ** -0.5 * LOG2_E)).astype(qt_ref.dtype)
    k = _rope_rows(_dot(xb, wk_ref[...]), ctab, s1tab, s2tab)
    k_ref[0] = k.astype(k_ref.dtype)
    vt_ref[0] = _dot_nt(wvt_ref[...], xb).astype(vt_ref.dtype)

    qit = _rope_cols(_dot_nt(wqit_ref[...], xb), cos_t, sin_t, IDX_HEADS)
    qit_ref[0] = qit.astype(qit_ref.dtype)
    ki = _layer_norm_rows(_dot(xb, wki_ref[...]), lnkg_ref[...], lnkb_ref[...])
    ki_ref[0] = _rope_rows(ki, ctab, s1tab, s2tab).astype(ki_ref.dtype)
    wit = _dot_nt(wwit_ref[...], xb) * (IDX_HEADS ** -0.5)
    wit_ref[0] = wit[:IDX_HEADS]


def _inproj_call(x, batch, meta, tabs, ln_g, ln_b, weights, conv_w, conv_b, lnc_g, lnc_b, lnk_g, lnk_b):
    _, seq, D = x.shape
    B = 1
    tr = SEQ_ALIGN
    tp = seq + tr
    nt = tp // tr
    ctab, s1tab, s2tab, cos_t, sin_t = tabs

    def row_spec(w):
        return pl.BlockSpec((1, tr, w), lambda b, t: (b, t, 0))

    def col_spec(r):
        return pl.BlockSpec((1, r, tr), lambda b, t: (b, 0, t))

    def full(a):
        return pl.BlockSpec(a.shape, lambda b, t: (0,) * a.ndim)

    tab_row = pl.BlockSpec((tr, LANES), lambda b, t: (t, 0))
    tab_col = pl.BlockSpec((ROPE_HALF, tr), lambda b, t: (0, t))
    consts = [ln_g, ln_b, *weights, conv_w, conv_b, lnc_g, lnc_b, lnk_g, lnk_b]
    out_shape = [
        jax.ShapeDtypeStruct((B, tp, D), F32),
        jax.ShapeDtypeStruct((B, tp, CONV_CH), BF16),
        jax.ShapeDtypeStruct((B, ATTN_W, tp), BF16),
        jax.ShapeDtypeStruct((B, tp, ATTN_W), BF16),
        jax.ShapeDtypeStruct((B, ATTN_W, tp), BF16),
        jax.ShapeDtypeStruct((B, IDX_HEADS * IDX_DIM, tp), BF16),
        jax.ShapeDtypeStruct((B, tp, 2 * IDX_DIM), BF16),
        jax.ShapeDtypeStruct((B, IDX_HEADS, tp), F32),
    ]
    out_specs = [row_spec(D), row_spec(CONV_CH), col_spec(ATTN_W), row_spec(ATTN_W), col_spec(ATTN_W),
                 col_spec(IDX_HEADS * IDX_DIM), row_spec(2 * IDX_DIM), col_spec(IDX_HEADS)]
    return pl.pallas_call(
        _inproj_kernel,
        grid=(B, nt),
        in_specs=[pl.BlockSpec((1, tr, D), lambda b, t: (batch, jnp.maximum(t - 1, 0), 0)), full(meta),
                  tab_row, tab_row, tab_row, tab_col, tab_col] + [full(a) for a in consts],
        out_specs=out_specs,
        out_shape=out_shape,
        scratch_shapes=[pltpu.VMEM((CONV_HALO + tr, CONV_CH), F32),
                        pltpu.VMEM((CONV_HALO + tr, CONV_CH), F32)],
        compiler_params=pltpu.CompilerParams(
            dimension_semantics=("arbitrary", "arbitrary"), vmem_limit_bytes=VMEM_LIMIT),
        name="inproj",
    )(x, meta, ctab, s1tab, s2tab, cos_t, sin_t, *consts)


def _key_to_float(key):
    bits = jnp.where(key >= 0, key, key ^ jnp.int32(0x7FFFFFFF))
    f = pltpu.bitcast(bits, F32)
    return jnp.where(key < jnp.int32(KEY_NEG_INF), NEG_INF, f)


def _tree(parts, op):
    parts = list(parts)
    while len(parts) > 1:
        nxt = [op(parts[j], parts[j + 1]) for j in range(0, len(parts) - 1, 2)]
        if len(parts) % 2:
            nxt.append(parts[-1])
        parts = nxt
    return parts[0]


def _fold_rows(x, rows, op):
    return _tree([x[j * rows:(j + 1) * rows] for j in range(x.shape[0] // rows)], op)


def _dsa_kernel(qt_ref, qit_ref, wit_ref, k_ref, ki_ref, vt_ref, wg_ref, wu_ref, wd_ref, *rest,
                k_sel, n_pad, n_aliased):
    (o_ref, wgb_ref, wub_ref, wdb_ref,
     sc_ref, sh_ref, qm_ref, qim_ref, m_ref, alpha_ref, shift_ref, lg_ref, acc_ref) = rest[n_aliased:]

    wgb_ref[...] = wg_ref[...].astype(BF16)
    wub_ref[...] = wu_ref[...].astype(BF16)
    wdb_ref[...] = wd_ref[...].astype(BF16)

    i = pl.program_id(1)
    tq = qt_ref.shape[2]
    n_chunks = (i * tq + tq + K_CHUNK - 1) // K_CHUNK
    v_rows = HEAD_DIM + BF16_SUBLANES

    def causal_mask(k0, rows=K_CHUNK):
        kpos = k0 + lax.broadcasted_iota(jnp.int32, (rows, tq), 0)
        return kpos <= i * tq + lax.broadcasted_iota(jnp.int32, (rows, tq), 1)

    def rows8(x):
        return jnp.broadcast_to(x, (SUBLANES, tq))

    def tiles(x):
        return x.reshape(x.shape[0] // SUBLANES, SUBLANES, tq)

    def head_slab(ref, h):
        slab = ref[0, (h // 2) * LANES:(h // 2 + 1) * LANES, :]
        zeros = jnp.zeros((HEAD_DIM, tq), slab.dtype)
        if h % 2 == 0:
            return jnp.concatenate([slab[:HEAD_DIM], zeros], axis=0)
        return jnp.concatenate([zeros, slab[HEAD_DIM:]], axis=0)

    for h in range(N_HEADS):
        qm_ref[h] = head_slab(qt_ref, h)
    for h in range(IDX_HEADS):
        qim_ref[h] = head_slab(qit_ref, h)
    wit = wit_ref[0]
    w_heads = [rows8(wit[h:h + 1]) for h in range(IDX_HEADS)]

    def for_each_chunk(chunk_fn, unroll):
        def group(j, carry):
            for u in range(unroll):
                chunk_fn(unroll * j + u, u % 2)
            return carry

        def single(c, carry):
            chunk_fn(c, 0)
            return carry

        n_groups = n_chunks // unroll
        lax.fori_loop(0, n_groups, group, 0)
        lax.fori_loop(n_groups * unroll, n_chunks, single, 0)

    def score_chunk(c, slot):
        del slot
        for s in range(K_CHUNK // K_SUB):
            k0 = pl.multiple_of(c * K_CHUNK + s * K_SUB, K_SUB)
            kic = ki_ref[0, pl.ds(k0, K_SUB), :]
            acc = jnp.zeros((K_SUB // SUBLANES, SUBLANES, tq), F32)
            for h in range(IDX_HEADS):
                acc = acc + w_heads[h][None] * jnp.maximum(tiles(_dot(kic, qim_ref[h])), 0.0)
            acc = (acc * (IDX_DIM ** -0.5)).reshape(K_SUB, tq)
            acc = jnp.where(causal_mask(k0, K_SUB), acc, NEG_INF)
            sc_ref[pl.ds(k0, K_SUB), :] = acc
            hi_bits = pltpu.bitcast(acc, jnp.int32) & jnp.int32(-65536)
            sh_ref[pl.ds(k0, K_SUB), :] = pltpu.bitcast(hi_bits, F32).astype(COARSE)

    for_each_chunk(score_chunk, 4)

    def drop_padding_keys():
        sh_ref[0:n_pad, :] = jnp.full((n_pad, tq), NEG_INF, COARSE)

    sc_ref[0:n_pad, :] = jnp.full((n_pad, tq), NEG_INF, F32)
    drop_padding_keys()

    def count_all(ref, thr_tile, preds):
        rows = thr_tile.shape[0]
        one, zero = jnp.ones((), ref.dtype), jnp.zeros((), ref.dtype)

        def body(c, cnts):
            k0 = pl.multiple_of(c * K_CHUNK, K_CHUNK)
            s = ref[pl.ds(k0, K_CHUNK), :]
            out = []
            for cnt, p in zip(cnts, preds):
                hits = [jnp.where(p(s[j * rows:(j + 1) * rows], thr_tile), one, zero)
                        for j in range(K_CHUNK // rows)]
                out.append(cnt + _tree(hits, jnp.add).astype(F32))
            return tuple(out)

        def group(j, cnts):
            for u in range(COUNT_UNROLL):
                cnts = body(COUNT_UNROLL * j + u, cnts)
            return cnts

        init = tuple(jnp.zeros((rows, tq), F32) for _ in preds)
        n_groups = n_chunks // COUNT_UNROLL
        cnts = lax.fori_loop(0, n_groups, group, init)
        cnts = lax.fori_loop(n_groups * COUNT_UNROLL, n_chunks, body, cnts)
        return [rows8(jnp.sum(cnt, axis=0, keepdims=True)) for cnt in cnts]

    def search_body(it, tkey, coarse):
        cand = tkey + lax.shift_left(jnp.int32(1), 31 - it)
        cf = _key_to_float(cand)
        if coarse:
            cf = pltpu.bitcast(pltpu.bitcast(cf, jnp.int32) & jnp.int32(-65536), F32).astype(COARSE)
            cf = jnp.concatenate([cf] * (BF16_SUBLANES // SUBLANES), axis=0)
        cnt, = count_all(sh_ref if coarse else sc_ref, cf, [lambda s, t: s >= t])
        return jnp.where(cnt >= k_sel, cand, tkey)

    tkey = jnp.full((SUBLANES, tq), INT_MIN, jnp.int32)
    tkey = lax.fori_loop(0, 16, functools.partial(search_body, coarse=True), tkey)
    tkey = lax.fori_loop(16, 32, functools.partial(search_body, coarse=False), tkey)
    thr = _key_to_float(tkey)

    n_ge, n_gt = count_all(sc_ref, thr, [lambda s, t: s >= t, lambda s, t: s > t])
    need = k_sel - n_gt
    has_ties = jnp.max(n_ge) > k_sel

    @pl.when(jnp.logical_not(has_ties))
    def _():
        def body(c, carry):
            k0 = pl.multiple_of(c * K_CHUNK, K_CHUNK)
            s = sc_ref[pl.ds(k0, K_CHUNK), :]
            bias = jnp.where(tiles(s) >= thr[None], 0.0, NEG_INF).reshape(K_CHUNK, tq)
            sh_ref[pl.ds(k0, K_CHUNK), :] = jnp.where(causal_mask(k0), bias, NEG_INF).astype(COARSE)
            return carry
        lax.fori_loop(0, n_chunks, body, 0)

    @pl.when(has_ties)
    def _():
        r_i = lax.broadcasted_iota(jnp.int32, (K_CHUNK, K_CHUNK), 0)
        c_i = lax.broadcasted_iota(jnp.int32, (K_CHUNK, K_CHUNK), 1)
        lower = jnp.where(c_i <= r_i, 1.0, 0.0).astype(BF16)
        thr_row, need_row = thr[0:1], need[0:1]

        def body(c, seen):
            k0 = pl.multiple_of(c * K_CHUNK, K_CHUNK)
            s = sc_ref[pl.ds(k0, K_CHUNK), :]
            eq = jnp.where(s == thr_row, 1.0, 0.0)
            rank = _dot(lower, eq.astype(BF16)) + seen
            keep_tie = jnp.where(rank <= need_row, eq, 0.0)
            sel = jnp.where(s > thr_row, 1.0, keep_tie)
            sh_ref[pl.ds(k0, K_CHUNK), :] = jnp.where(
                sel > 0.0, jnp.where(causal_mask(k0), 0.0, NEG_INF), NEG_INF).astype(COARSE)
            return rank[K_CHUNK - 1:K_CHUNK, :]
        lax.fori_loop(0, n_chunks, body, jnp.zeros((1, tq), F32))

    drop_padding_keys()

    m_ref[...] = jnp.full(m_ref.shape, NEG_INF, F32)
    acc_ref[...] = jnp.zeros(acc_ref.shape, F32)
    ones_rows = jnp.ones((BF16_SUBLANES, K_CHUNK), BF16)

    def logits_stage(c, slot):
        k0 = pl.multiple_of(c * K_CHUNK, K_CHUNK)
        for h in range(N_HEADS):
            pair = slice((h // 2) * LANES, (h // 2 + 1) * LANES)
            cmax = []
            for s in range(K_CHUNK // K_SUB):
                rows = pl.ds(pl.multiple_of(k0 + s * K_SUB, K_SUB), K_SUB)
                lg = _dot(k_ref[0, rows, pair], qm_ref[h]).astype(BF16) + sh_ref[rows, :].astype(BF16)
                lg_ref[slot, h, s * K_SUB:(s + 1) * K_SUB, :] = lg
                cmax.append(_fold_rows(lg, BF16_SUBLANES, jnp.maximum))
            cmax = _tree(cmax, jnp.maximum).astype(F32)
            m_old = m_ref[h]
            m_new = jnp.maximum(m_old, rows8(jnp.max(cmax, axis=0, keepdims=True)))
            m_safe = jnp.where(m_new == NEG_INF, 0.0, m_new)
            alpha_ref[slot, h] = jnp.exp2(m_old - m_safe)
            shift_ref[slot, h] = m_safe
            m_ref[h] = m_new

    def values_stage(c, slot):
        k0 = pl.multiple_of(c * K_CHUNK, K_CHUNK)
        for h in range(N_HEADS):
            m_safe = shift_ref[slot, h].astype(BF16)
            m_tile = jnp.concatenate([m_safe] * (BF16_SUBLANES // SUBLANES), axis=0)
            lg = lg_ref[slot, h].reshape(K_CHUNK // BF16_SUBLANES, BF16_SUBLANES, tq)
            p = jnp.exp2(lg - m_tile[None]).reshape(K_CHUNK, tq)
            v_aug = jnp.concatenate(
                [vt_ref[0, h * HEAD_DIM:(h + 1) * HEAD_DIM, pl.ds(k0, K_CHUNK)], ones_rows], axis=0)
            pv = _dot(v_aug, p)
            acc = acc_ref[h].reshape(v_rows // SUBLANES, SUBLANES, tq) * alpha_ref[slot, h][None]
            acc_ref[h] = acc.reshape(v_rows, tq) + pv

    def attn_chunk(c, slot):
        logits_stage(c, slot)
        values_stage(c, slot)

    for_each_chunk(attn_chunk, 4)

    for pair in range(N_HEADS // 2):
        halves = []
        for h in (2 * pair, 2 * pair + 1):
            a = acc_ref[h]
            total = a[HEAD_DIM:HEAD_DIM + 1]
            halves.append(a[:HEAD_DIM] / jnp.where(total == 0.0, 1.0, total))
        out_t = jnp.concatenate(halves, axis=0)
        o_ref[0, :, pair * LANES:(pair + 1) * LANES] = out_t.T.astype(o_ref.dtype)


def _dsa_call(qt, qit, wit, k, ki, vt, k_sel, n_pad, expert_w, expert_w_bf16, part, n_parts):
    B, tp, _ = k.shape
    nq = tp // Q_TILE
    n_exp = expert_w[0].shape[0]
    per_call = n_exp // n_parts
    assert per_call * n_parts == n_exp
    per_step = next(d for d in range(1, per_call + 1) if per_call % d == 0 and per_call // d <= nq)
    n_steps = per_call // per_step

    def q_cols(r):
        return pl.BlockSpec((1, r, Q_TILE), lambda b, i: (b, 0, i))

    def per_batch(a):
        return pl.BlockSpec((1,) + a.shape[1:], lambda b, i: (b, 0, 0), pipeline_mode=pl.Buffered(1))

    def expert_spec(a):
        return pl.BlockSpec((per_step,) + a.shape[1:],
                            lambda b, i: (part * n_steps + jnp.minimum(i, n_steps - 1), 0, 0))

    in_specs = [q_cols(ATTN_W), q_cols(IDX_HEADS * IDX_DIM), q_cols(IDX_HEADS),
                per_batch(k), per_batch(ki), per_batch(vt)] + [expert_spec(a) for a in expert_w]
    args = [qt, qit, wit, k, ki, vt, *expert_w]
    aliases = {}
    if expert_w_bf16 is not None:
        for j, a in enumerate(expert_w_bf16):
            aliases[len(args)] = 1 + j
            in_specs.append(pl.BlockSpec(memory_space=pl.ANY))
            args.append(a)

    return pl.pallas_call(
        functools.partial(_dsa_kernel, k_sel=k_sel, n_pad=n_pad, n_aliased=len(aliases)),
        grid=(B, nq),
        in_specs=in_specs,
        out_specs=[pl.BlockSpec((1, Q_TILE, ATTN_W), lambda b, i: (b, i, 0))] + [expert_spec(a) for a in expert_w],
        out_shape=[jax.ShapeDtypeStruct((B, tp, ATTN_W), BF16)]
        + [jax.ShapeDtypeStruct(a.shape, BF16) for a in expert_w],
        input_output_aliases=aliases,
        scratch_shapes=[pltpu.VMEM((tp, Q_TILE), F32),
                        pltpu.VMEM((tp, Q_TILE), COARSE),
                        pltpu.VMEM((N_HEADS, LANES, Q_TILE), BF16),
                        pltpu.VMEM((IDX_HEADS, LANES, Q_TILE), BF16),
                        pltpu.VMEM((N_HEADS, SUBLANES, Q_TILE), F32),
                        pltpu.VMEM((2, N_HEADS, SUBLANES, Q_TILE), F32),
                        pltpu.VMEM((2, N_HEADS, SUBLANES, Q_TILE), F32),
                        pltpu.VMEM((2, N_HEADS, K_CHUNK, Q_TILE), BF16),
                        pltpu.VMEM((N_HEADS, HEAD_DIM + BF16_SUBLANES, Q_TILE), F32)],
        compiler_params=pltpu.CompilerParams(
            dimension_semantics=("arbitrary", "arbitrary"), vmem_limit_bytes=VMEM_LIMIT),
        name="dsa",
    )(*args)


def _max_all(x):
    return jnp.max(jnp.max(x, axis=1, keepdims=True), axis=0, keepdims=True)


def _sum_all(x):
    return jnp.sum(jnp.sum(x, axis=1, keepdims=True), axis=0, keepdims=True)


def _pack_bf16_pairs(x):
    w = x.shape[1] // 2
    lo = pltpu.bitcast(x[:, :w].astype(BF16).astype(F32), jnp.int32)
    hi = pltpu.bitcast(x[:, w:].astype(BF16).astype(F32), jnp.int32)
    return lax.shift_right_logical(lo, 16) | (hi & jnp.int32(-65536))


def _unpack_bf16_pairs(p):
    lo = pltpu.bitcast(lax.shift_left(p, 16), F32)
    hi = pltpu.bitcast(p & jnp.int32(-65536), F32)
    return jnp.concatenate([lo, hi], axis=1)


def _router_topk(logits_t, rbias):
    r = logits_t.shape[1]
    shape3 = (N_GROUPS, GROUP_SIZE, r)
    scores = jax.nn.sigmoid(logits_t).reshape(shape3)
    biased = scores + rbias.reshape(N_GROUPS, GROUP_SIZE, 1)
    in_grp = lax.broadcasted_iota(jnp.int32, shape3, 1).astype(F32)
    m1 = jnp.max(biased, axis=1, keepdims=True)
    first = jnp.min(jnp.where(biased == m1, in_grp, float(GROUP_SIZE)), axis=1, keepdims=True)
    m2 = jnp.max(jnp.where(in_grp == first, NEG_INF, biased), axis=1, keepdims=True)
    cur = m1 + m2

    grp_idx = lax.broadcasted_iota(jnp.int32, (N_GROUPS, 1, r), 0).astype(F32)
    grp_sel = jnp.zeros((N_GROUPS, 1, r), F32)
    for _ in range(TOPK_GROUPS):
        m = jnp.max(cur, axis=0, keepdims=True)
        pick = grp_idx == jnp.min(jnp.where(cur == m, grp_idx, float(N_GROUPS)), axis=0, keepdims=True)
        grp_sel = jnp.where(pick, 1.0, grp_sel)
        cur = jnp.where(pick, NEG_INF, cur)

    cur = jnp.where(jnp.broadcast_to(grp_sel, shape3) > 0.0, biased, NEG_INF)
    exp_idx = lax.broadcasted_iota(jnp.int32, shape3, 0).astype(F32) * GROUP_SIZE + in_grp
    chosen = jnp.zeros(shape3, F32)
    ids, wts = [], []
    for _ in range(TOP_K):
        m = _max_all(cur)
        first = -_max_all(-jnp.where(cur == m, exp_idx, float(N_EXPERTS)))
        pick = exp_idx == first
        chosen = jnp.where(pick, 1.0, chosen)
        cur = jnp.where(pick, NEG_INF, cur)
        ids.append(first.reshape(1, r))
        wts.append(_sum_all(jnp.where(pick, scores, 0.0)).reshape(1, r))
    ids = jnp.concatenate(ids, axis=0)
    wts = jnp.concatenate(wts, axis=0)
    gates = wts / jnp.sum(wts, axis=0, keepdims=True) * ROUTED_SCALE
    return ids, gates, chosen.reshape(N_EXPERTS, r), exp_idx


def _route_kernel(conv_ref, attn_ref, hn_ref, woc_ref, woa_ref, g1_ref, b1_ref,
                  wsg_ref, wsu_ref, wsd_ref, wrh_ref, wrl_ref, rb_ref,
                  xp_ref, base_ref, gates_ref, ek_ref, rk_ref, cnt_ref):
    step = pl.program_id(0)
    tr = hn_ref.shape[0]
    mix = _dot(conv_ref[...], woc_ref[...]) + _dot(attn_ref[...], woa_ref[...])
    h1 = _layer_norm_rows(DN_ALPHA * hn_ref[...] + mix, g1_ref[...], b1_ref[...])
    xb = h1.astype(BF16)
    xp_ref[...] = _pack_bf16_pairs(h1)

    shared = jax.nn.silu(_dot(xb, wsg_ref[...])) * _dot(xb, wsu_ref[...])
    base_ref[...] = DN_ALPHA * h1 + _dot(shared.astype(BF16), wsd_ref[...])

    x_lo = (h1 - xb.astype(F32)).astype(BF16)
    logits_t = (_dot_nt(wrh_ref[...], xb) + _dot_nt(wrh_ref[...], x_lo) + _dot_nt(wrl_ref[...], xb))
    ids, gates, chosen, exp_idx = _router_topk(logits_t, rb_ref[...])
    padded = jnp.concatenate([gates, jnp.zeros((LANES - TOP_K, tr), F32)], axis=0)
    gates_ref[...] = padded.T
    ek_ref[...] = ids.astype(jnp.int32)

    @pl.when(step == 0)
    def _():
        cnt_ref[...] = jnp.zeros(cnt_ref.shape, F32)

    t_i = lax.broadcasted_iota(jnp.int32, (tr, tr), 0)
    t_j = lax.broadcasted_iota(jnp.int32, (tr, tr), 1)
    before = jnp.where(t_i < t_j, 1.0, 0.0).astype(BF16)
    chosen_b = chosen.astype(BF16)
    running = cnt_ref[...]
    rank = _dot(chosen_b, before) + jnp.concatenate([running] * (tr // LANES), axis=1)
    rank3 = rank.reshape(N_GROUPS, GROUP_SIZE, tr)
    rk = [_sum_all(jnp.where(exp_idx == ids[k:k + 1].reshape(1, 1, tr), rank3, 0.0)).reshape(1, tr)
          for k in range(TOP_K)]
    rk_ref[...] = jnp.concatenate(rk, axis=0).astype(jnp.int32)
    cnt_ref[...] = running + _dot(chosen_b, jnp.ones((tr, LANES), BF16))


def _route_call(conv, attn, hn, woc, woa, g1, b1, wsg, wsu, wsd, wrh, wrl, rbias):
    n, D = hn.shape
    tr = _pick_tile(n, (768, 512, 256))

    def row_spec(w):
        return pl.BlockSpec((tr, w), lambda i: (i, 0))

    def col_spec(r):
        return pl.BlockSpec((r, tr), lambda i: (0, i))

    def full(a):
        return pl.BlockSpec(a.shape, lambda i: (0,) * a.ndim)

    consts = [woc, woa, g1, b1, wsg, wsu, wsd, wrh, wrl, rbias]
    return pl.pallas_call(
        _route_kernel,
        grid=(n // tr,),
        in_specs=[row_spec(CONV_CH), row_spec(ATTN_W), row_spec(D)] + [full(a) for a in consts],
        out_specs=[row_spec(D // 2), row_spec(D), row_spec(LANES), col_spec(TOP_K), col_spec(TOP_K),
                   pl.BlockSpec((N_EXPERTS, LANES), lambda i: (0, 0))],
        out_shape=[jax.ShapeDtypeStruct((n, D // 2), jnp.int32),
                   jax.ShapeDtypeStruct((n, D), F32),
                   jax.ShapeDtypeStruct((n, LANES), F32),
                   jax.ShapeDtypeStruct((TOP_K, n), jnp.int32),
                   jax.ShapeDtypeStruct((TOP_K, n), jnp.int32),
                   jax.ShapeDtypeStruct((N_EXPERTS, LANES), F32)],
        compiler_params=pltpu.CompilerParams(
            dimension_semantics=("arbitrary",), vmem_limit_bytes=VMEM_LIMIT),
        name="route",
    )(conv, attn, hn, *consts)


def _plan_kernel(cnt_ref, ek_ref, rk_ref, slot_ref, blk_ref, *, n_blocks):
    tr = ek_ref.shape[1]
    counts = cnt_ref[...]
    padded = jnp.ceil(counts / EXPERT_BLOCK) * EXPERT_BLOCK
    starts = [jnp.zeros((1, LANES), F32)]
    for e in range(1, N_EXPERTS):
        starts.append(starts[-1] + padded[e - 1:e])
    ek = ek_ref[...]
    seg = jnp.zeros(ek.shape, F32)
    for e in range(N_EXPERTS):
        seg = jnp.where(ek == e, jnp.concatenate([starts[e]] * (tr // LANES), axis=1), seg)
    slot_ref[...] = seg.astype(jnp.int32) + rk_ref[...]

    ends = jnp.concatenate(starts, axis=0) + padded
    w = blk_ref.shape[1]
    blk_start = lax.broadcasted_iota(jnp.int32, (N_EXPERTS, w), 1).astype(F32) * EXPERT_BLOCK
    done = jnp.where(jnp.concatenate([ends] * (w // LANES), axis=1) <= blk_start, 1.0, 0.0)
    owner = jnp.minimum(jnp.sum(done, axis=0, keepdims=True), N_EXPERTS - 1.0)
    used = jnp.concatenate([ends[N_EXPERTS - 1:] * (1.0 / EXPERT_BLOCK)] * (w // LANES), axis=1)
    lane = lax.broadcasted_iota(jnp.int32, (1, w), 1)
    table = jnp.where(lane == n_blocks, used, owner)
    blk_ref[...] = jnp.broadcast_to(table, blk_ref.shape).astype(jnp.int32)


def _plan_call(cnt, ek, rk, n_blocks):
    k, n = ek.shape
    tr = _pick_tile(n, (768, 512, 256))
    w = -(-(n_blocks + 1) // LANES) * LANES
    col = pl.BlockSpec((k, tr), lambda i: (0, i))
    return pl.pallas_call(
        functools.partial(_plan_kernel, n_blocks=n_blocks),
        grid=(n // tr,),
        in_specs=[pl.BlockSpec(cnt.shape, lambda i: (0, 0)), col, col],
        out_specs=[col, pl.BlockSpec((SUBLANES, w), lambda i: (0, 0))],
        out_shape=[jax.ShapeDtypeStruct((k, n), jnp.int32), jax.ShapeDtypeStruct((SUBLANES, w), jnp.int32)],
        compiler_params=pltpu.CompilerParams(dimension_semantics=("arbitrary",)),
        name="plan",
    )(cnt, ek, rk)


def _sc_workers():
    info = plsc.get_sparse_core_info()
    return info.num_cores, info.num_subcores


def _sc_scatter_rows(src, slots, n_out):
    n_src, d = src.shape
    units = slots.shape[0]
    n_chunks = n_src // SC_ROWS
    n_cores, n_sub = _sc_workers()
    n_workers = n_cores * n_sub
    assert slots.shape[1] == SC_ROWS

    def body(src_hbm, slot_hbm, out_hbm, idx_v, rows_v):
        wid = lax.axis_index("s") * n_cores + lax.axis_index("c")

        @pl.loop(0, -(-units // n_workers))
        def _(j):
            u = j * n_workers + wid

            @pl.when(u < units)
            def _():
                pltpu.sync_copy(slot_hbm.at[pl.ds(u, 1)], idx_v)
                pltpu.sync_copy(src_hbm.at[pl.ds((u % n_chunks) * SC_ROWS, SC_ROWS)], rows_v)
                pltpu.sync_copy(rows_v, out_hbm.at[idx_v.at[0]])

    return pl.kernel(
        body, out_type=jax.ShapeDtypeStruct((n_out, d), src.dtype),
        mesh=plsc.VectorSubcoreMesh(core_axis_name="c", subcore_axis_name="s"),
        scratch_types=[pltpu.VMEM((1, SC_ROWS), jnp.int32), pltpu.VMEM((SC_ROWS, d), src.dtype)],
        name="dispatch_rows",
    )(src, slots)


def _sc_gather_rows(table, slots):
    units = slots.shape[0]
    d = table.shape[1]
    n_cores, n_sub = _sc_workers()
    n_workers = n_cores * n_sub
    assert slots.shape[1] == SC_ROWS

    def body(table_hbm, slot_hbm, out_hbm, idx_v, rows_v):
        wid = lax.axis_index("s") * n_cores + lax.axis_index("c")

        @pl.loop(0, -(-units // n_workers))
        def _(j):
            u = j * n_workers + wid

            @pl.when(u < units)
            def _():
                pltpu.sync_copy(slot_hbm.at[pl.ds(u, 1)], idx_v)
                pltpu.sync_copy(table_hbm.at[idx_v.at[0]], rows_v)
                pltpu.sync_copy(rows_v, out_hbm.at[pl.ds(u * SC_ROWS, SC_ROWS)])

    return pl.kernel(
        body, out_type=jax.ShapeDtypeStruct((units * SC_ROWS, d), table.dtype),
        mesh=plsc.VectorSubcoreMesh(core_axis_name="c", subcore_axis_name="s"),
        scratch_types=[pltpu.VMEM((1, SC_ROWS), jnp.int32), pltpu.VMEM((SC_ROWS, d), table.dtype)],
        name="collect_rows",
    )(table, slots)


def _expert_kernel(blk_ref, xs_ref, wg_ref, wu_ref, wd_ref, ys_ref):
    @pl.when(pl.program_id(0) < blk_ref[pl.num_programs(0)])
    def _():
        x = _unpack_bf16_pairs(xs_ref[...]).astype(BF16)
        hdn = (jax.nn.silu(_dot(x, wg_ref[0])) * _dot(x, wu_ref[0])).astype(BF16)
        ys_ref[...] = _pack_bf16_pairs(_dot(hdn, wd_ref[0]))


def _expert_call(blk_exp, xs, w_gate, w_up, w_down):
    p, half = xs.shape
    _, D, d_exp = w_gate.shape
    n_blocks = p // EXPERT_BLOCK
    assert blk_exp.shape == (n_blocks + 1,)

    def row_block(b, blk):
        return (jnp.minimum(b, blk[n_blocks] - 1), 0)

    rows = pl.BlockSpec((EXPERT_BLOCK, half), row_block)
    grid_spec = pltpu.PrefetchScalarGridSpec(
        num_scalar_prefetch=1,
        grid=(n_blocks,),
        in_specs=[rows,
                  pl.BlockSpec((1, D, d_exp), lambda b, blk: (blk[b], 0, 0)),
                  pl.BlockSpec((1, D, d_exp), lambda b, blk: (blk[b], 0, 0)),
                  pl.BlockSpec((1, d_exp, D), lambda b, blk: (blk[b], 0, 0))],
        out_specs=rows,
    )
    return pl.pallas_call(
        _expert_kernel,
        grid_spec=grid_spec,
        out_shape=jax.ShapeDtypeStruct((p, half), jnp.int32),
        compiler_params=pltpu.CompilerParams(
            dimension_semantics=("arbitrary",), vmem_limit_bytes=VMEM_LIMIT),
        name="experts",
    )(blk_exp, xs, w_gate, w_up, w_down)


def _combine_kernel(g_ref, gates_ref, base_ref, g2_ref, b2_ref, *rest):
    o_ref = rest[-1]
    gates = gates_ref[...]
    acc = base_ref[...]
    for k in range(g_ref.shape[0]):
        acc = acc + _unpack_bf16_pairs(g_ref[k]) * gates[:, k:k + 1]
    o_ref[0] = _layer_norm_rows(acc, g2_ref[...], b2_ref[...])


def _combine_call(g, gates, base, g2, b2, result, batch, n_batch):
    k, n, half = g.shape
    D = base.shape[1]
    tr = SEQ_ALIGN
    seq = n - tr

    def row_spec(w):
        return pl.BlockSpec((tr, w), lambda i: (i, 0))

    vec = pl.BlockSpec((1, D), lambda i: (0, 0))
    in_specs = [pl.BlockSpec((k, tr, half), lambda i: (0, i, 0)), row_spec(LANES), row_spec(D), vec, vec]
    args = [g, gates, base, g2, b2]
    aliases = {}
    if result is not None:
        in_specs.append(pl.BlockSpec(memory_space=pl.ANY))
        args.append(result)
        aliases = {len(args) - 1: 0}
    return pl.pallas_call(
        _combine_kernel,
        grid=(n // tr,),
        in_specs=in_specs,
        out_specs=pl.BlockSpec((1, tr, D), lambda i: (batch, jnp.maximum(i - 1, 0), 0)),
        out_shape=jax.ShapeDtypeStruct((n_batch, seq, D), F32),
        input_output_aliases=aliases,
        compiler_params=pltpu.CompilerParams(
            dimension_semantics=("arbitrary",), vmem_limit_bytes=VMEM_LIMIT),
        name="combine",
    )(*args)


def _rope_tables(tp, lead):
    pos = jnp.arange(tp, dtype=F32) - lead
    inv = jnp.power(ROPE_THETA, -2.0 * jnp.arange(ROPE_HALF, dtype=F32) / ROPE_DIM)
    ang = pos[:, None] * inv[None, :]
    cos, sin = jnp.cos(ang), jnp.sin(ang)
    zeros = jnp.zeros((tp, HEAD_DIM - ROPE_DIM), F32)
    zh = jnp.zeros((tp, ROPE_HALF), F32)
    c64 = jnp.concatenate([cos, cos, jnp.ones_like(zeros)], axis=1)
    s1_64 = jnp.concatenate([-sin, zh, zeros], axis=1)
    s2_64 = jnp.concatenate([zh, sin, zeros], axis=1)
    rep = LANES // HEAD_DIM
    return (jnp.tile(c64, (1, rep)), jnp.tile(s1_64, (1, rep)), jnp.tile(s2_64, (1, rep)),
            cos.T, sin.T)


def kernel(x, meta_tokens, ln_emb_g, ln_emb_b, w_in, conv_w, conv_b, ln_conv_g, ln_conv_b, ln_kidx_g, ln_kidx_b, w_out, ln1_g, ln1_b, w_router, router_bias, w_gate, w_up, w_down, ws_gate, ws_up, ws_down, ln2_g, ln2_b):
    B, seq, D = x.shape
    assert w_in.shape[0] == DEPTH
    assert seq % SEQ_ALIGN == 0 and meta_tokens.shape[0] == N_META <= SEQ_ALIGN
    k_sel = min(INDEX_TOPK, seq // 4)
    tp = seq + SEQ_ALIGN
    n_pad = SEQ_ALIGN - N_META

    def row(a):
        return a.reshape(1, -1).astype(F32)

    w = w_in[0]
    o = 0
    parts = []
    for width in (CONV_CH, CONV_CH, ATTN_W, ATTN_W, ATTN_W, IDX_HEADS * IDX_DIM, IDX_DIM, IDX_HEADS):
        parts.append(w[:, o:o + width])
        o += width
    wa, wgl, wq, wk, wv, wqi, wki, wwi = parts
    wwi_t = jnp.concatenate([wwi.T, jnp.zeros((BF16_SUBLANES - IDX_HEADS, D), w.dtype)], axis=0)
    weights = (jnp.concatenate([wa, wgl], axis=1).astype(BF16), wq.T.astype(BF16), wk.astype(BF16),
               wv.T.astype(BF16), wqi.T.astype(BF16), jnp.concatenate([wki, wki], axis=1).astype(BF16),
               wwi_t.astype(BF16))

    def twice(a):
        return row(jnp.concatenate([a, a]))

    tabs = _rope_tables(tp, n_pad)
    inproj_consts = (row(ln_emb_g), row(ln_emb_b), weights, conv_w[0].astype(F32), row(conv_b[0]),
                     row(ln_conv_g[0]), row(ln_conv_b[0]), twice(ln_kidx_g[0]), twice(ln_kidx_b[0]))
    wr_t = w_router[0].T.astype(F32)
    wr_hi = wr_t.astype(BF16)
    wr_lo = (wr_t - wr_hi.astype(F32)).astype(BF16)
    route_consts = (w_out[0][:CONV_CH].astype(BF16), w_out[0][CONV_CH:].astype(BF16), row(ln1_g[0]),
                    row(ln1_b[0]), ws_gate[0].astype(BF16), ws_up[0].astype(BF16), ws_down[0].astype(BF16),
                    wr_hi, wr_lo, router_bias[0].reshape(-1, 1).astype(F32))
    meta = meta_tokens.astype(F32)
    n_blocks = tp * TOP_K // EXPERT_BLOCK + N_EXPERTS

    expert_w = (w_gate[0], w_up[0], w_down[0])
    expert_w_bf16 = None
    routed = []
    for b in range(B):
        hn, conv, qt, k, vt, qit, ki, wit = _inproj_call(x, b, meta, tabs, *inproj_consts)
        attn, *expert_w_bf16 = _dsa_call(qt, qit, wit, k, ki, vt, k_sel, n_pad, expert_w, expert_w_bf16, b, B)
        xp, base, gates, ek, rk, cnt = _route_call(conv[0], attn[0], hn[0], *route_consts)

        slot, blk = _plan_call(cnt, ek, rk, n_blocks)
        slots = slot.reshape(TOP_K * tp // SC_ROWS, SC_ROWS)
        xs = _sc_scatter_rows(xp, slots, n_blocks * EXPERT_BLOCK)
        routed.append((xs, blk, slots, gates, base))

    result = None
    for b, (xs, blk, slots, gates, base) in enumerate(routed):
        ys = _expert_call(blk[0, :n_blocks + 1], xs, *expert_w_bf16)
        picked = _sc_gather_rows(ys, slots).reshape(TOP_K, tp, D // 2)
        result = _combine_call(picked, gates, base, row(ln2_g[0]), row(ln2_b[0]), result, b, B)
    return result
```

```python
import functools

import numpy as np
import jax
import jax.numpy as jnp
from jax import lax
from jax.experimental import pallas as pl
from jax.experimental.pallas import tpu as pltpu
from jax.experimental.pallas import tpu_sc as plsc

N_META = 16
CONV_CH = 512
CONV_WIDTH = 31
N_HEADS = 8
HEAD_DIM = 64
ATTN_W = N_HEADS * HEAD_DIM
IDX_HEADS = 8
IDX_DIM = 64
INDEX_TOPK = 256
ROPE_DIM = HEAD_DIM // 4
ROPE_HALF = ROPE_DIM // 2
ROPE_THETA = 500000.0
N_EXPERTS = 64
TOP_K = 8
N_GROUPS = 8
GROUP_SIZE = N_EXPERTS // N_GROUPS
TOPK_GROUPS = 4
ROUTED_SCALE = 2.5
LN_EPS = 1e-5
DEPTH = 1
DN_ALPHA = (2.0 * DEPTH) ** 0.25

LANES = 128
Q_TILE = 256
SUBLANES = 8
BF16_SUBLANES = 16
EXPERT_BLOCK = 512
SC_ROWS = 128
K_CHUNK = Q_TILE
K_SUB = 128
COUNT_UNROLL = 4
SEQ_ALIGN = 256
CONV_HALO = 32
VMEM_LIMIT = 56 * 1024 * 1024

F32 = jnp.float32
BF16 = jnp.bfloat16
COARSE = jnp.bfloat16
NEG_INF = float("-inf")
INT_MIN = -2 ** 31
KEY_NEG_INF = -2139095041
LOG2_E = 1.4426950408889634


def _dot(a, b):
    return jnp.dot(a, b, preferred_element_type=F32)


def _dot_nt(a, b):
    return lax.dot_general(a, b, (((1,), (1,)), ((), ())), preferred_element_type=F32)


def _layer_norm_rows(x, g, b):
    mu = jnp.mean(x, axis=-1, keepdims=True)
    xc = x - mu
    var = jnp.mean(xc * xc, axis=-1, keepdims=True)
    return xc * lax.rsqrt(var + LN_EPS) * g + b


def _pick_tile(n, candidates):
    for c in candidates:
        if n % c == 0:
            return c
    raise ValueError(f"no tile for {n}")


def _rope_rows(x, c_tab, s1_tab, s2_tab):
    outs = []
    for j in range(x.shape[1] // LANES):
        xs = x[:, j * LANES:(j + 1) * LANES]
        up = pltpu.roll(xs, LANES - ROPE_HALF, axis=1)
        dn = pltpu.roll(xs, ROPE_HALF, axis=1)
        outs.append(xs * c_tab + up * s1_tab + dn * s2_tab)
    return jnp.concatenate(outs, axis=1)


def _rope_cols(xt, cos_t, sin_t, heads):
    r = xt.shape[1]
    x3 = xt.reshape(heads, HEAD_DIM, r)
    x1 = x3[:, 0:ROPE_HALF, :]
    x2 = x3[:, ROPE_HALF:ROPE_DIM, :]
    n1 = x1 * cos_t - x2 * sin_t
    n2 = x2 * cos_t + x1 * sin_t
    out = jnp.concatenate([n1, n2, x3[:, ROPE_DIM:, :]], axis=1)
    return out.reshape(heads * HEAD_DIM, r)


def _inproj_kernel(x_ref, meta_ref, ctab_ref, s1tab_ref, s2tab_ref, cost_ref, sint_ref,
                   lng_ref, lnb_ref, wag_ref, wqt_ref, wk_ref, wvt_ref, wqit_ref, wki_ref, wwit_ref,
                   cw_ref, cb_ref, lncg_ref, lncb_ref, lnkg_ref, lnkb_ref,
                   hn_ref, conv_ref, qt_ref, k_ref, vt_ref, qit_ref, ki_ref, wit_ref,
                   ubuf_ref, wbuf_ref):
    t = pl.program_id(1)
    tr = x_ref.shape[1]
    n_meta = meta_ref.shape[0]
    first = jnp.concatenate([jnp.zeros((tr - n_meta, x_ref.shape[2]), F32), meta_ref[...]], axis=0)
    h = jnp.where(t == 0, first, x_ref[0])

    hn = _layer_norm_rows(h, lng_ref[...], lnb_ref[...])
    hn_ref[0] = hn
    xb = hn.astype(BF16)

    ag = _dot(xb, wag_ref[...])
    u = ag[:, :CONV_CH] * jax.nn.sigmoid(ag[:, CONV_CH:])
    row = lax.broadcasted_iota(jnp.int32, (tr, CONV_CH), 0)
    u = jnp.where(jnp.logical_or(t > 0, row >= tr - n_meta), u, 0.0)

    @pl.when(t == 0)
    def _():
        ubuf_ref[0:CONV_HALO, :] = jnp.zeros((CONV_HALO, CONV_CH), F32)

    ubuf_ref[CONV_HALO:CONV_HALO + tr, :] = u
    base = CONV_HALO - (CONV_WIDTH - 1)
    acc = jnp.zeros((tr, CONV_CH), F32)
    for r in range(SUBLANES):
        offsets = [o for o in range(base, base + CONV_WIDTH) if o % SUBLANES == r]
        rows = max(offsets) - r + tr
        wbuf_ref[0:rows, :] = ubuf_ref[r:r + rows, :]
        for o in offsets:
            acc = acc + cw_ref[o - base:o - base + 1, :] * wbuf_ref[o - r:o - r + tr, :]
    ubuf_ref[0:CONV_HALO, :] = ubuf_ref[tr:tr + CONV_HALO, :]
    c = _layer_norm_rows(acc + cb_ref[...], lncg_ref[...], lncb_ref[...])
    conv_ref[0] = (c * jax.nn.sigmoid(c)).astype(conv_ref.dtype)

    ctab, s1tab, s2tab = ctab_ref[...], s1tab_ref[...], s2tab_ref[...]
    cos_t, sin_t = cost_ref[...], sint_ref[...]

    qt = _rope_cols(_dot_nt(wqt_ref[...], xb), cos_t, sin_t, N_HEADS)
    qt_ref[0] = (qt * (HEAD_DIM ** -0.5 * LOG2_E)).astype(qt_ref.dtype)
    k = _rope_rows(_dot(xb, wk_ref[...]), ctab, s1tab, s2tab)
    k_ref[0] = k.astype(k_ref.dtype)
    vt_ref[0] = _dot_nt(wvt_ref[...], xb).astype(vt_ref.dtype)

    qit = _rope_cols(_dot_nt(wqit_ref[...], xb), cos_t, sin_t, IDX_HEADS)
    qit_ref[0] = qit.astype(qit_ref.dtype)
    ki = _layer_norm_rows(_dot(xb, wki_ref[...]), lnkg_ref[...], lnkb_ref[...])
    ki_ref[0] = _rope_rows(ki, ctab, s1tab, s2tab).astype(ki_ref.dtype)
    wit = _dot_nt(wwit_ref[...], xb) * (IDX_HEADS ** -0.5)
    wit_ref[0] = wit[:IDX_HEADS]


def _inproj_call(x, batch, meta, tabs, ln_g, ln_b, weights, conv_w, conv_b, lnc_g, lnc_b, lnk_g, lnk_b):
    _, seq, D = x.shape
    B = 1
    tr = SEQ_ALIGN
    tp = seq + tr
    nt = tp // tr
    ctab, s1tab, s2tab, cos_t, sin_t = tabs

    def row_spec(w):
        return pl.BlockSpec((1, tr, w), lambda b, t: (b, t, 0))

    def col_spec(r):
        return pl.BlockSpec((1, r, tr), lambda b, t: (b, 0, t))

    def full(a):
        return pl.BlockSpec(a.shape, lambda b, t: (0,) * a.ndim)

    tab_row = pl.BlockSpec((tr, LANES), lambda b, t: (t, 0))
    tab_col = pl.BlockSpec((ROPE_HALF, tr), lambda b, t: (0, t))
    consts = [ln_g, ln_b, *weights, conv_w, conv_b, lnc_g, lnc_b, lnk_g, lnk_b]
    out_shape = [
        jax.ShapeDtypeStruct((B, tp, D), F32),
        jax.ShapeDtypeStruct((B, tp, CONV_CH), BF16),
        jax.ShapeDtypeStruct((B, ATTN_W, tp), BF16),
        jax.ShapeDtypeStruct((B, tp, ATTN_W), BF16),
        jax.ShapeDtypeStruct((B, ATTN_W, tp), BF16),
        jax.ShapeDtypeStruct((B, IDX_HEADS * IDX_DIM, tp), BF16),
        jax.ShapeDtypeStruct((B, tp, 2 * IDX_DIM), BF16),
        jax.ShapeDtypeStruct((B, IDX_HEADS, tp), F32),
    ]
    out_specs = [row_spec(D), row_spec(CONV_CH), col_spec(ATTN_W), row_spec(ATTN_W), col_spec(ATTN_W),
                 col_spec(IDX_HEADS * IDX_DIM), row_spec(2 * IDX_DIM), col_spec(IDX_HEADS)]
    return pl.pallas_call(
        _inproj_kernel,
        grid=(B, nt),
        in_specs=[pl.BlockSpec((1, tr, D), lambda b, t: (batch, jnp.maximum(t - 1, 0), 0)), full(meta),
                  tab_row, tab_row, tab_row, tab_col, tab_col] + [full(a) for a in consts],
        out_specs=out_specs,
        out_shape=out_shape,
        scratch_shapes=[pltpu.VMEM((CONV_HALO + tr, CONV_CH), F32),
                        pltpu.VMEM((CONV_HALO + tr, CONV_CH), F32)],
        compiler_params=pltpu.CompilerParams(
            dimension_semantics=("arbitrary", "arbitrary"), vmem_limit_bytes=VMEM_LIMIT),
        name="inproj",
    )(x, meta, ctab, s1tab, s2tab, cos_t, sin_t, *consts)


def _key_to_float(key):
    bits = jnp.where(key >= 0, key, key ^ jnp.int32(0x7FFFFFFF))
    f = pltpu.bitcast(bits, F32)
    return jnp.where(key < jnp.int32(KEY_NEG_INF), NEG_INF, f)


def _tree(parts, op):
    parts = list(parts)
    while len(parts) > 1:
        nxt = [op(parts[j], parts[j + 1]) for j in range(0, len(parts) - 1, 2)]
        if len(parts) % 2:
            nxt.append(parts[-1])
        parts = nxt
    return parts[0]


def _fold_rows(x, rows, op):
    return _tree([x[j * rows:(j + 1) * rows] for j in range(x.shape[0] // rows)], op)


def _dsa_kernel(qt_ref, qit_ref, wit_ref, k_ref, ki_ref, vt_ref, wg_ref, wu_ref, wd_ref, *rest,
                k_sel, n_pad, n_aliased):
    (o_ref, wgb_ref, wub_ref, wdb_ref,
     sc_ref, sh_ref, qm_ref, qim_ref, m_ref, alpha_ref, shift_ref, lg_ref, acc_ref) = rest[n_aliased:]

    wgb_ref[...] = wg_ref[...].astype(BF16)
    wub_ref[...] = wu_ref[...].astype(BF16)
    wdb_ref[...] = wd_ref[...].astype(BF16)

    i = pl.program_id(1)
    tq = qt_ref.shape[2]
    n_chunks = (i * tq + tq + K_CHUNK - 1) // K_CHUNK
    v_rows = HEAD_DIM + BF16_SUBLANES

    def causal_mask(k0, rows=K_CHUNK):
        kpos = k0 + lax.broadcasted_iota(jnp.int32, (rows, tq), 0)
        return kpos <= i * tq + lax.broadcasted_iota(jnp.int32, (rows, tq), 1)

    def rows8(x):
        return jnp.broadcast_to(x, (SUBLANES, tq))

    def tiles(x):
        return x.reshape(x.shape[0] // SUBLANES, SUBLANES, tq)

    def head_slab(ref, h):
        slab = ref[0, (h // 2) * LANES:(h // 2 + 1) * LANES, :]
        zeros = jnp.zeros((HEAD_DIM, tq), slab.dtype)
        if h % 2 == 0:
            return jnp.concatenate([slab[:HEAD_DIM], zeros], axis=0)
        return jnp.concatenate([zeros, slab[HEAD_DIM:]], axis=0)

    for h in range(N_HEADS):
        qm_ref[h] = head_slab(qt_ref, h)
    for h in range(IDX_HEADS):
        qim_ref[h] = head_slab(qit_ref, h)
    wit = wit_ref[0]
    w_heads = [rows8(wit[h:h + 1] * (IDX_DIM ** -0.5)) for h in range(IDX_HEADS)]

    def for_each_chunk(chunk_fn, unroll, count=n_chunks):
        def group(j, carry):
            for u in range(unroll):
                chunk_fn(unroll * j + u, u % 2)
            return carry

        def single(c, carry):
            chunk_fn(c, 0)
            return carry

        n_groups = count // unroll
        lax.fori_loop(0, n_groups, group, 0)
        lax.fori_loop(n_groups * unroll, count, single, 0)

    def diagonal_visible(first_row, rows):
        r = first_row + lax.broadcasted_iota(jnp.int32, (rows, tq), 0)
        return r <= lax.broadcasted_iota(jnp.int32, (rows, tq), 1)

    def score_chunk(c, slot, diagonal=False):
        del slot
        for s in range(K_CHUNK // K_SUB):
            k0 = pl.multiple_of(c * K_CHUNK + s * K_SUB, K_SUB)
            kic = ki_ref[0, pl.ds(k0, K_SUB), :]
            acc = jnp.zeros((K_SUB // SUBLANES, SUBLANES, tq), F32)
            for h in range(IDX_HEADS):
                acc = acc + w_heads[h][None] * jnp.maximum(tiles(_dot(kic, qim_ref[h])), 0.0)
            acc = acc.reshape(K_SUB, tq)
            if diagonal:
                acc = jnp.where(diagonal_visible(s * K_SUB, K_SUB), acc, NEG_INF)
            sc_ref[pl.ds(k0, K_SUB), :] = acc
            hi_bits = pltpu.bitcast(acc, jnp.int32) & jnp.int32(-65536)
            sh_ref[pl.ds(k0, K_SUB), :] = pltpu.bitcast(hi_bits, F32).astype(COARSE)

    for_each_chunk(score_chunk, 4, n_chunks - 1)
    score_chunk(n_chunks - 1, 0, diagonal=True)

    def drop_padding_keys():
        sh_ref[0:n_pad, :] = jnp.full((n_pad, tq), NEG_INF, COARSE)

    sc_ref[0:n_pad, :] = jnp.full((n_pad, tq), NEG_INF, F32)
    drop_padding_keys()

    def count_all(ref, thr_tile, preds):
        rows = thr_tile.shape[0]
        one, zero = jnp.ones((), ref.dtype), jnp.zeros((), ref.dtype)

        def body(c, cnts):
            k0 = pl.multiple_of(c * K_CHUNK, K_CHUNK)
            s = ref[pl.ds(k0, K_CHUNK), :]
            out = []
            for cnt, p in zip(cnts, preds):
                hits = [jnp.where(p(s[j * rows:(j + 1) * rows], thr_tile), one, zero)
                        for j in range(K_CHUNK // rows)]
                out.append(cnt + _tree(hits, jnp.add).astype(F32))
            return tuple(out)

        def group(j, cnts):
            for u in range(COUNT_UNROLL):
                cnts = body(COUNT_UNROLL * j + u, cnts)
            return cnts

        init = tuple(jnp.zeros((rows, tq), F32) for _ in preds)
        n_groups = n_chunks // COUNT_UNROLL
        cnts = lax.fori_loop(0, n_groups, group, init)
        cnts = lax.fori_loop(n_groups * COUNT_UNROLL, n_chunks, body, cnts)
        return [rows8(jnp.sum(cnt, axis=0, keepdims=True)) for cnt in cnts]

    def search_body(it, tkey, coarse):
        cand = tkey + lax.shift_left(jnp.int32(1), 31 - it)
        cf = _key_to_float(cand)
        if coarse:
            cf = pltpu.bitcast(pltpu.bitcast(cf, jnp.int32) & jnp.int32(-65536), F32).astype(COARSE)
            cf = jnp.concatenate([cf] * (BF16_SUBLANES // SUBLANES), axis=0)
        cnt, = count_all(sh_ref if coarse else sc_ref, cf, [lambda s, t: s >= t])
        return jnp.where(cnt >= k_sel, cand, tkey)

    tkey = jnp.full((SUBLANES, tq), INT_MIN, jnp.int32)
    tkey = lax.fori_loop(0, 16, functools.partial(search_body, coarse=True), tkey)
    tkey = lax.fori_loop(16, 32, functools.partial(search_body, coarse=False), tkey)
    thr = _key_to_float(tkey)

    n_ge, n_gt = count_all(sc_ref, thr, [lambda s, t: s >= t, lambda s, t: s > t])
    need = k_sel - n_gt
    has_ties = jnp.max(n_ge) > k_sel

    @pl.when(jnp.logical_not(has_ties))
    def _():
        def mask_chunk(c, slot, diagonal=False):
            del slot
            k0 = pl.multiple_of(c * K_CHUNK, K_CHUNK)
            s = sc_ref[pl.ds(k0, K_CHUNK), :]
            bias = jnp.where(tiles(s) >= thr[None], 0.0, NEG_INF).reshape(K_CHUNK, tq)
            if diagonal:
                bias = jnp.where(diagonal_visible(0, K_CHUNK), bias, NEG_INF)
            sh_ref[pl.ds(k0, K_CHUNK), :] = bias.astype(COARSE)

        for_each_chunk(mask_chunk, 4, n_chunks - 1)
        mask_chunk(n_chunks - 1, 0, diagonal=True)

    @pl.when(has_ties)
    def _():
        r_i = lax.broadcasted_iota(jnp.int32, (K_CHUNK, K_CHUNK), 0)
        c_i = lax.broadcasted_iota(jnp.int32, (K_CHUNK, K_CHUNK), 1)
        lower = jnp.where(c_i <= r_i, 1.0, 0.0).astype(BF16)
        thr_row, need_row = thr[0:1], need[0:1]

        def body(c, seen):
            k0 = pl.multiple_of(c * K_CHUNK, K_CHUNK)
            s = sc_ref[pl.ds(k0, K_CHUNK), :]
            eq = jnp.where(s == thr_row, 1.0, 0.0)
            rank = _dot(lower, eq.astype(BF16)) + seen
            keep_tie = jnp.where(rank <= need_row, eq, 0.0)
            sel = jnp.where(s > thr_row, 1.0, keep_tie)
            sh_ref[pl.ds(k0, K_CHUNK), :] = jnp.where(
                sel > 0.0, jnp.where(causal_mask(k0), 0.0, NEG_INF), NEG_INF).astype(COARSE)
            return rank[K_CHUNK - 1:K_CHUNK, :]
        lax.fori_loop(0, n_chunks, body, jnp.zeros((1, tq), F32))

    drop_padding_keys()

    m_ref[...] = jnp.full(m_ref.shape, NEG_INF, F32)
    acc_ref[...] = jnp.zeros(acc_ref.shape, F32)
    ones_rows = jnp.ones((BF16_SUBLANES, K_CHUNK), BF16)

    def logits_stage(c, slot):
        k0 = pl.multiple_of(c * K_CHUNK, K_CHUNK)
        for h in range(N_HEADS):
            pair = slice((h // 2) * LANES, (h // 2 + 1) * LANES)
            cmax = []
            for s in range(K_CHUNK // K_SUB):
                rows = pl.ds(pl.multiple_of(k0 + s * K_SUB, K_SUB), K_SUB)
                lg = _dot(k_ref[0, rows, pair], qm_ref[h]).astype(BF16) + sh_ref[rows, :].astype(BF16)
                lg_ref[slot, h, s * K_SUB:(s + 1) * K_SUB, :] = lg
                cmax.append(_fold_rows(lg, BF16_SUBLANES, jnp.maximum))
            cmax = _tree(cmax, jnp.maximum).astype(F32)
            m_old = m_ref[h]
            m_new = jnp.maximum(m_old, rows8(jnp.max(cmax, axis=0, keepdims=True)))
            m_safe = jnp.where(m_new == NEG_INF, 0.0, m_new)
            alpha_ref[slot, h] = jnp.exp2(m_old - m_safe)
            shift_ref[slot, h] = m_safe
            m_ref[h] = m_new

    def values_stage(c, slot):
        k0 = pl.multiple_of(c * K_CHUNK, K_CHUNK)
        for h in range(N_HEADS):
            m_safe = shift_ref[slot, h].astype(BF16)
            m_tile = jnp.concatenate([m_safe] * (BF16_SUBLANES // SUBLANES), axis=0)
            lg = lg_ref[slot, h].reshape(K_CHUNK // BF16_SUBLANES, BF16_SUBLANES, tq)
            p = jnp.exp2(lg - m_tile[None]).reshape(K_CHUNK, tq)
            v_aug = jnp.concatenate(
                [vt_ref[0, h * HEAD_DIM:(h + 1) * HEAD_DIM, pl.ds(k0, K_CHUNK)], ones_rows], axis=0)
            pv = _dot(v_aug, p)
            acc = acc_ref[h].reshape(v_rows // SUBLANES, SUBLANES, tq) * alpha_ref[slot, h][None]
            acc_ref[h] = acc.reshape(v_rows, tq) + pv

    def attn_chunk(c, slot):
        logits_stage(c, slot)
        values_stage(c, slot)

    for_each_chunk(attn_chunk, 4)

    for pair in range(N_HEADS // 2):
        halves = []
        for h in (2 * pair, 2 * pair + 1):
            a = acc_ref[h]
            total = a[HEAD_DIM:HEAD_DIM + 1]
            halves.append(a[:HEAD_DIM] / jnp.where(total == 0.0, 1.0, total))
        out_t = jnp.concatenate(halves, axis=0)
        o_ref[0, :, pair * LANES:(pair + 1) * LANES] = out_t.T.astype(o_ref.dtype)


def _dsa_call(qt, qit, wit, k, ki, vt, k_sel, n_pad, expert_w, expert_w_bf16, part, n_parts):
    B, tp, _ = k.shape
    nq = tp // Q_TILE
    n_exp = expert_w[0].shape[0]
    per_call = n_exp // n_parts
    assert per_call * n_parts == n_exp
    per_step = next(d for d in range(1, per_call + 1) if per_call % d == 0 and per_call // d <= nq)
    n_steps = per_call // per_step

    def q_cols(r):
        return pl.BlockSpec((1, r, Q_TILE), lambda b, i: (b, 0, i))

    def per_batch(a):
        return pl.BlockSpec((1,) + a.shape[1:], lambda b, i: (b, 0, 0), pipeline_mode=pl.Buffered(1))

    def expert_spec(a):
        return pl.BlockSpec((per_step,) + a.shape[1:],
                            lambda b, i: (part * n_steps + jnp.minimum(i, n_steps - 1), 0, 0))

    in_specs = [q_cols(ATTN_W), q_cols(IDX_HEADS * IDX_DIM), q_cols(IDX_HEADS),
                per_batch(k), per_batch(ki), per_batch(vt)] + [expert_spec(a) for a in expert_w]
    args = [qt, qit, wit, k, ki, vt, *expert_w]
    aliases = {}
    if expert_w_bf16 is not None:
        for j, a in enumerate(expert_w_bf16):
            aliases[len(args)] = 1 + j
            in_specs.append(pl.BlockSpec(memory_space=pl.ANY))
            args.append(a)

    return pl.pallas_call(
        functools.partial(_dsa_kernel, k_sel=k_sel, n_pad=n_pad, n_aliased=len(aliases)),
        grid=(B, nq),
        in_specs=in_specs,
        out_specs=[pl.BlockSpec((1, Q_TILE, ATTN_W), lambda b, i: (b, i, 0))] + [expert_spec(a) for a in expert_w],
        out_shape=[jax.ShapeDtypeStruct((B, tp, ATTN_W), BF16)]
        + [jax.ShapeDtypeStruct(a.shape, BF16) for a in expert_w],
        input_output_aliases=aliases,
        scratch_shapes=[pltpu.VMEM((tp, Q_TILE), F32),
                        pltpu.VMEM((tp, Q_TILE), COARSE),
                        pltpu.VMEM((N_HEADS, LANES, Q_TILE), BF16),
                        pltpu.VMEM((IDX_HEADS, LANES, Q_TILE), BF16),
                        pltpu.VMEM((N_HEADS, SUBLANES, Q_TILE), F32),
                        pltpu.VMEM((2, N_HEADS, SUBLANES, Q_TILE), F32),
                        pltpu.VMEM((2, N_HEADS, SUBLANES, Q_TILE), F32),
                        pltpu.VMEM((2, N_HEADS, K_CHUNK, Q_TILE), BF16),
                        pltpu.VMEM((N_HEADS, HEAD_DIM + BF16_SUBLANES, Q_TILE), F32)],
        compiler_params=pltpu.CompilerParams(
            dimension_semantics=("arbitrary", "arbitrary"), vmem_limit_bytes=VMEM_LIMIT),
        name="dsa",
    )(*args)


def _max_all(x):
    return jnp.max(jnp.max(x, axis=1, keepdims=True), axis=0, keepdims=True)


def _sum_all(x):
    return jnp.sum(jnp.sum(x, axis=1, keepdims=True), axis=0, keepdims=True)


def _pack_bf16_pairs(x):
    w = x.shape[1] // 2
    lo = pltpu.bitcast(x[:, :w].astype(BF16).astype(F32), jnp.int32)
    hi = pltpu.bitcast(x[:, w:].astype(BF16).astype(F32), jnp.int32)
    return lax.shift_right_logical(lo, 16) | (hi & jnp.int32(-65536))


def _unpack_bf16_pairs(p):
    lo = pltpu.bitcast(lax.shift_left(p, 16), F32)
    hi = pltpu.bitcast(p & jnp.int32(-65536), F32)
    return jnp.concatenate([lo, hi], axis=1)


def _router_topk(logits_t, rbias):
    r = logits_t.shape[1]
    shape3 = (N_GROUPS, GROUP_SIZE, r)
    scores = jax.nn.sigmoid(logits_t).reshape(shape3)
    biased = scores + rbias.reshape(N_GROUPS, GROUP_SIZE, 1)
    in_grp = lax.broadcasted_iota(jnp.int32, shape3, 1).astype(F32)
    m1 = jnp.max(biased, axis=1, keepdims=True)
    first = jnp.min(jnp.where(biased == m1, in_grp, float(GROUP_SIZE)), axis=1, keepdims=True)
    m2 = jnp.max(jnp.where(in_grp == first, NEG_INF, biased), axis=1, keepdims=True)
    cur = m1 + m2

    grp_idx = lax.broadcasted_iota(jnp.int32, (N_GROUPS, 1, r), 0).astype(F32)
    grp_sel = jnp.zeros((N_GROUPS, 1, r), F32)
    for _ in range(TOPK_GROUPS):
        m = jnp.max(cur, axis=0, keepdims=True)
        pick = grp_idx == jnp.min(jnp.where(cur == m, grp_idx, float(N_GROUPS)), axis=0, keepdims=True)
        grp_sel = jnp.where(pick, 1.0, grp_sel)
        cur = jnp.where(pick, NEG_INF, cur)

    cur = jnp.where(jnp.broadcast_to(grp_sel, shape3) > 0.0, biased, NEG_INF)
    exp_idx = lax.broadcasted_iota(jnp.int32, shape3, 0).astype(F32) * GROUP_SIZE + in_grp
    chosen = jnp.zeros(shape3, F32)
    ids, wts = [], []
    for _ in range(TOP_K):
        m = _max_all(cur)
        first = -_max_all(-jnp.where(cur == m, exp_idx, float(N_EXPERTS)))
        pick = exp_idx == first
        chosen = jnp.where(pick, 1.0, chosen)
        cur = jnp.where(pick, NEG_INF, cur)
        ids.append(first.reshape(1, r))
        wts.append(_sum_all(jnp.where(pick, scores, 0.0)).reshape(1, r))
    ids = jnp.concatenate(ids, axis=0)
    wts = jnp.concatenate(wts, axis=0)
    gates = wts / jnp.sum(wts, axis=0, keepdims=True) * ROUTED_SCALE
    return ids, gates, chosen.reshape(N_EXPERTS, r), exp_idx


def _route_kernel(conv_ref, attn_ref, hn_ref, woc_ref, woa_ref, g1_ref, b1_ref,
                  wsg_ref, wsu_ref, wsd_ref, wrh_ref, wrl_ref, rb_ref,
                  xp_ref, base_ref, gates_ref, ek_ref, rk_ref, cnt_ref):
    step = pl.program_id(0)
    tr = hn_ref.shape[0]
    mix = _dot(conv_ref[...], woc_ref[...]) + _dot(attn_ref[...], woa_ref[...])
    h1 = _layer_norm_rows(DN_ALPHA * hn_ref[...] + mix, g1_ref[...], b1_ref[...])
    xb = h1.astype(BF16)
    xp_ref[...] = _pack_bf16_pairs(h1)

    shared = jax.nn.silu(_dot(xb, wsg_ref[...])) * _dot(xb, wsu_ref[...])
    base_ref[...] = DN_ALPHA * h1 + _dot(shared.astype(BF16), wsd_ref[...])

    x_lo = (h1 - xb.astype(F32)).astype(BF16)
    logits_t = (_dot_nt(wrh_ref[...], xb) + _dot_nt(wrh_ref[...], x_lo) + _dot_nt(wrl_ref[...], xb))
    ids, gates, chosen, exp_idx = _router_topk(logits_t, rb_ref[...])
    padded = jnp.concatenate([gates, jnp.zeros((LANES - TOP_K, tr), F32)], axis=0)
    gates_ref[...] = padded.T
    ek_ref[...] = ids.astype(jnp.int32)

    @pl.when(step == 0)
    def _():
        cnt_ref[...] = jnp.zeros(cnt_ref.shape, F32)

    t_i = lax.broadcasted_iota(jnp.int32, (tr, tr), 0)
    t_j = lax.broadcasted_iota(jnp.int32, (tr, tr), 1)
    before = jnp.where(t_i < t_j, 1.0, 0.0).astype(BF16)
    chosen_b = chosen.astype(BF16)
    running = cnt_ref[...]
    rank = _dot(chosen_b, before) + jnp.concatenate([running] * (tr // LANES), axis=1)
    rank3 = rank.reshape(N_GROUPS, GROUP_SIZE, tr)
    rk = [_sum_all(jnp.where(exp_idx == ids[k:k + 1].reshape(1, 1, tr), rank3, 0.0)).reshape(1, tr)
          for k in range(TOP_K)]
    rk_ref[...] = jnp.concatenate(rk, axis=0).astype(jnp.int32)
    cnt_ref[...] = running + _dot(chosen_b, jnp.ones((tr, LANES), BF16))


def _route_call(conv, attn, hn, woc, woa, g1, b1, wsg, wsu, wsd, wrh, wrl, rbias):
    n, D = hn.shape
    tr = _pick_tile(n, (768, 512, 256))

    def row_spec(w):
        return pl.BlockSpec((tr, w), lambda i: (i, 0))

    def col_spec(r):
        return pl.BlockSpec((r, tr), lambda i: (0, i))

    def full(a):
        return pl.BlockSpec(a.shape, lambda i: (0,) * a.ndim)

    consts = [woc, woa, g1, b1, wsg, wsu, wsd, wrh, wrl, rbias]
    return pl.pallas_call(
        _route_kernel,
        grid=(n // tr,),
        in_specs=[row_spec(CONV_CH), row_spec(ATTN_W), row_spec(D)] + [full(a) for a in consts],
        out_specs=[row_spec(D // 2), row_spec(D), row_spec(LANES), col_spec(TOP_K), col_spec(TOP_K),
                   pl.BlockSpec((N_EXPERTS, LANES), lambda i: (0, 0))],
        out_shape=[jax.ShapeDtypeStruct((n, D // 2), jnp.int32),
                   jax.ShapeDtypeStruct((n, D), F32),
                   jax.ShapeDtypeStruct((n, LANES), F32),
                   jax.ShapeDtypeStruct((TOP_K, n), jnp.int32),
                   jax.ShapeDtypeStruct((TOP_K, n), jnp.int32),
                   jax.ShapeDtypeStruct((N_EXPERTS, LANES), F32)],
        compiler_params=pltpu.CompilerParams(
            dimension_semantics=("arbitrary",), vmem_limit_bytes=VMEM_LIMIT),
        name="route",
    )(conv, attn, hn, *consts)


def _plan_kernel(cnt_ref, ek_ref, rk_ref, slot_ref, blk_ref, *, n_blocks):
    tr = ek_ref.shape[1]
    counts = cnt_ref[...]
    padded = jnp.ceil(counts / EXPERT_BLOCK) * EXPERT_BLOCK
    starts = [jnp.zeros((1, LANES), F32)]
    for e in range(1, N_EXPERTS):
        starts.append(starts[-1] + padded[e - 1:e])
    ek = ek_ref[...]
    seg = jnp.zeros(ek.shape, F32)
    for e in range(N_EXPERTS):
        seg = jnp.where(ek == e, jnp.concatenate([starts[e]] * (tr // LANES), axis=1), seg)
    slot_ref[...] = seg.astype(jnp.int32) + rk_ref[...]

    ends = jnp.concatenate(starts, axis=0) + padded
    w = blk_ref.shape[1]
    blk_start = lax.broadcasted_iota(jnp.int32, (N_EXPERTS, w), 1).astype(F32) * EXPERT_BLOCK
    done = jnp.where(jnp.concatenate([ends] * (w // LANES), axis=1) <= blk_start, 1.0, 0.0)
    owner = jnp.minimum(jnp.sum(done, axis=0, keepdims=True), N_EXPERTS - 1.0)
    used = jnp.concatenate([ends[N_EXPERTS - 1:] * (1.0 / EXPERT_BLOCK)] * (w // LANES), axis=1)
    lane = lax.broadcasted_iota(jnp.int32, (1, w), 1)
    table = jnp.where(lane == n_blocks, used, owner)
    blk_ref[...] = jnp.broadcast_to(table, blk_ref.shape).astype(jnp.int32)


def _plan_call(cnt, ek, rk, n_blocks):
    k, n = ek.shape
    tr = _pick_tile(n, (768, 512, 256))
    w = -(-(n_blocks + 1) // LANES) * LANES
    col = pl.BlockSpec((k, tr), lambda i: (0, i))
    return pl.pallas_call(
        functools.partial(_plan_kernel, n_blocks=n_blocks),
        grid=(n // tr,),
        in_specs=[pl.BlockSpec(cnt.shape, lambda i: (0, 0)), col, col],
        out_specs=[col, pl.BlockSpec((SUBLANES, w), lambda i: (0, 0))],
        out_shape=[jax.ShapeDtypeStruct((k, n), jnp.int32), jax.ShapeDtypeStruct((SUBLANES, w), jnp.int32)],
        compiler_params=pltpu.CompilerParams(dimension_semantics=("arbitrary",)),
        name="plan",
    )(cnt, ek, rk)


def _sc_workers():
    info = plsc.get_sparse_core_info()
    return info.num_cores, info.num_subcores


def _sc_scatter_rows(src, slots, n_out):
    n_src, d = src.shape
    units = slots.shape[0]
    n_chunks = n_src // SC_ROWS
    n_cores, n_sub = _sc_workers()
    n_workers = n_cores * n_sub
    assert slots.shape[1] == SC_ROWS

    def body(src_hbm, slot_hbm, out_hbm, idx_v, rows_v):
        wid = lax.axis_index("s") * n_cores + lax.axis_index("c")

        @pl.loop(0, -(-units // n_workers))
        def _(j):
            u = j * n_workers + wid

            @pl.when(u < units)
            def _():
                pltpu.sync_copy(slot_hbm.at[pl.ds(u, 1)], idx_v)
                pltpu.sync_copy(src_hbm.at[pl.ds((u % n_chunks) * SC_ROWS, SC_ROWS)], rows_v)
                pltpu.sync_copy(rows_v, out_hbm.at[idx_v.at[0]])

    return pl.kernel(
        body, out_type=jax.ShapeDtypeStruct((n_out, d), src.dtype),
        mesh=plsc.VectorSubcoreMesh(core_axis_name="c", subcore_axis_name="s"),
        scratch_types=[pltpu.VMEM((1, SC_ROWS), jnp.int32), pltpu.VMEM((SC_ROWS, d), src.dtype)],
        name="dispatch_rows",
    )(src, slots)


def _sc_gather_rows(table, slots):
    units = slots.shape[0]
    d = table.shape[1]
    n_cores, n_sub = _sc_workers()
    n_workers = n_cores * n_sub
    assert slots.shape[1] == SC_ROWS

    def body(table_hbm, slot_hbm, out_hbm, idx_v, rows_v):
        wid = lax.axis_index("s") * n_cores + lax.axis_index("c")

        @pl.loop(0, -(-units // n_workers))
        def _(j):
            u = j * n_workers + wid

            @pl.when(u < units)
            def _():
                pltpu.sync_copy(slot_hbm.at[pl.ds(u, 1)], idx_v)
                pltpu.sync_copy(table_hbm.at[idx_v.at[0]], rows_v)
                pltpu.sync_copy(rows_v, out_hbm.at[pl.ds(u * SC_ROWS, SC_ROWS)])

    return pl.kernel(
        body, out_type=jax.ShapeDtypeStruct((units * SC_ROWS, d), table.dtype),
        mesh=plsc.VectorSubcoreMesh(core_axis_name="c", subcore_axis_name="s"),
        scratch_types=[pltpu.VMEM((1, SC_ROWS), jnp.int32), pltpu.VMEM((SC_ROWS, d), table.dtype)],
        name="collect_rows",
    )(table, slots)


def _expert_kernel(blk_ref, xs_ref, wg_ref, wu_ref, wd_ref, ys_ref):
    @pl.when(pl.program_id(0) < blk_ref[pl.num_programs(0)])
    def _():
        x = _unpack_bf16_pairs(xs_ref[...]).astype(BF16)
        hdn = (jax.nn.silu(_dot(x, wg_ref[0])) * _dot(x, wu_ref[0])).astype(BF16)
        ys_ref[...] = _pack_bf16_pairs(_dot(hdn, wd_ref[0]))


def _expert_call(blk_exp, xs, w_gate, w_up, w_down):
    p, half = xs.shape
    _, D, d_exp = w_gate.shape
    n_blocks = p // EXPERT_BLOCK
    assert blk_exp.shape == (n_blocks + 1,)

    def row_block(b, blk):
        return (jnp.minimum(b, blk[n_blocks] - 1), 0)

    rows = pl.BlockSpec((EXPERT_BLOCK, half), row_block)
    grid_spec = pltpu.PrefetchScalarGridSpec(
        num_scalar_prefetch=1,
        grid=(n_blocks,),
        in_specs=[rows,
                  pl.BlockSpec((1, D, d_exp), lambda b, blk: (blk[b], 0, 0)),
                  pl.BlockSpec((1, D, d_exp), lambda b, blk: (blk[b], 0, 0)),
                  pl.BlockSpec((1, d_exp, D), lambda b, blk: (blk[b], 0, 0))],
        out_specs=rows,
    )
    return pl.pallas_call(
        _expert_kernel,
        grid_spec=grid_spec,
        out_shape=jax.ShapeDtypeStruct((p, half), jnp.int32),
        compiler_params=pltpu.CompilerParams(
            dimension_semantics=("arbitrary",), vmem_limit_bytes=VMEM_LIMIT),
        name="experts",
    )(blk_exp, xs, w_gate, w_up, w_down)


def _combine_kernel(g_ref, gates_ref, base_ref, g2_ref, b2_ref, *rest):
    o_ref = rest[-1]
    gates = gates_ref[...]
    acc = base_ref[...]
    for k in range(g_ref.shape[0]):
        acc = acc + _unpack_bf16_pairs(g_ref[k]) * gates[:, k:k + 1]
    o_ref[0] = _layer_norm_rows(acc, g2_ref[...], b2_ref[...])


def _combine_call(g, gates, base, g2, b2, result, batch, n_batch):
    k, n, half = g.shape
    D = base.shape[1]
    tr = SEQ_ALIGN
    seq = n - tr

    def row_spec(w):
        return pl.BlockSpec((tr, w), lambda i: (i, 0))

    vec = pl.BlockSpec((1, D), lambda i: (0, 0))
    in_specs = [pl.BlockSpec((k, tr, half), lambda i: (0, i, 0)), row_spec(LANES), row_spec(D), vec, vec]
    args = [g, gates, base, g2, b2]
    aliases = {}
    if result is not None:
        in_specs.append(pl.BlockSpec(memory_space=pl.ANY))
        args.append(result)
        aliases = {len(args) - 1: 0}
    return pl.pallas_call(
        _combine_kernel,
        grid=(n // tr,),
        in_specs=in_specs,
        out_specs=pl.BlockSpec((1, tr, D), lambda i: (batch, jnp.maximum(i - 1, 0), 0)),
        out_shape=jax.ShapeDtypeStruct((n_batch, seq, D), F32),
        input_output_aliases=aliases,
        compiler_params=pltpu.CompilerParams(
            dimension_semantics=("arbitrary",), vmem_limit_bytes=VMEM_LIMIT),
        name="combine",
    )(*args)


def _rope_tables(tp, lead):
    pos = jnp.arange(tp, dtype=F32) - lead
    inv = jnp.power(ROPE_THETA, -2.0 * jnp.arange(ROPE_HALF, dtype=F32) / ROPE_DIM)
    ang = pos[:, None] * inv[None, :]
    cos, sin = jnp.cos(ang), jnp.sin(ang)
    zeros = jnp.zeros((tp, HEAD_DIM - ROPE_DIM), F32)
    zh = jnp.zeros((tp, ROPE_HALF), F32)
    c64 = jnp.concatenate([cos, cos, jnp.ones_like(zeros)], axis=1)
    s1_64 = jnp.concatenate([-sin, zh, zeros], axis=1)
    s2_64 = jnp.concatenate([zh, sin, zeros], axis=1)
    rep = LANES // HEAD_DIM
    return (jnp.tile(c64, (1, rep)), jnp.tile(s1_64, (1, rep)), jnp.tile(s2_64, (1, rep)),
            cos.T, sin.T)


def kernel(x, meta_tokens, ln_emb_g, ln_emb_b, w_in, conv_w, conv_b, ln_conv_g, ln_conv_b, ln_kidx_g, ln_kidx_b, w_out, ln1_g, ln1_b, w_router, router_bias, w_gate, w_up, w_down, ws_gate, ws_up, ws_down, ln2_g, ln2_b):
    B, seq, D = x.shape
    assert w_in.shape[0] == DEPTH
    assert seq % SEQ_ALIGN == 0 and meta_tokens.shape[0] == N_META <= SEQ_ALIGN
    k_sel = min(INDEX_TOPK, seq // 4)
    tp = seq + SEQ_ALIGN
    n_pad = SEQ_ALIGN - N_META

    def row(a):
        return a.reshape(1, -1).astype(F32)

    w = w_in[0]
    o = 0
    parts = []
    for width in (CONV_CH, CONV_CH, ATTN_W, ATTN_W, ATTN_W, IDX_HEADS * IDX_DIM, IDX_DIM, IDX_HEADS):
        parts.append(w[:, o:o + width])
        o += width
    wa, wgl, wq, wk, wv, wqi, wki, wwi = parts
    wwi_t = jnp.concatenate([wwi.T, jnp.zeros((BF16_SUBLANES - IDX_HEADS, D), w.dtype)], axis=0)
    weights = (jnp.concatenate([wa, wgl], axis=1).astype(BF16), wq.T.astype(BF16), wk.astype(BF16),
               wv.T.astype(BF16), wqi.T.astype(BF16), jnp.concatenate([wki, wki], axis=1).astype(BF16),
               wwi_t.astype(BF16))

    def twice(a):
        return row(jnp.concatenate([a, a]))

    tabs = _rope_tables(tp, n_pad)
    inproj_consts = (row(ln_emb_g), row(ln_emb_b), weights, conv_w[0].astype(F32), row(conv_b[0]),
                     row(ln_conv_g[0]), row(ln_conv_b[0]), twice(ln_kidx_g[0]), twice(ln_kidx_b[0]))
    wr_t = w_router[0].T.astype(F32)
    wr_hi = wr_t.astype(BF16)
    wr_lo = (wr_t - wr_hi.astype(F32)).astype(BF16)
    route_consts = (w_out[0][:CONV_CH].astype(BF16), w_out[0][CONV_CH:].astype(BF16), row(ln1_g[0]),
                    row(ln1_b[0]), ws_gate[0].astype(BF16), ws_up[0].astype(BF16), ws_down[0].astype(BF16),
                    wr_hi, wr_lo, router_bias[0].reshape(-1, 1).astype(F32))
    meta = meta_tokens.astype(F32)
    n_blocks = tp * TOP_K // EXPERT_BLOCK + N_EXPERTS

    expert_w = (w_gate[0], w_up[0], w_down[0])
    expert_w_bf16 = None
    routed = []
    for b in range(B):
        hn, conv, qt, k, vt, qit, ki, wit = _inproj_call(x, b, meta, tabs, *inproj_consts)
        attn, *expert_w_bf16 = _dsa_call(qt, qit, wit, k, ki, vt, k_sel, n_pad, expert_w, expert_w_bf16, b, B)
        xp, base, gates, ek, rk, cnt = _route_call(conv[0], attn[0], hn[0], *route_consts)

        slot, blk = _plan_call(cnt, ek, rk, n_blocks)
        slots = slot.reshape(TOP_K * tp // SC_ROWS, SC_ROWS)
        xs = _sc_scatter_rows(xp, slots, n_blocks * EXPERT_BLOCK)
        routed.append((xs, blk, slots, gates, base))

    result = None
    for b, (xs, blk, slots, gates, base) in enumerate(routed):
        ys = _expert_call(blk[0, :n_blocks + 1], xs, *expert_w_bf16)
        picked = _sc_gather_rows(ys, slots).reshape(TOP_K, tp, D // 2)
        result = _combine_call(picked, gates, base, row(ln2_g[0]), row(ln2_b[0]), result, b, B)
    return result
```

```python
import functools

import numpy as np
import jax
import jax.numpy as jnp
from jax import lax
from jax.experimental import pallas as pl
from jax.experimental.pallas import tpu as pltpu
from jax.experimental.pallas import tpu_sc as plsc

N_META = 16
CONV_CH = 512
CONV_WIDTH = 31
N_HEADS = 8
HEAD_DIM = 64
ATTN_W = N_HEADS * HEAD_DIM
IDX_HEADS = 8
IDX_DIM = 64
INDEX_TOPK = 256
ROPE_DIM = HEAD_DIM // 4
ROPE_HALF = ROPE_DIM // 2
ROPE_THETA = 500000.0
N_EXPERTS = 64
TOP_K = 8
N_GROUPS = 8
GROUP_SIZE = N_EXPERTS // N_GROUPS
TOPK_GROUPS = 4
ROUTED_SCALE = 2.5
LN_EPS = 1e-5
DEPTH = 1
DN_ALPHA = (2.0 * DEPTH) ** 0.25

LANES = 128
Q_TILE = 256
SUBLANES = 8
BF16_SUBLANES = 16
EXPERT_BLOCK = 512
SC_ROWS = 128
SC_SCATTER_ROWS = 64
K_CHUNK = Q_TILE
K_SUB = 128
COUNT_UNROLL = 4
SEQ_ALIGN = 256
CONV_HALO = 32
VMEM_LIMIT = 56 * 1024 * 1024

F32 = jnp.float32
BF16 = jnp.bfloat16
COARSE = jnp.bfloat16
NEG_INF = float("-inf")
INT_MIN = -2 ** 31
KEY_NEG_INF = -2139095041
LOG2_E = 1.4426950408889634


def _dot(a, b):
    return jnp.dot(a, b, preferred_element_type=F32)


def _dot_nt(a, b):
    return lax.dot_general(a, b, (((1,), (1,)), ((), ())), preferred_element_type=F32)


def _layer_norm_rows(x, g, b):
    mu = jnp.mean(x, axis=-1, keepdims=True)
    xc = x - mu
    var = jnp.mean(xc * xc, axis=-1, keepdims=True)
    return xc * lax.rsqrt(var + LN_EPS) * g + b


def _pick_tile(n, candidates):
    for c in candidates:
        if n % c == 0:
            return c
    raise ValueError(f"no tile for {n}")


def _rope_rows(x, c_tab, s1_tab, s2_tab):
    outs = []
    for j in range(x.shape[1] // LANES):
        xs = x[:, j * LANES:(j + 1) * LANES]
        up = pltpu.roll(xs, LANES - ROPE_HALF, axis=1)
        dn = pltpu.roll(xs, ROPE_HALF, axis=1)
        outs.append(xs * c_tab + up * s1_tab + dn * s2_tab)
    return jnp.concatenate(outs, axis=1)


def _rope_cols(xt, cos_t, sin_t, heads):
    r = xt.shape[1]
    x3 = xt.reshape(heads, HEAD_DIM, r)
    x1 = x3[:, 0:ROPE_HALF, :]
    x2 = x3[:, ROPE_HALF:ROPE_DIM, :]
    n1 = x1 * cos_t - x2 * sin_t
    n2 = x2 * cos_t + x1 * sin_t
    out = jnp.concatenate([n1, n2, x3[:, ROPE_DIM:, :]], axis=1)
    return out.reshape(heads * HEAD_DIM, r)


def _inproj_kernel(x_ref, meta_ref, ctab_ref, s1tab_ref, s2tab_ref, cost_ref, sint_ref,
                   lng_ref, lnb_ref, wag_ref, wqt_ref, wk_ref, wvt_ref, wqit_ref, wki_ref, wwit_ref,
                   cw_ref, cb_ref, lncg_ref, lncb_ref, lnkg_ref, lnkb_ref,
                   hn_ref, conv_ref, qt_ref, k_ref, vt_ref, qit_ref, ki_ref, wit_ref,
                   ubuf_ref, wbuf_ref):
    t = pl.program_id(1)
    tr = x_ref.shape[1]
    n_meta = meta_ref.shape[0]
    first = jnp.concatenate([jnp.zeros((tr - n_meta, x_ref.shape[2]), F32), meta_ref[...]], axis=0)
    h = jnp.where(t == 0, first, x_ref[0])

    hn = _layer_norm_rows(h, lng_ref[...], lnb_ref[...])
    hn_ref[0] = hn
    xb = hn.astype(BF16)

    ag = _dot(xb, wag_ref[...])
    u = ag[:, :CONV_CH] * jax.nn.sigmoid(ag[:, CONV_CH:])
    row = lax.broadcasted_iota(jnp.int32, (tr, CONV_CH), 0)
    u = jnp.where(jnp.logical_or(t > 0, row >= tr - n_meta), u, 0.0)

    @pl.when(t == 0)
    def _():
        ubuf_ref[0:CONV_HALO, :] = jnp.zeros((CONV_HALO, CONV_CH), F32)

    ubuf_ref[CONV_HALO:CONV_HALO + tr, :] = u
    base = CONV_HALO - (CONV_WIDTH - 1)
    acc = jnp.zeros((tr, CONV_CH), F32)
    for r in range(SUBLANES):
        offsets = [o for o in range(base, base + CONV_WIDTH) if o % SUBLANES == r]
        rows = max(offsets) - r + tr
        wbuf_ref[0:rows, :] = ubuf_ref[r:r + rows, :]
        for o in offsets:
            acc = acc + cw_ref[o - base:o - base + 1, :] * wbuf_ref[o - r:o - r + tr, :]
    ubuf_ref[0:CONV_HALO, :] = ubuf_ref[tr:tr + CONV_HALO, :]
    c = _layer_norm_rows(acc + cb_ref[...], lncg_ref[...], lncb_ref[...])
    conv_ref[0] = (c * jax.nn.sigmoid(c)).astype(conv_ref.dtype)

    ctab, s1tab, s2tab = ctab_ref[...], s1tab_ref[...], s2tab_ref[...]
    cos_t, sin_t = cost_ref[...], sint_ref[...]

    qt = _rope_cols(_dot_nt(wqt_ref[...], xb), cos_t, sin_t, N_HEADS)
    qt_ref[0] = (qt * (HEAD_DIM ** -0.5 * LOG2_E)).astype(qt_ref.dtype)
    k = _rope_rows(_dot(xb, wk_ref[...]), ctab, s1tab, s2tab)
    k_ref[0] = k.astype(k_ref.dtype)
    vt_ref[0] = _dot_nt(wvt_ref[...], xb).astype(vt_ref.dtype)

    qit = _rope_cols(_dot_nt(wqit_ref[...], xb), cos_t, sin_t, IDX_HEADS)
    qit_ref[0] = qit.astype(qit_ref.dtype)
    ki = _layer_norm_rows(_dot(xb, wki_ref[...]), lnkg_ref[...], lnkb_ref[...])
    ki_ref[0] = _rope_rows(ki, ctab, s1tab, s2tab).astype(ki_ref.dtype)
    wit = _dot_nt(wwit_ref[...], xb) * (IDX_HEADS ** -0.5)
    wit_ref[0] = wit[:IDX_HEADS]


def _inproj_call(x, batch, meta, tabs, ln_g, ln_b, weights, conv_w, conv_b, lnc_g, lnc_b, lnk_g, lnk_b):
    _, seq, D = x.shape
    B = 1
    tr = SEQ_ALIGN
    tp = seq + tr
    nt = tp // tr
    ctab, s1tab, s2tab, cos_t, sin_t = tabs

    def row_spec(w):
        return pl.BlockSpec((1, tr, w), lambda b, t: (b, t, 0))

    def col_spec(r):
        return pl.BlockSpec((1, r, tr), lambda b, t: (b, 0, t))

    def full(a):
        return pl.BlockSpec(a.shape, lambda b, t: (0,) * a.ndim)

    tab_row = pl.BlockSpec((tr, LANES), lambda b, t: (t, 0))
    tab_col = pl.BlockSpec((ROPE_HALF, tr), lambda b, t: (0, t))
    consts = [ln_g, ln_b, *weights, conv_w, conv_b, lnc_g, lnc_b, lnk_g, lnk_b]
    out_shape = [
        jax.ShapeDtypeStruct((B, tp, D), F32),
        jax.ShapeDtypeStruct((B, tp, CONV_CH), BF16),
        jax.ShapeDtypeStruct((B, ATTN_W, tp), BF16),
        jax.ShapeDtypeStruct((B, tp, ATTN_W), BF16),
        jax.ShapeDtypeStruct((B, ATTN_W, tp), BF16),
        jax.ShapeDtypeStruct((B, IDX_HEADS * IDX_DIM, tp), BF16),
        jax.ShapeDtypeStruct((B, tp, 2 * IDX_DIM), BF16),
        jax.ShapeDtypeStruct((B, IDX_HEADS, tp), F32),
    ]
    out_specs = [row_spec(D), row_spec(CONV_CH), col_spec(ATTN_W), row_spec(ATTN_W), col_spec(ATTN_W),
                 col_spec(IDX_HEADS * IDX_DIM), row_spec(2 * IDX_DIM), col_spec(IDX_HEADS)]
    return pl.pallas_call(
        _inproj_kernel,
        grid=(B, nt),
        in_specs=[pl.BlockSpec((1, tr, D), lambda b, t: (batch, jnp.maximum(t - 1, 0), 0)), full(meta),
                  tab_row, tab_row, tab_row, tab_col, tab_col] + [full(a) for a in consts],
        out_specs=out_specs,
        out_shape=out_shape,
        scratch_shapes=[pltpu.VMEM((CONV_HALO + tr, CONV_CH), F32),
                        pltpu.VMEM((CONV_HALO + tr, CONV_CH), F32)],
        compiler_params=pltpu.CompilerParams(
            dimension_semantics=("arbitrary", "arbitrary"), vmem_limit_bytes=VMEM_LIMIT),
        name="inproj",
    )(x, meta, ctab, s1tab, s2tab, cos_t, sin_t, *consts)


def _key_to_float(key):
    bits = jnp.where(key >= 0, key, key ^ jnp.int32(0x7FFFFFFF))
    f = pltpu.bitcast(bits, F32)
    return jnp.where(key < jnp.int32(KEY_NEG_INF), NEG_INF, f)


def _tree(parts, op):
    parts = list(parts)
    while len(parts) > 1:
        nxt = [op(parts[j], parts[j + 1]) for j in range(0, len(parts) - 1, 2)]
        if len(parts) % 2:
            nxt.append(parts[-1])
        parts = nxt
    return parts[0]


def _fold_rows(x, rows, op):
    return _tree([x[j * rows:(j + 1) * rows] for j in range(x.shape[0] // rows)], op)


def _dsa_kernel(qt_ref, qit_ref, wit_ref, k_ref, ki_ref, vt_ref, *rest, k_sel, n_pad, n_convert):
    f32_w, rest = rest[:n_convert], rest[n_convert:]
    o_ref, bf16_w, rest = rest[0], rest[1:1 + n_convert], rest[1 + n_convert:]
    sc_ref, sh_ref, qm_ref, qim_ref, m_ref, alpha_ref, shift_ref, lg_ref, acc_ref = rest

    for src_ref, dst_ref in zip(f32_w, bf16_w):
        dst_ref[...] = src_ref[...].astype(BF16)

    i = pl.program_id(1)
    tq = qt_ref.shape[2]
    n_chunks = (i * tq + tq + K_CHUNK - 1) // K_CHUNK
    v_rows = HEAD_DIM + BF16_SUBLANES

    def causal_mask(k0, rows=K_CHUNK):
        kpos = k0 + lax.broadcasted_iota(jnp.int32, (rows, tq), 0)
        return kpos <= i * tq + lax.broadcasted_iota(jnp.int32, (rows, tq), 1)

    def rows8(x):
        return jnp.broadcast_to(x, (SUBLANES, tq))

    def tiles(x):
        return x.reshape(x.shape[0] // SUBLANES, SUBLANES, tq)

    def head_slab(ref, h):
        slab = ref[0, (h // 2) * LANES:(h // 2 + 1) * LANES, :]
        zeros = jnp.zeros((HEAD_DIM, tq), slab.dtype)
        if h % 2 == 0:
            return jnp.concatenate([slab[:HEAD_DIM], zeros], axis=0)
        return jnp.concatenate([zeros, slab[HEAD_DIM:]], axis=0)

    for h in range(N_HEADS):
        qm_ref[h] = head_slab(qt_ref, h)
    for h in range(IDX_HEADS):
        qim_ref[h] = head_slab(qit_ref, h)
    wit = wit_ref[0]
    w_heads = [rows8(wit[h:h + 1] * (IDX_DIM ** -0.5)) for h in range(IDX_HEADS)]

    def for_each_chunk(chunk_fn, unroll, count=n_chunks):
        def group(j, carry):
            for u in range(unroll):
                chunk_fn(unroll * j + u, u % 2)
            return carry

        def single(c, carry):
            chunk_fn(c, 0)
            return carry

        n_groups = count // unroll
        lax.fori_loop(0, n_groups, group, 0)
        lax.fori_loop(n_groups * unroll, count, single, 0)

    def diagonal_visible(first_row, rows):
        r = first_row + lax.broadcasted_iota(jnp.int32, (rows, tq), 0)
        return r <= lax.broadcasted_iota(jnp.int32, (rows, tq), 1)

    def score_chunk(c, slot, diagonal=False):
        del slot
        for s in range(K_CHUNK // K_SUB):
            k0 = pl.multiple_of(c * K_CHUNK + s * K_SUB, K_SUB)
            kic = ki_ref[0, pl.ds(k0, K_SUB), :]
            acc = jnp.zeros((K_SUB // SUBLANES, SUBLANES, tq), F32)
            for h in range(IDX_HEADS):
                acc = acc + w_heads[h][None] * jnp.maximum(tiles(_dot(kic, qim_ref[h])), 0.0)
            acc = acc.reshape(K_SUB, tq)
            if diagonal:
                acc = jnp.where(diagonal_visible(s * K_SUB, K_SUB), acc, NEG_INF)
            sc_ref[pl.ds(k0, K_SUB), :] = acc
            hi_bits = pltpu.bitcast(acc, jnp.int32) & jnp.int32(-65536)
            sh_ref[pl.ds(k0, K_SUB), :] = pltpu.bitcast(hi_bits, F32).astype(COARSE)

    for_each_chunk(score_chunk, 4, n_chunks - 1)
    score_chunk(n_chunks - 1, 0, diagonal=True)

    def drop_padding_keys():
        sh_ref[0:n_pad, :] = jnp.full((n_pad, tq), NEG_INF, COARSE)

    sc_ref[0:n_pad, :] = jnp.full((n_pad, tq), NEG_INF, F32)
    drop_padding_keys()

    def count_all(ref, thr_tile, preds):
        rows = thr_tile.shape[0]
        one, zero = jnp.ones((), ref.dtype), jnp.zeros((), ref.dtype)

        def body(c, cnts):
            k0 = pl.multiple_of(c * K_CHUNK, K_CHUNK)
            s = ref[pl.ds(k0, K_CHUNK), :]
            out = []
            for cnt, p in zip(cnts, preds):
                hits = [jnp.where(p(s[j * rows:(j + 1) * rows], thr_tile), one, zero)
                        for j in range(K_CHUNK // rows)]
                out.append(cnt + _tree(hits, jnp.add).astype(F32))
            return tuple(out)

        def group(j, cnts):
            for u in range(COUNT_UNROLL):
                cnts = body(COUNT_UNROLL * j + u, cnts)
            return cnts

        init = tuple(jnp.zeros((rows, tq), F32) for _ in preds)
        n_groups = n_chunks // COUNT_UNROLL
        cnts = lax.fori_loop(0, n_groups, group, init)
        cnts = lax.fori_loop(n_groups * COUNT_UNROLL, n_chunks, body, cnts)
        return [rows8(jnp.sum(cnt, axis=0, keepdims=True)) for cnt in cnts]

    def search_body(it, tkey, coarse):
        cand = tkey + lax.shift_left(jnp.int32(1), 31 - it)
        cf = _key_to_float(cand)
        if coarse:
            cf = pltpu.bitcast(pltpu.bitcast(cf, jnp.int32) & jnp.int32(-65536), F32).astype(COARSE)
            cf = jnp.concatenate([cf] * (BF16_SUBLANES // SUBLANES), axis=0)
        cnt, = count_all(sh_ref if coarse else sc_ref, cf, [lambda s, t: s >= t])
        return jnp.where(cnt >= k_sel, cand, tkey)

    tkey = jnp.full((SUBLANES, tq), INT_MIN, jnp.int32)
    tkey = lax.fori_loop(0, 16, functools.partial(search_body, coarse=True), tkey)
    tkey = lax.fori_loop(16, 32, functools.partial(search_body, coarse=False), tkey)
    thr = _key_to_float(tkey)

    n_ge, n_gt = count_all(sc_ref, thr, [lambda s, t: s >= t, lambda s, t: s > t])
    need = k_sel - n_gt
    has_ties = jnp.max(n_ge) > k_sel

    @pl.when(jnp.logical_not(has_ties))
    def _():
        def mask_chunk(c, slot, diagonal=False):
            del slot
            k0 = pl.multiple_of(c * K_CHUNK, K_CHUNK)
            s = sc_ref[pl.ds(k0, K_CHUNK), :]
            bias = jnp.where(tiles(s) >= thr[None], 0.0, NEG_INF).reshape(K_CHUNK, tq)
            if diagonal:
                bias = jnp.where(diagonal_visible(0, K_CHUNK), bias, NEG_INF)
            sh_ref[pl.ds(k0, K_CHUNK), :] = bias.astype(COARSE)

        for_each_chunk(mask_chunk, 4, n_chunks - 1)
        mask_chunk(n_chunks - 1, 0, diagonal=True)

    @pl.when(has_ties)
    def _():
        r_i = lax.broadcasted_iota(jnp.int32, (K_CHUNK, K_CHUNK), 0)
        c_i = lax.broadcasted_iota(jnp.int32, (K_CHUNK, K_CHUNK), 1)
        lower = jnp.where(c_i <= r_i, 1.0, 0.0).astype(BF16)
        thr_row, need_row = thr[0:1], need[0:1]

        def body(c, seen):
            k0 = pl.multiple_of(c * K_CHUNK, K_CHUNK)
            s = sc_ref[pl.ds(k0, K_CHUNK), :]
            eq = jnp.where(s == thr_row, 1.0, 0.0)
            rank = _dot(lower, eq.astype(BF16)) + seen
            keep_tie = jnp.where(rank <= need_row, eq, 0.0)
            sel = jnp.where(s > thr_row, 1.0, keep_tie)
            sh_ref[pl.ds(k0, K_CHUNK), :] = jnp.where(
                sel > 0.0, jnp.where(causal_mask(k0), 0.0, NEG_INF), NEG_INF).astype(COARSE)
            return rank[K_CHUNK - 1:K_CHUNK, :]
        lax.fori_loop(0, n_chunks, body, jnp.zeros((1, tq), F32))

    drop_padding_keys()

    m_ref[...] = jnp.full(m_ref.shape, NEG_INF, F32)
    acc_ref[...] = jnp.zeros(acc_ref.shape, F32)
    ones_rows = jnp.ones((BF16_SUBLANES, K_CHUNK), BF16)

    def logits_stage(c, slot):
        k0 = pl.multiple_of(c * K_CHUNK, K_CHUNK)
        for h in range(N_HEADS):
            pair = slice((h // 2) * LANES, (h // 2 + 1) * LANES)
            cmax = []
            for s in range(K_CHUNK // K_SUB):
                rows = pl.ds(pl.multiple_of(k0 + s * K_SUB, K_SUB), K_SUB)
                lg = _dot(k_ref[0, rows, pair], qm_ref[h]).astype(BF16) + sh_ref[rows, :].astype(BF16)
                lg_ref[slot, h, s * K_SUB:(s + 1) * K_SUB, :] = lg
                cmax.append(_fold_rows(lg, BF16_SUBLANES, jnp.maximum))
            cmax = _tree(cmax, jnp.maximum).astype(F32)
            m_old = m_ref[h]
            m_new = jnp.maximum(m_old, rows8(jnp.max(cmax, axis=0, keepdims=True)))
            m_safe = jnp.where(m_new == NEG_INF, 0.0, m_new)
            alpha_ref[slot, h] = jnp.exp2(m_old - m_safe)
            shift_ref[slot, h] = m_safe
            m_ref[h] = m_new

    def values_stage(c, slot):
        k0 = pl.multiple_of(c * K_CHUNK, K_CHUNK)
        for h in range(N_HEADS):
            m_safe = shift_ref[slot, h].astype(BF16)
            m_tile = jnp.concatenate([m_safe] * (BF16_SUBLANES // SUBLANES), axis=0)
            lg = lg_ref[slot, h].reshape(K_CHUNK // BF16_SUBLANES, BF16_SUBLANES, tq)
            p = jnp.exp2(lg - m_tile[None]).reshape(K_CHUNK, tq)
            v_aug = jnp.concatenate(
                [vt_ref[0, h * HEAD_DIM:(h + 1) * HEAD_DIM, pl.ds(k0, K_CHUNK)], ones_rows], axis=0)
            pv = _dot(v_aug, p)
            acc = acc_ref[h].reshape(v_rows // SUBLANES, SUBLANES, tq) * alpha_ref[slot, h][None]
            acc_ref[h] = acc.reshape(v_rows, tq) + pv

    def attn_chunk(c, slot):
        logits_stage(c, slot)
        values_stage(c, slot)

    for_each_chunk(attn_chunk, 4)

    for pair in range(N_HEADS // 2):
        halves = []
        for h in (2 * pair, 2 * pair + 1):
            a = acc_ref[h]
            total = a[HEAD_DIM:HEAD_DIM + 1]
            halves.append(a[:HEAD_DIM] / jnp.where(total == 0.0, 1.0, total))
        out_t = jnp.concatenate(halves, axis=0)
        o_ref[0, :, pair * LANES:(pair + 1) * LANES] = out_t.T.astype(o_ref.dtype)


def _dsa_call(qt, qit, wit, k, ki, vt, k_sel, n_pad, expert_w=()):
    B, tp, _ = k.shape
    nq = tp // Q_TILE

    def q_cols(r):
        return pl.BlockSpec((1, r, Q_TILE), lambda b, i: (b, 0, i))

    def per_batch(a):
        return pl.BlockSpec((1,) + a.shape[1:], lambda b, i: (b, 0, 0), pipeline_mode=pl.Buffered(1))

    def expert_spec(a, **kw):
        n_exp = a.shape[0]
        per_step = next(d for d in range(1, n_exp + 1) if n_exp % d == 0 and n_exp // d <= nq)
        n_steps = n_exp // per_step
        return pl.BlockSpec((per_step,) + a.shape[1:], lambda b, i: (jnp.minimum(i, n_steps - 1), 0, 0), **kw)

    in_specs = [q_cols(ATTN_W), q_cols(IDX_HEADS * IDX_DIM), q_cols(IDX_HEADS),
                per_batch(k), per_batch(ki), per_batch(vt)]
    in_specs += [expert_spec(a, pipeline_mode=pl.Buffered(1)) for a in expert_w]

    return pl.pallas_call(
        functools.partial(_dsa_kernel, k_sel=k_sel, n_pad=n_pad, n_convert=len(expert_w)),
        grid=(B, nq),
        in_specs=in_specs,
        out_specs=[pl.BlockSpec((1, Q_TILE, ATTN_W), lambda b, i: (b, i, 0))] + [expert_spec(a) for a in expert_w],
        out_shape=[jax.ShapeDtypeStruct((B, tp, ATTN_W), BF16)]
        + [jax.ShapeDtypeStruct(a.shape, BF16) for a in expert_w],
        scratch_shapes=[pltpu.VMEM((tp, Q_TILE), F32),
                        pltpu.VMEM((tp, Q_TILE), COARSE),
                        pltpu.VMEM((N_HEADS, LANES, Q_TILE), BF16),
                        pltpu.VMEM((IDX_HEADS, LANES, Q_TILE), BF16),
                        pltpu.VMEM((N_HEADS, SUBLANES, Q_TILE), F32),
                        pltpu.VMEM((2, N_HEADS, SUBLANES, Q_TILE), F32),
                        pltpu.VMEM((2, N_HEADS, SUBLANES, Q_TILE), F32),
                        pltpu.VMEM((2, N_HEADS, K_CHUNK, Q_TILE), BF16),
                        pltpu.VMEM((N_HEADS, HEAD_DIM + BF16_SUBLANES, Q_TILE), F32)],
        compiler_params=pltpu.CompilerParams(
            dimension_semantics=("arbitrary", "arbitrary"), vmem_limit_bytes=VMEM_LIMIT),
        name="dsa",
    )(qt, qit, wit, k, ki, vt, *expert_w)


def _max_all(x):
    return jnp.max(jnp.max(x, axis=0, keepdims=True), axis=1, keepdims=True)


def _sum_all(x):
    return jnp.sum(jnp.sum(x, axis=0, keepdims=True), axis=1, keepdims=True)


def _pack_bf16_pairs(x):
    w = x.shape[1] // 2
    lo = pltpu.bitcast(x[:, :w].astype(BF16).astype(F32), jnp.int32)
    hi = pltpu.bitcast(x[:, w:].astype(BF16).astype(F32), jnp.int32)
    return lax.shift_right_logical(lo, 16) | (hi & jnp.int32(-65536))


def _unpack_bf16_pairs(p):
    lo = pltpu.bitcast(lax.shift_left(p, 16), F32)
    hi = pltpu.bitcast(p & jnp.int32(-65536), F32)
    return jnp.concatenate([lo, hi], axis=1)


def _router_topk(logits_t, rbias):
    r = logits_t.shape[1]
    shape3 = (N_GROUPS, GROUP_SIZE, r)
    scores = jax.nn.sigmoid(logits_t).reshape(shape3)
    biased = scores + rbias.reshape(N_GROUPS, GROUP_SIZE, 1)
    in_grp = lax.broadcasted_iota(jnp.int32, shape3, 1).astype(F32)
    m1 = jnp.max(biased, axis=1, keepdims=True)
    first = jnp.min(jnp.where(biased == m1, in_grp, float(GROUP_SIZE)), axis=1, keepdims=True)
    m2 = jnp.max(jnp.where(in_grp == first, NEG_INF, biased), axis=1, keepdims=True)
    cur = m1 + m2

    grp_idx = lax.broadcasted_iota(jnp.int32, (N_GROUPS, 1, r), 0).astype(F32)
    grp_sel = jnp.zeros((N_GROUPS, 1, r), F32)
    for _ in range(TOPK_GROUPS):
        m = jnp.max(cur, axis=0, keepdims=True)
        pick = grp_idx == jnp.min(jnp.where(cur == m, grp_idx, float(N_GROUPS)), axis=0, keepdims=True)
        grp_sel = jnp.where(pick, 1.0, grp_sel)
        cur = jnp.where(pick, NEG_INF, cur)

    cur = jnp.where(jnp.broadcast_to(grp_sel, shape3) > 0.0, biased, NEG_INF)
    exp_idx = lax.broadcasted_iota(jnp.int32, shape3, 0).astype(F32) * GROUP_SIZE + in_grp
    chosen = jnp.zeros(shape3, F32)
    ids, wts = [], []
    for _ in range(TOP_K):
        m = _max_all(cur)
        first = -_max_all(-jnp.where(cur == m, exp_idx, float(N_EXPERTS)))
        pick = exp_idx == first
        chosen = jnp.where(pick, 1.0, chosen)
        cur = jnp.where(pick, NEG_INF, cur)
        ids.append(first.reshape(1, r))
        wts.append(_sum_all(jnp.where(pick, scores, 0.0)).reshape(1, r))
    ids = jnp.concatenate(ids, axis=0)
    wts = jnp.concatenate(wts, axis=0)
    gates = wts / jnp.sum(wts, axis=0, keepdims=True) * ROUTED_SCALE
    return ids, gates, chosen.reshape(N_EXPERTS, r), exp_idx


def _route_kernel(conv_ref, attn_ref, hn_ref, woc_ref, woa_ref, g1_ref, b1_ref,
                  wsg_ref, wsu_ref, wsd_ref, wrh_ref, wrl_ref, rb_ref,
                  xp_ref, base_ref, gates_ref, ek_ref, rk_ref, cnt_ref, before_ref):
    step = pl.program_id(0)
    tr = hn_ref.shape[0]
    mix = _dot(conv_ref[...], woc_ref[...]) + _dot(attn_ref[...], woa_ref[...])
    h1 = _layer_norm_rows(DN_ALPHA * hn_ref[...] + mix, g1_ref[...], b1_ref[...])
    xb = h1.astype(BF16)
    xp_ref[...] = _pack_bf16_pairs(h1)

    shared = jax.nn.silu(_dot(xb, wsg_ref[...])) * _dot(xb, wsu_ref[...])
    base_ref[...] = DN_ALPHA * h1 + _dot(shared.astype(BF16), wsd_ref[...])

    x_lo = (h1 - xb.astype(F32)).astype(BF16)
    logits_t = (_dot_nt(wrh_ref[...], xb) + _dot_nt(wrh_ref[...], x_lo) + _dot_nt(wrl_ref[...], xb))
    ids, gates, chosen, exp_idx = _router_topk(logits_t, rb_ref[...])
    padded = jnp.concatenate([gates, jnp.zeros((LANES - TOP_K, tr), F32)], axis=0)
    gates_ref[...] = padded.T
    ek_ref[...] = ids.astype(jnp.int32)

    @pl.when(step == 0)
    def _():
        cnt_ref[...] = jnp.zeros(cnt_ref.shape, F32)
        t_i = lax.broadcasted_iota(jnp.int32, (tr, tr), 0)
        t_j = lax.broadcasted_iota(jnp.int32, (tr, tr), 1)
        before_ref[...] = jnp.where(t_i < t_j, 1.0, 0.0).astype(BF16)

    chosen_b = chosen.astype(BF16)
    running = cnt_ref[...]
    rank = _dot(chosen_b, before_ref[...]) + jnp.concatenate([running] * (tr // LANES), axis=1)
    rank3 = rank.reshape(N_GROUPS, GROUP_SIZE, tr)
    rk = [_sum_all(jnp.where(exp_idx == ids[k:k + 1].reshape(1, 1, tr), rank3, 0.0)).reshape(1, tr)
          for k in range(TOP_K)]
    rk_ref[...] = jnp.concatenate(rk, axis=0).astype(jnp.int32)
    cnt_ref[...] = running + _dot(chosen_b, jnp.ones((tr, LANES), BF16))


def _route_call(conv, attn, hn, woc, woa, g1, b1, wsg, wsu, wsd, wrh, wrl, rbias):
    n, D = hn.shape
    tr = _pick_tile(n, (768, 512, 256))

    def row_spec(w):
        return pl.BlockSpec((tr, w), lambda i: (i, 0))

    def col_spec(r):
        return pl.BlockSpec((r, tr), lambda i: (0, i))

    def full(a):
        return pl.BlockSpec(a.shape, lambda i: (0,) * a.ndim)

    consts = [woc, woa, g1, b1, wsg, wsu, wsd, wrh, wrl, rbias]
    return pl.pallas_call(
        _route_kernel,
        grid=(n // tr,),
        in_specs=[row_spec(CONV_CH), row_spec(ATTN_W), row_spec(D)] + [full(a) for a in consts],
        out_specs=[row_spec(D // 2), row_spec(D), row_spec(LANES), col_spec(TOP_K), col_spec(TOP_K),
                   pl.BlockSpec((N_EXPERTS, LANES), lambda i: (0, 0))],
        out_shape=[jax.ShapeDtypeStruct((n, D // 2), jnp.int32),
                   jax.ShapeDtypeStruct((n, D), F32),
                   jax.ShapeDtypeStruct((n, LANES), F32),
                   jax.ShapeDtypeStruct((TOP_K, n), jnp.int32),
                   jax.ShapeDtypeStruct((TOP_K, n), jnp.int32),
                   jax.ShapeDtypeStruct((N_EXPERTS, LANES), F32)],
        scratch_shapes=[pltpu.VMEM((tr, tr), BF16)],
        compiler_params=pltpu.CompilerParams(
            dimension_semantics=("arbitrary",), vmem_limit_bytes=VMEM_LIMIT),
        name="route",
    )(conv, attn, hn, *consts)


def _plan_kernel(cnt_ref, ek_ref, rk_ref, slot_ref, blk_ref, *, n_blocks):
    tr = ek_ref.shape[1]
    counts = cnt_ref[...]
    padded = jnp.ceil(counts / EXPERT_BLOCK) * EXPERT_BLOCK
    starts = [jnp.zeros((1, LANES), F32)]
    for e in range(1, N_EXPERTS):
        starts.append(starts[-1] + padded[e - 1:e])
    ek = ek_ref[...]
    seg = jnp.zeros(ek.shape, F32)
    for e in range(N_EXPERTS):
        seg = jnp.where(ek == e, jnp.concatenate([starts[e]] * (tr // LANES), axis=1), seg)
    slot_ref[...] = seg.astype(jnp.int32) + rk_ref[...]

    ends = jnp.concatenate(starts, axis=0) + padded
    w = blk_ref.shape[1]
    blk_start = lax.broadcasted_iota(jnp.int32, (N_EXPERTS, w), 1).astype(F32) * EXPERT_BLOCK
    done = jnp.where(jnp.concatenate([ends] * (w // LANES), axis=1) <= blk_start, 1.0, 0.0)
    owner = jnp.minimum(jnp.sum(done, axis=0, keepdims=True), N_EXPERTS - 1.0)
    used = jnp.concatenate([ends[N_EXPERTS - 1:] * (1.0 / EXPERT_BLOCK)] * (w // LANES), axis=1)
    lane = lax.broadcasted_iota(jnp.int32, (1, w), 1)
    table = jnp.where(lane == n_blocks, used, owner)
    blk_ref[...] = jnp.broadcast_to(table, blk_ref.shape).astype(jnp.int32)


def _plan_call(cnt, ek, rk, n_blocks):
    k, n = ek.shape
    tr = _pick_tile(n, (768, 512, 256))
    w = -(-(n_blocks + 1) // LANES) * LANES
    col = pl.BlockSpec((k, tr), lambda i: (0, i))
    return pl.pallas_call(
        functools.partial(_plan_kernel, n_blocks=n_blocks),
        grid=(n // tr,),
        in_specs=[pl.BlockSpec(cnt.shape, lambda i: (0, 0)), col, col],
        out_specs=[col, pl.BlockSpec((SUBLANES, w), lambda i: (0, 0))],
        out_shape=[jax.ShapeDtypeStruct((k, n), jnp.int32), jax.ShapeDtypeStruct((SUBLANES, w), jnp.int32)],
        compiler_params=pltpu.CompilerParams(dimension_semantics=("arbitrary",)),
        name="plan",
    )(cnt, ek, rk)


def _sc_workers():
    info = plsc.get_sparse_core_info()
    return info.num_cores, info.num_subcores


def _sc_scatter_rows(src, slot, n_out):
    n_src, d = src.shape
    n_choices = slot.shape[0]
    rows = SC_SCATTER_ROWS
    n_chunks = n_src // rows
    slots = slot.reshape(n_choices * n_chunks, rows)
    n_cores, n_sub = _sc_workers()
    n_workers = n_cores * n_sub

    def body(src_hbm, slot_hbm, out_hbm, idx_v, rows_v):
        wid = lax.axis_index("s") * n_cores + lax.axis_index("c")

        @pl.loop(0, -(-n_chunks // n_workers))
        def _(j):
            u = j * n_workers + wid

            @pl.when(u < n_chunks)
            def _():
                pltpu.sync_copy(src_hbm.at[pl.ds(u * rows, rows)], rows_v)
                for k in range(n_choices):
                    pltpu.sync_copy(slot_hbm.at[pl.ds(k * n_chunks + u, 1)], idx_v)
                    pltpu.sync_copy(rows_v, out_hbm.at[idx_v.at[0]])

    return pl.kernel(
        body, out_type=jax.ShapeDtypeStruct((n_out, d), src.dtype),
        mesh=plsc.VectorSubcoreMesh(core_axis_name="c", subcore_axis_name="s"),
        scratch_types=[pltpu.VMEM((1, rows), jnp.int32), pltpu.VMEM((rows, d), src.dtype)],
        name="dispatch_rows",
    )(src, slots)


def _sc_gather_rows(table, slots):
    units = slots.shape[0]
    d = table.shape[1]
    n_cores, n_sub = _sc_workers()
    n_workers = n_cores * n_sub
    assert slots.shape[1] == SC_ROWS

    def body(table_hbm, slot_hbm, out_hbm, idx_v, rows_v):
        wid = lax.axis_index("s") * n_cores + lax.axis_index("c")

        @pl.loop(0, -(-units // n_workers))
        def _(j):
            u = j * n_workers + wid

            @pl.when(u < units)
            def _():
                pltpu.sync_copy(slot_hbm.at[pl.ds(u, 1)], idx_v)
                pltpu.sync_copy(table_hbm.at[idx_v.at[0]], rows_v)
                pltpu.sync_copy(rows_v, out_hbm.at[pl.ds(u * SC_ROWS, SC_ROWS)])

    return pl.kernel(
        body, out_type=jax.ShapeDtypeStruct((units * SC_ROWS, d), table.dtype),
        mesh=plsc.VectorSubcoreMesh(core_axis_name="c", subcore_axis_name="s"),
        scratch_types=[pltpu.VMEM((1, SC_ROWS), jnp.int32), pltpu.VMEM((SC_ROWS, d), table.dtype)],
        name="collect_rows",
    )(table, slots)


def _expert_kernel(blk_ref, xs_ref, wg_ref, wu_ref, wd_ref, ys_ref):
    @pl.when(pl.program_id(0) < blk_ref[pl.num_programs(0)])
    def _():
        x = _unpack_bf16_pairs(xs_ref[...]).astype(BF16)
        hdn = (jax.nn.silu(_dot(x, wg_ref[0])) * _dot(x, wu_ref[0])).astype(BF16)
        ys_ref[...] = _pack_bf16_pairs(_dot(hdn, wd_ref[0]))


def _expert_call(blk_exp, xs, w_gate, w_up, w_down):
    p, half = xs.shape
    _, D, d_exp = w_gate.shape
    n_blocks = p // EXPERT_BLOCK
    assert blk_exp.shape == (n_blocks + 1,)

    def row_block(b, blk):
        return (jnp.minimum(b, blk[n_blocks] - 1), 0)

    rows = pl.BlockSpec((EXPERT_BLOCK, half), row_block)
    grid_spec = pltpu.PrefetchScalarGridSpec(
        num_scalar_prefetch=1,
        grid=(n_blocks,),
        in_specs=[rows,
                  pl.BlockSpec((1, D, d_exp), lambda b, blk: (blk[b], 0, 0)),
                  pl.BlockSpec((1, D, d_exp), lambda b, blk: (blk[b], 0, 0)),
                  pl.BlockSpec((1, d_exp, D), lambda b, blk: (blk[b], 0, 0))],
        out_specs=rows,
    )
    return pl.pallas_call(
        _expert_kernel,
        grid_spec=grid_spec,
        out_shape=jax.ShapeDtypeStruct((p, half), jnp.int32),
        compiler_params=pltpu.CompilerParams(
            dimension_semantics=("arbitrary",), vmem_limit_bytes=VMEM_LIMIT),
        name="experts",
    )(blk_exp, xs, w_gate, w_up, w_down)


def _combine_kernel(g_ref, gates_ref, base_ref, g2_ref, b2_ref, result_ref, o_ref):
    del result_ref
    gates = gates_ref[...]
    acc = base_ref[...]
    for k in range(g_ref.shape[0]):
        acc = acc + _unpack_bf16_pairs(g_ref[k]) * gates[:, k:k + 1]
    o_ref[0] = _layer_norm_rows(acc, g2_ref[...], b2_ref[...])


def _combine_call(g, gates, base, g2, b2, result, batch):
    k, n, half = g.shape
    D = base.shape[1]
    tr = SEQ_ALIGN

    def row_spec(w):
        return pl.BlockSpec((tr, w), lambda i: (i, 0))

    vec = pl.BlockSpec((1, D), lambda i: (0, 0))
    return pl.pallas_call(
        _combine_kernel,
        grid=(n // tr,),
        in_specs=[pl.BlockSpec((k, tr, half), lambda i: (0, i, 0)), row_spec(LANES), row_spec(D), vec, vec,
                  pl.BlockSpec(memory_space=pl.ANY)],
        out_specs=pl.BlockSpec((1, tr, D), lambda i: (batch, jnp.maximum(i - 1, 0), 0)),
        out_shape=jax.ShapeDtypeStruct(result.shape, result.dtype),
        input_output_aliases={5: 0},
        compiler_params=pltpu.CompilerParams(
            dimension_semantics=("arbitrary",), vmem_limit_bytes=VMEM_LIMIT),
        name="combine",
    )(g, gates, base, g2, b2, result)


def _rope_tables(tp, lead):
    pos = jnp.arange(tp, dtype=F32) - lead
    inv = jnp.power(ROPE_THETA, -2.0 * jnp.arange(ROPE_HALF, dtype=F32) / ROPE_DIM)
    ang = pos[:, None] * inv[None, :]
    cos, sin = jnp.cos(ang), jnp.sin(ang)
    zeros = jnp.zeros((tp, HEAD_DIM - ROPE_DIM), F32)
    zh = jnp.zeros((tp, ROPE_HALF), F32)
    c64 = jnp.concatenate([cos, cos, jnp.ones_like(zeros)], axis=1)
    s1_64 = jnp.concatenate([-sin, zh, zeros], axis=1)
    s2_64 = jnp.concatenate([zh, sin, zeros], axis=1)
    rep = LANES // HEAD_DIM
    return (jnp.tile(c64, (1, rep)), jnp.tile(s1_64, (1, rep)), jnp.tile(s2_64, (1, rep)),
            cos.T, sin.T)


def kernel(x, meta_tokens, ln_emb_g, ln_emb_b, w_in, conv_w, conv_b, ln_conv_g, ln_conv_b, ln_kidx_g, ln_kidx_b, w_out, ln1_g, ln1_b, w_router, router_bias, w_gate, w_up, w_down, ws_gate, ws_up, ws_down, ln2_g, ln2_b):
    B, seq, D = x.shape
    assert w_in.shape[0] == DEPTH
    assert seq % SEQ_ALIGN == 0 and meta_tokens.shape[0] == N_META <= SEQ_ALIGN
    k_sel = min(INDEX_TOPK, seq // 4)
    tp = seq + SEQ_ALIGN
    n_pad = SEQ_ALIGN - N_META

    def row(a):
        return a.reshape(1, -1).astype(F32)

    w = w_in[0]
    o = 0
    parts = []
    for width in (CONV_CH, CONV_CH, ATTN_W, ATTN_W, ATTN_W, IDX_HEADS * IDX_DIM, IDX_DIM, IDX_HEADS):
        parts.append(w[:, o:o + width])
        o += width
    wa, wgl, wq, wk, wv, wqi, wki, wwi = parts
    wwi_t = jnp.concatenate([wwi.T, jnp.zeros((BF16_SUBLANES - IDX_HEADS, D), w.dtype)], axis=0)
    weights = (jnp.concatenate([wa, wgl], axis=1).astype(BF16), wq.T.astype(BF16), wk.astype(BF16),
               wv.T.astype(BF16), wqi.T.astype(BF16), jnp.concatenate([wki, wki], axis=1).astype(BF16),
               wwi_t.astype(BF16))

    def twice(a):
        return row(jnp.concatenate([a, a]))

    tabs = _rope_tables(tp, n_pad)
    inproj_consts = (row(ln_emb_g), row(ln_emb_b), weights, conv_w[0].astype(F32), row(conv_b[0]),
                     row(ln_conv_g[0]), row(ln_conv_b[0]), twice(ln_kidx_g[0]), twice(ln_kidx_b[0]))
    wr_t = w_router[0].T.astype(F32)
    wr_hi = wr_t.astype(BF16)
    wr_lo = (wr_t - wr_hi.astype(F32)).astype(BF16)
    route_consts = (w_out[0][:CONV_CH].astype(BF16), w_out[0][CONV_CH:].astype(BF16), row(ln1_g[0]),
                    row(ln1_b[0]), ws_gate[0].astype(BF16), ws_up[0].astype(BF16), ws_down[0].astype(BF16),
                    wr_hi, wr_lo, router_bias[0].reshape(-1, 1).astype(F32))
    meta = meta_tokens.astype(F32)
    n_blocks = tp * TOP_K // EXPERT_BLOCK + N_EXPERTS

    expert_w_bf16 = None
    routed = []
    for b in range(B):
        hn, conv, qt, k, vt, qit, ki, wit = _inproj_call(x, b, meta, tabs, *inproj_consts)
        if b == 0:
            attn, *expert_w_bf16 = _dsa_call(qt, qit, wit, k, ki, vt, k_sel, n_pad,
                                             (w_gate[0], w_up[0], w_down[0]))
        else:
            attn, = _dsa_call(qt, qit, wit, k, ki, vt, k_sel, n_pad)
        xp, base, gates, ek, rk, cnt = _route_call(conv[0], attn[0], hn[0], *route_consts)

        slot, blk = _plan_call(cnt, ek, rk, n_blocks)
        slots = slot.reshape(TOP_K * tp // SC_ROWS, SC_ROWS)
        xs = _sc_scatter_rows(xp, slot, n_blocks * EXPERT_BLOCK)
        routed.append((xs, blk, slots, gates, base))

    result = jnp.zeros((B, seq, D), F32)
    for b, (xs, blk, slots, gates, base) in enumerate(routed):
        ys = _expert_call(blk[0, :n_blocks + 1], xs, *expert_w_bf16)
        picked = _sc_gather_rows(ys, slots).reshape(TOP_K, tp, D // 2)
        result = _combine_call(picked, gates, base, row(ln2_g[0]), row(ln2_b[0]), result, b)
    return result
```

```python
import functools

import numpy as np
import jax
import jax.numpy as jnp
from jax import lax
from jax.experimental import pallas as pl
from jax.experimental.pallas import tpu as pltpu
from jax.experimental.pallas import tpu_sc as plsc

N_META = 16
CONV_CH = 512
CONV_WIDTH = 31
N_HEADS = 8
HEAD_DIM = 64
ATTN_W = N_HEADS * HEAD_DIM
IDX_HEADS = 8
IDX_DIM = 64
INDEX_TOPK = 256
ROPE_DIM = HEAD_DIM // 4
ROPE_HALF = ROPE_DIM // 2
ROPE_THETA = 500000.0
N_EXPERTS = 64
TOP_K = 8
N_GROUPS = 8
GROUP_SIZE = N_EXPERTS // N_GROUPS
TOPK_GROUPS = 4
ROUTED_SCALE = 2.5
LN_EPS = 1e-5
DEPTH = 1
DN_ALPHA = (2.0 * DEPTH) ** 0.25

LANES = 128
Q_TILE = 256
SUBLANES = 8
BF16_SUBLANES = 16
EXPERT_BLOCK = 512
SC_ROWS = 128
SC_SCATTER_ROWS = 64
K_CHUNK = Q_TILE
K_SUB = 128
COUNT_UNROLL = 4
SEQ_ALIGN = 256
CONV_HALO = 32
VMEM_LIMIT = 56 * 1024 * 1024

F32 = jnp.float32
BF16 = jnp.bfloat16
COARSE = jnp.bfloat16
NEG_INF = float("-inf")
INT_MIN = -2 ** 31
KEY_NEG_INF = -2139095041
LOG2_E = 1.4426950408889634


def _dot(a, b):
    return jnp.dot(a, b, preferred_element_type=F32)


def _dot_nt(a, b):
    return lax.dot_general(a, b, (((1,), (1,)), ((), ())), preferred_element_type=F32)


def _layer_norm_rows(x, g, b):
    mu = jnp.mean(x, axis=-1, keepdims=True)
    xc = x - mu
    var = jnp.mean(xc * xc, axis=-1, keepdims=True)
    return xc * lax.rsqrt(var + LN_EPS) * g + b


def _pick_tile(n, candidates):
    for c in candidates:
        if n % c == 0:
            return c
    raise ValueError(f"no tile for {n}")


def _rope_rows(x, c_tab, s1_tab, s2_tab):
    outs = []
    for j in range(x.shape[1] // LANES):
        xs = x[:, j * LANES:(j + 1) * LANES]
        up = pltpu.roll(xs, LANES - ROPE_HALF, axis=1)
        dn = pltpu.roll(xs, ROPE_HALF, axis=1)
        outs.append(xs * c_tab + up * s1_tab + dn * s2_tab)
    return jnp.concatenate(outs, axis=1)


def _rope_cols(xt, cos_t, sin_t, heads):
    r = xt.shape[1]
    x3 = xt.reshape(heads, HEAD_DIM, r)
    x1 = x3[:, 0:ROPE_HALF, :]
    x2 = x3[:, ROPE_HALF:ROPE_DIM, :]
    n1 = x1 * cos_t - x2 * sin_t
    n2 = x2 * cos_t + x1 * sin_t
    out = jnp.concatenate([n1, n2, x3[:, ROPE_DIM:, :]], axis=1)
    return out.reshape(heads * HEAD_DIM, r)


def _inproj_kernel(x_ref, meta_ref, ctab_ref, s1tab_ref, s2tab_ref, cost_ref, sint_ref,
                   lng_ref, lnb_ref, wag_ref, wqt_ref, wk_ref, wvt_ref, wqit_ref, wki_ref, wwit_ref,
                   cw_ref, cb_ref, lncg_ref, lncb_ref, lnkg_ref, lnkb_ref,
                   hn_ref, conv_ref, qt_ref, k_ref, vt_ref, qit_ref, ki_ref, wit_ref,
                   ubuf_ref, wbuf_ref):
    t = pl.program_id(1)
    tr = x_ref.shape[1]
    n_meta = meta_ref.shape[0]
    first = jnp.concatenate([jnp.zeros((tr - n_meta, x_ref.shape[2]), F32), meta_ref[...]], axis=0)
    h = jnp.where(t == 0, first, x_ref[0])

    hn = _layer_norm_rows(h, lng_ref[...], lnb_ref[...])
    hn_ref[0] = hn
    xb = hn.astype(BF16)

    ag = _dot(xb, wag_ref[...])
    u = ag[:, :CONV_CH] * jax.nn.sigmoid(ag[:, CONV_CH:])
    row = lax.broadcasted_iota(jnp.int32, (tr, CONV_CH), 0)
    u = jnp.where(jnp.logical_or(t > 0, row >= tr - n_meta), u, 0.0)

    @pl.when(t == 0)
    def _():
        ubuf_ref[0:CONV_HALO, :] = jnp.zeros((CONV_HALO, CONV_CH), F32)

    ubuf_ref[CONV_HALO:CONV_HALO + tr, :] = u
    base = CONV_HALO - (CONV_WIDTH - 1)
    acc = jnp.zeros((tr, CONV_CH), F32)
    for r in range(SUBLANES):
        offsets = [o for o in range(base, base + CONV_WIDTH) if o % SUBLANES == r]
        rows = max(offsets) - r + tr
        wbuf_ref[0:rows, :] = ubuf_ref[r:r + rows, :]
        for o in offsets:
            acc = acc + cw_ref[o - base:o - base + 1, :] * wbuf_ref[o - r:o - r + tr, :]
    ubuf_ref[0:CONV_HALO, :] = ubuf_ref[tr:tr + CONV_HALO, :]
    c = _layer_norm_rows(acc + cb_ref[...], lncg_ref[...], lncb_ref[...])
    conv_ref[0] = (c * jax.nn.sigmoid(c)).astype(conv_ref.dtype)

    ctab, s1tab, s2tab = ctab_ref[...], s1tab_ref[...], s2tab_ref[...]
    cos_t, sin_t = cost_ref[...], sint_ref[...]

    qt = _rope_cols(_dot_nt(wqt_ref[...], xb), cos_t, sin_t, N_HEADS)
    qt_ref[0] = (qt * (HEAD_DIM ** -0.5 * LOG2_E)).astype(qt_ref.dtype)
    k = _rope_rows(_dot(xb, wk_ref[...]), ctab, s1tab, s2tab)
    k_ref[0] = k.astype(k_ref.dtype)
    vt_ref[0] = _dot_nt(wvt_ref[...], xb).astype(vt_ref.dtype)

    qit = _rope_cols(_dot_nt(wqit_ref[...], xb), cos_t, sin_t, IDX_HEADS)
    qit_ref[0] = qit.astype(qit_ref.dtype)
    ki = _layer_norm_rows(_dot(xb, wki_ref[...]), lnkg_ref[...], lnkb_ref[...])
    ki_ref[0] = _rope_rows(ki, ctab, s1tab, s2tab).astype(ki_ref.dtype)
    wit = _dot_nt(wwit_ref[...], xb) * (IDX_HEADS ** -0.5)
    wit_ref[0] = wit[:IDX_HEADS]


def _inproj_call(x, batch, meta, tabs, ln_g, ln_b, weights, conv_w, conv_b, lnc_g, lnc_b, lnk_g, lnk_b):
    _, seq, D = x.shape
    B = 1
    tr = SEQ_ALIGN
    tp = seq + tr
    nt = tp // tr
    ctab, s1tab, s2tab, cos_t, sin_t = tabs

    def row_spec(w):
        return pl.BlockSpec((1, tr, w), lambda b, t: (b, t, 0))

    def col_spec(r):
        return pl.BlockSpec((1, r, tr), lambda b, t: (b, 0, t))

    def full(a):
        return pl.BlockSpec(a.shape, lambda b, t: (0,) * a.ndim)

    tab_row = pl.BlockSpec((tr, LANES), lambda b, t: (t, 0))
    tab_col = pl.BlockSpec((ROPE_HALF, tr), lambda b, t: (0, t))
    consts = [ln_g, ln_b, *weights, conv_w, conv_b, lnc_g, lnc_b, lnk_g, lnk_b]
    out_shape = [
        jax.ShapeDtypeStruct((B, tp, D), F32),
        jax.ShapeDtypeStruct((B, tp, CONV_CH), BF16),
        jax.ShapeDtypeStruct((B, ATTN_W, tp), BF16),
        jax.ShapeDtypeStruct((B, tp, ATTN_W), BF16),
        jax.ShapeDtypeStruct((B, ATTN_W, tp), BF16),
        jax.ShapeDtypeStruct((B, IDX_HEADS * IDX_DIM, tp), BF16),
        jax.ShapeDtypeStruct((B, tp, 2 * IDX_DIM), BF16),
        jax.ShapeDtypeStruct((B, IDX_HEADS, tp), F32),
    ]
    out_specs = [row_spec(D), row_spec(CONV_CH), col_spec(ATTN_W), row_spec(ATTN_W), col_spec(ATTN_W),
                 col_spec(IDX_HEADS * IDX_DIM), row_spec(2 * IDX_DIM), col_spec(IDX_HEADS)]
    return pl.pallas_call(
        _inproj_kernel,
        grid=(B, nt),
        in_specs=[pl.BlockSpec((1, tr, D), lambda b, t: (batch, jnp.maximum(t - 1, 0), 0)), full(meta),
                  tab_row, tab_row, tab_row, tab_col, tab_col] + [full(a) for a in consts],
        out_specs=out_specs,
        out_shape=out_shape,
        scratch_shapes=[pltpu.VMEM((CONV_HALO + tr, CONV_CH), F32),
                        pltpu.VMEM((CONV_HALO + tr, CONV_CH), F32)],
        compiler_params=pltpu.CompilerParams(
            dimension_semantics=("arbitrary", "arbitrary"), vmem_limit_bytes=VMEM_LIMIT),
        name="inproj",
    )(x, meta, ctab, s1tab, s2tab, cos_t, sin_t, *consts)


def _key_to_float(key):
    bits = jnp.where(key >= 0, key, key ^ jnp.int32(0x7FFFFFFF))
    f = pltpu.bitcast(bits, F32)
    return jnp.where(key < jnp.int32(KEY_NEG_INF), NEG_INF, f)


def _tree(parts, op):
    parts = list(parts)
    while len(parts) > 1:
        nxt = [op(parts[j], parts[j + 1]) for j in range(0, len(parts) - 1, 2)]
        if len(parts) % 2:
            nxt.append(parts[-1])
        parts = nxt
    return parts[0]


def _fold_rows(x, rows, op):
    return _tree([x[j * rows:(j + 1) * rows] for j in range(x.shape[0] // rows)], op)


def _dsa_kernel(qt_ref, qit_ref, wit_ref, k_ref, ki_ref, vt_ref, *rest, k_sel, n_pad, n_convert):
    f32_w, rest = rest[:n_convert], rest[n_convert:]
    o_ref, bf16_w, rest = rest[0], rest[1:1 + n_convert], rest[1 + n_convert:]
    sc_ref, sh_ref, qm_ref, qim_ref, m_ref, alpha_ref, shift_ref, lg_ref, acc_ref = rest

    for src_ref, dst_ref in zip(f32_w, bf16_w):
        dst_ref[...] = src_ref[...].astype(BF16)

    i = pl.program_id(1)
    tq = qt_ref.shape[2]
    n_chunks = (i * tq + tq + K_CHUNK - 1) // K_CHUNK
    v_rows = HEAD_DIM + BF16_SUBLANES

    def causal_mask(k0, rows=K_CHUNK):
        kpos = k0 + lax.broadcasted_iota(jnp.int32, (rows, tq), 0)
        return kpos <= i * tq + lax.broadcasted_iota(jnp.int32, (rows, tq), 1)

    def rows8(x):
        return jnp.broadcast_to(x, (SUBLANES, tq))

    def tiles(x):
        return x.reshape(x.shape[0] // SUBLANES, SUBLANES, tq)

    def head_slab(ref, h):
        slab = ref[0, (h // 2) * LANES:(h // 2 + 1) * LANES, :]
        zeros = jnp.zeros((HEAD_DIM, tq), slab.dtype)
        if h % 2 == 0:
            return jnp.concatenate([slab[:HEAD_DIM], zeros], axis=0)
        return jnp.concatenate([zeros, slab[HEAD_DIM:]], axis=0)

    for h in range(N_HEADS):
        qm_ref[h] = head_slab(qt_ref, h)
    for h in range(IDX_HEADS):
        qim_ref[h] = head_slab(qit_ref, h)
    wit = wit_ref[0]
    w_heads = [rows8(wit[h:h + 1] * (IDX_DIM ** -0.5)) for h in range(IDX_HEADS)]

    def for_each_chunk(chunk_fn, unroll, count=n_chunks):
        def group(j, carry):
            for u in range(unroll):
                chunk_fn(unroll * j + u, u % 2)
            return carry

        def single(c, carry):
            chunk_fn(c, 0)
            return carry

        n_groups = count // unroll
        lax.fori_loop(0, n_groups, group, 0)
        lax.fori_loop(n_groups * unroll, count, single, 0)

    def diagonal_visible(first_row, rows):
        r = first_row + lax.broadcasted_iota(jnp.int32, (rows, tq), 0)
        return r <= lax.broadcasted_iota(jnp.int32, (rows, tq), 1)

    def score_chunk(c, slot, diagonal=False):
        del slot
        for s in range(K_CHUNK // K_SUB):
            k0 = pl.multiple_of(c * K_CHUNK + s * K_SUB, K_SUB)
            kic = ki_ref[0, pl.ds(k0, K_SUB), :]
            acc = jnp.zeros((K_SUB // SUBLANES, SUBLANES, tq), F32)
            for h in range(IDX_HEADS):
                acc = acc + w_heads[h][None] * jnp.maximum(tiles(_dot(kic, qim_ref[h])), 0.0)
            acc = acc.reshape(K_SUB, tq)
            if diagonal:
                acc = jnp.where(diagonal_visible(s * K_SUB, K_SUB), acc, NEG_INF)
            sc_ref[pl.ds(k0, K_SUB), :] = acc
            hi_bits = pltpu.bitcast(acc, jnp.int32) & jnp.int32(-65536)
            sh_ref[pl.ds(k0, K_SUB), :] = pltpu.bitcast(hi_bits, F32).astype(COARSE)

    for_each_chunk(score_chunk, 4, n_chunks - 1)
    score_chunk(n_chunks - 1, 0, diagonal=True)

    def drop_padding_keys():
        sh_ref[0:n_pad, :] = jnp.full((n_pad, tq), NEG_INF, COARSE)

    sc_ref[0:n_pad, :] = jnp.full((n_pad, tq), NEG_INF, F32)
    drop_padding_keys()

    def count_all(ref, thr_tile, preds):
        rows = thr_tile.shape[0]
        one, zero = jnp.ones((), ref.dtype), jnp.zeros((), ref.dtype)

        def body(c, cnts):
            k0 = pl.multiple_of(c * K_CHUNK, K_CHUNK)
            s = ref[pl.ds(k0, K_CHUNK), :]
            out = []
            for cnt, p in zip(cnts, preds):
                hits = [jnp.where(p(s[j * rows:(j + 1) * rows], thr_tile), one, zero)
                        for j in range(K_CHUNK // rows)]
                out.append(cnt + _tree(hits, jnp.add).astype(F32))
            return tuple(out)

        def group(j, cnts):
            for u in range(COUNT_UNROLL):
                cnts = body(COUNT_UNROLL * j + u, cnts)
            return cnts

        init = tuple(jnp.zeros((rows, tq), F32) for _ in preds)
        n_groups = n_chunks // COUNT_UNROLL
        cnts = lax.fori_loop(0, n_groups, group, init)
        cnts = lax.fori_loop(n_groups * COUNT_UNROLL, n_chunks, body, cnts)
        return [rows8(jnp.sum(cnt, axis=0, keepdims=True)) for cnt in cnts]

    def search_body(it, tkey, coarse):
        cand = tkey + lax.shift_left(jnp.int32(1), 31 - it)
        cf = _key_to_float(cand)
        if coarse:
            cf = pltpu.bitcast(pltpu.bitcast(cf, jnp.int32) & jnp.int32(-65536), F32).astype(COARSE)
            cf = jnp.concatenate([cf] * (BF16_SUBLANES // SUBLANES), axis=0)
        cnt, = count_all(sh_ref if coarse else sc_ref, cf, [lambda s, t: s >= t])
        return jnp.where(cnt >= k_sel, cand, tkey)

    tkey = jnp.full((SUBLANES, tq), INT_MIN, jnp.int32)
    tkey = lax.fori_loop(0, 16, functools.partial(search_body, coarse=True), tkey)
    tkey = lax.fori_loop(16, 32, functools.partial(search_body, coarse=False), tkey)
    thr = _key_to_float(tkey)

    n_ge, n_gt = count_all(sc_ref, thr, [lambda s, t: s >= t, lambda s, t: s > t])
    need = k_sel - n_gt
    has_ties = jnp.max(n_ge) > k_sel

    @pl.when(jnp.logical_not(has_ties))
    def _():
        def mask_chunk(c, slot, diagonal=False):
            del slot
            k0 = pl.multiple_of(c * K_CHUNK, K_CHUNK)
            s = sc_ref[pl.ds(k0, K_CHUNK), :]
            bias = jnp.where(tiles(s) >= thr[None], 0.0, NEG_INF).reshape(K_CHUNK, tq)
            if diagonal:
                bias = jnp.where(diagonal_visible(0, K_CHUNK), bias, NEG_INF)
            sh_ref[pl.ds(k0, K_CHUNK), :] = bias.astype(COARSE)

        for_each_chunk(mask_chunk, 4, n_chunks - 1)
        mask_chunk(n_chunks - 1, 0, diagonal=True)

    @pl.when(has_ties)
    def _():
        r_i = lax.broadcasted_iota(jnp.int32, (K_CHUNK, K_CHUNK), 0)
        c_i = lax.broadcasted_iota(jnp.int32, (K_CHUNK, K_CHUNK), 1)
        lower = jnp.where(c_i <= r_i, 1.0, 0.0).astype(BF16)
        thr_row, need_row = thr[0:1], need[0:1]

        def body(c, seen):
            k0 = pl.multiple_of(c * K_CHUNK, K_CHUNK)
            s = sc_ref[pl.ds(k0, K_CHUNK), :]
            eq = jnp.where(s == thr_row, 1.0, 0.0)
            rank = _dot(lower, eq.astype(BF16)) + seen
            keep_tie = jnp.where(rank <= need_row, eq, 0.0)
            sel = jnp.where(s > thr_row, 1.0, keep_tie)
            sh_ref[pl.ds(k0, K_CHUNK), :] = jnp.where(
                sel > 0.0, jnp.where(causal_mask(k0), 0.0, NEG_INF), NEG_INF).astype(COARSE)
            return rank[K_CHUNK - 1:K_CHUNK, :]
        lax.fori_loop(0, n_chunks, body, jnp.zeros((1, tq), F32))

    drop_padding_keys()

    m_ref[...] = jnp.full(m_ref.shape, NEG_INF, F32)
    acc_ref[...] = jnp.zeros(acc_ref.shape, F32)
    ones_rows = jnp.ones((BF16_SUBLANES, K_CHUNK), BF16)

    def logits_stage(c, slot):
        k0 = pl.multiple_of(c * K_CHUNK, K_CHUNK)
        for h in range(N_HEADS):
            pair = slice((h // 2) * LANES, (h // 2 + 1) * LANES)
            cmax = []
            for s in range(K_CHUNK // K_SUB):
                rows = pl.ds(pl.multiple_of(k0 + s * K_SUB, K_SUB), K_SUB)
                lg = _dot(k_ref[0, rows, pair], qm_ref[h]).astype(BF16) + sh_ref[rows, :].astype(BF16)
                lg_ref[slot, h, s * K_SUB:(s + 1) * K_SUB, :] = lg
                cmax.append(_fold_rows(lg, BF16_SUBLANES, jnp.maximum))
            cmax = _tree(cmax, jnp.maximum).astype(F32)
            m_old = m_ref[h]
            m_new = jnp.maximum(m_old, rows8(jnp.max(cmax, axis=0, keepdims=True)))
            m_safe = jnp.where(m_new == NEG_INF, 0.0, m_new)
            alpha_ref[slot, h] = jnp.exp2(m_old - m_safe)
            shift_ref[slot, h] = m_safe
            m_ref[h] = m_new

    def values_stage(c, slot):
        k0 = pl.multiple_of(c * K_CHUNK, K_CHUNK)
        for h in range(N_HEADS):
            m_safe = shift_ref[slot, h].astype(BF16)
            m_tile = jnp.concatenate([m_safe] * (BF16_SUBLANES // SUBLANES), axis=0)
            lg = lg_ref[slot, h].reshape(K_CHUNK // BF16_SUBLANES, BF16_SUBLANES, tq)
            p = jnp.exp2(lg - m_tile[None]).reshape(K_CHUNK, tq)
            v_aug = jnp.concatenate(
                [vt_ref[0, h * HEAD_DIM:(h + 1) * HEAD_DIM, pl.ds(k0, K_CHUNK)], ones_rows], axis=0)
            pv = _dot(v_aug, p)
            acc = acc_ref[h].reshape(v_rows // SUBLANES, SUBLANES, tq) * alpha_ref[slot, h][None]
            acc_ref[h] = acc.reshape(v_rows, tq) + pv

    def attn_chunk(c, slot):
        logits_stage(c, slot)
        values_stage(c, slot)

    for_each_chunk(attn_chunk, 4)

    for pair in range(N_HEADS // 2):
        halves = []
        for h in (2 * pair, 2 * pair + 1):
            a = acc_ref[h]
            total = a[HEAD_DIM:HEAD_DIM + 1]
            halves.append(a[:HEAD_DIM] / jnp.where(total == 0.0, 1.0, total))
        out_t = jnp.concatenate(halves, axis=0)
        o_ref[0, :, pair * LANES:(pair + 1) * LANES] = out_t.T.astype(o_ref.dtype)


def _dsa_call(qt, qit, wit, k, ki, vt, k_sel, n_pad, expert_w, part, n_parts):
    B, tp, _ = k.shape
    nq = tp // Q_TILE
    n_conv = expert_w[0].shape[0] // n_parts
    assert n_conv * n_parts == expert_w[0].shape[0]
    per_step = next(d for d in range(1, n_conv + 1) if n_conv % d == 0 and n_conv // d <= nq)
    n_steps = n_conv // per_step

    def q_cols(r):
        return pl.BlockSpec((1, r, Q_TILE), lambda b, i: (b, 0, i))

    def per_batch(a):
        return pl.BlockSpec((1,) + a.shape[1:], lambda b, i: (b, 0, 0), pipeline_mode=pl.Buffered(1))

    def expert_spec(a, first_block):
        return pl.BlockSpec((per_step,) + a.shape[1:],
                            lambda b, i: (first_block + jnp.minimum(i, n_steps - 1), 0, 0))

    in_specs = [q_cols(ATTN_W), q_cols(IDX_HEADS * IDX_DIM), q_cols(IDX_HEADS),
                per_batch(k), per_batch(ki), per_batch(vt)]
    in_specs += [expert_spec(a, part * n_steps) for a in expert_w]

    return pl.pallas_call(
        functools.partial(_dsa_kernel, k_sel=k_sel, n_pad=n_pad, n_convert=len(expert_w)),
        grid=(B, nq),
        in_specs=in_specs,
        out_specs=[pl.BlockSpec((1, Q_TILE, ATTN_W), lambda b, i: (b, i, 0))]
        + [expert_spec(a, 0) for a in expert_w],
        out_shape=[jax.ShapeDtypeStruct((B, tp, ATTN_W), BF16)]
        + [jax.ShapeDtypeStruct((n_conv,) + a.shape[1:], BF16) for a in expert_w],
        scratch_shapes=[pltpu.VMEM((tp, Q_TILE), F32),
                        pltpu.VMEM((tp, Q_TILE), COARSE),
                        pltpu.VMEM((N_HEADS, LANES, Q_TILE), BF16),
                        pltpu.VMEM((IDX_HEADS, LANES, Q_TILE), BF16),
                        pltpu.VMEM((N_HEADS, SUBLANES, Q_TILE), F32),
                        pltpu.VMEM((2, N_HEADS, SUBLANES, Q_TILE), F32),
                        pltpu.VMEM((2, N_HEADS, SUBLANES, Q_TILE), F32),
                        pltpu.VMEM((2, N_HEADS, K_CHUNK, Q_TILE), BF16),
                        pltpu.VMEM((N_HEADS, HEAD_DIM + BF16_SUBLANES, Q_TILE), F32)],
        compiler_params=pltpu.CompilerParams(
            dimension_semantics=("arbitrary", "arbitrary"), vmem_limit_bytes=VMEM_LIMIT),
        name="dsa",
    )(qt, qit, wit, k, ki, vt, *expert_w)


def _max_all(x):
    return jnp.max(jnp.max(x, axis=0, keepdims=True), axis=1, keepdims=True)


def _sum_all(x):
    return jnp.sum(jnp.sum(x, axis=0, keepdims=True), axis=1, keepdims=True)


def _pack_bf16_pairs(x):
    w = x.shape[1] // 2
    lo = pltpu.bitcast(x[:, :w].astype(BF16).astype(F32), jnp.int32)
    hi = pltpu.bitcast(x[:, w:].astype(BF16).astype(F32), jnp.int32)
    return lax.shift_right_logical(lo, 16) | (hi & jnp.int32(-65536))


def _unpack_bf16_pairs(p):
    lo = pltpu.bitcast(lax.shift_left(p, 16), F32)
    hi = pltpu.bitcast(p & jnp.int32(-65536), F32)
    return jnp.concatenate([lo, hi], axis=1)


def _router_topk(logits_t, rbias):
    r = logits_t.shape[1]
    shape3 = (N_GROUPS, GROUP_SIZE, r)
    scores = jax.nn.sigmoid(logits_t).reshape(shape3)
    biased = scores + rbias.reshape(N_GROUPS, GROUP_SIZE, 1)
    in_grp = lax.broadcasted_iota(jnp.int32, shape3, 1).astype(F32)
    m1 = jnp.max(biased, axis=1, keepdims=True)
    first = jnp.min(jnp.where(biased == m1, in_grp, float(GROUP_SIZE)), axis=1, keepdims=True)
    m2 = jnp.max(jnp.where(in_grp == first, NEG_INF, biased), axis=1, keepdims=True)
    cur = m1 + m2

    grp_idx = lax.broadcasted_iota(jnp.int32, (N_GROUPS, 1, r), 0).astype(F32)
    grp_sel = jnp.zeros((N_GROUPS, 1, r), F32)
    for _ in range(TOPK_GROUPS):
        m = jnp.max(cur, axis=0, keepdims=True)
        pick = grp_idx == jnp.min(jnp.where(cur == m, grp_idx, float(N_GROUPS)), axis=0, keepdims=True)
        grp_sel = jnp.where(pick, 1.0, grp_sel)
        cur = jnp.where(pick, NEG_INF, cur)

    cur = jnp.where(jnp.broadcast_to(grp_sel, shape3) > 0.0, biased, NEG_INF)
    exp_idx = lax.broadcasted_iota(jnp.int32, shape3, 0).astype(F32) * GROUP_SIZE + in_grp
    chosen = jnp.zeros(shape3, F32)
    ids, wts = [], []
    for _ in range(TOP_K):
        m = _max_all(cur)
        first = -_max_all(-jnp.where(cur == m, exp_idx, float(N_EXPERTS)))
        pick = exp_idx == first
        chosen = jnp.where(pick, 1.0, chosen)
        cur = jnp.where(pick, NEG_INF, cur)
        ids.append(first.reshape(1, r))
        wts.append(_sum_all(jnp.where(pick, scores, 0.0)).reshape(1, r))
    ids = jnp.concatenate(ids, axis=0)
    wts = jnp.concatenate(wts, axis=0)
    gates = wts / jnp.sum(wts, axis=0, keepdims=True) * ROUTED_SCALE
    return ids, gates, chosen.reshape(N_EXPERTS, r), exp_idx


def _route_kernel(conv_ref, attn_ref, hn_ref, woc_ref, woa_ref, g1_ref, b1_ref,
                  wsg_ref, wsu_ref, wsd_ref, wrh_ref, wrl_ref, rb_ref,
                  xp_ref, base_ref, gates_ref, ek_ref, rk_ref, cnt_ref, before_ref):
    step = pl.program_id(0)
    tr = hn_ref.shape[0]
    mix = _dot(conv_ref[...], woc_ref[...]) + _dot(attn_ref[...], woa_ref[...])
    h1 = _layer_norm_rows(DN_ALPHA * hn_ref[...] + mix, g1_ref[...], b1_ref[...])
    xb = h1.astype(BF16)
    xp_ref[...] = _pack_bf16_pairs(h1)

    shared = jax.nn.silu(_dot(xb, wsg_ref[...])) * _dot(xb, wsu_ref[...])
    base_ref[...] = DN_ALPHA * h1 + _dot(shared.astype(BF16), wsd_ref[...])

    x_lo = (h1 - xb.astype(F32)).astype(BF16)
    logits_t = (_dot_nt(wrh_ref[...], xb) + _dot_nt(wrh_ref[...], x_lo) + _dot_nt(wrl_ref[...], xb))
    ids, gates, chosen, exp_idx = _router_topk(logits_t, rb_ref[...])
    padded = jnp.concatenate([gates, jnp.zeros((LANES - TOP_K, tr), F32)], axis=0)
    gates_ref[...] = padded.T
    ek_ref[...] = ids.astype(jnp.int32)

    @pl.when(step == 0)
    def _():
        cnt_ref[...] = jnp.zeros(cnt_ref.shape, F32)
        t_i = lax.broadcasted_iota(jnp.int32, (tr, tr), 0)
        t_j = lax.broadcasted_iota(jnp.int32, (tr, tr), 1)
        before_ref[...] = jnp.where(t_i < t_j, 1.0, 0.0).astype(BF16)

    chosen_b = chosen.astype(BF16)
    running = cnt_ref[...]
    rank = _dot(chosen_b, before_ref[...]) + jnp.concatenate([running] * (tr // LANES), axis=1)
    rank3 = rank.reshape(N_GROUPS, GROUP_SIZE, tr)
    rk = [_sum_all(jnp.where(exp_idx == ids[k:k + 1].reshape(1, 1, tr), rank3, 0.0)).reshape(1, tr)
          for k in range(TOP_K)]
    rk_ref[...] = jnp.concatenate(rk, axis=0).astype(jnp.int32)
    cnt_ref[...] = running + _dot(chosen_b, jnp.ones((tr, LANES), BF16))


def _route_call(conv, attn, hn, woc, woa, g1, b1, wsg, wsu, wsd, wrh, wrl, rbias):
    n, D = hn.shape
    tr = _pick_tile(n, (768, 512, 256))

    def row_spec(w):
        return pl.BlockSpec((tr, w), lambda i: (i, 0))

    def col_spec(r):
        return pl.BlockSpec((r, tr), lambda i: (0, i))

    def full(a):
        return pl.BlockSpec(a.shape, lambda i: (0,) * a.ndim)

    consts = [woc, woa, g1, b1, wsg, wsu, wsd, wrh, wrl, rbias]
    return pl.pallas_call(
        _route_kernel,
        grid=(n // tr,),
        in_specs=[row_spec(CONV_CH), row_spec(ATTN_W), row_spec(D)] + [full(a) for a in consts],
        out_specs=[row_spec(D // 2), row_spec(D), row_spec(LANES), col_spec(TOP_K), col_spec(TOP_K),
                   pl.BlockSpec((N_EXPERTS, LANES), lambda i: (0, 0))],
        out_shape=[jax.ShapeDtypeStruct((n, D // 2), jnp.int32),
                   jax.ShapeDtypeStruct((n, D), F32),
                   jax.ShapeDtypeStruct((n, LANES), F32),
                   jax.ShapeDtypeStruct((TOP_K, n), jnp.int32),
                   jax.ShapeDtypeStruct((TOP_K, n), jnp.int32),
                   jax.ShapeDtypeStruct((N_EXPERTS, LANES), F32)],
        scratch_shapes=[pltpu.VMEM((tr, tr), BF16)],
        compiler_params=pltpu.CompilerParams(
            dimension_semantics=("arbitrary",), vmem_limit_bytes=VMEM_LIMIT),
        name="route",
    )(conv, attn, hn, *consts)


def _plan_kernel(cnt_ref, ek_ref, rk_ref, slot_ref, blk_ref, *, n_blocks):
    tr = ek_ref.shape[1]
    counts = cnt_ref[...]
    padded = jnp.ceil(counts / EXPERT_BLOCK) * EXPERT_BLOCK
    starts = [jnp.zeros((1, LANES), F32)]
    for e in range(1, N_EXPERTS):
        starts.append(starts[-1] + padded[e - 1:e])
    ek = ek_ref[...]
    seg = jnp.zeros(ek.shape, F32)
    for e in range(N_EXPERTS):
        seg = jnp.where(ek == e, jnp.concatenate([starts[e]] * (tr // LANES), axis=1), seg)
    slot_ref[...] = seg.astype(jnp.int32) + rk_ref[...]

    ends = jnp.concatenate(starts, axis=0) + padded
    w = blk_ref.shape[1]
    blk_start = lax.broadcasted_iota(jnp.int32, (N_EXPERTS, w), 1).astype(F32) * EXPERT_BLOCK
    done = jnp.where(jnp.concatenate([ends] * (w // LANES), axis=1) <= blk_start, 1.0, 0.0)
    owner = jnp.minimum(jnp.sum(done, axis=0, keepdims=True), N_EXPERTS - 1.0)
    used = jnp.concatenate([ends[N_EXPERTS - 1:] * (1.0 / EXPERT_BLOCK)] * (w // LANES), axis=1)
    lane = lax.broadcasted_iota(jnp.int32, (1, w), 1)
    table = jnp.where(lane == n_blocks, used, owner)
    blk_ref[...] = jnp.broadcast_to(table, blk_ref.shape).astype(jnp.int32)


def _plan_call(cnt, ek, rk, n_blocks):
    k, n = ek.shape
    tr = _pick_tile(n, (768, 512, 256))
    w = -(-(n_blocks + 1) // LANES) * LANES
    col = pl.BlockSpec((k, tr), lambda i: (0, i))
    return pl.pallas_call(
        functools.partial(_plan_kernel, n_blocks=n_blocks),
        grid=(n // tr,),
        in_specs=[pl.BlockSpec(cnt.shape, lambda i: (0, 0)), col, col],
        out_specs=[col, pl.BlockSpec((SUBLANES, w), lambda i: (0, 0))],
        out_shape=[jax.ShapeDtypeStruct((k, n), jnp.int32), jax.ShapeDtypeStruct((SUBLANES, w), jnp.int32)],
        compiler_params=pltpu.CompilerParams(dimension_semantics=("arbitrary",)),
        name="plan",
    )(cnt, ek, rk)


def _sc_workers():
    info = plsc.get_sparse_core_info()
    return info.num_cores, info.num_subcores


def _sc_scatter_rows(src, slot, n_out):
    n_src, d = src.shape
    n_choices = slot.shape[0]
    rows = SC_SCATTER_ROWS
    n_chunks = n_src // rows
    slots = slot.reshape(n_choices * n_chunks, rows)
    n_cores, n_sub = _sc_workers()
    n_workers = n_cores * n_sub

    def body(src_hbm, slot_hbm, out_hbm, idx_v, rows_v):
        wid = lax.axis_index("s") * n_cores + lax.axis_index("c")

        @pl.loop(0, -(-n_chunks // n_workers))
        def _(j):
            u = j * n_workers + wid

            @pl.when(u < n_chunks)
            def _():
                pltpu.sync_copy(src_hbm.at[pl.ds(u * rows, rows)], rows_v)
                for k in range(n_choices):
                    pltpu.sync_copy(slot_hbm.at[pl.ds(k * n_chunks + u, 1)], idx_v)
                    pltpu.sync_copy(rows_v, out_hbm.at[idx_v.at[0]])

    return pl.kernel(
        body, out_type=jax.ShapeDtypeStruct((n_out, d), src.dtype),
        mesh=plsc.VectorSubcoreMesh(core_axis_name="c", subcore_axis_name="s"),
        scratch_types=[pltpu.VMEM((1, rows), jnp.int32), pltpu.VMEM((rows, d), src.dtype)],
        name="dispatch_rows",
    )(src, slots)


def _sc_gather_rows(table, slots):
    units = slots.shape[0]
    d = table.shape[1]
    n_cores, n_sub = _sc_workers()
    n_workers = n_cores * n_sub
    assert slots.shape[1] == SC_ROWS

    def body(table_hbm, slot_hbm, out_hbm, idx_v, rows_v):
        wid = lax.axis_index("s") * n_cores + lax.axis_index("c")

        @pl.loop(0, -(-units // n_workers))
        def _(j):
            u = j * n_workers + wid

            @pl.when(u < units)
            def _():
                pltpu.sync_copy(slot_hbm.at[pl.ds(u, 1)], idx_v)
                pltpu.sync_copy(table_hbm.at[idx_v.at[0]], rows_v)
                pltpu.sync_copy(rows_v, out_hbm.at[pl.ds(u * SC_ROWS, SC_ROWS)])

    return pl.kernel(
        body, out_type=jax.ShapeDtypeStruct((units * SC_ROWS, d), table.dtype),
        mesh=plsc.VectorSubcoreMesh(core_axis_name="c", subcore_axis_name="s"),
        scratch_types=[pltpu.VMEM((1, SC_ROWS), jnp.int32), pltpu.VMEM((SC_ROWS, d), table.dtype)],
        name="collect_rows",
    )(table, slots)


def _expert_kernel(blk_ref, xs_ref, *rest, per_share):
    ys_ref = rest[-1]
    b = pl.program_id(0)
    in_use = b < blk_ref[pl.num_programs(0)]
    share = blk_ref[b] // per_share
    for s in range(len(rest) // 3):
        wg_ref, wu_ref, wd_ref = rest[3 * s:3 * s + 3]

        @pl.when(jnp.logical_and(in_use, share == s))
        def _():
            x = _unpack_bf16_pairs(xs_ref[...]).astype(BF16)
            hdn = (jax.nn.silu(_dot(x, wg_ref[0])) * _dot(x, wu_ref[0])).astype(BF16)
            ys_ref[...] = _pack_bf16_pairs(_dot(hdn, wd_ref[0]))


def _expert_call(blk_exp, xs, weight_shares):
    p, half = xs.shape
    per_share = weight_shares[0][0].shape[0]
    n_blocks = p // EXPERT_BLOCK
    assert blk_exp.shape == (n_blocks + 1,)

    def row_block(b, blk):
        return (jnp.minimum(b, blk[n_blocks] - 1), 0)

    def weight_spec(a, s):
        return pl.BlockSpec((1,) + a.shape[1:],
                            lambda b, blk: (jnp.clip(blk[b] - s * per_share, 0, per_share - 1), 0, 0))

    rows = pl.BlockSpec((EXPERT_BLOCK, half), row_block)
    grid_spec = pltpu.PrefetchScalarGridSpec(
        num_scalar_prefetch=1,
        grid=(n_blocks,),
        in_specs=[rows] + [weight_spec(a, s) for s, share in enumerate(weight_shares) for a in share],
        out_specs=rows,
    )
    return pl.pallas_call(
        functools.partial(_expert_kernel, per_share=per_share),
        grid_spec=grid_spec,
        out_shape=jax.ShapeDtypeStruct((p, half), jnp.int32),
        compiler_params=pltpu.CompilerParams(
            dimension_semantics=("arbitrary",), vmem_limit_bytes=VMEM_LIMIT),
        name="experts",
    )(blk_exp, xs, *[a for share in weight_shares for a in share])


def _combine_kernel(g_ref, gates_ref, base_ref, g2_ref, b2_ref, *rest):
    o_ref = rest[-1]
    gates = gates_ref[...]
    acc = base_ref[...]
    for k in range(g_ref.shape[0]):
        acc = acc + _unpack_bf16_pairs(g_ref[k]) * gates[:, k:k + 1]
    o_ref[0] = _layer_norm_rows(acc, g2_ref[...], b2_ref[...])


def _combine_call(g, gates, base, g2, b2, result, batch, n_batch):
    k, n, half = g.shape
    D = base.shape[1]
    tr = SEQ_ALIGN
    seq = n - tr

    def row_spec(w):
        return pl.BlockSpec((tr, w), lambda i: (i, 0))

    vec = pl.BlockSpec((1, D), lambda i: (0, 0))
    in_specs = [pl.BlockSpec((k, tr, half), lambda i: (0, i, 0)), row_spec(LANES), row_spec(D), vec, vec]
    args = [g, gates, base, g2, b2]
    aliases = {}
    if result is not None:
        in_specs.append(pl.BlockSpec(memory_space=pl.ANY))
        args.append(result)
        aliases = {len(args) - 1: 0}
    return pl.pallas_call(
        _combine_kernel,
        grid=(n // tr,),
        in_specs=in_specs,
        out_specs=pl.BlockSpec((1, tr, D), lambda i: (batch, jnp.maximum(i - 1, 0), 0)),
        out_shape=jax.ShapeDtypeStruct((n_batch, seq, D), F32),
        input_output_aliases=aliases,
        compiler_params=pltpu.CompilerParams(
            dimension_semantics=("arbitrary",), vmem_limit_bytes=VMEM_LIMIT),
        name="combine",
    )(*args)


def _rope_tables(tp, lead):
    pos = jnp.arange(tp, dtype=F32) - lead
    inv = jnp.power(ROPE_THETA, -2.0 * jnp.arange(ROPE_HALF, dtype=F32) / ROPE_DIM)
    ang = pos[:, None] * inv[None, :]
    cos, sin = jnp.cos(ang), jnp.sin(ang)
    zeros = jnp.zeros((tp, HEAD_DIM - ROPE_DIM), F32)
    zh = jnp.zeros((tp, ROPE_HALF), F32)
    c64 = jnp.concatenate([cos, cos, jnp.ones_like(zeros)], axis=1)
    s1_64 = jnp.concatenate([-sin, zh, zeros], axis=1)
    s2_64 = jnp.concatenate([zh, sin, zeros], axis=1)
    rep = LANES // HEAD_DIM
    return (jnp.tile(c64, (1, rep)), jnp.tile(s1_64, (1, rep)), jnp.tile(s2_64, (1, rep)),
            cos.T, sin.T)


def kernel(x, meta_tokens, ln_emb_g, ln_emb_b, w_in, conv_w, conv_b, ln_conv_g, ln_conv_b, ln_kidx_g, ln_kidx_b, w_out, ln1_g, ln1_b, w_router, router_bias, w_gate, w_up, w_down, ws_gate, ws_up, ws_down, ln2_g, ln2_b):
    B, seq, D = x.shape
    assert w_in.shape[0] == DEPTH
    assert seq % SEQ_ALIGN == 0 and meta_tokens.shape[0] == N_META <= SEQ_ALIGN
    k_sel = min(INDEX_TOPK, seq // 4)
    tp = seq + SEQ_ALIGN
    n_pad = SEQ_ALIGN - N_META

    def row(a):
        return a.reshape(1, -1).astype(F32)

    w = w_in[0]
    o = 0
    parts = []
    for width in (CONV_CH, CONV_CH, ATTN_W, ATTN_W, ATTN_W, IDX_HEADS * IDX_DIM, IDX_DIM, IDX_HEADS):
        parts.append(w[:, o:o + width])
        o += width
    wa, wgl, wq, wk, wv, wqi, wki, wwi = parts
    wwi_t = jnp.concatenate([wwi.T, jnp.zeros((BF16_SUBLANES - IDX_HEADS, D), w.dtype)], axis=0)
    weights = (jnp.concatenate([wa, wgl], axis=1).astype(BF16), wq.T.astype(BF16), wk.astype(BF16),
               wv.T.astype(BF16), wqi.T.astype(BF16), jnp.concatenate([wki, wki], axis=1).astype(BF16),
               wwi_t.astype(BF16))

    def twice(a):
        return row(jnp.concatenate([a, a]))

    tabs = _rope_tables(tp, n_pad)
    inproj_consts = (row(ln_emb_g), row(ln_emb_b), weights, conv_w[0].astype(F32), row(conv_b[0]),
                     row(ln_conv_g[0]), row(ln_conv_b[0]), twice(ln_kidx_g[0]), twice(ln_kidx_b[0]))
    wr_t = w_router[0].T.astype(F32)
    wr_hi = wr_t.astype(BF16)
    wr_lo = (wr_t - wr_hi.astype(F32)).astype(BF16)
    route_consts = (w_out[0][:CONV_CH].astype(BF16), w_out[0][CONV_CH:].astype(BF16), row(ln1_g[0]),
                    row(ln1_b[0]), ws_gate[0].astype(BF16), ws_up[0].astype(BF16), ws_down[0].astype(BF16),
                    wr_hi, wr_lo, router_bias[0].reshape(-1, 1).astype(F32))
    meta = meta_tokens.astype(F32)
    n_blocks = tp * TOP_K // EXPERT_BLOCK + N_EXPERTS

    expert_w = (w_gate[0], w_up[0], w_down[0])
    weight_shares = []
    routed = []
    for b in range(B):
        hn, conv, qt, k, vt, qit, ki, wit = _inproj_call(x, b, meta, tabs, *inproj_consts)
        attn, *share = _dsa_call(qt, qit, wit, k, ki, vt, k_sel, n_pad, expert_w, b, B)
        weight_shares.append(share)
        xp, base, gates, ek, rk, cnt = _route_call(conv[0], attn[0], hn[0], *route_consts)

        slot, blk = _plan_call(cnt, ek, rk, n_blocks)
        slots = slot.reshape(TOP_K * tp // SC_ROWS, SC_ROWS)
        xs = _sc_scatter_rows(xp, slot, n_blocks * EXPERT_BLOCK)
        routed.append((xs, blk, slots, gates, base))

    result = None
    for b, (xs, blk, slots, gates, base) in enumerate(routed):
        ys = _expert_call(blk[0, :n_blocks + 1], xs, weight_shares)
        picked = _sc_gather_rows(ys, slots).reshape(TOP_K, tp, D // 2)
        result = _combine_call(picked, gates, base, row(ln2_g[0]), row(ln2_b[0]), result, b, B)
    return result
```

```python
import functools

import numpy as np
import jax
import jax.numpy as jnp
from jax import lax
from jax.experimental import pallas as pl
from jax.experimental.pallas import tpu as pltpu
from jax.experimental.pallas import tpu_sc as plsc

N_META = 16
CONV_CH = 512
CONV_WIDTH = 31
N_HEADS = 8
HEAD_DIM = 64
ATTN_W = N_HEADS * HEAD_DIM
IDX_HEADS = 8
IDX_DIM = 64
INDEX_TOPK = 256
ROPE_DIM = HEAD_DIM // 4
ROPE_HALF = ROPE_DIM // 2
ROPE_THETA = 500000.0
N_EXPERTS = 64
TOP_K = 8
N_GROUPS = 8
GROUP_SIZE = N_EXPERTS // N_GROUPS
TOPK_GROUPS = 4
ROUTED_SCALE = 2.5
LN_EPS = 1e-5
DEPTH = 1
DN_ALPHA = (2.0 * DEPTH) ** 0.25

LANES = 128
Q_TILE = 256
SUBLANES = 8
BF16_SUBLANES = 16
EXPERT_BLOCK = 512
SC_ROWS = 128
SC_SCATTER_ROWS = 64
K_CHUNK = Q_TILE
K_SUB = 128
COUNT_UNROLL = 4
SEQ_ALIGN = 256
CONV_HALO = 32
CONV_ROWS = 128
VMEM_LIMIT = 56 * 1024 * 1024

F32 = jnp.float32
BF16 = jnp.bfloat16
COARSE = jnp.bfloat16
NEG_INF = float("-inf")
INT_MIN = -2 ** 31
KEY_NEG_INF = -2139095041
LOG2_E = 1.4426950408889634


def _dot(a, b):
    return jnp.dot(a, b, preferred_element_type=F32)


def _dot_nt(a, b):
    return lax.dot_general(a, b, (((1,), (1,)), ((), ())), preferred_element_type=F32)


def _layer_norm_rows(x, g, b):
    mu = jnp.mean(x, axis=-1, keepdims=True)
    xc = x - mu
    var = jnp.mean(xc * xc, axis=-1, keepdims=True)
    return xc * lax.rsqrt(var + LN_EPS) * g + b


def _pick_tile(n, candidates):
    for c in candidates:
        if n % c == 0:
            return c
    raise ValueError(f"no tile for {n}")


def _rope_rows(x, c_tab, s1_tab, s2_tab):
    outs = []
    for j in range(x.shape[1] // LANES):
        xs = x[:, j * LANES:(j + 1) * LANES]
        up = pltpu.roll(xs, LANES - ROPE_HALF, axis=1)
        dn = pltpu.roll(xs, ROPE_HALF, axis=1)
        outs.append(xs * c_tab + up * s1_tab + dn * s2_tab)
    return jnp.concatenate(outs, axis=1)


def _rope_cols(xt, cos_t, sin_t, heads):
    r = xt.shape[1]
    x3 = xt.reshape(heads, HEAD_DIM, r)
    x1 = x3[:, 0:ROPE_HALF, :]
    x2 = x3[:, ROPE_HALF:ROPE_DIM, :]
    n1 = x1 * cos_t - x2 * sin_t
    n2 = x2 * cos_t + x1 * sin_t
    out = jnp.concatenate([n1, n2, x3[:, ROPE_DIM:, :]], axis=1)
    return out.reshape(heads * HEAD_DIM, r)


def _inproj_kernel(x_ref, meta_ref, ctab_ref, s1tab_ref, s2tab_ref, cost_ref, sint_ref,
                   lng_ref, lnb_ref, wag_ref, wqt_ref, wk_ref, wvt_ref, wqit_ref, wki_ref, wwit_ref,
                   cw_ref, cb_ref, lncg_ref, lncb_ref, lnkg_ref, lnkb_ref,
                   hn_ref, conv_ref, qt_ref, k_ref, vt_ref, qit_ref, ki_ref, wit_ref,
                   ubuf_ref, wbuf_ref, cbuf_ref):
    t = pl.program_id(1)
    tr = x_ref.shape[1]
    n_meta = meta_ref.shape[0]

    @pl.when(t == 0)
    def _():
        ubuf_ref[0:CONV_HALO, :] = jnp.zeros((CONV_HALO, CONV_CH), F32)

    first = jnp.concatenate([jnp.zeros((tr - n_meta, x_ref.shape[2]), F32), meta_ref[...]], axis=0)
    h = jnp.where(t == 0, first, x_ref[0])

    hn = _layer_norm_rows(h, lng_ref[...], lnb_ref[...])
    hn_ref[0] = hn
    xb = hn.astype(BF16)

    ag = _dot(xb, wag_ref[...])
    u = ag[:, :CONV_CH] * jax.nn.sigmoid(ag[:, CONV_CH:])
    row = lax.broadcasted_iota(jnp.int32, (tr, CONV_CH), 0)
    u = jnp.where(jnp.logical_or(t > 0, row >= tr - n_meta), u, 0.0)
    ubuf_ref[CONV_HALO:CONV_HALO + tr, :] = u
    base = CONV_HALO - (CONV_WIDTH - 1)
    for r in range(SUBLANES):
        rows = max(o for o in range(base, base + CONV_WIDTH) if o % SUBLANES == r) - r + tr
        wbuf_ref[r, 0:rows, :] = ubuf_ref[r:r + rows, :]
    ubuf_ref[0:CONV_HALO, :] = ubuf_ref[tr:tr + CONV_HALO, :]
    for lo in range(0, CONV_CH, LANES):
        for r0 in range(0, tr, CONV_ROWS):
            acc = jnp.zeros((CONV_ROWS, LANES), F32)
            for o in range(base, base + CONV_WIDTH):
                r = o % SUBLANES
                acc = acc + (cw_ref[o - base:o - base + 1, lo:lo + LANES]
                             * wbuf_ref[r, o - r + r0:o - r + r0 + CONV_ROWS, lo:lo + LANES])
            cbuf_ref[r0:r0 + CONV_ROWS, lo:lo + LANES] = acc
    c = _layer_norm_rows(cbuf_ref[...] + cb_ref[...], lncg_ref[...], lncb_ref[...])
    conv_ref[0] = (c * jax.nn.sigmoid(c)).astype(conv_ref.dtype)

    ctab, s1tab, s2tab = ctab_ref[...], s1tab_ref[...], s2tab_ref[...]
    cos_t, sin_t = cost_ref[...], sint_ref[...]

    qt = _rope_cols(_dot_nt(wqt_ref[...], xb), cos_t, sin_t, N_HEADS)
    qt_ref[0] = (qt * (HEAD_DIM ** -0.5 * LOG2_E)).astype(qt_ref.dtype)
    k = _rope_rows(_dot(xb, wk_ref[...]), ctab, s1tab, s2tab)
    k_ref[0] = k.astype(k_ref.dtype)
    vt_ref[0] = _dot_nt(wvt_ref[...], xb).astype(vt_ref.dtype)

    qit = _rope_cols(_dot_nt(wqit_ref[...], xb), cos_t, sin_t, IDX_HEADS)
    qit_ref[0] = qit.astype(qit_ref.dtype)
    ki = _layer_norm_rows(_dot(xb, wki_ref[...]), lnkg_ref[...], lnkb_ref[...])
    ki_ref[0] = _rope_rows(ki, ctab, s1tab, s2tab).astype(ki_ref.dtype)
    wit = _dot_nt(wwit_ref[...], xb) * (IDX_HEADS ** -0.5)
    wit_ref[0] = wit[:IDX_HEADS]


def _inproj_call(x, batch, meta, tabs, ln_g, ln_b, weights, conv_w, conv_b, lnc_g, lnc_b, lnk_g, lnk_b):
    _, seq, D = x.shape
    B = 1
    tr = SEQ_ALIGN
    tp = seq + tr
    nt = tp // tr
    ctab, s1tab, s2tab, cos_t, sin_t = tabs

    def row_spec(w):
        return pl.BlockSpec((1, tr, w), lambda b, t: (b, t, 0))

    def col_spec(r):
        return pl.BlockSpec((1, r, tr), lambda b, t: (b, 0, t))

    def full(a):
        return pl.BlockSpec(a.shape, lambda b, t: (0,) * a.ndim)

    tab_row = pl.BlockSpec((tr, LANES), lambda b, t: (t, 0))
    tab_col = pl.BlockSpec((ROPE_HALF, tr), lambda b, t: (0, t))
    consts = [ln_g, ln_b, *weights, conv_w, conv_b, lnc_g, lnc_b, lnk_g, lnk_b]
    out_shape = [
        jax.ShapeDtypeStruct((B, tp, D), F32),
        jax.ShapeDtypeStruct((B, tp, CONV_CH), BF16),
        jax.ShapeDtypeStruct((B, ATTN_W, tp), BF16),
        jax.ShapeDtypeStruct((B, tp, ATTN_W), BF16),
        jax.ShapeDtypeStruct((B, ATTN_W, tp), BF16),
        jax.ShapeDtypeStruct((B, IDX_HEADS * IDX_DIM, tp), BF16),
        jax.ShapeDtypeStruct((B, tp, 2 * IDX_DIM), BF16),
        jax.ShapeDtypeStruct((B, IDX_HEADS, tp), F32),
    ]
    out_specs = [row_spec(D), row_spec(CONV_CH), col_spec(ATTN_W), row_spec(ATTN_W), col_spec(ATTN_W),
                 col_spec(IDX_HEADS * IDX_DIM), row_spec(2 * IDX_DIM), col_spec(IDX_HEADS)]
    return pl.pallas_call(
        _inproj_kernel,
        grid=(B, nt),
        in_specs=[pl.BlockSpec((1, tr, D), lambda b, t: (batch, jnp.maximum(t - 1, 0), 0)), full(meta),
                  tab_row, tab_row, tab_row, tab_col, tab_col] + [full(a) for a in consts],
        out_specs=out_specs,
        out_shape=out_shape,
        scratch_shapes=[pltpu.VMEM((CONV_HALO + tr, CONV_CH), F32),
                        pltpu.VMEM((SUBLANES, CONV_HALO + tr, CONV_CH), F32),
                        pltpu.VMEM((tr, CONV_CH), F32)],
        compiler_params=pltpu.CompilerParams(
            dimension_semantics=("arbitrary", "arbitrary"), vmem_limit_bytes=VMEM_LIMIT),
        name="inproj",
    )(x, meta, ctab, s1tab, s2tab, cos_t, sin_t, *consts)


def _key_to_float(key):
    bits = jnp.where(key >= 0, key, key ^ jnp.int32(0x7FFFFFFF))
    f = pltpu.bitcast(bits, F32)
    return jnp.where(key < jnp.int32(KEY_NEG_INF), NEG_INF, f)


def _tree(parts, op):
    parts = list(parts)
    while len(parts) > 1:
        nxt = [op(parts[j], parts[j + 1]) for j in range(0, len(parts) - 1, 2)]
        if len(parts) % 2:
            nxt.append(parts[-1])
        parts = nxt
    return parts[0]


def _fold_rows(x, rows, op):
    return _tree([x[j * rows:(j + 1) * rows] for j in range(x.shape[0] // rows)], op)


def _dsa_kernel(qt_ref, qit_ref, wit_ref, k_ref, ki_ref, vt_ref, *rest, k_sel, n_pad, n_convert):
    f32_w, rest = rest[:n_convert], rest[n_convert:]
    o_ref, bf16_w, rest = rest[0], rest[1:1 + n_convert], rest[1 + n_convert:]
    sc_ref, sh_ref, qm_ref, qim_ref, m_ref, alpha_ref, shift_ref, lg_ref, acc_ref = rest

    for src_ref, dst_ref in zip(f32_w, bf16_w):
        dst_ref[...] = src_ref[...].astype(BF16)

    i = pl.program_id(1)
    tq = qt_ref.shape[2]
    n_chunks = (i * tq + tq + K_CHUNK - 1) // K_CHUNK
    v_rows = HEAD_DIM + BF16_SUBLANES

    def causal_mask(k0, rows=K_CHUNK):
        kpos = k0 + lax.broadcasted_iota(jnp.int32, (rows, tq), 0)
        return kpos <= i * tq + lax.broadcasted_iota(jnp.int32, (rows, tq), 1)

    def rows8(x):
        return jnp.broadcast_to(x, (SUBLANES, tq))

    def tiles(x):
        return x.reshape(x.shape[0] // SUBLANES, SUBLANES, tq)

    def head_slab(ref, h):
        slab = ref[0, (h // 2) * LANES:(h // 2 + 1) * LANES, :]
        zeros = jnp.zeros((HEAD_DIM, tq), slab.dtype)
        if h % 2 == 0:
            return jnp.concatenate([slab[:HEAD_DIM], zeros], axis=0)
        return jnp.concatenate([zeros, slab[HEAD_DIM:]], axis=0)

    for h in range(N_HEADS):
        qm_ref[h] = head_slab(qt_ref, h)
    for h in range(IDX_HEADS):
        qim_ref[h] = head_slab(qit_ref, h)
    wit = wit_ref[0]
    w_heads = [rows8(wit[h:h + 1] * (IDX_DIM ** -0.5)) for h in range(IDX_HEADS)]

    def for_each_chunk(chunk_fn, unroll, count=n_chunks):
        def group(j, carry):
            for u in range(unroll):
                chunk_fn(unroll * j + u, u % 2)
            return carry

        def single(c, carry):
            chunk_fn(c, 0)
            return carry

        n_groups = count // unroll
        lax.fori_loop(0, n_groups, group, 0)
        lax.fori_loop(n_groups * unroll, count, single, 0)

    def diagonal_visible(first_row, rows):
        r = first_row + lax.broadcasted_iota(jnp.int32, (rows, tq), 0)
        return r <= lax.broadcasted_iota(jnp.int32, (rows, tq), 1)

    def score_chunk(c, slot, diagonal=False):
        del slot
        for s in range(K_CHUNK // K_SUB):
            k0 = pl.multiple_of(c * K_CHUNK + s * K_SUB, K_SUB)
            kic = ki_ref[0, pl.ds(k0, K_SUB), :]
            acc = jnp.zeros((K_SUB // SUBLANES, SUBLANES, tq), F32)
            for h in range(IDX_HEADS):
                acc = acc + w_heads[h][None] * jnp.maximum(tiles(_dot(kic, qim_ref[h])), 0.0)
            acc = acc.reshape(K_SUB, tq)
            if diagonal:
                acc = jnp.where(diagonal_visible(s * K_SUB, K_SUB), acc, NEG_INF)
            sc_ref[pl.ds(k0, K_SUB), :] = acc
            hi_bits = pltpu.bitcast(acc, jnp.int32) & jnp.int32(-65536)
            sh_ref[pl.ds(k0, K_SUB), :] = pltpu.bitcast(hi_bits, F32).astype(COARSE)

    for_each_chunk(score_chunk, 4, n_chunks - 1)
    score_chunk(n_chunks - 1, 0, diagonal=True)

    def drop_padding_keys():
        sh_ref[0:n_pad, :] = jnp.full((n_pad, tq), NEG_INF, COARSE)

    sc_ref[0:n_pad, :] = jnp.full((n_pad, tq), NEG_INF, F32)
    drop_padding_keys()

    def count_all(ref, thr_tile, preds):
        rows = thr_tile.shape[0]
        one, zero = jnp.ones((), ref.dtype), jnp.zeros((), ref.dtype)

        def body(c, cnts):
            k0 = pl.multiple_of(c * K_CHUNK, K_CHUNK)
            s = ref[pl.ds(k0, K_CHUNK), :]
            out = []
            for cnt, p in zip(cnts, preds):
                hits = [jnp.where(p(s[j * rows:(j + 1) * rows], thr_tile), one, zero)
                        for j in range(K_CHUNK // rows)]
                out.append(cnt + _tree(hits, jnp.add).astype(F32))
            return tuple(out)

        def group(j, cnts):
            for u in range(COUNT_UNROLL):
                cnts = body(COUNT_UNROLL * j + u, cnts)
            return cnts

        init = tuple(jnp.zeros((rows, tq), F32) for _ in preds)
        n_groups = n_chunks // COUNT_UNROLL
        cnts = lax.fori_loop(0, n_groups, group, init)
        cnts = lax.fori_loop(n_groups * COUNT_UNROLL, n_chunks, body, cnts)
        return [rows8(jnp.sum(cnt, axis=0, keepdims=True)) for cnt in cnts]

    def search_body(it, tkey, coarse):
        cand = tkey + lax.shift_left(jnp.int32(1), 31 - it)
        cf = _key_to_float(cand)
        if coarse:
            cf = pltpu.bitcast(pltpu.bitcast(cf, jnp.int32) & jnp.int32(-65536), F32).astype(COARSE)
            cf = jnp.concatenate([cf] * (BF16_SUBLANES // SUBLANES), axis=0)
        cnt, = count_all(sh_ref if coarse else sc_ref, cf, [lambda s, t: s >= t])
        return jnp.where(cnt >= k_sel, cand, tkey)

    tkey = jnp.full((SUBLANES, tq), INT_MIN, jnp.int32)
    tkey = lax.fori_loop(0, 16, functools.partial(search_body, coarse=True), tkey)
    tkey = lax.fori_loop(16, 32, functools.partial(search_body, coarse=False), tkey)
    thr = _key_to_float(tkey)

    n_ge, n_gt = count_all(sc_ref, thr, [lambda s, t: s >= t, lambda s, t: s > t])
    need = k_sel - n_gt
    has_ties = jnp.max(n_ge) > k_sel

    @pl.when(jnp.logical_not(has_ties))
    def _():
        def mask_chunk(c, slot, diagonal=False):
            del slot
            k0 = pl.multiple_of(c * K_CHUNK, K_CHUNK)
            s = sc_ref[pl.ds(k0, K_CHUNK), :]
            bias = jnp.where(tiles(s) >= thr[None], 0.0, NEG_INF).reshape(K_CHUNK, tq)
            if diagonal:
                bias = jnp.where(diagonal_visible(0, K_CHUNK), bias, NEG_INF)
            sh_ref[pl.ds(k0, K_CHUNK), :] = bias.astype(COARSE)

        for_each_chunk(mask_chunk, 4, n_chunks - 1)
        mask_chunk(n_chunks - 1, 0, diagonal=True)

    @pl.when(has_ties)
    def _():
        r_i = lax.broadcasted_iota(jnp.int32, (K_CHUNK, K_CHUNK), 0)
        c_i = lax.broadcasted_iota(jnp.int32, (K_CHUNK, K_CHUNK), 1)
        lower = jnp.where(c_i <= r_i, 1.0, 0.0).astype(BF16)
        thr_row, need_row = thr[0:1], need[0:1]

        def body(c, seen):
            k0 = pl.multiple_of(c * K_CHUNK, K_CHUNK)
            s = sc_ref[pl.ds(k0, K_CHUNK), :]
            eq = jnp.where(s == thr_row, 1.0, 0.0)
            rank = _dot(lower, eq.astype(BF16)) + seen
            keep_tie = jnp.where(rank <= need_row, eq, 0.0)
            sel = jnp.where(s > thr_row, 1.0, keep_tie)
            sh_ref[pl.ds(k0, K_CHUNK), :] = jnp.where(
                sel > 0.0, jnp.where(causal_mask(k0), 0.0, NEG_INF), NEG_INF).astype(COARSE)
            return rank[K_CHUNK - 1:K_CHUNK, :]
        lax.fori_loop(0, n_chunks, body, jnp.zeros((1, tq), F32))

    drop_padding_keys()

    m_ref[...] = jnp.full(m_ref.shape, NEG_INF, F32)
    acc_ref[...] = jnp.zeros(acc_ref.shape, F32)
    ones_rows = jnp.ones((BF16_SUBLANES, K_CHUNK), BF16)

    def logits_stage(c, slot):
        k0 = pl.multiple_of(c * K_CHUNK, K_CHUNK)
        for h in range(N_HEADS):
            pair = slice((h // 2) * LANES, (h // 2 + 1) * LANES)
            cmax = []
            for s in range(K_CHUNK // K_SUB):
                rows = pl.ds(pl.multiple_of(k0 + s * K_SUB, K_SUB), K_SUB)
                lg = _dot(k_ref[0, rows, pair], qm_ref[h]).astype(BF16) + sh_ref[rows, :].astype(BF16)
                lg_ref[slot, h, s * K_SUB:(s + 1) * K_SUB, :] = lg
                cmax.append(_fold_rows(lg, BF16_SUBLANES, jnp.maximum))
            cmax = _tree(cmax, jnp.maximum).astype(F32)
            m_old = m_ref[h]
            m_new = jnp.maximum(m_old, rows8(jnp.max(cmax, axis=0, keepdims=True)))
            m_safe = jnp.where(m_new == NEG_INF, 0.0, m_new)
            alpha_ref[slot, h] = jnp.exp2(m_old - m_safe)
            shift_ref[slot, h] = m_safe
            m_ref[h] = m_new

    def values_stage(c, slot):
        k0 = pl.multiple_of(c * K_CHUNK, K_CHUNK)
        for h in range(N_HEADS):
            m_safe = shift_ref[slot, h].astype(BF16)
            m_tile = jnp.concatenate([m_safe] * (BF16_SUBLANES // SUBLANES), axis=0)
            lg = lg_ref[slot, h].reshape(K_CHUNK // BF16_SUBLANES, BF16_SUBLANES, tq)
            p = jnp.exp2(lg - m_tile[None]).reshape(K_CHUNK, tq)
            v_aug = jnp.concatenate(
                [vt_ref[0, h * HEAD_DIM:(h + 1) * HEAD_DIM, pl.ds(k0, K_CHUNK)], ones_rows], axis=0)
            pv = _dot(v_aug, p)
            acc = acc_ref[h].reshape(v_rows // SUBLANES, SUBLANES, tq) * alpha_ref[slot, h][None]
            acc_ref[h] = acc.reshape(v_rows, tq) + pv

    def attn_chunk(c, slot):
        logits_stage(c, slot)
        values_stage(c, slot)

    for_each_chunk(attn_chunk, 4)

    for pair in range(N_HEADS // 2):
        halves = []
        for h in (2 * pair, 2 * pair + 1):
            a = acc_ref[h]
            total = a[HEAD_DIM:HEAD_DIM + 1]
            halves.append(a[:HEAD_DIM] / jnp.where(total == 0.0, 1.0, total))
        out_t = jnp.concatenate(halves, axis=0)
        o_ref[0, :, pair * LANES:(pair + 1) * LANES] = out_t.T.astype(o_ref.dtype)


def _dsa_call(qt, qit, wit, k, ki, vt, k_sel, n_pad, expert_w, part, n_parts):
    B, tp, _ = k.shape
    nq = tp // Q_TILE
    n_conv = expert_w[0].shape[0] // n_parts
    assert n_conv * n_parts == expert_w[0].shape[0]
    per_step = next(d for d in range(1, n_conv + 1) if n_conv % d == 0 and n_conv // d <= nq)
    n_steps = n_conv // per_step

    def q_cols(r):
        return pl.BlockSpec((1, r, Q_TILE), lambda b, i: (b, 0, i))

    def per_batch(a):
        return pl.BlockSpec((1,) + a.shape[1:], lambda b, i: (b, 0, 0), pipeline_mode=pl.Buffered(1))

    def expert_spec(a, first_block):
        return pl.BlockSpec((per_step,) + a.shape[1:],
                            lambda b, i: (first_block + jnp.minimum(i, n_steps - 1), 0, 0))

    in_specs = [q_cols(ATTN_W), q_cols(IDX_HEADS * IDX_DIM), q_cols(IDX_HEADS),
                per_batch(k), per_batch(ki), per_batch(vt)]
    in_specs += [expert_spec(a, part * n_steps) for a in expert_w]

    return pl.pallas_call(
        functools.partial(_dsa_kernel, k_sel=k_sel, n_pad=n_pad, n_convert=len(expert_w)),
        grid=(B, nq),
        in_specs=in_specs,
        out_specs=[pl.BlockSpec((1, Q_TILE, ATTN_W), lambda b, i: (b, i, 0))]
        + [expert_spec(a, 0) for a in expert_w],
        out_shape=[jax.ShapeDtypeStruct((B, tp, ATTN_W), BF16)]
        + [jax.ShapeDtypeStruct((n_conv,) + a.shape[1:], BF16) for a in expert_w],
        scratch_shapes=[pltpu.VMEM((tp, Q_TILE), F32),
                        pltpu.VMEM((tp, Q_TILE), COARSE),
                        pltpu.VMEM((N_HEADS, LANES, Q_TILE), BF16),
                        pltpu.VMEM((IDX_HEADS, LANES, Q_TILE), BF16),
                        pltpu.VMEM((N_HEADS, SUBLANES, Q_TILE), F32),
                        pltpu.VMEM((2, N_HEADS, SUBLANES, Q_TILE), F32),
                        pltpu.VMEM((2, N_HEADS, SUBLANES, Q_TILE), F32),
                        pltpu.VMEM((2, N_HEADS, K_CHUNK, Q_TILE), BF16),
                        pltpu.VMEM((N_HEADS, HEAD_DIM + BF16_SUBLANES, Q_TILE), F32)],
        compiler_params=pltpu.CompilerParams(
            dimension_semantics=("arbitrary", "arbitrary"), vmem_limit_bytes=VMEM_LIMIT),
        name="dsa",
    )(qt, qit, wit, k, ki, vt, *expert_w)


def _max_all(x):
    return jnp.max(jnp.max(x, axis=0, keepdims=True), axis=1, keepdims=True)


def _sum_all(x):
    return jnp.sum(jnp.sum(x, axis=0, keepdims=True), axis=1, keepdims=True)


def _pack_bf16_pairs(x):
    w = x.shape[1] // 2
    lo = pltpu.bitcast(x[:, :w].astype(BF16).astype(F32), jnp.int32)
    hi = pltpu.bitcast(x[:, w:].astype(BF16).astype(F32), jnp.int32)
    return lax.shift_right_logical(lo, 16) | (hi & jnp.int32(-65536))


def _unpack_bf16_pairs(p):
    lo = pltpu.bitcast(lax.shift_left(p, 16), F32)
    hi = pltpu.bitcast(p & jnp.int32(-65536), F32)
    return jnp.concatenate([lo, hi], axis=1)


def _router_topk(logits_t, rbias):
    r = logits_t.shape[1]
    shape3 = (N_GROUPS, GROUP_SIZE, r)
    scores = jax.nn.sigmoid(logits_t).reshape(shape3)
    biased = scores + rbias.reshape(N_GROUPS, GROUP_SIZE, 1)
    in_grp = lax.broadcasted_iota(jnp.int32, shape3, 1).astype(F32)
    m1 = jnp.max(biased, axis=1, keepdims=True)
    first = jnp.min(jnp.where(biased == m1, in_grp, float(GROUP_SIZE)), axis=1, keepdims=True)
    m2 = jnp.max(jnp.where(in_grp == first, NEG_INF, biased), axis=1, keepdims=True)
    cur = m1 + m2

    grp_idx = lax.broadcasted_iota(jnp.int32, (N_GROUPS, 1, r), 0).astype(F32)
    grp_sel = jnp.zeros((N_GROUPS, 1, r), F32)
    for _ in range(TOPK_GROUPS):
        m = jnp.max(cur, axis=0, keepdims=True)
        pick = grp_idx == jnp.min(jnp.where(cur == m, grp_idx, float(N_GROUPS)), axis=0, keepdims=True)
        grp_sel = jnp.where(pick, 1.0, grp_sel)
        cur = jnp.where(pick, NEG_INF, cur)

    cur = jnp.where(jnp.broadcast_to(grp_sel, shape3) > 0.0, biased, NEG_INF)
    exp_idx = lax.broadcasted_iota(jnp.int32, shape3, 0).astype(F32) * GROUP_SIZE + in_grp
    chosen = jnp.zeros(shape3, F32)
    ids, wts = [], []
    for _ in range(TOP_K):
        m = _max_all(cur)
        first = -_max_all(-jnp.where(cur == m, exp_idx, float(N_EXPERTS)))
        pick = exp_idx == first
        chosen = jnp.where(pick, 1.0, chosen)
        cur = jnp.where(pick, NEG_INF, cur)
        ids.append(first.reshape(1, r))
        wts.append(_sum_all(jnp.where(pick, scores, 0.0)).reshape(1, r))
    ids = jnp.concatenate(ids, axis=0)
    wts = jnp.concatenate(wts, axis=0)
    gates = wts / jnp.sum(wts, axis=0, keepdims=True) * ROUTED_SCALE
    return ids, gates, chosen.reshape(N_EXPERTS, r), exp_idx


def _route_kernel(conv_ref, attn_ref, hn_ref, woc_ref, woa_ref, g1_ref, b1_ref,
                  wsg_ref, wsu_ref, wsd_ref, wrh_ref, wrl_ref, rb_ref,
                  xp_ref, base_ref, gates_ref, ek_ref, rk_ref, cnt_ref, before_ref):
    step = pl.program_id(0)
    tr = hn_ref.shape[0]
    mix = _dot(conv_ref[...], woc_ref[...]) + _dot(attn_ref[...], woa_ref[...])
    h1 = _layer_norm_rows(DN_ALPHA * hn_ref[...] + mix, g1_ref[...], b1_ref[...])
    xb = h1.astype(BF16)
    xp_ref[...] = _pack_bf16_pairs(h1)

    shared = jax.nn.silu(_dot(xb, wsg_ref[...])) * _dot(xb, wsu_ref[...])
    base_ref[...] = DN_ALPHA * h1 + _dot(shared.astype(BF16), wsd_ref[...])

    x_lo = (h1 - xb.astype(F32)).astype(BF16)
    logits_t = (_dot_nt(wrh_ref[...], xb) + _dot_nt(wrh_ref[...], x_lo) + _dot_nt(wrl_ref[...], xb))
    ids, gates, chosen, exp_idx = _router_topk(logits_t, rb_ref[...])
    padded = jnp.concatenate([gates, jnp.zeros((LANES - TOP_K, tr), F32)], axis=0)
    gates_ref[...] = padded.T
    ek_ref[...] = ids.astype(jnp.int32)

    @pl.when(step == 0)
    def _():
        cnt_ref[...] = jnp.zeros(cnt_ref.shape, F32)
        t_i = lax.broadcasted_iota(jnp.int32, (tr, tr), 0)
        t_j = lax.broadcasted_iota(jnp.int32, (tr, tr), 1)
        before_ref[...] = jnp.where(t_i < t_j, 1.0, 0.0).astype(BF16)

    chosen_b = chosen.astype(BF16)
    running = cnt_ref[...]
    rank = _dot(chosen_b, before_ref[...]) + jnp.concatenate([running] * (tr // LANES), axis=1)
    rank3 = rank.reshape(N_GROUPS, GROUP_SIZE, tr)
    rk = [_sum_all(jnp.where(exp_idx == ids[k:k + 1].reshape(1, 1, tr), rank3, 0.0)).reshape(1, tr)
          for k in range(TOP_K)]
    rk_ref[...] = jnp.concatenate(rk, axis=0).astype(jnp.int32)
    cnt_ref[...] = running + _dot(chosen_b, jnp.ones((tr, LANES), BF16))


def _route_call(conv, attn, hn, woc, woa, g1, b1, wsg, wsu, wsd, wrh, wrl, rbias):
    n, D = hn.shape
    tr = _pick_tile(n, (768, 512, 256))

    def row_spec(w):
        return pl.BlockSpec((tr, w), lambda i: (i, 0))

    def col_spec(r):
        return pl.BlockSpec((r, tr), lambda i: (0, i))

    def full(a):
        return pl.BlockSpec(a.shape, lambda i: (0,) * a.ndim)

    consts = [woc, woa, g1, b1, wsg, wsu, wsd, wrh, wrl, rbias]
    return pl.pallas_call(
        _route_kernel,
        grid=(n // tr,),
        in_specs=[row_spec(CONV_CH), row_spec(ATTN_W), row_spec(D)] + [full(a) for a in consts],
        out_specs=[row_spec(D // 2), row_spec(D), row_spec(LANES), col_spec(TOP_K), col_spec(TOP_K),
                   pl.BlockSpec((N_EXPERTS, LANES), lambda i: (0, 0))],
        out_shape=[jax.ShapeDtypeStruct((n, D // 2), jnp.int32),
                   jax.ShapeDtypeStruct((n, D), F32),
                   jax.ShapeDtypeStruct((n, LANES), F32),
                   jax.ShapeDtypeStruct((TOP_K, n), jnp.int32),
                   jax.ShapeDtypeStruct((TOP_K, n), jnp.int32),
                   jax.ShapeDtypeStruct((N_EXPERTS, LANES), F32)],
        scratch_shapes=[pltpu.VMEM((tr, tr), BF16)],
        compiler_params=pltpu.CompilerParams(
            dimension_semantics=("arbitrary",), vmem_limit_bytes=VMEM_LIMIT),
        name="route",
    )(conv, attn, hn, *consts)


def _plan_kernel(cnt_ref, ek_ref, rk_ref, slot_ref, blk_ref, *, n_blocks):
    tr = ek_ref.shape[1]
    counts = cnt_ref[...]
    padded = jnp.ceil(counts / EXPERT_BLOCK) * EXPERT_BLOCK
    starts = [jnp.zeros((1, LANES), F32)]
    for e in range(1, N_EXPERTS):
        starts.append(starts[-1] + padded[e - 1:e])
    ek = ek_ref[...]
    seg = jnp.zeros(ek.shape, F32)
    for e in range(N_EXPERTS):
        seg = jnp.where(ek == e, jnp.concatenate([starts[e]] * (tr // LANES), axis=1), seg)
    slot_ref[...] = seg.astype(jnp.int32) + rk_ref[...]

    ends = jnp.concatenate(starts, axis=0) + padded
    w = blk_ref.shape[1]
    blk_start = lax.broadcasted_iota(jnp.int32, (N_EXPERTS, w), 1).astype(F32) * EXPERT_BLOCK
    done = jnp.where(jnp.concatenate([ends] * (w // LANES), axis=1) <= blk_start, 1.0, 0.0)
    owner = jnp.minimum(jnp.sum(done, axis=0, keepdims=True), N_EXPERTS - 1.0)
    used = jnp.concatenate([ends[N_EXPERTS - 1:] * (1.0 / EXPERT_BLOCK)] * (w // LANES), axis=1)
    lane = lax.broadcasted_iota(jnp.int32, (1, w), 1)
    table = jnp.where(lane == n_blocks, used, owner)
    blk_ref[...] = jnp.broadcast_to(table, blk_ref.shape).astype(jnp.int32)


def _plan_call(cnt, ek, rk, n_blocks):
    k, n = ek.shape
    tr = _pick_tile(n, (768, 512, 256))
    w = -(-(n_blocks + 1) // LANES) * LANES
    col = pl.BlockSpec((k, tr), lambda i: (0, i))
    return pl.pallas_call(
        functools.partial(_plan_kernel, n_blocks=n_blocks),
        grid=(n // tr,),
        in_specs=[pl.BlockSpec(cnt.shape, lambda i: (0, 0)), col, col],
        out_specs=[col, pl.BlockSpec((SUBLANES, w), lambda i: (0, 0))],
        out_shape=[jax.ShapeDtypeStruct((k, n), jnp.int32), jax.ShapeDtypeStruct((SUBLANES, w), jnp.int32)],
        compiler_params=pltpu.CompilerParams(dimension_semantics=("arbitrary",)),
        name="plan",
    )(cnt, ek, rk)


def _sc_workers():
    info = plsc.get_sparse_core_info()
    return info.num_cores, info.num_subcores


def _sc_scatter_rows(src, slot, n_out):
    n_src, d = src.shape
    n_choices = slot.shape[0]
    rows = SC_SCATTER_ROWS
    n_chunks = n_src // rows
    slots = slot.reshape(n_choices * n_chunks, rows)
    n_cores, n_sub = _sc_workers()
    n_workers = n_cores * n_sub

    def body(src_hbm, slot_hbm, out_hbm, idx_v, rows_v):
        wid = lax.axis_index("s") * n_cores + lax.axis_index("c")

        @pl.loop(0, -(-n_chunks // n_workers))
        def _(j):
            u = j * n_workers + wid

            @pl.when(u < n_chunks)
            def _():
                pltpu.sync_copy(src_hbm.at[pl.ds(u * rows, rows)], rows_v)
                for k in range(n_choices):
                    pltpu.sync_copy(slot_hbm.at[pl.ds(k * n_chunks + u, 1)], idx_v)
                    pltpu.sync_copy(rows_v, out_hbm.at[idx_v.at[0]])

    return pl.kernel(
        body, out_type=jax.ShapeDtypeStruct((n_out, d), src.dtype),
        mesh=plsc.VectorSubcoreMesh(core_axis_name="c", subcore_axis_name="s"),
        scratch_types=[pltpu.VMEM((1, rows), jnp.int32), pltpu.VMEM((rows, d), src.dtype)],
        name="dispatch_rows",
    )(src, slots)


def _sc_gather_rows(table, slots):
    units = slots.shape[0]
    d = table.shape[1]
    n_cores, n_sub = _sc_workers()
    n_workers = n_cores * n_sub
    assert slots.shape[1] == SC_ROWS

    def body(table_hbm, slot_hbm, out_hbm, idx_v, rows_v):
        wid = lax.axis_index("s") * n_cores + lax.axis_index("c")

        @pl.loop(0, -(-units // n_workers))
        def _(j):
            u = j * n_workers + wid

            @pl.when(u < units)
            def _():
                pltpu.sync_copy(slot_hbm.at[pl.ds(u, 1)], idx_v)
                pltpu.sync_copy(table_hbm.at[idx_v.at[0]], rows_v)
                pltpu.sync_copy(rows_v, out_hbm.at[pl.ds(u * SC_ROWS, SC_ROWS)])

    return pl.kernel(
        body, out_type=jax.ShapeDtypeStruct((units * SC_ROWS, d), table.dtype),
        mesh=plsc.VectorSubcoreMesh(core_axis_name="c", subcore_axis_name="s"),
        scratch_types=[pltpu.VMEM((1, SC_ROWS), jnp.int32), pltpu.VMEM((SC_ROWS, d), table.dtype)],
        name="collect_rows",
    )(table, slots)


def _expert_kernel(blk_ref, xs_ref, *rest, per_share):
    ys_ref = rest[-1]
    b = pl.program_id(0)
    in_use = b < blk_ref[pl.num_programs(0)]
    share = blk_ref[b] // per_share
    for s in range(len(rest) // 3):
        wg_ref, wu_ref, wd_ref = rest[3 * s:3 * s + 3]

        @pl.when(jnp.logical_and(in_use, share == s))
        def _():
            x = _unpack_bf16_pairs(xs_ref[...]).astype(BF16)
            hdn = (jax.nn.silu(_dot(x, wg_ref[0])) * _dot(x, wu_ref[0])).astype(BF16)
            ys_ref[...] = _pack_bf16_pairs(_dot(hdn, wd_ref[0]))


def _expert_call(blk_exp, xs, weight_shares):
    p, half = xs.shape
    per_share = weight_shares[0][0].shape[0]
    n_blocks = p // EXPERT_BLOCK
    assert blk_exp.shape == (n_blocks + 1,)

    def row_block(b, blk):
        return (jnp.minimum(b, blk[n_blocks] - 1), 0)

    def weight_spec(a, s):
        return pl.BlockSpec((1,) + a.shape[1:],
                            lambda b, blk: (jnp.clip(blk[b] - s * per_share, 0, per_share - 1), 0, 0))

    rows = pl.BlockSpec((EXPERT_BLOCK, half), row_block)
    grid_spec = pltpu.PrefetchScalarGridSpec(
        num_scalar_prefetch=1,
        grid=(n_blocks,),
        in_specs=[rows] + [weight_spec(a, s) for s, share in enumerate(weight_shares) for a in share],
        out_specs=rows,
    )
    return pl.pallas_call(
        functools.partial(_expert_kernel, per_share=per_share),
        grid_spec=grid_spec,
        out_shape=jax.ShapeDtypeStruct((p, half), jnp.int32),
        compiler_params=pltpu.CompilerParams(
            dimension_semantics=("arbitrary",), vmem_limit_bytes=VMEM_LIMIT),
        name="experts",
    )(blk_exp, xs, *[a for share in weight_shares for a in share])


def _combine_kernel(g_ref, gates_ref, base_ref, g2_ref, b2_ref, *rest):
    o_ref = rest[-1]
    gates = gates_ref[...]
    acc = base_ref[...]
    for k in range(g_ref.shape[0]):
        acc = acc + _unpack_bf16_pairs(g_ref[k]) * gates[:, k:k + 1]
    o_ref[0] = _layer_norm_rows(acc, g2_ref[...], b2_ref[...])


def _combine_call(g, gates, base, g2, b2, result, batch, n_batch):
    k, n, half = g.shape
    D = base.shape[1]
    tr = SEQ_ALIGN
    seq = n - tr

    def row_spec(w):
        return pl.BlockSpec((tr, w), lambda i: (i, 0))

    vec = pl.BlockSpec((1, D), lambda i: (0, 0))
    in_specs = [pl.BlockSpec((k, tr, half), lambda i: (0, i, 0)), row_spec(LANES), row_spec(D), vec, vec]
    args = [g, gates, base, g2, b2]
    aliases = {}
    if result is not None:
        in_specs.append(pl.BlockSpec(memory_space=pl.ANY))
        args.append(result)
        aliases = {len(args) - 1: 0}
    return pl.pallas_call(
        _combine_kernel,
        grid=(n // tr,),
        in_specs=in_specs,
        out_specs=pl.BlockSpec((1, tr, D), lambda i: (batch, jnp.maximum(i - 1, 0), 0)),
        out_shape=jax.ShapeDtypeStruct((n_batch, seq, D), F32),
        input_output_aliases=aliases,
        compiler_params=pltpu.CompilerParams(
            dimension_semantics=("arbitrary",), vmem_limit_bytes=VMEM_LIMIT),
        name="combine",
    )(*args)


def _rope_tables(tp, lead):
    pos = jnp.arange(tp, dtype=F32) - lead
    inv = jnp.power(ROPE_THETA, -2.0 * jnp.arange(ROPE_HALF, dtype=F32) / ROPE_DIM)
    ang = pos[:, None] * inv[None, :]
    cos, sin = jnp.cos(ang), jnp.sin(ang)
    zeros = jnp.zeros((tp, HEAD_DIM - ROPE_DIM), F32)
    zh = jnp.zeros((tp, ROPE_HALF), F32)
    c64 = jnp.concatenate([cos, cos, jnp.ones_like(zeros)], axis=1)
    s1_64 = jnp.concatenate([-sin, zh, zeros], axis=1)
    s2_64 = jnp.concatenate([zh, sin, zeros], axis=1)
    rep = LANES // HEAD_DIM
    return (jnp.tile(c64, (1, rep)), jnp.tile(s1_64, (1, rep)), jnp.tile(s2_64, (1, rep)),
            cos.T, sin.T)


def kernel(x, meta_tokens, ln_emb_g, ln_emb_b, w_in, conv_w, conv_b, ln_conv_g, ln_conv_b, ln_kidx_g, ln_kidx_b, w_out, ln1_g, ln1_b, w_router, router_bias, w_gate, w_up, w_down, ws_gate, ws_up, ws_down, ln2_g, ln2_b):
    B, seq, D = x.shape
    assert w_in.shape[0] == DEPTH
    assert seq % SEQ_ALIGN == 0 and meta_tokens.shape[0] == N_META <= SEQ_ALIGN
    k_sel = min(INDEX_TOPK, seq // 4)
    tp = seq + SEQ_ALIGN
    n_pad = SEQ_ALIGN - N_META

    def row(a):
        return a.reshape(1, -1).astype(F32)

    w = w_in[0]
    o = 0
    parts = []
    for width in (CONV_CH, CONV_CH, ATTN_W, ATTN_W, ATTN_W, IDX_HEADS * IDX_DIM, IDX_DIM, IDX_HEADS):
        parts.append(w[:, o:o + width])
        o += width
    wa, wgl, wq, wk, wv, wqi, wki, wwi = parts
    wwi_t = jnp.concatenate([wwi.T, jnp.zeros((BF16_SUBLANES - IDX_HEADS, D), w.dtype)], axis=0)
    weights = (jnp.concatenate([wa, wgl], axis=1).astype(BF16), wq.T.astype(BF16), wk.astype(BF16),
               wv.T.astype(BF16), wqi.T.astype(BF16), jnp.concatenate([wki, wki], axis=1).astype(BF16),
               wwi_t.astype(BF16))

    def twice(a):
        return row(jnp.concatenate([a, a]))

    tabs = _rope_tables(tp, n_pad)
    inproj_consts = (row(ln_emb_g), row(ln_emb_b), weights, conv_w[0].astype(F32), row(conv_b[0]),
                     row(ln_conv_g[0]), row(ln_conv_b[0]), twice(ln_kidx_g[0]), twice(ln_kidx_b[0]))
    wr_t = w_router[0].T.astype(F32)
    wr_hi = wr_t.astype(BF16)
    wr_lo = (wr_t - wr_hi.astype(F32)).astype(BF16)
    route_consts = (w_out[0][:CONV_CH].astype(BF16), w_out[0][CONV_CH:].astype(BF16), row(ln1_g[0]),
                    row(ln1_b[0]), ws_gate[0].astype(BF16), ws_up[0].astype(BF16), ws_down[0].astype(BF16),
                    wr_hi, wr_lo, router_bias[0].reshape(-1, 1).astype(F32))
    meta = meta_tokens.astype(F32)
    n_blocks = tp * TOP_K // EXPERT_BLOCK + N_EXPERTS

    expert_w = (w_gate[0], w_up[0], w_down[0])
    weight_shares = []
    routed = []
    for b in range(B):
        hn, conv, qt, k, vt, qit, ki, wit = _inproj_call(x, b, meta, tabs, *inproj_consts)
        attn, *share = _dsa_call(qt, qit, wit, k, ki, vt, k_sel, n_pad, expert_w, b, B)
        weight_shares.append(share)
        xp, base, gates, ek, rk, cnt = _route_call(conv[0], attn[0], hn[0], *route_consts)

        slot, blk = _plan_call(cnt, ek, rk, n_blocks)
        slots = slot.reshape(TOP_K * tp // SC_ROWS, SC_ROWS)
        xs = _sc_scatter_rows(xp, slot, n_blocks * EXPERT_BLOCK)
        routed.append((xs, blk, slots, gates, base))

    result = None
    for b, (xs, blk, slots, gates, base) in enumerate(routed):
        ys = _expert_call(blk[0, :n_blocks + 1], xs, weight_shares)
        picked = _sc_gather_rows(ys, slots).reshape(TOP_K, tp, D // 2)
        result = _combine_call(picked, gates, base, row(ln2_g[0]), row(ln2_b[0]), result, b, B)
    return result
```

```python
import functools

import numpy as np
import jax
import jax.numpy as jnp
from jax import lax
from jax.experimental import pallas as pl
from jax.experimental.pallas import tpu as pltpu
from jax.experimental.pallas import tpu_sc as plsc

N_META = 16
CONV_CH = 512
CONV_WIDTH = 31
N_HEADS = 8
HEAD_DIM = 64
ATTN_W = N_HEADS * HEAD_DIM
IDX_HEADS = 8
IDX_DIM = 64
INDEX_TOPK = 256
ROPE_DIM = HEAD_DIM // 4
ROPE_HALF = ROPE_DIM // 2
ROPE_THETA = 500000.0
N_EXPERTS = 64
TOP_K = 8
N_GROUPS = 8
GROUP_SIZE = N_EXPERTS // N_GROUPS
TOPK_GROUPS = 4
ROUTED_SCALE = 2.5
LN_EPS = 1e-5
DEPTH = 1
DN_ALPHA = (2.0 * DEPTH) ** 0.25

LANES = 128
Q_TILE = 256
SUBLANES = 8
BF16_SUBLANES = 16
EXPERT_BLOCK = 384
_INV_EXPERT_BLOCK = float(np.nextafter(np.float32(1.0 / EXPERT_BLOCK), np.float32(1.0)))
SC_ROWS = 128
SC_SCATTER_ROWS = 64
K_CHUNK = Q_TILE
K_SUB = 128
COUNT_UNROLL = 4
SEQ_ALIGN = 256
CONV_HALO = 32
CONV_ROWS = 128
VMEM_LIMIT = 56 * 1024 * 1024

F32 = jnp.float32
BF16 = jnp.bfloat16
COARSE = jnp.bfloat16
NEG_INF = float("-inf")
INT_MIN = -2 ** 31
KEY_NEG_INF = -2139095041
LOG2_E = 1.4426950408889634


def _dot(a, b):
    return jnp.dot(a, b, preferred_element_type=F32)


def _dot_nt(a, b):
    return lax.dot_general(a, b, (((1,), (1,)), ((), ())), preferred_element_type=F32)


def _layer_norm_rows(x, g, b):
    mu = jnp.mean(x, axis=-1, keepdims=True)
    xc = x - mu
    var = jnp.mean(xc * xc, axis=-1, keepdims=True)
    return xc * lax.rsqrt(var + LN_EPS) * g + b


def _pick_tile(n, candidates):
    for c in candidates:
        if n % c == 0:
            return c
    raise ValueError(f"no tile for {n}")


def _rope_rows(x, c_tab, s1_tab, s2_tab):
    outs = []
    for j in range(x.shape[1] // LANES):
        xs = x[:, j * LANES:(j + 1) * LANES]
        up = pltpu.roll(xs, LANES - ROPE_HALF, axis=1)
        dn = pltpu.roll(xs, ROPE_HALF, axis=1)
        outs.append(xs * c_tab + up * s1_tab + dn * s2_tab)
    return jnp.concatenate(outs, axis=1)


def _rope_cols(xt, cos_t, sin_t, heads):
    r = xt.shape[1]
    x3 = xt.reshape(heads, HEAD_DIM, r)
    x1 = x3[:, 0:ROPE_HALF, :]
    x2 = x3[:, ROPE_HALF:ROPE_DIM, :]
    n1 = x1 * cos_t - x2 * sin_t
    n2 = x2 * cos_t + x1 * sin_t
    out = jnp.concatenate([n1, n2, x3[:, ROPE_DIM:, :]], axis=1)
    return out.reshape(heads * HEAD_DIM, r)


def _inproj_kernel(x_ref, meta_ref, ctab_ref, s1tab_ref, s2tab_ref, cost_ref, sint_ref,
                   lng_ref, lnb_ref, wag_ref, wqt_ref, wk_ref, wvt_ref, wqit_ref, wki_ref, wwit_ref,
                   cw_ref, cb_ref, lncg_ref, lncb_ref, lnkg_ref, lnkb_ref,
                   hn_ref, conv_ref, qt_ref, k_ref, vt_ref, qit_ref, ki_ref, wit_ref,
                   ubuf_ref, wbuf_ref, cbuf_ref):
    t = pl.program_id(1)
    tr = x_ref.shape[1]
    n_meta = meta_ref.shape[0]

    @pl.when(t == 0)
    def _():
        ubuf_ref[0:CONV_HALO, :] = jnp.zeros((CONV_HALO, CONV_CH), F32)

    first = jnp.concatenate([jnp.zeros((tr - n_meta, x_ref.shape[2]), F32), meta_ref[...]], axis=0)
    h = jnp.where(t == 0, first, x_ref[0])

    hn = _layer_norm_rows(h, lng_ref[...], lnb_ref[...])
    hn_ref[0] = hn
    xb = hn.astype(BF16)

    ag = _dot(xb, wag_ref[...])
    u = ag[:, :CONV_CH] * jax.nn.sigmoid(ag[:, CONV_CH:])
    row = lax.broadcasted_iota(jnp.int32, (tr, CONV_CH), 0)
    u = jnp.where(jnp.logical_or(t > 0, row >= tr - n_meta), u, 0.0)
    ubuf_ref[CONV_HALO:CONV_HALO + tr, :] = u
    base = CONV_HALO - (CONV_WIDTH - 1)
    for r in range(SUBLANES):
        rows = max(o for o in range(base, base + CONV_WIDTH) if o % SUBLANES == r) - r + tr
        wbuf_ref[r, 0:rows, :] = ubuf_ref[r:r + rows, :]
    ubuf_ref[0:CONV_HALO, :] = ubuf_ref[tr:tr + CONV_HALO, :]
    for lo in range(0, CONV_CH, LANES):
        for r0 in range(0, tr, CONV_ROWS):
            acc = jnp.zeros((CONV_ROWS, LANES), F32)
            for o in range(base, base + CONV_WIDTH):
                r = o % SUBLANES
                acc = acc + (cw_ref[o - base:o - base + 1, lo:lo + LANES]
                             * wbuf_ref[r, o - r + r0:o - r + r0 + CONV_ROWS, lo:lo + LANES])
            cbuf_ref[r0:r0 + CONV_ROWS, lo:lo + LANES] = acc
    c = _layer_norm_rows(cbuf_ref[...] + cb_ref[...], lncg_ref[...], lncb_ref[...])
    conv_ref[0] = (c * jax.nn.sigmoid(c)).astype(conv_ref.dtype)

    ctab, s1tab, s2tab = ctab_ref[...], s1tab_ref[...], s2tab_ref[...]
    cos_t, sin_t = cost_ref[...], sint_ref[...]

    qt = _rope_cols(_dot_nt(wqt_ref[...], xb), cos_t, sin_t, N_HEADS)
    qt_ref[0] = (qt * (HEAD_DIM ** -0.5 * LOG2_E)).astype(qt_ref.dtype)
    k = _rope_rows(_dot(xb, wk_ref[...]), ctab, s1tab, s2tab)
    k_ref[0] = k.astype(k_ref.dtype)
    vt_ref[0] = _dot_nt(wvt_ref[...], xb).astype(vt_ref.dtype)

    qit = _rope_cols(_dot_nt(wqit_ref[...], xb), cos_t, sin_t, IDX_HEADS)
    qit_ref[0] = qit.astype(qit_ref.dtype)
    ki = _layer_norm_rows(_dot(xb, wki_ref[...]), lnkg_ref[...], lnkb_ref[...])
    ki_ref[0] = _rope_rows(ki, ctab, s1tab, s2tab).astype(ki_ref.dtype)
    wit = _dot_nt(wwit_ref[...], xb) * (IDX_HEADS ** -0.5)
    wit_ref[0] = wit[:IDX_HEADS]


def _inproj_call(x, batch, meta, tabs, ln_g, ln_b, weights, conv_w, conv_b, lnc_g, lnc_b, lnk_g, lnk_b):
    _, seq, D = x.shape
    B = 1
    tr = SEQ_ALIGN
    tp = seq + tr
    nt = tp // tr
    ctab, s1tab, s2tab, cos_t, sin_t = tabs

    def row_spec(w):
        return pl.BlockSpec((1, tr, w), lambda b, t: (b, t, 0))

    def col_spec(r):
        return pl.BlockSpec((1, r, tr), lambda b, t: (b, 0, t))

    def full(a):
        return pl.BlockSpec(a.shape, lambda b, t: (0,) * a.ndim)

    tab_row = pl.BlockSpec((tr, LANES), lambda b, t: (t, 0))
    tab_col = pl.BlockSpec((ROPE_HALF, tr), lambda b, t: (0, t))
    consts = [ln_g, ln_b, *weights, conv_w, conv_b, lnc_g, lnc_b, lnk_g, lnk_b]
    out_shape = [
        jax.ShapeDtypeStruct((B, tp, D), F32),
        jax.ShapeDtypeStruct((B, tp, CONV_CH), BF16),
        jax.ShapeDtypeStruct((B, ATTN_W, tp), BF16),
        jax.ShapeDtypeStruct((B, tp, ATTN_W), BF16),
        jax.ShapeDtypeStruct((B, ATTN_W, tp), BF16),
        jax.ShapeDtypeStruct((B, IDX_HEADS * IDX_DIM, tp), BF16),
        jax.ShapeDtypeStruct((B, tp, 2 * IDX_DIM), BF16),
        jax.ShapeDtypeStruct((B, IDX_HEADS, tp), F32),
    ]
    out_specs = [row_spec(D), row_spec(CONV_CH), col_spec(ATTN_W), row_spec(ATTN_W), col_spec(ATTN_W),
                 col_spec(IDX_HEADS * IDX_DIM), row_spec(2 * IDX_DIM), col_spec(IDX_HEADS)]
    return pl.pallas_call(
        _inproj_kernel,
        grid=(B, nt),
        in_specs=[pl.BlockSpec((1, tr, D), lambda b, t: (batch, jnp.maximum(t - 1, 0), 0)), full(meta),
                  tab_row, tab_row, tab_row, tab_col, tab_col] + [full(a) for a in consts],
        out_specs=out_specs,
        out_shape=out_shape,
        scratch_shapes=[pltpu.VMEM((CONV_HALO + tr, CONV_CH), F32),
                        pltpu.VMEM((SUBLANES, CONV_HALO + tr, CONV_CH), F32),
                        pltpu.VMEM((tr, CONV_CH), F32)],
        compiler_params=pltpu.CompilerParams(
            dimension_semantics=("arbitrary", "arbitrary"), vmem_limit_bytes=VMEM_LIMIT),
        name="inproj",
    )(x, meta, ctab, s1tab, s2tab, cos_t, sin_t, *consts)


def _key_to_float(key):
    bits = jnp.where(key >= 0, key, key ^ jnp.int32(0x7FFFFFFF))
    f = pltpu.bitcast(bits, F32)
    return jnp.where(key < jnp.int32(KEY_NEG_INF), NEG_INF, f)


def _tree(parts, op):
    parts = list(parts)
    while len(parts) > 1:
        nxt = [op(parts[j], parts[j + 1]) for j in range(0, len(parts) - 1, 2)]
        if len(parts) % 2:
            nxt.append(parts[-1])
        parts = nxt
    return parts[0]


def _fold_rows(x, rows, op):
    return _tree([x[j * rows:(j + 1) * rows] for j in range(x.shape[0] // rows)], op)


def _dsa_kernel(qt_ref, qit_ref, wit_ref, k_ref, ki_ref, vt_ref, *rest, k_sel, n_pad, n_convert):
    f32_w, rest = rest[:n_convert], rest[n_convert:]
    o_ref, bf16_w, rest = rest[0], rest[1:1 + n_convert], rest[1 + n_convert:]
    sc_ref, sh_ref, qm_ref, qim_ref, m_ref, alpha_ref, shift_ref, lg_ref, acc_ref = rest

    for src_ref, dst_ref in zip(f32_w, bf16_w):
        dst_ref[...] = src_ref[...].astype(BF16)

    i = pl.program_id(1)
    tq = qt_ref.shape[2]
    n_chunks = (i * tq + tq + K_CHUNK - 1) // K_CHUNK
    v_rows = HEAD_DIM + BF16_SUBLANES

    def causal_mask(k0, rows=K_CHUNK):
        kpos = k0 + lax.broadcasted_iota(jnp.int32, (rows, tq), 0)
        return kpos <= i * tq + lax.broadcasted_iota(jnp.int32, (rows, tq), 1)

    def rows8(x):
        return jnp.broadcast_to(x, (SUBLANES, tq))

    def tiles(x):
        return x.reshape(x.shape[0] // SUBLANES, SUBLANES, tq)

    def head_slab(ref, h):
        slab = ref[0, (h // 2) * LANES:(h // 2 + 1) * LANES, :]
        zeros = jnp.zeros((HEAD_DIM, tq), slab.dtype)
        if h % 2 == 0:
            return jnp.concatenate([slab[:HEAD_DIM], zeros], axis=0)
        return jnp.concatenate([zeros, slab[HEAD_DIM:]], axis=0)

    for h in range(N_HEADS):
        qm_ref[h] = head_slab(qt_ref, h)
    for h in range(IDX_HEADS):
        qim_ref[h] = head_slab(qit_ref, h)
    wit = wit_ref[0]
    w_heads = [rows8(wit[h:h + 1] * (IDX_DIM ** -0.5)) for h in range(IDX_HEADS)]

    def for_each_chunk(chunk_fn, unroll, count=n_chunks):
        def group(j, carry):
            for u in range(unroll):
                chunk_fn(unroll * j + u, u % 2)
            return carry

        def single(c, carry):
            chunk_fn(c, 0)
            return carry

        n_groups = count // unroll
        lax.fori_loop(0, n_groups, group, 0)
        lax.fori_loop(n_groups * unroll, count, single, 0)

    def diagonal_visible(first_row, rows):
        r = first_row + lax.broadcasted_iota(jnp.int32, (rows, tq), 0)
        return r <= lax.broadcasted_iota(jnp.int32, (rows, tq), 1)

    def score_chunk(c, slot, diagonal=False):
        del slot
        for s in range(K_CHUNK // K_SUB):
            k0 = pl.multiple_of(c * K_CHUNK + s * K_SUB, K_SUB)
            kic = ki_ref[0, pl.ds(k0, K_SUB), :]
            acc = jnp.zeros((K_SUB // SUBLANES, SUBLANES, tq), F32)
            for h in range(IDX_HEADS):
                acc = acc + w_heads[h][None] * jnp.maximum(tiles(_dot(kic, qim_ref[h])), 0.0)
            acc = acc.reshape(K_SUB, tq)
            if diagonal:
                acc = jnp.where(diagonal_visible(s * K_SUB, K_SUB), acc, NEG_INF)
            sc_ref[pl.ds(k0, K_SUB), :] = acc
            hi_bits = pltpu.bitcast(acc, jnp.int32) & jnp.int32(-65536)
            sh_ref[pl.ds(k0, K_SUB), :] = pltpu.bitcast(hi_bits, F32).astype(COARSE)

    for_each_chunk(score_chunk, 4, n_chunks - 1)
    score_chunk(n_chunks - 1, 0, diagonal=True)

    def drop_padding_keys():
        sh_ref[0:n_pad, :] = jnp.full((n_pad, tq), NEG_INF, COARSE)

    sc_ref[0:n_pad, :] = jnp.full((n_pad, tq), NEG_INF, F32)
    drop_padding_keys()

    def count_all(ref, thr_tile, preds):
        rows = thr_tile.shape[0]
        one, zero = jnp.ones((), ref.dtype), jnp.zeros((), ref.dtype)

        def body(c, cnts):
            k0 = pl.multiple_of(c * K_CHUNK, K_CHUNK)
            s = ref[pl.ds(k0, K_CHUNK), :]
            out = []
            for cnt, p in zip(cnts, preds):
                hits = [jnp.where(p(s[j * rows:(j + 1) * rows], thr_tile), one, zero)
                        for j in range(K_CHUNK // rows)]
                out.append(cnt + _tree(hits, jnp.add).astype(F32))
            return tuple(out)

        def group(j, cnts):
            for u in range(COUNT_UNROLL):
                cnts = body(COUNT_UNROLL * j + u, cnts)
            return cnts

        init = tuple(jnp.zeros((rows, tq), F32) for _ in preds)
        n_groups = n_chunks // COUNT_UNROLL
        cnts = lax.fori_loop(0, n_groups, group, init)
        cnts = lax.fori_loop(n_groups * COUNT_UNROLL, n_chunks, body, cnts)
        return [rows8(jnp.sum(cnt, axis=0, keepdims=True)) for cnt in cnts]

    def search_body(it, tkey, coarse):
        cand = tkey + lax.shift_left(jnp.int32(1), 31 - it)
        cf = _key_to_float(cand)
        if coarse:
            cf = pltpu.bitcast(pltpu.bitcast(cf, jnp.int32) & jnp.int32(-65536), F32).astype(COARSE)
            cf = jnp.concatenate([cf] * (BF16_SUBLANES // SUBLANES), axis=0)
        cnt, = count_all(sh_ref if coarse else sc_ref, cf, [lambda s, t: s >= t])
        return jnp.where(cnt >= k_sel, cand, tkey)

    tkey = jnp.full((SUBLANES, tq), INT_MIN, jnp.int32)
    tkey = lax.fori_loop(0, 16, functools.partial(search_body, coarse=True), tkey)
    tkey = lax.fori_loop(16, 32, functools.partial(search_body, coarse=False), tkey)
    thr = _key_to_float(tkey)

    n_ge, n_gt = count_all(sc_ref, thr, [lambda s, t: s >= t, lambda s, t: s > t])
    need = k_sel - n_gt
    has_ties = jnp.max(n_ge) > k_sel

    @pl.when(jnp.logical_not(has_ties))
    def _():
        def mask_chunk(c, slot, diagonal=False):
            del slot
            k0 = pl.multiple_of(c * K_CHUNK, K_CHUNK)
            s = sc_ref[pl.ds(k0, K_CHUNK), :]
            bias = jnp.where(tiles(s) >= thr[None], 0.0, NEG_INF).reshape(K_CHUNK, tq)
            if diagonal:
                bias = jnp.where(diagonal_visible(0, K_CHUNK), bias, NEG_INF)
            sh_ref[pl.ds(k0, K_CHUNK), :] = bias.astype(COARSE)

        for_each_chunk(mask_chunk, 4, n_chunks - 1)
        mask_chunk(n_chunks - 1, 0, diagonal=True)

    @pl.when(has_ties)
    def _():
        r_i = lax.broadcasted_iota(jnp.int32, (K_CHUNK, K_CHUNK), 0)
        c_i = lax.broadcasted_iota(jnp.int32, (K_CHUNK, K_CHUNK), 1)
        lower = jnp.where(c_i <= r_i, 1.0, 0.0).astype(BF16)
        thr_row, need_row = thr[0:1], need[0:1]

        def body(c, seen):
            k0 = pl.multiple_of(c * K_CHUNK, K_CHUNK)
            s = sc_ref[pl.ds(k0, K_CHUNK), :]
            eq = jnp.where(s == thr_row, 1.0, 0.0)
            rank = _dot(lower, eq.astype(BF16)) + seen
            keep_tie = jnp.where(rank <= need_row, eq, 0.0)
            sel = jnp.where(s > thr_row, 1.0, keep_tie)
            sh_ref[pl.ds(k0, K_CHUNK), :] = jnp.where(
                sel > 0.0, jnp.where(causal_mask(k0), 0.0, NEG_INF), NEG_INF).astype(COARSE)
            return rank[K_CHUNK - 1:K_CHUNK, :]
        lax.fori_loop(0, n_chunks, body, jnp.zeros((1, tq), F32))

    drop_padding_keys()

    m_ref[...] = jnp.full(m_ref.shape, NEG_INF, F32)
    acc_ref[...] = jnp.zeros(acc_ref.shape, F32)
    ones_rows = jnp.ones((BF16_SUBLANES, K_CHUNK), BF16)

    def logits_stage(c, slot):
        k0 = pl.multiple_of(c * K_CHUNK, K_CHUNK)
        for h in range(N_HEADS):
            pair = slice((h // 2) * LANES, (h // 2 + 1) * LANES)
            cmax = []
            for s in range(K_CHUNK // K_SUB):
                rows = pl.ds(pl.multiple_of(k0 + s * K_SUB, K_SUB), K_SUB)
                lg = _dot(k_ref[0, rows, pair], qm_ref[h]).astype(BF16) + sh_ref[rows, :].astype(BF16)
                lg_ref[slot, h, s * K_SUB:(s + 1) * K_SUB, :] = lg
                cmax.append(_fold_rows(lg, BF16_SUBLANES, jnp.maximum))
            cmax = _tree(cmax, jnp.maximum).astype(F32)
            m_old = m_ref[h]
            m_new = jnp.maximum(m_old, rows8(jnp.max(cmax, axis=0, keepdims=True)))
            m_safe = jnp.where(m_new == NEG_INF, 0.0, m_new)
            alpha_ref[slot, h] = jnp.exp2(m_old - m_safe)
            shift_ref[slot, h] = m_safe
            m_ref[h] = m_new

    def values_stage(c, slot):
        k0 = pl.multiple_of(c * K_CHUNK, K_CHUNK)
        for h in range(N_HEADS):
            m_safe = shift_ref[slot, h].astype(BF16)
            m_tile = jnp.concatenate([m_safe] * (BF16_SUBLANES // SUBLANES), axis=0)
            lg = lg_ref[slot, h].reshape(K_CHUNK // BF16_SUBLANES, BF16_SUBLANES, tq)
            p = jnp.exp2(lg - m_tile[None]).reshape(K_CHUNK, tq)
            v_aug = jnp.concatenate(
                [vt_ref[0, h * HEAD_DIM:(h + 1) * HEAD_DIM, pl.ds(k0, K_CHUNK)], ones_rows], axis=0)
            pv = _dot(v_aug, p)
            acc = acc_ref[h].reshape(v_rows // SUBLANES, SUBLANES, tq) * alpha_ref[slot, h][None]
            acc_ref[h] = acc.reshape(v_rows, tq) + pv

    def attn_chunk(c, slot):
        logits_stage(c, slot)
        values_stage(c, slot)

    for_each_chunk(attn_chunk, 4)

    for pair in range(N_HEADS // 2):
        halves = []
        for h in (2 * pair, 2 * pair + 1):
            a = acc_ref[h]
            total = a[HEAD_DIM:HEAD_DIM + 1]
            halves.append(a[:HEAD_DIM] / jnp.where(total == 0.0, 1.0, total))
        out_t = jnp.concatenate(halves, axis=0)
        o_ref[0, :, pair * LANES:(pair + 1) * LANES] = out_t.T.astype(o_ref.dtype)


def _dsa_call(qt, qit, wit, k, ki, vt, k_sel, n_pad, expert_w, part, n_parts):
    B, tp, _ = k.shape
    nq = tp // Q_TILE
    n_conv = expert_w[0].shape[0] // n_parts
    assert n_conv * n_parts == expert_w[0].shape[0]
    per_step = next(d for d in range(1, n_conv + 1) if n_conv % d == 0 and n_conv // d <= nq)
    n_steps = n_conv // per_step

    def q_cols(r):
        return pl.BlockSpec((1, r, Q_TILE), lambda b, i: (b, 0, i))

    def per_batch(a):
        return pl.BlockSpec((1,) + a.shape[1:], lambda b, i: (b, 0, 0), pipeline_mode=pl.Buffered(1))

    def expert_spec(a, first_block):
        return pl.BlockSpec((per_step,) + a.shape[1:],
                            lambda b, i: (first_block + jnp.minimum(i, n_steps - 1), 0, 0))

    in_specs = [q_cols(ATTN_W), q_cols(IDX_HEADS * IDX_DIM), q_cols(IDX_HEADS),
                per_batch(k), per_batch(ki), per_batch(vt)]
    in_specs += [expert_spec(a, part * n_steps) for a in expert_w]

    return pl.pallas_call(
        functools.partial(_dsa_kernel, k_sel=k_sel, n_pad=n_pad, n_convert=len(expert_w)),
        grid=(B, nq),
        in_specs=in_specs,
        out_specs=[pl.BlockSpec((1, Q_TILE, ATTN_W), lambda b, i: (b, i, 0))]
        + [expert_spec(a, 0) for a in expert_w],
        out_shape=[jax.ShapeDtypeStruct((B, tp, ATTN_W), BF16)]
        + [jax.ShapeDtypeStruct((n_conv,) + a.shape[1:], BF16) for a in expert_w],
        scratch_shapes=[pltpu.VMEM((tp, Q_TILE), F32),
                        pltpu.VMEM((tp, Q_TILE), COARSE),
                        pltpu.VMEM((N_HEADS, LANES, Q_TILE), BF16),
                        pltpu.VMEM((IDX_HEADS, LANES, Q_TILE), BF16),
                        pltpu.VMEM((N_HEADS, SUBLANES, Q_TILE), F32),
                        pltpu.VMEM((2, N_HEADS, SUBLANES, Q_TILE), F32),
                        pltpu.VMEM((2, N_HEADS, SUBLANES, Q_TILE), F32),
                        pltpu.VMEM((2, N_HEADS, K_CHUNK, Q_TILE), BF16),
                        pltpu.VMEM((N_HEADS, HEAD_DIM + BF16_SUBLANES, Q_TILE), F32)],
        compiler_params=pltpu.CompilerParams(
            dimension_semantics=("arbitrary", "arbitrary"), vmem_limit_bytes=VMEM_LIMIT),
        name="dsa",
    )(qt, qit, wit, k, ki, vt, *expert_w)


def _max_all(x):
    return jnp.max(jnp.max(x, axis=0, keepdims=True), axis=1, keepdims=True)


def _sum_all(x):
    return jnp.sum(jnp.sum(x, axis=0, keepdims=True), axis=1, keepdims=True)


def _pack_bf16_pairs(x):
    w = x.shape[1] // 2
    lo = pltpu.bitcast(x[:, :w].astype(BF16).astype(F32), jnp.int32)
    hi = pltpu.bitcast(x[:, w:].astype(BF16).astype(F32), jnp.int32)
    return lax.shift_right_logical(lo, 16) | (hi & jnp.int32(-65536))


def _unpack_bf16_pairs(p):
    lo = pltpu.bitcast(lax.shift_left(p, 16), F32)
    hi = pltpu.bitcast(p & jnp.int32(-65536), F32)
    return jnp.concatenate([lo, hi], axis=1)


def _router_topk(logits_t, rbias):
    r = logits_t.shape[1]
    shape3 = (N_GROUPS, GROUP_SIZE, r)
    scores = jax.nn.sigmoid(logits_t).reshape(shape3)
    biased = scores + rbias.reshape(N_GROUPS, GROUP_SIZE, 1)
    in_grp = lax.broadcasted_iota(jnp.int32, shape3, 1).astype(F32)
    m1 = jnp.max(biased, axis=1, keepdims=True)
    first = jnp.min(jnp.where(biased == m1, in_grp, float(GROUP_SIZE)), axis=1, keepdims=True)
    m2 = jnp.max(jnp.where(in_grp == first, NEG_INF, biased), axis=1, keepdims=True)
    cur = m1 + m2

    grp_idx = lax.broadcasted_iota(jnp.int32, (N_GROUPS, 1, r), 0).astype(F32)
    grp_sel = jnp.zeros((N_GROUPS, 1, r), F32)
    for _ in range(TOPK_GROUPS):
        m = jnp.max(cur, axis=0, keepdims=True)
        pick = grp_idx == jnp.min(jnp.where(cur == m, grp_idx, float(N_GROUPS)), axis=0, keepdims=True)
        grp_sel = jnp.where(pick, 1.0, grp_sel)
        cur = jnp.where(pick, NEG_INF, cur)

    cur = jnp.where(jnp.broadcast_to(grp_sel, shape3) > 0.0, biased, NEG_INF)
    exp_idx = lax.broadcasted_iota(jnp.int32, shape3, 0).astype(F32) * GROUP_SIZE + in_grp
    chosen = jnp.zeros(shape3, F32)
    ids, wts = [], []
    for _ in range(TOP_K):
        m = _max_all(cur)
        first = -_max_all(-jnp.where(cur == m, exp_idx, float(N_EXPERTS)))
        pick = exp_idx == first
        chosen = jnp.where(pick, 1.0, chosen)
        cur = jnp.where(pick, NEG_INF, cur)
        ids.append(first.reshape(1, r))
        wts.append(_sum_all(jnp.where(pick, scores, 0.0)).reshape(1, r))
    ids = jnp.concatenate(ids, axis=0)
    wts = jnp.concatenate(wts, axis=0)
    gates = wts / jnp.sum(wts, axis=0, keepdims=True) * ROUTED_SCALE
    return ids, gates, chosen.reshape(N_EXPERTS, r), exp_idx


def _route_kernel(conv_ref, attn_ref, hn_ref, woc_ref, woa_ref, g1_ref, b1_ref,
                  wsg_ref, wsu_ref, wsd_ref, wrh_ref, wrl_ref, rb_ref,
                  xp_ref, base_ref, gates_ref, ek_ref, rk_ref, cnt_ref, before_ref):
    step = pl.program_id(0)
    tr = hn_ref.shape[0]
    mix = _dot(conv_ref[...], woc_ref[...]) + _dot(attn_ref[...], woa_ref[...])
    h1 = _layer_norm_rows(DN_ALPHA * hn_ref[...] + mix, g1_ref[...], b1_ref[...])
    xb = h1.astype(BF16)
    xp_ref[...] = _pack_bf16_pairs(h1)

    shared = jax.nn.silu(_dot(xb, wsg_ref[...])) * _dot(xb, wsu_ref[...])
    base_ref[...] = DN_ALPHA * h1 + _dot(shared.astype(BF16), wsd_ref[...])

    x_lo = (h1 - xb.astype(F32)).astype(BF16)
    logits_t = (_dot_nt(wrh_ref[...], xb) + _dot_nt(wrh_ref[...], x_lo) + _dot_nt(wrl_ref[...], xb))
    ids, gates, chosen, exp_idx = _router_topk(logits_t, rb_ref[...])
    padded = jnp.concatenate([gates, jnp.zeros((LANES - TOP_K, tr), F32)], axis=0)
    gates_ref[...] = padded.T
    ek_ref[...] = ids.astype(jnp.int32)

    @pl.when(step == 0)
    def _():
        cnt_ref[...] = jnp.zeros(cnt_ref.shape, F32)
        t_i = lax.broadcasted_iota(jnp.int32, (tr, tr), 0)
        t_j = lax.broadcasted_iota(jnp.int32, (tr, tr), 1)
        before_ref[...] = jnp.where(t_i < t_j, 1.0, 0.0).astype(BF16)

    chosen_b = chosen.astype(BF16)
    running = cnt_ref[...]
    rank = _dot(chosen_b, before_ref[...]) + jnp.concatenate([running] * (tr // LANES), axis=1)
    rank3 = rank.reshape(N_GROUPS, GROUP_SIZE, tr)
    rk = [_sum_all(jnp.where(exp_idx == ids[k:k + 1].reshape(1, 1, tr), rank3, 0.0)).reshape(1, tr)
          for k in range(TOP_K)]
    rk_ref[...] = jnp.concatenate(rk, axis=0).astype(jnp.int32)
    cnt_ref[...] = running + _dot(chosen_b, jnp.ones((tr, LANES), BF16))


def _route_call(conv, attn, hn, woc, woa, g1, b1, wsg, wsu, wsd, wrh, wrl, rbias):
    n, D = hn.shape
    tr = _pick_tile(n, (768, 512, 256))

    def row_spec(w):
        return pl.BlockSpec((tr, w), lambda i: (i, 0))

    def col_spec(r):
        return pl.BlockSpec((r, tr), lambda i: (0, i))

    def full(a):
        return pl.BlockSpec(a.shape, lambda i: (0,) * a.ndim)

    consts = [woc, woa, g1, b1, wsg, wsu, wsd, wrh, wrl, rbias]
    return pl.pallas_call(
        _route_kernel,
        grid=(n // tr,),
        in_specs=[row_spec(CONV_CH), row_spec(ATTN_W), row_spec(D)] + [full(a) for a in consts],
        out_specs=[row_spec(D // 2), row_spec(D), row_spec(LANES), col_spec(TOP_K), col_spec(TOP_K),
                   pl.BlockSpec((N_EXPERTS, LANES), lambda i: (0, 0))],
        out_shape=[jax.ShapeDtypeStruct((n, D // 2), jnp.int32),
                   jax.ShapeDtypeStruct((n, D), F32),
                   jax.ShapeDtypeStruct((n, LANES), F32),
                   jax.ShapeDtypeStruct((TOP_K, n), jnp.int32),
                   jax.ShapeDtypeStruct((TOP_K, n), jnp.int32),
                   jax.ShapeDtypeStruct((N_EXPERTS, LANES), F32)],
        scratch_shapes=[pltpu.VMEM((tr, tr), BF16)],
        compiler_params=pltpu.CompilerParams(
            dimension_semantics=("arbitrary",), vmem_limit_bytes=VMEM_LIMIT),
        name="route",
    )(conv, attn, hn, *consts)


def _plan_kernel(cnt_ref, ek_ref, rk_ref, slot_ref, blk_ref, *, n_blocks):
    tr = ek_ref.shape[1]
    counts = cnt_ref[...]
    blocks = jnp.floor((counts + (EXPERT_BLOCK - 1.0)) * _INV_EXPERT_BLOCK)
    first_blk = [jnp.zeros((1, LANES), F32)]
    for e in range(1, N_EXPERTS):
        first_blk.append(first_blk[-1] + blocks[e - 1:e])
    ek = ek_ref[...]
    seg = jnp.zeros(ek.shape, F32)
    for e in range(N_EXPERTS):
        start = first_blk[e] * EXPERT_BLOCK
        seg = jnp.where(ek == e, jnp.concatenate([start] * (tr // LANES), axis=1), seg)
    slot_ref[...] = seg.astype(jnp.int32) + rk_ref[...]

    end_blk = jnp.concatenate(first_blk, axis=0) + blocks
    w = blk_ref.shape[1]
    blk_idx = lax.broadcasted_iota(jnp.int32, (N_EXPERTS, w), 1).astype(F32)
    done = jnp.where(jnp.concatenate([end_blk] * (w // LANES), axis=1) <= blk_idx, 1.0, 0.0)
    owner = jnp.minimum(jnp.sum(done, axis=0, keepdims=True), N_EXPERTS - 1.0)
    used = jnp.concatenate([end_blk[N_EXPERTS - 1:]] * (w // LANES), axis=1)
    lane = lax.broadcasted_iota(jnp.int32, (1, w), 1)
    table = jnp.where(lane == n_blocks, used, owner)
    blk_ref[...] = jnp.broadcast_to(table, blk_ref.shape).astype(jnp.int32)


def _plan_call(cnt, ek, rk, n_blocks):
    k, n = ek.shape
    tr = _pick_tile(n, (768, 512, 256))
    w = -(-(n_blocks + 1) // LANES) * LANES
    col = pl.BlockSpec((k, tr), lambda i: (0, i))
    return pl.pallas_call(
        functools.partial(_plan_kernel, n_blocks=n_blocks),
        grid=(n // tr,),
        in_specs=[pl.BlockSpec(cnt.shape, lambda i: (0, 0)), col, col],
        out_specs=[col, pl.BlockSpec((SUBLANES, w), lambda i: (0, 0))],
        out_shape=[jax.ShapeDtypeStruct((k, n), jnp.int32), jax.ShapeDtypeStruct((SUBLANES, w), jnp.int32)],
        compiler_params=pltpu.CompilerParams(dimension_semantics=("arbitrary",)),
        name="plan",
    )(cnt, ek, rk)


def _sc_workers():
    info = plsc.get_sparse_core_info()
    return info.num_cores, info.num_subcores


def _sc_scatter_rows(src, slot, n_out):
    n_src, d = src.shape
    n_choices = slot.shape[0]
    rows = SC_SCATTER_ROWS
    n_chunks = n_src // rows
    slots = slot.reshape(n_choices * n_chunks, rows)
    n_cores, n_sub = _sc_workers()
    n_workers = n_cores * n_sub

    def body(src_hbm, slot_hbm, out_hbm, idx_v, rows_v):
        wid = lax.axis_index("s") * n_cores + lax.axis_index("c")

        @pl.loop(0, -(-n_chunks // n_workers))
        def _(j):
            u = j * n_workers + wid

            @pl.when(u < n_chunks)
            def _():
                pltpu.sync_copy(src_hbm.at[pl.ds(u * rows, rows)], rows_v)
                for k in range(n_choices):
                    pltpu.sync_copy(slot_hbm.at[pl.ds(k * n_chunks + u, 1)], idx_v)
                    pltpu.sync_copy(rows_v, out_hbm.at[idx_v.at[0]])

    return pl.kernel(
        body, out_type=jax.ShapeDtypeStruct((n_out, d), src.dtype),
        mesh=plsc.VectorSubcoreMesh(core_axis_name="c", subcore_axis_name="s"),
        scratch_types=[pltpu.VMEM((1, rows), jnp.int32), pltpu.VMEM((rows, d), src.dtype)],
        name="dispatch_rows",
    )(src, slots)


def _sc_gather_rows(table, slots):
    units = slots.shape[0]
    d = table.shape[1]
    n_cores, n_sub = _sc_workers()
    n_workers = n_cores * n_sub
    assert slots.shape[1] == SC_ROWS

    def body(table_hbm, slot_hbm, out_hbm, idx_v, rows_v):
        wid = lax.axis_index("s") * n_cores + lax.axis_index("c")

        @pl.loop(0, -(-units // n_workers))
        def _(j):
            u = j * n_workers + wid

            @pl.when(u < units)
            def _():
                pltpu.sync_copy(slot_hbm.at[pl.ds(u, 1)], idx_v)
                pltpu.sync_copy(table_hbm.at[idx_v.at[0]], rows_v)
                pltpu.sync_copy(rows_v, out_hbm.at[pl.ds(u * SC_ROWS, SC_ROWS)])

    return pl.kernel(
        body, out_type=jax.ShapeDtypeStruct((units * SC_ROWS, d), table.dtype),
        mesh=plsc.VectorSubcoreMesh(core_axis_name="c", subcore_axis_name="s"),
        scratch_types=[pltpu.VMEM((1, SC_ROWS), jnp.int32), pltpu.VMEM((SC_ROWS, d), table.dtype)],
        name="collect_rows",
    )(table, slots)


def _expert_kernel(blk_ref, xs_ref, *rest, per_share):
    ys_ref = rest[-1]
    b = pl.program_id(0)
    in_use = b < blk_ref[pl.num_programs(0)]
    share = blk_ref[b] // per_share
    for s in range(len(rest) // 3):
        wg_ref, wu_ref, wd_ref = rest[3 * s:3 * s + 3]

        @pl.when(jnp.logical_and(in_use, share == s))
        def _():
            x = _unpack_bf16_pairs(xs_ref[...]).astype(BF16)
            hdn = (jax.nn.silu(_dot(x, wg_ref[0])) * _dot(x, wu_ref[0])).astype(BF16)
            ys_ref[...] = _pack_bf16_pairs(_dot(hdn, wd_ref[0]))


def _expert_call(blk_exp, xs, weight_shares):
    p, half = xs.shape
    per_share = weight_shares[0][0].shape[0]
    n_blocks = p // EXPERT_BLOCK
    assert blk_exp.shape == (n_blocks + 1,)

    def row_block(b, blk):
        return (jnp.minimum(b, blk[n_blocks] - 1), 0)

    def weight_spec(a, s):
        return pl.BlockSpec((1,) + a.shape[1:],
                            lambda b, blk: (jnp.clip(blk[b] - s * per_share, 0, per_share - 1), 0, 0))

    rows = pl.BlockSpec((EXPERT_BLOCK, half), row_block)
    grid_spec = pltpu.PrefetchScalarGridSpec(
        num_scalar_prefetch=1,
        grid=(n_blocks,),
        in_specs=[rows] + [weight_spec(a, s) for s, share in enumerate(weight_shares) for a in share],
        out_specs=rows,
    )
    return pl.pallas_call(
        functools.partial(_expert_kernel, per_share=per_share),
        grid_spec=grid_spec,
        out_shape=jax.ShapeDtypeStruct((p, half), jnp.int32),
        compiler_params=pltpu.CompilerParams(
            dimension_semantics=("arbitrary",), vmem_limit_bytes=VMEM_LIMIT),
        name="experts",
    )(blk_exp, xs, *[a for share in weight_shares for a in share])


def _combine_kernel(g_ref, gates_ref, base_ref, g2_ref, b2_ref, *rest):
    o_ref = rest[-1]
    gates = gates_ref[...]
    acc = base_ref[...]
    for k in range(g_ref.shape[0]):
        acc = acc + _unpack_bf16_pairs(g_ref[k]) * gates[:, k:k + 1]
    o_ref[0] = _layer_norm_rows(acc, g2_ref[...], b2_ref[...])


def _combine_call(g, gates, base, g2, b2, result, batch, n_batch):
    k, n, half = g.shape
    D = base.shape[1]
    tr = SEQ_ALIGN
    seq = n - tr

    def row_spec(w):
        return pl.BlockSpec((tr, w), lambda i: (i, 0))

    vec = pl.BlockSpec((1, D), lambda i: (0, 0))
    in_specs = [pl.BlockSpec((k, tr, half), lambda i: (0, i, 0)), row_spec(LANES), row_spec(D), vec, vec]
    args = [g, gates, base, g2, b2]
    aliases = {}
    if result is not None:
        in_specs.append(pl.BlockSpec(memory_space=pl.ANY))
        args.append(result)
        aliases = {len(args) - 1: 0}
    return pl.pallas_call(
        _combine_kernel,
        grid=(n // tr,),
        in_specs=in_specs,
        out_specs=pl.BlockSpec((1, tr, D), lambda i: (batch, jnp.maximum(i - 1, 0), 0)),
        out_shape=jax.ShapeDtypeStruct((n_batch, seq, D), F32),
        input_output_aliases=aliases,
        compiler_params=pltpu.CompilerParams(
            dimension_semantics=("arbitrary",), vmem_limit_bytes=VMEM_LIMIT),
        name="combine",
    )(*args)


def _rope_tables(tp, lead):
    pos = jnp.arange(tp, dtype=F32) - lead
    inv = jnp.power(ROPE_THETA, -2.0 * jnp.arange(ROPE_HALF, dtype=F32) / ROPE_DIM)
    ang = pos[:, None] * inv[None, :]
    cos, sin = jnp.cos(ang), jnp.sin(ang)
    zeros = jnp.zeros((tp, HEAD_DIM - ROPE_DIM), F32)
    zh = jnp.zeros((tp, ROPE_HALF), F32)
    c64 = jnp.concatenate([cos, cos, jnp.ones_like(zeros)], axis=1)
    s1_64 = jnp.concatenate([-sin, zh, zeros], axis=1)
    s2_64 = jnp.concatenate([zh, sin, zeros], axis=1)
    rep = LANES // HEAD_DIM
    return (jnp.tile(c64, (1, rep)), jnp.tile(s1_64, (1, rep)), jnp.tile(s2_64, (1, rep)),
            cos.T, sin.T)


def kernel(x, meta_tokens, ln_emb_g, ln_emb_b, w_in, conv_w, conv_b, ln_conv_g, ln_conv_b, ln_kidx_g, ln_kidx_b, w_out, ln1_g, ln1_b, w_router, router_bias, w_gate, w_up, w_down, ws_gate, ws_up, ws_down, ln2_g, ln2_b):
    B, seq, D = x.shape
    assert w_in.shape[0] == DEPTH
    assert seq % SEQ_ALIGN == 0 and meta_tokens.shape[0] == N_META <= SEQ_ALIGN
    k_sel = min(INDEX_TOPK, seq // 4)
    tp = seq + SEQ_ALIGN
    n_pad = SEQ_ALIGN - N_META

    def row(a):
        return a.reshape(1, -1).astype(F32)

    w = w_in[0]
    o = 0
    parts = []
    for width in (CONV_CH, CONV_CH, ATTN_W, ATTN_W, ATTN_W, IDX_HEADS * IDX_DIM, IDX_DIM, IDX_HEADS):
        parts.append(w[:, o:o + width])
        o += width
    wa, wgl, wq, wk, wv, wqi, wki, wwi = parts
    wwi_t = jnp.concatenate([wwi.T, jnp.zeros((BF16_SUBLANES - IDX_HEADS, D), w.dtype)], axis=0)
    weights = (jnp.concatenate([wa, wgl], axis=1).astype(BF16), wq.T.astype(BF16), wk.astype(BF16),
               wv.T.astype(BF16), wqi.T.astype(BF16), jnp.concatenate([wki, wki], axis=1).astype(BF16),
               wwi_t.astype(BF16))

    def twice(a):
        return row(jnp.concatenate([a, a]))

    tabs = _rope_tables(tp, n_pad)
    inproj_consts = (row(ln_emb_g), row(ln_emb_b), weights, conv_w[0].astype(F32), row(conv_b[0]),
                     row(ln_conv_g[0]), row(ln_conv_b[0]), twice(ln_kidx_g[0]), twice(ln_kidx_b[0]))
    wr_t = w_router[0].T.astype(F32)
    wr_hi = wr_t.astype(BF16)
    wr_lo = (wr_t - wr_hi.astype(F32)).astype(BF16)
    route_consts = (w_out[0][:CONV_CH].astype(BF16), w_out[0][CONV_CH:].astype(BF16), row(ln1_g[0]),
                    row(ln1_b[0]), ws_gate[0].astype(BF16), ws_up[0].astype(BF16), ws_down[0].astype(BF16),
                    wr_hi, wr_lo, router_bias[0].reshape(-1, 1).astype(F32))
    meta = meta_tokens.astype(F32)
    n_blocks = -(-tp * TOP_K // EXPERT_BLOCK) + N_EXPERTS

    expert_w = (w_gate[0], w_up[0], w_down[0])
    weight_shares = []
    routed = []
    for b in range(B):
        hn, conv, qt, k, vt, qit, ki, wit = _inproj_call(x, b, meta, tabs, *inproj_consts)
        attn, *share = _dsa_call(qt, qit, wit, k, ki, vt, k_sel, n_pad, expert_w, b, B)
        weight_shares.append(share)
        xp, base, gates, ek, rk, cnt = _route_call(conv[0], attn[0], hn[0], *route_consts)

        slot, blk = _plan_call(cnt, ek, rk, n_blocks)
        slots = slot.reshape(TOP_K * tp // SC_ROWS, SC_ROWS)
        xs = _sc_scatter_rows(xp, slot, n_blocks * EXPERT_BLOCK)
        routed.append((xs, blk, slots, gates, base))

    result = None
    for b, (xs, blk, slots, gates, base) in enumerate(routed):
        ys = _expert_call(blk[0, :n_blocks + 1], xs, weight_shares)
        picked = _sc_gather_rows(ys, slots).reshape(TOP_K, tp, D // 2)
        result = _combine_call(picked, gates, base, row(ln2_g[0]), row(ln2_b[0]), result, b, B)
    return result
```

```python
import functools

import numpy as np
import jax
import jax.numpy as jnp
from jax import lax
from jax.experimental import pallas as pl
from jax.experimental.pallas import tpu as pltpu
from jax.experimental.pallas import tpu_sc as plsc

N_META = 16
CONV_CH = 512
CONV_WIDTH = 31
N_HEADS = 8
HEAD_DIM = 64
ATTN_W = N_HEADS * HEAD_DIM
IDX_HEADS = 8
IDX_DIM = 64
INDEX_TOPK = 256
ROPE_DIM = HEAD_DIM // 4
ROPE_HALF = ROPE_DIM // 2
ROPE_THETA = 500000.0
N_EXPERTS = 64
TOP_K = 8
N_GROUPS = 8
GROUP_SIZE = N_EXPERTS // N_GROUPS
TOPK_GROUPS = 4
ROUTED_SCALE = 2.5
LN_EPS = 1e-5
DEPTH = 1
DN_ALPHA = (2.0 * DEPTH) ** 0.25

LANES = 128
Q_TILE = 256
SUBLANES = 8
BF16_SUBLANES = 16
EXPERT_BLOCK = 1152
EXPERT_SUB = 384
_INV_EXPERT_BLOCK = float(np.nextafter(np.float32(1.0 / EXPERT_BLOCK), np.float32(1.0)))
SC_ROWS = 128
SC_SCATTER_ROWS = 64
K_CHUNK = Q_TILE
K_SUB = 128
COUNT_UNROLL = 4
SEQ_ALIGN = 256
CONV_HALO = 32
CONV_ROWS = 128
VMEM_LIMIT = 56 * 1024 * 1024

F32 = jnp.float32
BF16 = jnp.bfloat16
COARSE = jnp.bfloat16
NEG_INF = float("-inf")
INT_MIN = -2 ** 31
KEY_NEG_INF = -2139095041
LOG2_E = 1.4426950408889634


def _dot(a, b):
    return jnp.dot(a, b, preferred_element_type=F32)


def _dot_nt(a, b):
    return lax.dot_general(a, b, (((1,), (1,)), ((), ())), preferred_element_type=F32)


def _layer_norm_rows(x, g, b):
    mu = jnp.mean(x, axis=-1, keepdims=True)
    xc = x - mu
    var = jnp.mean(xc * xc, axis=-1, keepdims=True)
    return xc * lax.rsqrt(var + LN_EPS) * g + b


def _pick_tile(n, candidates):
    for c in candidates:
        if n % c == 0:
            return c
    raise ValueError(f"no tile for {n}")


def _rope_rows(x, c_tab, s1_tab, s2_tab):
    outs = []
    for j in range(x.shape[1] // LANES):
        xs = x[:, j * LANES:(j + 1) * LANES]
        up = pltpu.roll(xs, LANES - ROPE_HALF, axis=1)
        dn = pltpu.roll(xs, ROPE_HALF, axis=1)
        outs.append(xs * c_tab + up * s1_tab + dn * s2_tab)
    return jnp.concatenate(outs, axis=1)


def _rope_cols(xt, cos_t, sin_t, heads):
    r = xt.shape[1]
    x3 = xt.reshape(heads, HEAD_DIM, r)
    x1 = x3[:, 0:ROPE_HALF, :]
    x2 = x3[:, ROPE_HALF:ROPE_DIM, :]
    n1 = x1 * cos_t - x2 * sin_t
    n2 = x2 * cos_t + x1 * sin_t
    out = jnp.concatenate([n1, n2, x3[:, ROPE_DIM:, :]], axis=1)
    return out.reshape(heads * HEAD_DIM, r)


def _inproj_kernel(x_ref, meta_ref, ctab_ref, s1tab_ref, s2tab_ref, cost_ref, sint_ref,
                   lng_ref, lnb_ref, wag_ref, wqt_ref, wk_ref, wvt_ref, wqit_ref, wki_ref, wwit_ref,
                   cw_ref, cb_ref, lncg_ref, lncb_ref, lnkg_ref, lnkb_ref,
                   hn_ref, conv_ref, qt_ref, k_ref, vt_ref, qit_ref, ki_ref, wit_ref,
                   ubuf_ref, wbuf_ref, cbuf_ref):
    t = pl.program_id(1)
    tr = x_ref.shape[1]
    n_meta = meta_ref.shape[0]

    @pl.when(t == 0)
    def _():
        ubuf_ref[0:CONV_HALO, :] = jnp.zeros((CONV_HALO, CONV_CH), F32)

    first = jnp.concatenate([jnp.zeros((tr - n_meta, x_ref.shape[2]), F32), meta_ref[...]], axis=0)
    h = jnp.where(t == 0, first, x_ref[0])

    hn = _layer_norm_rows(h, lng_ref[...], lnb_ref[...])
    hn_ref[0] = hn
    xb = hn.astype(BF16)

    ag = _dot(xb, wag_ref[...])
    u = ag[:, :CONV_CH] * jax.nn.sigmoid(ag[:, CONV_CH:])
    row = lax.broadcasted_iota(jnp.int32, (tr, CONV_CH), 0)
    u = jnp.where(jnp.logical_or(t > 0, row >= tr - n_meta), u, 0.0)
    ubuf_ref[CONV_HALO:CONV_HALO + tr, :] = u
    base = CONV_HALO - (CONV_WIDTH - 1)
    for r in range(SUBLANES):
        rows = max(o for o in range(base, base + CONV_WIDTH) if o % SUBLANES == r) - r + tr
        wbuf_ref[r, 0:rows, :] = ubuf_ref[r:r + rows, :]
    ubuf_ref[0:CONV_HALO, :] = ubuf_ref[tr:tr + CONV_HALO, :]
    for lo in range(0, CONV_CH, LANES):
        for r0 in range(0, tr, CONV_ROWS):
            acc = jnp.zeros((CONV_ROWS, LANES), F32)
            for o in range(base, base + CONV_WIDTH):
                r = o % SUBLANES
                acc = acc + (cw_ref[o - base:o - base + 1, lo:lo + LANES]
                             * wbuf_ref[r, o - r + r0:o - r + r0 + CONV_ROWS, lo:lo + LANES])
            cbuf_ref[r0:r0 + CONV_ROWS, lo:lo + LANES] = acc
    c = _layer_norm_rows(cbuf_ref[...] + cb_ref[...], lncg_ref[...], lncb_ref[...])
    conv_ref[0] = (c * jax.nn.sigmoid(c)).astype(conv_ref.dtype)

    ctab, s1tab, s2tab = ctab_ref[...], s1tab_ref[...], s2tab_ref[...]
    cos_t, sin_t = cost_ref[...], sint_ref[...]

    qt = _rope_cols(_dot_nt(wqt_ref[...], xb), cos_t, sin_t, N_HEADS)
    qt_ref[0] = (qt * (HEAD_DIM ** -0.5 * LOG2_E)).astype(qt_ref.dtype)
    k = _rope_rows(_dot(xb, wk_ref[...]), ctab, s1tab, s2tab)
    k_ref[0] = k.astype(k_ref.dtype)
    vt_ref[0] = _dot_nt(wvt_ref[...], xb).astype(vt_ref.dtype)

    qit = _rope_cols(_dot_nt(wqit_ref[...], xb), cos_t, sin_t, IDX_HEADS)
    qit_ref[0] = qit.astype(qit_ref.dtype)
    ki = _layer_norm_rows(_dot(xb, wki_ref[...]), lnkg_ref[...], lnkb_ref[...])
    ki_ref[0] = _rope_rows(ki, ctab, s1tab, s2tab).astype(ki_ref.dtype)
    wit = _dot_nt(wwit_ref[...], xb) * (IDX_HEADS ** -0.5)
    wit_ref[0] = wit[:IDX_HEADS]


def _inproj_call(x, batch, meta, tabs, ln_g, ln_b, weights, conv_w, conv_b, lnc_g, lnc_b, lnk_g, lnk_b):
    _, seq, D = x.shape
    B = 1
    tr = SEQ_ALIGN
    tp = seq + tr
    nt = tp // tr
    ctab, s1tab, s2tab, cos_t, sin_t = tabs

    def row_spec(w):
        return pl.BlockSpec((1, tr, w), lambda b, t: (b, t, 0))

    def col_spec(r):
        return pl.BlockSpec((1, r, tr), lambda b, t: (b, 0, t))

    def full(a):
        return pl.BlockSpec(a.shape, lambda b, t: (0,) * a.ndim)

    tab_row = pl.BlockSpec((tr, LANES), lambda b, t: (t, 0))
    tab_col = pl.BlockSpec((ROPE_HALF, tr), lambda b, t: (0, t))
    consts = [ln_g, ln_b, *weights, conv_w, conv_b, lnc_g, lnc_b, lnk_g, lnk_b]
    out_shape = [
        jax.ShapeDtypeStruct((B, tp, D), F32),
        jax.ShapeDtypeStruct((B, tp, CONV_CH), BF16),
        jax.ShapeDtypeStruct((B, ATTN_W, tp), BF16),
        jax.ShapeDtypeStruct((B, tp, ATTN_W), BF16),
        jax.ShapeDtypeStruct((B, ATTN_W, tp), BF16),
        jax.ShapeDtypeStruct((B, IDX_HEADS * IDX_DIM, tp), BF16),
        jax.ShapeDtypeStruct((B, tp, 2 * IDX_DIM), BF16),
        jax.ShapeDtypeStruct((B, IDX_HEADS, tp), F32),
    ]
    out_specs = [row_spec(D), row_spec(CONV_CH), col_spec(ATTN_W), row_spec(ATTN_W), col_spec(ATTN_W),
                 col_spec(IDX_HEADS * IDX_DIM), row_spec(2 * IDX_DIM), col_spec(IDX_HEADS)]
    return pl.pallas_call(
        _inproj_kernel,
        grid=(B, nt),
        in_specs=[pl.BlockSpec((1, tr, D), lambda b, t: (batch, jnp.maximum(t - 1, 0), 0)), full(meta),
                  tab_row, tab_row, tab_row, tab_col, tab_col] + [full(a) for a in consts],
        out_specs=out_specs,
        out_shape=out_shape,
        scratch_shapes=[pltpu.VMEM((CONV_HALO + tr, CONV_CH), F32),
                        pltpu.VMEM((SUBLANES, CONV_HALO + tr, CONV_CH), F32),
                        pltpu.VMEM((tr, CONV_CH), F32)],
        compiler_params=pltpu.CompilerParams(
            dimension_semantics=("arbitrary", "arbitrary"), vmem_limit_bytes=VMEM_LIMIT),
        name="inproj",
    )(x, meta, ctab, s1tab, s2tab, cos_t, sin_t, *consts)


def _key_to_float(key):
    bits = jnp.where(key >= 0, key, key ^ jnp.int32(0x7FFFFFFF))
    f = pltpu.bitcast(bits, F32)
    return jnp.where(key < jnp.int32(KEY_NEG_INF), NEG_INF, f)


def _tree(parts, op):
    parts = list(parts)
    while len(parts) > 1:
        nxt = [op(parts[j], parts[j + 1]) for j in range(0, len(parts) - 1, 2)]
        if len(parts) % 2:
            nxt.append(parts[-1])
        parts = nxt
    return parts[0]


def _fold_rows(x, rows, op):
    return _tree([x[j * rows:(j + 1) * rows] for j in range(x.shape[0] // rows)], op)


def _dsa_kernel(qt_ref, qit_ref, wit_ref, k_ref, ki_ref, vt_ref, *rest, k_sel, n_pad, n_convert):
    f32_w, rest = rest[:n_convert], rest[n_convert:]
    o_ref, bf16_w, rest = rest[0], rest[1:1 + n_convert], rest[1 + n_convert:]
    sc_ref, sh_ref, qm_ref, qim_ref, m_ref, alpha_ref, shift_ref, lg_ref, acc_ref = rest

    for src_ref, dst_ref in zip(f32_w, bf16_w):
        dst_ref[...] = src_ref[...].astype(BF16)

    i = pl.program_id(1)
    tq = qt_ref.shape[2]
    n_chunks = (i * tq + tq + K_CHUNK - 1) // K_CHUNK
    v_rows = HEAD_DIM + BF16_SUBLANES

    def causal_mask(k0, rows=K_CHUNK):
        kpos = k0 + lax.broadcasted_iota(jnp.int32, (rows, tq), 0)
        return kpos <= i * tq + lax.broadcasted_iota(jnp.int32, (rows, tq), 1)

    def rows8(x):
        return jnp.broadcast_to(x, (SUBLANES, tq))

    def tiles(x):
        return x.reshape(x.shape[0] // SUBLANES, SUBLANES, tq)

    def head_slab(ref, h):
        slab = ref[0, (h // 2) * LANES:(h // 2 + 1) * LANES, :]
        zeros = jnp.zeros((HEAD_DIM, tq), slab.dtype)
        if h % 2 == 0:
            return jnp.concatenate([slab[:HEAD_DIM], zeros], axis=0)
        return jnp.concatenate([zeros, slab[HEAD_DIM:]], axis=0)

    for h in range(N_HEADS):
        qm_ref[h] = head_slab(qt_ref, h)
    for h in range(IDX_HEADS):
        qim_ref[h] = head_slab(qit_ref, h)
    wit = wit_ref[0]
    w_heads = [rows8(wit[h:h + 1] * (IDX_DIM ** -0.5)) for h in range(IDX_HEADS)]

    def for_each_chunk(chunk_fn, unroll, count=n_chunks):
        def group(j, carry):
            for u in range(unroll):
                chunk_fn(unroll * j + u, u % 2)
            return carry

        def single(c, carry):
            chunk_fn(c, 0)
            return carry

        n_groups = count // unroll
        lax.fori_loop(0, n_groups, group, 0)
        lax.fori_loop(n_groups * unroll, count, single, 0)

    def diagonal_visible(first_row, rows):
        r = first_row + lax.broadcasted_iota(jnp.int32, (rows, tq), 0)
        return r <= lax.broadcasted_iota(jnp.int32, (rows, tq), 1)

    def score_chunk(c, slot, diagonal=False):
        del slot
        for s in range(K_CHUNK // K_SUB):
            k0 = pl.multiple_of(c * K_CHUNK + s * K_SUB, K_SUB)
            kic = ki_ref[0, pl.ds(k0, K_SUB), :]
            acc = jnp.zeros((K_SUB // SUBLANES, SUBLANES, tq), F32)
            for h in range(IDX_HEADS):
                acc = acc + w_heads[h][None] * jnp.maximum(tiles(_dot(kic, qim_ref[h])), 0.0)
            acc = acc.reshape(K_SUB, tq)
            if diagonal:
                acc = jnp.where(diagonal_visible(s * K_SUB, K_SUB), acc, NEG_INF)
            sc_ref[pl.ds(k0, K_SUB), :] = acc
            hi_bits = pltpu.bitcast(acc, jnp.int32) & jnp.int32(-65536)
            sh_ref[pl.ds(k0, K_SUB), :] = pltpu.bitcast(hi_bits, F32).astype(COARSE)

    for_each_chunk(score_chunk, 4, n_chunks - 1)
    score_chunk(n_chunks - 1, 0, diagonal=True)

    def drop_padding_keys():
        sh_ref[0:n_pad, :] = jnp.full((n_pad, tq), NEG_INF, COARSE)

    sc_ref[0:n_pad, :] = jnp.full((n_pad, tq), NEG_INF, F32)
    drop_padding_keys()

    def count_all(ref, thr_tile, preds):
        rows = thr_tile.shape[0]
        one, zero = jnp.ones((), ref.dtype), jnp.zeros((), ref.dtype)

        def body(c, cnts):
            k0 = pl.multiple_of(c * K_CHUNK, K_CHUNK)
            s = ref[pl.ds(k0, K_CHUNK), :]
            out = []
            for cnt, p in zip(cnts, preds):
                hits = [jnp.where(p(s[j * rows:(j + 1) * rows], thr_tile), one, zero)
                        for j in range(K_CHUNK // rows)]
                out.append(cnt + _tree(hits, jnp.add).astype(F32))
            return tuple(out)

        def group(j, cnts):
            for u in range(COUNT_UNROLL):
                cnts = body(COUNT_UNROLL * j + u, cnts)
            return cnts

        init = tuple(jnp.zeros((rows, tq), F32) for _ in preds)
        n_groups = n_chunks // COUNT_UNROLL
        cnts = lax.fori_loop(0, n_groups, group, init)
        cnts = lax.fori_loop(n_groups * COUNT_UNROLL, n_chunks, body, cnts)
        return [rows8(jnp.sum(cnt, axis=0, keepdims=True)) for cnt in cnts]

    def search_body(it, tkey, coarse):
        cand = tkey + lax.shift_left(jnp.int32(1), 31 - it)
        cf = _key_to_float(cand)
        if coarse:
            cf = pltpu.bitcast(pltpu.bitcast(cf, jnp.int32) & jnp.int32(-65536), F32).astype(COARSE)
            cf = jnp.concatenate([cf] * (BF16_SUBLANES // SUBLANES), axis=0)
        cnt, = count_all(sh_ref if coarse else sc_ref, cf, [lambda s, t: s >= t])
        return jnp.where(cnt >= k_sel, cand, tkey)

    tkey = jnp.full((SUBLANES, tq), INT_MIN, jnp.int32)
    tkey = lax.fori_loop(0, 16, functools.partial(search_body, coarse=True), tkey)
    tkey = lax.fori_loop(16, 32, functools.partial(search_body, coarse=False), tkey)
    thr = _key_to_float(tkey)

    n_ge, n_gt = count_all(sc_ref, thr, [lambda s, t: s >= t, lambda s, t: s > t])
    need = k_sel - n_gt
    has_ties = jnp.max(n_ge) > k_sel

    @pl.when(jnp.logical_not(has_ties))
    def _():
        def mask_chunk(c, slot, diagonal=False):
            del slot
            k0 = pl.multiple_of(c * K_CHUNK, K_CHUNK)
            s = sc_ref[pl.ds(k0, K_CHUNK), :]
            bias = jnp.where(tiles(s) >= thr[None], 0.0, NEG_INF).reshape(K_CHUNK, tq)
            if diagonal:
                bias = jnp.where(diagonal_visible(0, K_CHUNK), bias, NEG_INF)
            sh_ref[pl.ds(k0, K_CHUNK), :] = bias.astype(COARSE)

        for_each_chunk(mask_chunk, 4, n_chunks - 1)
        mask_chunk(n_chunks - 1, 0, diagonal=True)

    @pl.when(has_ties)
    def _():
        r_i = lax.broadcasted_iota(jnp.int32, (K_CHUNK, K_CHUNK), 0)
        c_i = lax.broadcasted_iota(jnp.int32, (K_CHUNK, K_CHUNK), 1)
        lower = jnp.where(c_i <= r_i, 1.0, 0.0).astype(BF16)
        thr_row, need_row = thr[0:1], need[0:1]

        def body(c, seen):
            k0 = pl.multiple_of(c * K_CHUNK, K_CHUNK)
            s = sc_ref[pl.ds(k0, K_CHUNK), :]
            eq = jnp.where(s == thr_row, 1.0, 0.0)
            rank = _dot(lower, eq.astype(BF16)) + seen
            keep_tie = jnp.where(rank <= need_row, eq, 0.0)
            sel = jnp.where(s > thr_row, 1.0, keep_tie)
            sh_ref[pl.ds(k0, K_CHUNK), :] = jnp.where(
                sel > 0.0, jnp.where(causal_mask(k0), 0.0, NEG_INF), NEG_INF).astype(COARSE)
            return rank[K_CHUNK - 1:K_CHUNK, :]
        lax.fori_loop(0, n_chunks, body, jnp.zeros((1, tq), F32))

    drop_padding_keys()

    m_ref[...] = jnp.full(m_ref.shape, NEG_INF, F32)
    acc_ref[...] = jnp.zeros(acc_ref.shape, F32)
    ones_rows = jnp.ones((BF16_SUBLANES, K_CHUNK), BF16)

    def logits_stage(c, slot):
        k0 = pl.multiple_of(c * K_CHUNK, K_CHUNK)
        for h in range(N_HEADS):
            pair = slice((h // 2) * LANES, (h // 2 + 1) * LANES)
            cmax = []
            for s in range(K_CHUNK // K_SUB):
                rows = pl.ds(pl.multiple_of(k0 + s * K_SUB, K_SUB), K_SUB)
                lg = _dot(k_ref[0, rows, pair], qm_ref[h]).astype(BF16) + sh_ref[rows, :].astype(BF16)
                lg_ref[slot, h, s * K_SUB:(s + 1) * K_SUB, :] = lg
                cmax.append(_fold_rows(lg, BF16_SUBLANES, jnp.maximum))
            cmax = _tree(cmax, jnp.maximum).astype(F32)
            m_old = m_ref[h]
            m_new = jnp.maximum(m_old, rows8(jnp.max(cmax, axis=0, keepdims=True)))
            m_safe = jnp.where(m_new == NEG_INF, 0.0, m_new)
            alpha_ref[slot, h] = jnp.exp2(m_old - m_safe)
            shift_ref[slot, h] = m_safe
            m_ref[h] = m_new

    def values_stage(c, slot):
        k0 = pl.multiple_of(c * K_CHUNK, K_CHUNK)
        for h in range(N_HEADS):
            m_safe = shift_ref[slot, h].astype(BF16)
            m_tile = jnp.concatenate([m_safe] * (BF16_SUBLANES // SUBLANES), axis=0)
            lg = lg_ref[slot, h].reshape(K_CHUNK // BF16_SUBLANES, BF16_SUBLANES, tq)
            p = jnp.exp2(lg - m_tile[None]).reshape(K_CHUNK, tq)
            v_aug = jnp.concatenate(
                [vt_ref[0, h * HEAD_DIM:(h + 1) * HEAD_DIM, pl.ds(k0, K_CHUNK)], ones_rows], axis=0)
            pv = _dot(v_aug, p)
            acc = acc_ref[h].reshape(v_rows // SUBLANES, SUBLANES, tq) * alpha_ref[slot, h][None]
            acc_ref[h] = acc.reshape(v_rows, tq) + pv

    def attn_chunk(c, slot):
        logits_stage(c, slot)
        values_stage(c, slot)

    for_each_chunk(attn_chunk, 4)

    for pair in range(N_HEADS // 2):
        halves = []
        for h in (2 * pair, 2 * pair + 1):
            a = acc_ref[h]
            total = a[HEAD_DIM:HEAD_DIM + 1]
            halves.append(a[:HEAD_DIM] / jnp.where(total == 0.0, 1.0, total))
        out_t = jnp.concatenate(halves, axis=0)
        o_ref[0, :, pair * LANES:(pair + 1) * LANES] = out_t.T.astype(o_ref.dtype)


def _dsa_call(qt, qit, wit, k, ki, vt, k_sel, n_pad, expert_w, part, n_parts):
    B, tp, _ = k.shape
    nq = tp // Q_TILE
    n_conv = expert_w[0].shape[0] // n_parts
    assert n_conv * n_parts == expert_w[0].shape[0]
    per_step = next(d for d in range(1, n_conv + 1) if n_conv % d == 0 and n_conv // d <= nq)
    n_steps = n_conv // per_step

    def q_cols(r):
        return pl.BlockSpec((1, r, Q_TILE), lambda b, i: (b, 0, i))

    def per_batch(a):
        return pl.BlockSpec((1,) + a.shape[1:], lambda b, i: (b, 0, 0), pipeline_mode=pl.Buffered(1))

    def expert_spec(a, first_block):
        return pl.BlockSpec((per_step,) + a.shape[1:],
                            lambda b, i: (first_block + jnp.minimum(i, n_steps - 1), 0, 0))

    in_specs = [q_cols(ATTN_W), q_cols(IDX_HEADS * IDX_DIM), q_cols(IDX_HEADS),
                per_batch(k), per_batch(ki), per_batch(vt)]
    in_specs += [expert_spec(a, part * n_steps) for a in expert_w]

    return pl.pallas_call(
        functools.partial(_dsa_kernel, k_sel=k_sel, n_pad=n_pad, n_convert=len(expert_w)),
        grid=(B, nq),
        in_specs=in_specs,
        out_specs=[pl.BlockSpec((1, Q_TILE, ATTN_W), lambda b, i: (b, i, 0))]
        + [expert_spec(a, 0) for a in expert_w],
        out_shape=[jax.ShapeDtypeStruct((B, tp, ATTN_W), BF16)]
        + [jax.ShapeDtypeStruct((n_conv,) + a.shape[1:], BF16) for a in expert_w],
        scratch_shapes=[pltpu.VMEM((tp, Q_TILE), F32),
                        pltpu.VMEM((tp, Q_TILE), COARSE),
                        pltpu.VMEM((N_HEADS, LANES, Q_TILE), BF16),
                        pltpu.VMEM((IDX_HEADS, LANES, Q_TILE), BF16),
                        pltpu.VMEM((N_HEADS, SUBLANES, Q_TILE), F32),
                        pltpu.VMEM((2, N_HEADS, SUBLANES, Q_TILE), F32),
                        pltpu.VMEM((2, N_HEADS, SUBLANES, Q_TILE), F32),
                        pltpu.VMEM((2, N_HEADS, K_CHUNK, Q_TILE), BF16),
                        pltpu.VMEM((N_HEADS, HEAD_DIM + BF16_SUBLANES, Q_TILE), F32)],
        compiler_params=pltpu.CompilerParams(
            dimension_semantics=("arbitrary", "arbitrary"), vmem_limit_bytes=VMEM_LIMIT),
        name="dsa",
    )(qt, qit, wit, k, ki, vt, *expert_w)


def _max_all(x):
    return jnp.max(jnp.max(x, axis=0, keepdims=True), axis=1, keepdims=True)


def _sum_all(x):
    return jnp.sum(jnp.sum(x, axis=0, keepdims=True), axis=1, keepdims=True)


def _pack_bf16_pairs(x):
    w = x.shape[1] // 2
    lo = pltpu.bitcast(x[:, :w].astype(BF16).astype(F32), jnp.int32)
    hi = pltpu.bitcast(x[:, w:].astype(BF16).astype(F32), jnp.int32)
    return lax.shift_right_logical(lo, 16) | (hi & jnp.int32(-65536))


def _unpack_bf16_pairs(p):
    lo = pltpu.bitcast(lax.shift_left(p, 16), F32)
    hi = pltpu.bitcast(p & jnp.int32(-65536), F32)
    return jnp.concatenate([lo, hi], axis=1)


def _router_topk(logits_t, rbias):
    r = logits_t.shape[1]
    shape3 = (N_GROUPS, GROUP_SIZE, r)
    scores = jax.nn.sigmoid(logits_t).reshape(shape3)
    biased = scores + rbias.reshape(N_GROUPS, GROUP_SIZE, 1)
    in_grp = lax.broadcasted_iota(jnp.int32, shape3, 1).astype(F32)
    m1 = jnp.max(biased, axis=1, keepdims=True)
    first = jnp.min(jnp.where(biased == m1, in_grp, float(GROUP_SIZE)), axis=1, keepdims=True)
    m2 = jnp.max(jnp.where(in_grp == first, NEG_INF, biased), axis=1, keepdims=True)
    cur = m1 + m2

    grp_idx = lax.broadcasted_iota(jnp.int32, (N_GROUPS, 1, r), 0).astype(F32)
    grp_sel = jnp.zeros((N_GROUPS, 1, r), F32)
    for _ in range(TOPK_GROUPS):
        m = jnp.max(cur, axis=0, keepdims=True)
        pick = grp_idx == jnp.min(jnp.where(cur == m, grp_idx, float(N_GROUPS)), axis=0, keepdims=True)
        grp_sel = jnp.where(pick, 1.0, grp_sel)
        cur = jnp.where(pick, NEG_INF, cur)

    cur = jnp.where(jnp.broadcast_to(grp_sel, shape3) > 0.0, biased, NEG_INF)
    exp_idx = lax.broadcasted_iota(jnp.int32, shape3, 0).astype(F32) * GROUP_SIZE + in_grp
    chosen = jnp.zeros(shape3, F32)
    ids, wts = [], []
    for _ in range(TOP_K):
        m = _max_all(cur)
        first = -_max_all(-jnp.where(cur == m, exp_idx, float(N_EXPERTS)))
        pick = exp_idx == first
        chosen = jnp.where(pick, 1.0, chosen)
        cur = jnp.where(pick, NEG_INF, cur)
        ids.append(first.reshape(1, r))
        wts.append(_sum_all(jnp.where(pick, scores, 0.0)).reshape(1, r))
    ids = jnp.concatenate(ids, axis=0)
    wts = jnp.concatenate(wts, axis=0)
    gates = wts / jnp.sum(wts, axis=0, keepdims=True) * ROUTED_SCALE
    return ids, gates, chosen.reshape(N_EXPERTS, r), exp_idx


def _route_kernel(conv_ref, attn_ref, hn_ref, woc_ref, woa_ref, g1_ref, b1_ref,
                  wsg_ref, wsu_ref, wsd_ref, wrh_ref, wrl_ref, rb_ref,
                  xp_ref, base_ref, gates_ref, ek_ref, rk_ref, cnt_ref, before_ref):
    step = pl.program_id(0)
    tr = hn_ref.shape[0]
    mix = _dot(conv_ref[...], woc_ref[...]) + _dot(attn_ref[...], woa_ref[...])
    h1 = _layer_norm_rows(DN_ALPHA * hn_ref[...] + mix, g1_ref[...], b1_ref[...])
    xb = h1.astype(BF16)
    xp_ref[...] = _pack_bf16_pairs(h1)

    shared = jax.nn.silu(_dot(xb, wsg_ref[...])) * _dot(xb, wsu_ref[...])
    base_ref[...] = DN_ALPHA * h1 + _dot(shared.astype(BF16), wsd_ref[...])

    x_lo = (h1 - xb.astype(F32)).astype(BF16)
    logits_t = (_dot_nt(wrh_ref[...], xb) + _dot_nt(wrh_ref[...], x_lo) + _dot_nt(wrl_ref[...], xb))
    ids, gates, chosen, exp_idx = _router_topk(logits_t, rb_ref[...])
    padded = jnp.concatenate([gates, jnp.zeros((LANES - TOP_K, tr), F32)], axis=0)
    gates_ref[...] = padded.T
    ek_ref[...] = ids.astype(jnp.int32)

    @pl.when(step == 0)
    def _():
        cnt_ref[...] = jnp.zeros(cnt_ref.shape, F32)
        t_i = lax.broadcasted_iota(jnp.int32, (tr, tr), 0)
        t_j = lax.broadcasted_iota(jnp.int32, (tr, tr), 1)
        before_ref[...] = jnp.where(t_i < t_j, 1.0, 0.0).astype(BF16)

    chosen_b = chosen.astype(BF16)
    running = cnt_ref[...]
    rank = _dot(chosen_b, before_ref[...]) + jnp.concatenate([running] * (tr // LANES), axis=1)
    rank3 = rank.reshape(N_GROUPS, GROUP_SIZE, tr)
    rk = [_sum_all(jnp.where(exp_idx == ids[k:k + 1].reshape(1, 1, tr), rank3, 0.0)).reshape(1, tr)
          for k in range(TOP_K)]
    rk_ref[...] = jnp.concatenate(rk, axis=0).astype(jnp.int32)
    cnt_ref[...] = running + _dot(chosen_b, jnp.ones((tr, LANES), BF16))


def _route_call(conv, attn, hn, woc, woa, g1, b1, wsg, wsu, wsd, wrh, wrl, rbias):
    n, D = hn.shape
    tr = _pick_tile(n, (768, 512, 256))

    def row_spec(w):
        return pl.BlockSpec((tr, w), lambda i: (i, 0))

    def col_spec(r):
        return pl.BlockSpec((r, tr), lambda i: (0, i))

    def full(a):
        return pl.BlockSpec(a.shape, lambda i: (0,) * a.ndim)

    consts = [woc, woa, g1, b1, wsg, wsu, wsd, wrh, wrl, rbias]
    return pl.pallas_call(
        _route_kernel,
        grid=(n // tr,),
        in_specs=[row_spec(CONV_CH), row_spec(ATTN_W), row_spec(D)] + [full(a) for a in consts],
        out_specs=[row_spec(D // 2), row_spec(D), row_spec(LANES), col_spec(TOP_K), col_spec(TOP_K),
                   pl.BlockSpec((N_EXPERTS, LANES), lambda i: (0, 0))],
        out_shape=[jax.ShapeDtypeStruct((n, D // 2), jnp.int32),
                   jax.ShapeDtypeStruct((n, D), F32),
                   jax.ShapeDtypeStruct((n, LANES), F32),
                   jax.ShapeDtypeStruct((TOP_K, n), jnp.int32),
                   jax.ShapeDtypeStruct((TOP_K, n), jnp.int32),
                   jax.ShapeDtypeStruct((N_EXPERTS, LANES), F32)],
        scratch_shapes=[pltpu.VMEM((tr, tr), BF16)],
        compiler_params=pltpu.CompilerParams(
            dimension_semantics=("arbitrary",), vmem_limit_bytes=VMEM_LIMIT),
        name="route",
    )(conv, attn, hn, *consts)


def _plan_kernel(cnt_ref, ek_ref, rk_ref, slot_ref, blk_ref, *, n_blocks):
    tr = ek_ref.shape[1]
    counts = cnt_ref[...]
    blocks = jnp.floor((counts + (EXPERT_BLOCK - 1.0)) * _INV_EXPERT_BLOCK)
    first_blk = [jnp.zeros((1, LANES), F32)]
    for e in range(1, N_EXPERTS):
        first_blk.append(first_blk[-1] + blocks[e - 1:e])
    ek = ek_ref[...]
    seg = jnp.zeros(ek.shape, F32)
    for e in range(N_EXPERTS):
        start = first_blk[e] * EXPERT_BLOCK
        seg = jnp.where(ek == e, jnp.concatenate([start] * (tr // LANES), axis=1), seg)
    slot_ref[...] = seg.astype(jnp.int32) + rk_ref[...]

    end_blk = jnp.concatenate(first_blk, axis=0) + blocks
    w = blk_ref.shape[1]
    blk_idx = lax.broadcasted_iota(jnp.int32, (N_EXPERTS, w), 1).astype(F32)
    done = jnp.where(jnp.concatenate([end_blk] * (w // LANES), axis=1) <= blk_idx, 1.0, 0.0)
    owner = jnp.minimum(jnp.sum(done, axis=0, keepdims=True), N_EXPERTS - 1.0)
    used = jnp.concatenate([end_blk[N_EXPERTS - 1:]] * (w // LANES), axis=1)
    lane = lax.broadcasted_iota(jnp.int32, (1, w), 1)
    table = jnp.where(lane == n_blocks, used, owner)
    blk_ref[...] = jnp.broadcast_to(table, blk_ref.shape).astype(jnp.int32)


def _plan_call(cnt, ek, rk, n_blocks):
    k, n = ek.shape
    tr = _pick_tile(n, (768, 512, 256))
    w = -(-(n_blocks + 1) // LANES) * LANES
    col = pl.BlockSpec((k, tr), lambda i: (0, i))
    return pl.pallas_call(
        functools.partial(_plan_kernel, n_blocks=n_blocks),
        grid=(n // tr,),
        in_specs=[pl.BlockSpec(cnt.shape, lambda i: (0, 0)), col, col],
        out_specs=[col, pl.BlockSpec((SUBLANES, w), lambda i: (0, 0))],
        out_shape=[jax.ShapeDtypeStruct((k, n), jnp.int32), jax.ShapeDtypeStruct((SUBLANES, w), jnp.int32)],
        compiler_params=pltpu.CompilerParams(dimension_semantics=("arbitrary",)),
        name="plan",
    )(cnt, ek, rk)


def _sc_workers():
    info = plsc.get_sparse_core_info()
    return info.num_cores, info.num_subcores


def _sc_scatter_rows(src, slot, n_out):
    n_src, d = src.shape
    n_choices = slot.shape[0]
    rows = SC_SCATTER_ROWS
    n_chunks = n_src // rows
    slots = slot.reshape(n_choices * n_chunks, rows)
    n_cores, n_sub = _sc_workers()
    n_workers = n_cores * n_sub

    def body(src_hbm, slot_hbm, out_hbm, idx_v, rows_v):
        wid = lax.axis_index("s") * n_cores + lax.axis_index("c")

        @pl.loop(0, -(-n_chunks // n_workers))
        def _(j):
            u = j * n_workers + wid

            @pl.when(u < n_chunks)
            def _():
                pltpu.sync_copy(src_hbm.at[pl.ds(u * rows, rows)], rows_v)
                for k in range(n_choices):
                    pltpu.sync_copy(slot_hbm.at[pl.ds(k * n_chunks + u, 1)], idx_v)
                    pltpu.sync_copy(rows_v, out_hbm.at[idx_v.at[0]])

    return pl.kernel(
        body, out_type=jax.ShapeDtypeStruct((n_out, d), src.dtype),
        mesh=plsc.VectorSubcoreMesh(core_axis_name="c", subcore_axis_name="s"),
        scratch_types=[pltpu.VMEM((1, rows), jnp.int32), pltpu.VMEM((rows, d), src.dtype)],
        name="dispatch_rows",
    )(src, slots)


def _sc_gather_rows(table, slots):
    units = slots.shape[0]
    d = table.shape[1]
    n_cores, n_sub = _sc_workers()
    n_workers = n_cores * n_sub
    assert slots.shape[1] == SC_ROWS

    def body(table_hbm, slot_hbm, out_hbm, idx_v, rows_v):
        wid = lax.axis_index("s") * n_cores + lax.axis_index("c")

        @pl.loop(0, -(-units // n_workers))
        def _(j):
            u = j * n_workers + wid

            @pl.when(u < units)
            def _():
                pltpu.sync_copy(slot_hbm.at[pl.ds(u, 1)], idx_v)
                pltpu.sync_copy(table_hbm.at[idx_v.at[0]], rows_v)
                pltpu.sync_copy(rows_v, out_hbm.at[pl.ds(u * SC_ROWS, SC_ROWS)])

    return pl.kernel(
        body, out_type=jax.ShapeDtypeStruct((units * SC_ROWS, d), table.dtype),
        mesh=plsc.VectorSubcoreMesh(core_axis_name="c", subcore_axis_name="s"),
        scratch_types=[pltpu.VMEM((1, SC_ROWS), jnp.int32), pltpu.VMEM((SC_ROWS, d), table.dtype)],
        name="collect_rows",
    )(table, slots)


def _expert_kernel(blk_ref, xs_ref, *rest, per_share):
    ys_ref = rest[-1]
    b = pl.program_id(0)
    in_use = b < blk_ref[pl.num_programs(0)]
    share = blk_ref[b] // per_share
    for s in range(len(rest) // 3):
        wg_ref, wu_ref, wd_ref = rest[3 * s:3 * s + 3]

        @pl.when(jnp.logical_and(in_use, share == s))
        def _():
            for r0 in range(0, EXPERT_BLOCK, EXPERT_SUB):
                x = _unpack_bf16_pairs(xs_ref[r0:r0 + EXPERT_SUB, :]).astype(BF16)
                hdn = (jax.nn.silu(_dot(x, wg_ref[0])) * _dot(x, wu_ref[0])).astype(BF16)
                ys_ref[r0:r0 + EXPERT_SUB, :] = _pack_bf16_pairs(_dot(hdn, wd_ref[0]))


def _expert_call(blk_exp, xs, weight_shares):
    p, half = xs.shape
    per_share = weight_shares[0][0].shape[0]
    n_blocks = p // EXPERT_BLOCK
    assert blk_exp.shape == (n_blocks + 1,)

    def row_block(b, blk):
        return (jnp.minimum(b, blk[n_blocks] - 1), 0)

    def weight_spec(a, s):
        return pl.BlockSpec((1,) + a.shape[1:],
                            lambda b, blk: (jnp.clip(blk[b] - s * per_share, 0, per_share - 1), 0, 0))

    rows = pl.BlockSpec((EXPERT_BLOCK, half), row_block)
    grid_spec = pltpu.PrefetchScalarGridSpec(
        num_scalar_prefetch=1,
        grid=(n_blocks,),
        in_specs=[rows] + [weight_spec(a, s) for s, share in enumerate(weight_shares) for a in share],
        out_specs=rows,
    )
    return pl.pallas_call(
        functools.partial(_expert_kernel, per_share=per_share),
        grid_spec=grid_spec,
        out_shape=jax.ShapeDtypeStruct((p, half), jnp.int32),
        compiler_params=pltpu.CompilerParams(
            dimension_semantics=("arbitrary",), vmem_limit_bytes=VMEM_LIMIT),
        name="experts",
    )(blk_exp, xs, *[a for share in weight_shares for a in share])


def _combine_kernel(g_ref, gates_ref, base_ref, g2_ref, b2_ref, *rest):
    o_ref = rest[-1]
    gates = gates_ref[...]
    acc = base_ref[...]
    for k in range(g_ref.shape[0]):
        acc = acc + _unpack_bf16_pairs(g_ref[k]) * gates[:, k:k + 1]
    o_ref[0] = _layer_norm_rows(acc, g2_ref[...], b2_ref[...])


def _combine_call(g, gates, base, g2, b2, result, batch, n_batch):
    k, n, half = g.shape
    D = base.shape[1]
    tr = SEQ_ALIGN
    seq = n - tr

    def row_spec(w):
        return pl.BlockSpec((tr, w), lambda i: (i, 0))

    vec = pl.BlockSpec((1, D), lambda i: (0, 0))
    in_specs = [pl.BlockSpec((k, tr, half), lambda i: (0, i, 0)), row_spec(LANES), row_spec(D), vec, vec]
    args = [g, gates, base, g2, b2]
    aliases = {}
    if result is not None:
        in_specs.append(pl.BlockSpec(memory_space=pl.ANY))
        args.append(result)
        aliases = {len(args) - 1: 0}
    return pl.pallas_call(
        _combine_kernel,
        grid=(n // tr,),
        in_specs=in_specs,
        out_specs=pl.BlockSpec((1, tr, D), lambda i: (batch, jnp.maximum(i - 1, 0), 0)),
        out_shape=jax.ShapeDtypeStruct((n_batch, seq, D), F32),
        input_output_aliases=aliases,
        compiler_params=pltpu.CompilerParams(
            dimension_semantics=("arbitrary",), vmem_limit_bytes=VMEM_LIMIT),
        name="combine",
    )(*args)


def _rope_tables(tp, lead):
    pos = jnp.arange(tp, dtype=F32) - lead
    inv = jnp.power(ROPE_THETA, -2.0 * jnp.arange(ROPE_HALF, dtype=F32) / ROPE_DIM)
    ang = pos[:, None] * inv[None, :]
    cos, sin = jnp.cos(ang), jnp.sin(ang)
    zeros = jnp.zeros((tp, HEAD_DIM - ROPE_DIM), F32)
    zh = jnp.zeros((tp, ROPE_HALF), F32)
    c64 = jnp.concatenate([cos, cos, jnp.ones_like(zeros)], axis=1)
    s1_64 = jnp.concatenate([-sin, zh, zeros], axis=1)
    s2_64 = jnp.concatenate([zh, sin, zeros], axis=1)
    rep = LANES // HEAD_DIM
    return (jnp.tile(c64, (1, rep)), jnp.tile(s1_64, (1, rep)), jnp.tile(s2_64, (1, rep)),
            cos.T, sin.T)


def kernel(x, meta_tokens, ln_emb_g, ln_emb_b, w_in, conv_w, conv_b, ln_conv_g, ln_conv_b, ln_kidx_g, ln_kidx_b, w_out, ln1_g, ln1_b, w_router, router_bias, w_gate, w_up, w_down, ws_gate, ws_up, ws_down, ln2_g, ln2_b):
    B, seq, D = x.shape
    assert w_in.shape[0] == DEPTH
    assert seq % SEQ_ALIGN == 0 and meta_tokens.shape[0] == N_META <= SEQ_ALIGN
    k_sel = min(INDEX_TOPK, seq // 4)
    tp = seq + SEQ_ALIGN
    n_pad = SEQ_ALIGN - N_META

    def row(a):
        return a.reshape(1, -1).astype(F32)

    w = w_in[0]
    o = 0
    parts = []
    for width in (CONV_CH, CONV_CH, ATTN_W, ATTN_W, ATTN_W, IDX_HEADS * IDX_DIM, IDX_DIM, IDX_HEADS):
        parts.append(w[:, o:o + width])
        o += width
    wa, wgl, wq, wk, wv, wqi, wki, wwi = parts
    wwi_t = jnp.concatenate([wwi.T, jnp.zeros((BF16_SUBLANES - IDX_HEADS, D), w.dtype)], axis=0)
    weights = (jnp.concatenate([wa, wgl], axis=1).astype(BF16), wq.T.astype(BF16), wk.astype(BF16),
               wv.T.astype(BF16), wqi.T.astype(BF16), jnp.concatenate([wki, wki], axis=1).astype(BF16),
               wwi_t.astype(BF16))

    def twice(a):
        return row(jnp.concatenate([a, a]))

    tabs = _rope_tables(tp, n_pad)
    inproj_consts = (row(ln_emb_g), row(ln_emb_b), weights, conv_w[0].astype(F32), row(conv_b[0]),
                     row(ln_conv_g[0]), row(ln_conv_b[0]), twice(ln_kidx_g[0]), twice(ln_kidx_b[0]))
    wr_t = w_router[0].T.astype(F32)
    wr_hi = wr_t.astype(BF16)
    wr_lo = (wr_t - wr_hi.astype(F32)).astype(BF16)
    route_consts = (w_out[0][:CONV_CH].astype(BF16), w_out[0][CONV_CH:].astype(BF16), row(ln1_g[0]),
                    row(ln1_b[0]), ws_gate[0].astype(BF16), ws_up[0].astype(BF16), ws_down[0].astype(BF16),
                    wr_hi, wr_lo, router_bias[0].reshape(-1, 1).astype(F32))
    meta = meta_tokens.astype(F32)
    n_blocks = -(-tp * TOP_K // EXPERT_BLOCK) + N_EXPERTS

    expert_w = (w_gate[0], w_up[0], w_down[0])
    weight_shares = []
    routed = []
    for b in range(B):
        hn, conv, qt, k, vt, qit, ki, wit = _inproj_call(x, b, meta, tabs, *inproj_consts)
        attn, *share = _dsa_call(qt, qit, wit, k, ki, vt, k_sel, n_pad, expert_w, b, B)
        weight_shares.append(share)
        xp, base, gates, ek, rk, cnt = _route_call(conv[0], attn[0], hn[0], *route_consts)

        slot, blk = _plan_call(cnt, ek, rk, n_blocks)
        slots = slot.reshape(TOP_K * tp // SC_ROWS, SC_ROWS)
        xs = _sc_scatter_rows(xp, slot, n_blocks * EXPERT_BLOCK)
        routed.append((xs, blk, slots, gates, base))

    result = None
    for b, (xs, blk, slots, gates, base) in enumerate(routed):
        ys = _expert_call(blk[0, :n_blocks + 1], xs, weight_shares)
        picked = _sc_gather_rows(ys, slots).reshape(TOP_K, tp, D // 2)
        result = _combine_call(picked, gates, base, row(ln2_g[0]), row(ln2_b[0]), result, b, B)
    return result
```

```python
import functools

import numpy as np
import jax
import jax.numpy as jnp
from jax import lax
from jax.experimental import pallas as pl
from jax.experimental.pallas import tpu as pltpu
from jax.experimental.pallas import tpu_sc as plsc

N_META = 16
CONV_CH = 512
CONV_WIDTH = 31
N_HEADS = 8
HEAD_DIM = 64
ATTN_W = N_HEADS * HEAD_DIM
IDX_HEADS = 8
IDX_DIM = 64
INDEX_TOPK = 256
ROPE_DIM = HEAD_DIM // 4
ROPE_HALF = ROPE_DIM // 2
ROPE_THETA = 500000.0
N_EXPERTS = 64
TOP_K = 8
N_GROUPS = 8
GROUP_SIZE = N_EXPERTS // N_GROUPS
TOPK_GROUPS = 4
ROUTED_SCALE = 2.5
LN_EPS = 1e-5
DEPTH = 1
DN_ALPHA = (2.0 * DEPTH) ** 0.25

LANES = 128
Q_TILE = 256
SUBLANES = 8
BF16_SUBLANES = 16
EXPERT_BLOCK = 1152
EXPERT_SUB = 384
EXPERT_LOOKAHEAD = 2
EXPERT_WEIGHT_SLOTS = EXPERT_LOOKAHEAD + 1
_INV_EXPERT_BLOCK = float(np.nextafter(np.float32(1.0 / EXPERT_BLOCK), np.float32(1.0)))
SC_ROWS = 128
SC_SCATTER_ROWS = 64
K_CHUNK = Q_TILE
K_SUB = 128
COUNT_UNROLL = 4
SEQ_ALIGN = 256
CONV_HALO = 32
CONV_ROWS = 128
VMEM_LIMIT = 56 * 1024 * 1024

F32 = jnp.float32
BF16 = jnp.bfloat16
COARSE = jnp.bfloat16
NEG_INF = float("-inf")
INT_MIN = -2 ** 31
KEY_NEG_INF = -2139095041
LOG2_E = 1.4426950408889634


def _dot(a, b):
    return jnp.dot(a, b, preferred_element_type=F32)


def _dot_nt(a, b):
    return lax.dot_general(a, b, (((1,), (1,)), ((), ())), preferred_element_type=F32)


def _layer_norm_rows(x, g, b):
    mu = jnp.mean(x, axis=-1, keepdims=True)
    xc = x - mu
    var = jnp.mean(xc * xc, axis=-1, keepdims=True)
    return xc * lax.rsqrt(var + LN_EPS) * g + b


def _pick_tile(n, candidates):
    for c in candidates:
        if n % c == 0:
            return c
    raise ValueError(f"no tile for {n}")


def _rope_rows(x, c_tab, s1_tab, s2_tab):
    outs = []
    for j in range(x.shape[1] // LANES):
        xs = x[:, j * LANES:(j + 1) * LANES]
        up = pltpu.roll(xs, LANES - ROPE_HALF, axis=1)
        dn = pltpu.roll(xs, ROPE_HALF, axis=1)
        outs.append(xs * c_tab + up * s1_tab + dn * s2_tab)
    return jnp.concatenate(outs, axis=1)


def _rope_cols(xt, cos_t, sin_t, heads):
    r = xt.shape[1]
    x3 = xt.reshape(heads, HEAD_DIM, r)
    x1 = x3[:, 0:ROPE_HALF, :]
    x2 = x3[:, ROPE_HALF:ROPE_DIM, :]
    n1 = x1 * cos_t - x2 * sin_t
    n2 = x2 * cos_t + x1 * sin_t
    out = jnp.concatenate([n1, n2, x3[:, ROPE_DIM:, :]], axis=1)
    return out.reshape(heads * HEAD_DIM, r)


def _inproj_kernel(x_ref, meta_ref, ctab_ref, s1tab_ref, s2tab_ref, cost_ref, sint_ref,
                   lng_ref, lnb_ref, wag_ref, wqt_ref, wk_ref, wvt_ref, wqit_ref, wki_ref, wwit_ref,
                   cw_ref, cb_ref, lncg_ref, lncb_ref, lnkg_ref, lnkb_ref,
                   hn_ref, conv_ref, qt_ref, k_ref, vt_ref, qit_ref, ki_ref, wit_ref,
                   ubuf_ref, wbuf_ref, cbuf_ref):
    t = pl.program_id(1)
    tr = x_ref.shape[1]
    n_meta = meta_ref.shape[0]

    @pl.when(t == 0)
    def _():
        ubuf_ref[0:CONV_HALO, :] = jnp.zeros((CONV_HALO, CONV_CH), F32)

    first = jnp.concatenate([jnp.zeros((tr - n_meta, x_ref.shape[2]), F32), meta_ref[...]], axis=0)
    h = jnp.where(t == 0, first, x_ref[0])

    hn = _layer_norm_rows(h, lng_ref[...], lnb_ref[...])
    hn_ref[0] = hn
    xb = hn.astype(BF16)

    ag = _dot(xb, wag_ref[...])
    u = ag[:, :CONV_CH] * jax.nn.sigmoid(ag[:, CONV_CH:])
    row = lax.broadcasted_iota(jnp.int32, (tr, CONV_CH), 0)
    u = jnp.where(jnp.logical_or(t > 0, row >= tr - n_meta), u, 0.0)
    ubuf_ref[CONV_HALO:CONV_HALO + tr, :] = u
    base = CONV_HALO - (CONV_WIDTH - 1)
    for r in range(SUBLANES):
        rows = max(o for o in range(base, base + CONV_WIDTH) if o % SUBLANES == r) - r + tr
        wbuf_ref[r, 0:rows, :] = ubuf_ref[r:r + rows, :]
    ubuf_ref[0:CONV_HALO, :] = ubuf_ref[tr:tr + CONV_HALO, :]
    for lo in range(0, CONV_CH, LANES):
        for r0 in range(0, tr, CONV_ROWS):
            acc = jnp.zeros((CONV_ROWS, LANES), F32)
            for o in range(base, base + CONV_WIDTH):
                r = o % SUBLANES
                acc = acc + (cw_ref[o - base:o - base + 1, lo:lo + LANES]
                             * wbuf_ref[r, o - r + r0:o - r + r0 + CONV_ROWS, lo:lo + LANES])
            cbuf_ref[r0:r0 + CONV_ROWS, lo:lo + LANES] = acc
    c = _layer_norm_rows(cbuf_ref[...] + cb_ref[...], lncg_ref[...], lncb_ref[...])
    conv_ref[0] = (c * jax.nn.sigmoid(c)).astype(conv_ref.dtype)

    ctab, s1tab, s2tab = ctab_ref[...], s1tab_ref[...], s2tab_ref[...]
    cos_t, sin_t = cost_ref[...], sint_ref[...]

    qt = _rope_cols(_dot_nt(wqt_ref[...], xb), cos_t, sin_t, N_HEADS)
    qt_ref[0] = (qt * (HEAD_DIM ** -0.5 * LOG2_E)).astype(qt_ref.dtype)
    k = _rope_rows(_dot(xb, wk_ref[...]), ctab, s1tab, s2tab)
    k_ref[0] = k.astype(k_ref.dtype)
    vt_ref[0] = _dot_nt(wvt_ref[...], xb).astype(vt_ref.dtype)

    qit = _rope_cols(_dot_nt(wqit_ref[...], xb), cos_t, sin_t, IDX_HEADS)
    qit_ref[0] = qit.astype(qit_ref.dtype)
    ki = _layer_norm_rows(_dot(xb, wki_ref[...]), lnkg_ref[...], lnkb_ref[...])
    ki_ref[0] = _rope_rows(ki, ctab, s1tab, s2tab).astype(ki_ref.dtype)
    wit = _dot_nt(wwit_ref[...], xb) * (IDX_HEADS ** -0.5)
    wit_ref[0] = wit[:IDX_HEADS]


def _inproj_call(x, batch, meta, tabs, ln_g, ln_b, weights, conv_w, conv_b, lnc_g, lnc_b, lnk_g, lnk_b):
    _, seq, D = x.shape
    B = 1
    tr = SEQ_ALIGN
    tp = seq + tr
    nt = tp // tr
    ctab, s1tab, s2tab, cos_t, sin_t = tabs

    def row_spec(w):
        return pl.BlockSpec((1, tr, w), lambda b, t: (b, t, 0))

    def col_spec(r):
        return pl.BlockSpec((1, r, tr), lambda b, t: (b, 0, t))

    def full(a):
        return pl.BlockSpec(a.shape, lambda b, t: (0,) * a.ndim)

    tab_row = pl.BlockSpec((tr, LANES), lambda b, t: (t, 0))
    tab_col = pl.BlockSpec((ROPE_HALF, tr), lambda b, t: (0, t))
    consts = [ln_g, ln_b, *weights, conv_w, conv_b, lnc_g, lnc_b, lnk_g, lnk_b]
    out_shape = [
        jax.ShapeDtypeStruct((B, tp, D), F32),
        jax.ShapeDtypeStruct((B, tp, CONV_CH), BF16),
        jax.ShapeDtypeStruct((B, ATTN_W, tp), BF16),
        jax.ShapeDtypeStruct((B, tp, ATTN_W), BF16),
        jax.ShapeDtypeStruct((B, ATTN_W, tp), BF16),
        jax.ShapeDtypeStruct((B, IDX_HEADS * IDX_DIM, tp), BF16),
        jax.ShapeDtypeStruct((B, tp, 2 * IDX_DIM), BF16),
        jax.ShapeDtypeStruct((B, IDX_HEADS, tp), F32),
    ]
    out_specs = [row_spec(D), row_spec(CONV_CH), col_spec(ATTN_W), row_spec(ATTN_W), col_spec(ATTN_W),
                 col_spec(IDX_HEADS * IDX_DIM), row_spec(2 * IDX_DIM), col_spec(IDX_HEADS)]
    return pl.pallas_call(
        _inproj_kernel,
        grid=(B, nt),
        in_specs=[pl.BlockSpec((1, tr, D), lambda b, t: (batch, jnp.maximum(t - 1, 0), 0)), full(meta),
                  tab_row, tab_row, tab_row, tab_col, tab_col] + [full(a) for a in consts],
        out_specs=out_specs,
        out_shape=out_shape,
        scratch_shapes=[pltpu.VMEM((CONV_HALO + tr, CONV_CH), F32),
                        pltpu.VMEM((SUBLANES, CONV_HALO + tr, CONV_CH), F32),
                        pltpu.VMEM((tr, CONV_CH), F32)],
        compiler_params=pltpu.CompilerParams(
            dimension_semantics=("arbitrary", "arbitrary"), vmem_limit_bytes=VMEM_LIMIT),
        name="inproj",
    )(x, meta, ctab, s1tab, s2tab, cos_t, sin_t, *consts)


def _key_to_float(key):
    bits = jnp.where(key >= 0, key, key ^ jnp.int32(0x7FFFFFFF))
    f = pltpu.bitcast(bits, F32)
    return jnp.where(key < jnp.int32(KEY_NEG_INF), NEG_INF, f)


def _tree(parts, op):
    parts = list(parts)
    while len(parts) > 1:
        nxt = [op(parts[j], parts[j + 1]) for j in range(0, len(parts) - 1, 2)]
        if len(parts) % 2:
            nxt.append(parts[-1])
        parts = nxt
    return parts[0]


def _fold_rows(x, rows, op):
    return _tree([x[j * rows:(j + 1) * rows] for j in range(x.shape[0] // rows)], op)


def _dsa_kernel(qt_ref, qit_ref, wit_ref, k_ref, ki_ref, vt_ref, *rest, k_sel, n_pad, n_convert):
    f32_w, rest = rest[:n_convert], rest[n_convert:]
    o_ref, bf16_w, rest = rest[0], rest[1:1 + n_convert], rest[1 + n_convert:]
    sc_ref, sh_ref, qm_ref, qim_ref, m_ref, alpha_ref, shift_ref, lg_ref, acc_ref = rest

    for src_ref, dst_ref in zip(f32_w, bf16_w):
        dst_ref[...] = src_ref[...].astype(BF16)

    i = pl.program_id(1)
    tq = qt_ref.shape[2]
    n_chunks = (i * tq + tq + K_CHUNK - 1) // K_CHUNK
    v_rows = HEAD_DIM + BF16_SUBLANES

    def causal_mask(k0, rows=K_CHUNK):
        kpos = k0 + lax.broadcasted_iota(jnp.int32, (rows, tq), 0)
        return kpos <= i * tq + lax.broadcasted_iota(jnp.int32, (rows, tq), 1)

    def rows8(x):
        return jnp.broadcast_to(x, (SUBLANES, tq))

    def tiles(x):
        return x.reshape(x.shape[0] // SUBLANES, SUBLANES, tq)

    def head_slab(ref, h):
        slab = ref[0, (h // 2) * LANES:(h // 2 + 1) * LANES, :]
        zeros = jnp.zeros((HEAD_DIM, tq), slab.dtype)
        if h % 2 == 0:
            return jnp.concatenate([slab[:HEAD_DIM], zeros], axis=0)
        return jnp.concatenate([zeros, slab[HEAD_DIM:]], axis=0)

    for h in range(N_HEADS):
        qm_ref[h] = head_slab(qt_ref, h)
    for h in range(IDX_HEADS):
        qim_ref[h] = head_slab(qit_ref, h)
    wit = wit_ref[0]
    w_heads = [rows8(wit[h:h + 1] * (IDX_DIM ** -0.5)) for h in range(IDX_HEADS)]

    def for_each_chunk(chunk_fn, unroll, count=n_chunks):
        def group(j, carry):
            for u in range(unroll):
                chunk_fn(unroll * j + u, u % 2)
            return carry

        def single(c, carry):
            chunk_fn(c, 0)
            return carry

        n_groups = count // unroll
        lax.fori_loop(0, n_groups, group, 0)
        lax.fori_loop(n_groups * unroll, count, single, 0)

    def diagonal_visible(first_row, rows):
        r = first_row + lax.broadcasted_iota(jnp.int32, (rows, tq), 0)
        return r <= lax.broadcasted_iota(jnp.int32, (rows, tq), 1)

    def score_chunk(c, slot, diagonal=False):
        del slot
        for s in range(K_CHUNK // K_SUB):
            k0 = pl.multiple_of(c * K_CHUNK + s * K_SUB, K_SUB)
            kic = ki_ref[0, pl.ds(k0, K_SUB), :]
            acc = jnp.zeros((K_SUB // SUBLANES, SUBLANES, tq), F32)
            for h in range(IDX_HEADS):
                acc = acc + w_heads[h][None] * jnp.maximum(tiles(_dot(kic, qim_ref[h])), 0.0)
            acc = acc.reshape(K_SUB, tq)
            if diagonal:
                acc = jnp.where(diagonal_visible(s * K_SUB, K_SUB), acc, NEG_INF)
            sc_ref[pl.ds(k0, K_SUB), :] = acc
            hi_bits = pltpu.bitcast(acc, jnp.int32) & jnp.int32(-65536)
            sh_ref[pl.ds(k0, K_SUB), :] = pltpu.bitcast(hi_bits, F32).astype(COARSE)

    for_each_chunk(score_chunk, 4, n_chunks - 1)
    score_chunk(n_chunks - 1, 0, diagonal=True)

    def drop_padding_keys():
        sh_ref[0:n_pad, :] = jnp.full((n_pad, tq), NEG_INF, COARSE)

    sc_ref[0:n_pad, :] = jnp.full((n_pad, tq), NEG_INF, F32)
    drop_padding_keys()

    def count_all(ref, thr_tile, preds):
        rows = thr_tile.shape[0]
        one, zero = jnp.ones((), ref.dtype), jnp.zeros((), ref.dtype)

        def body(c, cnts):
            k0 = pl.multiple_of(c * K_CHUNK, K_CHUNK)
            s = ref[pl.ds(k0, K_CHUNK), :]
            out = []
            for cnt, p in zip(cnts, preds):
                hits = [jnp.where(p(s[j * rows:(j + 1) * rows], thr_tile), one, zero)
                        for j in range(K_CHUNK // rows)]
                out.append(cnt + _tree(hits, jnp.add).astype(F32))
            return tuple(out)

        def group(j, cnts):
            for u in range(COUNT_UNROLL):
                cnts = body(COUNT_UNROLL * j + u, cnts)
            return cnts

        init = tuple(jnp.zeros((rows, tq), F32) for _ in preds)
        n_groups = n_chunks // COUNT_UNROLL
        cnts = lax.fori_loop(0, n_groups, group, init)
        cnts = lax.fori_loop(n_groups * COUNT_UNROLL, n_chunks, body, cnts)
        return [rows8(jnp.sum(cnt, axis=0, keepdims=True)) for cnt in cnts]

    def search_body(it, tkey, coarse):
        cand = tkey + lax.shift_left(jnp.int32(1), 31 - it)
        cf = _key_to_float(cand)
        if coarse:
            cf = pltpu.bitcast(pltpu.bitcast(cf, jnp.int32) & jnp.int32(-65536), F32).astype(COARSE)
            cf = jnp.concatenate([cf] * (BF16_SUBLANES // SUBLANES), axis=0)
        cnt, = count_all(sh_ref if coarse else sc_ref, cf, [lambda s, t: s >= t])
        return jnp.where(cnt >= k_sel, cand, tkey)

    tkey = jnp.full((SUBLANES, tq), INT_MIN, jnp.int32)
    tkey = lax.fori_loop(0, 16, functools.partial(search_body, coarse=True), tkey)
    tkey = lax.fori_loop(16, 32, functools.partial(search_body, coarse=False), tkey)
    thr = _key_to_float(tkey)

    n_ge, n_gt = count_all(sc_ref, thr, [lambda s, t: s >= t, lambda s, t: s > t])
    need = k_sel - n_gt
    has_ties = jnp.max(n_ge) > k_sel

    @pl.when(jnp.logical_not(has_ties))
    def _():
        def mask_chunk(c, slot, diagonal=False):
            del slot
            k0 = pl.multiple_of(c * K_CHUNK, K_CHUNK)
            s = sc_ref[pl.ds(k0, K_CHUNK), :]
            bias = jnp.where(tiles(s) >= thr[None], 0.0, NEG_INF).reshape(K_CHUNK, tq)
            if diagonal:
                bias = jnp.where(diagonal_visible(0, K_CHUNK), bias, NEG_INF)
            sh_ref[pl.ds(k0, K_CHUNK), :] = bias.astype(COARSE)

        for_each_chunk(mask_chunk, 4, n_chunks - 1)
        mask_chunk(n_chunks - 1, 0, diagonal=True)

    @pl.when(has_ties)
    def _():
        r_i = lax.broadcasted_iota(jnp.int32, (K_CHUNK, K_CHUNK), 0)
        c_i = lax.broadcasted_iota(jnp.int32, (K_CHUNK, K_CHUNK), 1)
        lower = jnp.where(c_i <= r_i, 1.0, 0.0).astype(BF16)
        thr_row, need_row = thr[0:1], need[0:1]

        def body(c, seen):
            k0 = pl.multiple_of(c * K_CHUNK, K_CHUNK)
            s = sc_ref[pl.ds(k0, K_CHUNK), :]
            eq = jnp.where(s == thr_row, 1.0, 0.0)
            rank = _dot(lower, eq.astype(BF16)) + seen
            keep_tie = jnp.where(rank <= need_row, eq, 0.0)
            sel = jnp.where(s > thr_row, 1.0, keep_tie)
            sh_ref[pl.ds(k0, K_CHUNK), :] = jnp.where(
                sel > 0.0, jnp.where(causal_mask(k0), 0.0, NEG_INF), NEG_INF).astype(COARSE)
            return rank[K_CHUNK - 1:K_CHUNK, :]
        lax.fori_loop(0, n_chunks, body, jnp.zeros((1, tq), F32))

    drop_padding_keys()

    m_ref[...] = jnp.full(m_ref.shape, NEG_INF, F32)
    acc_ref[...] = jnp.zeros(acc_ref.shape, F32)
    ones_rows = jnp.ones((BF16_SUBLANES, K_CHUNK), BF16)

    def logits_stage(c, slot):
        k0 = pl.multiple_of(c * K_CHUNK, K_CHUNK)
        for h in range(N_HEADS):
            pair = slice((h // 2) * LANES, (h // 2 + 1) * LANES)
            cmax = []
            for s in range(K_CHUNK // K_SUB):
                rows = pl.ds(pl.multiple_of(k0 + s * K_SUB, K_SUB), K_SUB)
                lg = _dot(k_ref[0, rows, pair], qm_ref[h]).astype(BF16) + sh_ref[rows, :].astype(BF16)
                lg_ref[slot, h, s * K_SUB:(s + 1) * K_SUB, :] = lg
                cmax.append(_fold_rows(lg, BF16_SUBLANES, jnp.maximum))
            cmax = _tree(cmax, jnp.maximum).astype(F32)
            m_old = m_ref[h]
            m_new = jnp.maximum(m_old, rows8(jnp.max(cmax, axis=0, keepdims=True)))
            m_safe = jnp.where(m_new == NEG_INF, 0.0, m_new)
            alpha_ref[slot, h] = jnp.exp2(m_old - m_safe)
            shift_ref[slot, h] = m_safe
            m_ref[h] = m_new

    def values_stage(c, slot):
        k0 = pl.multiple_of(c * K_CHUNK, K_CHUNK)
        for h in range(N_HEADS):
            m_safe = shift_ref[slot, h].astype(BF16)
            m_tile = jnp.concatenate([m_safe] * (BF16_SUBLANES // SUBLANES), axis=0)
            lg = lg_ref[slot, h].reshape(K_CHUNK // BF16_SUBLANES, BF16_SUBLANES, tq)
            p = jnp.exp2(lg - m_tile[None]).reshape(K_CHUNK, tq)
            v_aug = jnp.concatenate(
                [vt_ref[0, h * HEAD_DIM:(h + 1) * HEAD_DIM, pl.ds(k0, K_CHUNK)], ones_rows], axis=0)
            pv = _dot(v_aug, p)
            acc = acc_ref[h].reshape(v_rows // SUBLANES, SUBLANES, tq) * alpha_ref[slot, h][None]
            acc_ref[h] = acc.reshape(v_rows, tq) + pv

    def attn_chunk(c, slot):
        logits_stage(c, slot)
        values_stage(c, slot)

    for_each_chunk(attn_chunk, 4)

    for pair in range(N_HEADS // 2):
        halves = []
        for h in (2 * pair, 2 * pair + 1):
            a = acc_ref[h]
            total = a[HEAD_DIM:HEAD_DIM + 1]
            halves.append(a[:HEAD_DIM] / jnp.where(total == 0.0, 1.0, total))
        out_t = jnp.concatenate(halves, axis=0)
        o_ref[0, :, pair * LANES:(pair + 1) * LANES] = out_t.T.astype(o_ref.dtype)


def _dsa_call(qt, qit, wit, k, ki, vt, k_sel, n_pad, expert_w, part, n_parts):
    B, tp, _ = k.shape
    nq = tp // Q_TILE
    n_conv = expert_w[0].shape[0] // n_parts
    assert n_conv * n_parts == expert_w[0].shape[0]
    per_step = next(d for d in range(1, n_conv + 1) if n_conv % d == 0 and n_conv // d <= nq)
    n_steps = n_conv // per_step

    def q_cols(r):
        return pl.BlockSpec((1, r, Q_TILE), lambda b, i: (b, 0, i))

    def per_batch(a):
        return pl.BlockSpec((1,) + a.shape[1:], lambda b, i: (b, 0, 0), pipeline_mode=pl.Buffered(1))

    def expert_spec(a, first_block):
        return pl.BlockSpec((per_step,) + a.shape[1:],
                            lambda b, i: (first_block + jnp.minimum(i, n_steps - 1), 0, 0))

    in_specs = [q_cols(ATTN_W), q_cols(IDX_HEADS * IDX_DIM), q_cols(IDX_HEADS),
                per_batch(k), per_batch(ki), per_batch(vt)]
    in_specs += [expert_spec(a, part * n_steps) for a in expert_w]

    return pl.pallas_call(
        functools.partial(_dsa_kernel, k_sel=k_sel, n_pad=n_pad, n_convert=len(expert_w)),
        grid=(B, nq),
        in_specs=in_specs,
        out_specs=[pl.BlockSpec((1, Q_TILE, ATTN_W), lambda b, i: (b, i, 0))]
        + [expert_spec(a, 0) for a in expert_w],
        out_shape=[jax.ShapeDtypeStruct((B, tp, ATTN_W), BF16)]
        + [jax.ShapeDtypeStruct((n_conv,) + a.shape[1:], BF16) for a in expert_w],
        scratch_shapes=[pltpu.VMEM((tp, Q_TILE), F32),
                        pltpu.VMEM((tp, Q_TILE), COARSE),
                        pltpu.VMEM((N_HEADS, LANES, Q_TILE), BF16),
                        pltpu.VMEM((IDX_HEADS, LANES, Q_TILE), BF16),
                        pltpu.VMEM((N_HEADS, SUBLANES, Q_TILE), F32),
                        pltpu.VMEM((2, N_HEADS, SUBLANES, Q_TILE), F32),
                        pltpu.VMEM((2, N_HEADS, SUBLANES, Q_TILE), F32),
                        pltpu.VMEM((2, N_HEADS, K_CHUNK, Q_TILE), BF16),
                        pltpu.VMEM((N_HEADS, HEAD_DIM + BF16_SUBLANES, Q_TILE), F32)],
        compiler_params=pltpu.CompilerParams(
            dimension_semantics=("arbitrary", "arbitrary"), vmem_limit_bytes=VMEM_LIMIT),
        name="dsa",
    )(qt, qit, wit, k, ki, vt, *expert_w)


def _max_all(x):
    return jnp.max(jnp.max(x, axis=0, keepdims=True), axis=1, keepdims=True)


def _sum_all(x):
    return jnp.sum(jnp.sum(x, axis=0, keepdims=True), axis=1, keepdims=True)


def _pack_bf16_pairs(x):
    w = x.shape[1] // 2
    lo = pltpu.bitcast(x[:, :w].astype(BF16).astype(F32), jnp.int32)
    hi = pltpu.bitcast(x[:, w:].astype(BF16).astype(F32), jnp.int32)
    return lax.shift_right_logical(lo, 16) | (hi & jnp.int32(-65536))


def _unpack_bf16_pairs(p):
    lo = pltpu.bitcast(lax.shift_left(p, 16), F32)
    hi = pltpu.bitcast(p & jnp.int32(-65536), F32)
    return jnp.concatenate([lo, hi], axis=1)


def _router_topk(logits_t, rbias):
    r = logits_t.shape[1]
    shape3 = (N_GROUPS, GROUP_SIZE, r)
    scores = jax.nn.sigmoid(logits_t).reshape(shape3)
    biased = scores + rbias.reshape(N_GROUPS, GROUP_SIZE, 1)
    in_grp = lax.broadcasted_iota(jnp.int32, shape3, 1).astype(F32)
    m1 = jnp.max(biased, axis=1, keepdims=True)
    first = jnp.min(jnp.where(biased == m1, in_grp, float(GROUP_SIZE)), axis=1, keepdims=True)
    m2 = jnp.max(jnp.where(in_grp == first, NEG_INF, biased), axis=1, keepdims=True)
    cur = m1 + m2

    grp_idx = lax.broadcasted_iota(jnp.int32, (N_GROUPS, 1, r), 0).astype(F32)
    grp_sel = jnp.zeros((N_GROUPS, 1, r), F32)
    for _ in range(TOPK_GROUPS):
        m = jnp.max(cur, axis=0, keepdims=True)
        pick = grp_idx == jnp.min(jnp.where(cur == m, grp_idx, float(N_GROUPS)), axis=0, keepdims=True)
        grp_sel = jnp.where(pick, 1.0, grp_sel)
        cur = jnp.where(pick, NEG_INF, cur)

    cur = jnp.where(jnp.broadcast_to(grp_sel, shape3) > 0.0, biased, NEG_INF)
    exp_idx = lax.broadcasted_iota(jnp.int32, shape3, 0).astype(F32) * GROUP_SIZE + in_grp
    chosen = jnp.zeros(shape3, F32)
    ids, wts = [], []
    for _ in range(TOP_K):
        m = _max_all(cur)
        first = -_max_all(-jnp.where(cur == m, exp_idx, float(N_EXPERTS)))
        pick = exp_idx == first
        chosen = jnp.where(pick, 1.0, chosen)
        cur = jnp.where(pick, NEG_INF, cur)
        ids.append(first.reshape(1, r))
        wts.append(_sum_all(jnp.where(pick, scores, 0.0)).reshape(1, r))
    ids = jnp.concatenate(ids, axis=0)
    wts = jnp.concatenate(wts, axis=0)
    gates = wts / jnp.sum(wts, axis=0, keepdims=True) * ROUTED_SCALE
    return ids, gates, chosen.reshape(N_EXPERTS, r), exp_idx


def _route_kernel(conv_ref, attn_ref, hn_ref, woc_ref, woa_ref, g1_ref, b1_ref,
                  wsg_ref, wsu_ref, wsd_ref, wrh_ref, wrl_ref, rb_ref,
                  xp_ref, base_ref, gates_ref, ek_ref, rk_ref, cnt_ref, before_ref):
    step = pl.program_id(0)
    tr = hn_ref.shape[0]
    mix = _dot(conv_ref[...], woc_ref[...]) + _dot(attn_ref[...], woa_ref[...])
    h1 = _layer_norm_rows(DN_ALPHA * hn_ref[...] + mix, g1_ref[...], b1_ref[...])
    xb = h1.astype(BF16)
    xp_ref[...] = _pack_bf16_pairs(h1)

    shared = jax.nn.silu(_dot(xb, wsg_ref[...])) * _dot(xb, wsu_ref[...])
    base_ref[...] = DN_ALPHA * h1 + _dot(shared.astype(BF16), wsd_ref[...])

    x_lo = (h1 - xb.astype(F32)).astype(BF16)
    logits_t = (_dot_nt(wrh_ref[...], xb) + _dot_nt(wrh_ref[...], x_lo) + _dot_nt(wrl_ref[...], xb))
    ids, gates, chosen, exp_idx = _router_topk(logits_t, rb_ref[...])
    padded = jnp.concatenate([gates, jnp.zeros((LANES - TOP_K, tr), F32)], axis=0)
    gates_ref[...] = padded.T
    ek_ref[...] = ids.astype(jnp.int32)

    @pl.when(step == 0)
    def _():
        cnt_ref[...] = jnp.zeros(cnt_ref.shape, F32)
        t_i = lax.broadcasted_iota(jnp.int32, (tr, tr), 0)
        t_j = lax.broadcasted_iota(jnp.int32, (tr, tr), 1)
        before_ref[...] = jnp.where(t_i < t_j, 1.0, 0.0).astype(BF16)

    chosen_b = chosen.astype(BF16)
    running = cnt_ref[...]
    rank = _dot(chosen_b, before_ref[...]) + jnp.concatenate([running] * (tr // LANES), axis=1)
    rank3 = rank.reshape(N_GROUPS, GROUP_SIZE, tr)
    rk = [_sum_all(jnp.where(exp_idx == ids[k:k + 1].reshape(1, 1, tr), rank3, 0.0)).reshape(1, tr)
          for k in range(TOP_K)]
    rk_ref[...] = jnp.concatenate(rk, axis=0).astype(jnp.int32)
    cnt_ref[...] = running + _dot(chosen_b, jnp.ones((tr, LANES), BF16))


def _route_call(conv, attn, hn, woc, woa, g1, b1, wsg, wsu, wsd, wrh, wrl, rbias):
    n, D = hn.shape
    tr = _pick_tile(n, (768, 512, 256))

    def row_spec(w):
        return pl.BlockSpec((tr, w), lambda i: (i, 0))

    def col_spec(r):
        return pl.BlockSpec((r, tr), lambda i: (0, i))

    def full(a):
        return pl.BlockSpec(a.shape, lambda i: (0,) * a.ndim)

    consts = [woc, woa, g1, b1, wsg, wsu, wsd, wrh, wrl, rbias]
    return pl.pallas_call(
        _route_kernel,
        grid=(n // tr,),
        in_specs=[row_spec(CONV_CH), row_spec(ATTN_W), row_spec(D)] + [full(a) for a in consts],
        out_specs=[row_spec(D // 2), row_spec(D), row_spec(LANES), col_spec(TOP_K), col_spec(TOP_K),
                   pl.BlockSpec((N_EXPERTS, LANES), lambda i: (0, 0))],
        out_shape=[jax.ShapeDtypeStruct((n, D // 2), jnp.int32),
                   jax.ShapeDtypeStruct((n, D), F32),
                   jax.ShapeDtypeStruct((n, LANES), F32),
                   jax.ShapeDtypeStruct((TOP_K, n), jnp.int32),
                   jax.ShapeDtypeStruct((TOP_K, n), jnp.int32),
                   jax.ShapeDtypeStruct((N_EXPERTS, LANES), F32)],
        scratch_shapes=[pltpu.VMEM((tr, tr), BF16)],
        compiler_params=pltpu.CompilerParams(
            dimension_semantics=("arbitrary",), vmem_limit_bytes=VMEM_LIMIT),
        name="route",
    )(conv, attn, hn, *consts)


def _plan_kernel(cnt_ref, ek_ref, rk_ref, slot_ref, blk_ref, *, n_blocks):
    tr = ek_ref.shape[1]
    counts = cnt_ref[...]
    blocks = jnp.floor((counts + (EXPERT_BLOCK - 1.0)) * _INV_EXPERT_BLOCK)
    first_blk = [jnp.zeros((1, LANES), F32)]
    for e in range(1, N_EXPERTS):
        first_blk.append(first_blk[-1] + blocks[e - 1:e])
    ek = ek_ref[...]
    seg = jnp.zeros(ek.shape, F32)
    for e in range(N_EXPERTS):
        start = first_blk[e] * EXPERT_BLOCK
        seg = jnp.where(ek == e, jnp.concatenate([start] * (tr // LANES), axis=1), seg)
    slot_ref[...] = seg.astype(jnp.int32) + rk_ref[...]

    end_blk = jnp.concatenate(first_blk, axis=0) + blocks
    w = blk_ref.shape[1]
    blk_idx = lax.broadcasted_iota(jnp.int32, (N_EXPERTS, w), 1).astype(F32)
    done = jnp.where(jnp.concatenate([end_blk] * (w // LANES), axis=1) <= blk_idx, 1.0, 0.0)
    owner = jnp.minimum(jnp.sum(done, axis=0, keepdims=True), N_EXPERTS - 1.0)
    used = jnp.concatenate([end_blk[N_EXPERTS - 1:]] * (w // LANES), axis=1)
    lane = lax.broadcasted_iota(jnp.int32, (1, w), 1)
    table = jnp.where(lane == n_blocks, used, owner)
    blk_ref[...] = jnp.broadcast_to(table, blk_ref.shape).astype(jnp.int32)


def _plan_call(cnt, ek, rk, n_blocks):
    k, n = ek.shape
    tr = _pick_tile(n, (768, 512, 256))
    w = -(-(n_blocks + 1) // LANES) * LANES
    col = pl.BlockSpec((k, tr), lambda i: (0, i))
    return pl.pallas_call(
        functools.partial(_plan_kernel, n_blocks=n_blocks),
        grid=(n // tr,),
        in_specs=[pl.BlockSpec(cnt.shape, lambda i: (0, 0)), col, col],
        out_specs=[col, pl.BlockSpec((SUBLANES, w), lambda i: (0, 0))],
        out_shape=[jax.ShapeDtypeStruct((k, n), jnp.int32), jax.ShapeDtypeStruct((SUBLANES, w), jnp.int32)],
        compiler_params=pltpu.CompilerParams(dimension_semantics=("arbitrary",)),
        name="plan",
    )(cnt, ek, rk)


def _sc_workers():
    info = plsc.get_sparse_core_info()
    return info.num_cores, info.num_subcores


def _sc_scatter_rows(src, slot, n_out):
    n_src, d = src.shape
    n_choices = slot.shape[0]
    rows = SC_SCATTER_ROWS
    n_chunks = n_src // rows
    slots = slot.reshape(n_choices * n_chunks, rows)
    n_cores, n_sub = _sc_workers()
    n_workers = n_cores * n_sub

    def body(src_hbm, slot_hbm, out_hbm, idx_v, rows_v):
        wid = lax.axis_index("s") * n_cores + lax.axis_index("c")

        @pl.loop(0, -(-n_chunks // n_workers))
        def _(j):
            u = j * n_workers + wid

            @pl.when(u < n_chunks)
            def _():
                pltpu.sync_copy(src_hbm.at[pl.ds(u * rows, rows)], rows_v)
                for k in range(n_choices):
                    pltpu.sync_copy(slot_hbm.at[pl.ds(k * n_chunks + u, 1)], idx_v)
                    pltpu.sync_copy(rows_v, out_hbm.at[idx_v.at[0]])

    return pl.kernel(
        body, out_type=jax.ShapeDtypeStruct((n_out, d), src.dtype),
        mesh=plsc.VectorSubcoreMesh(core_axis_name="c", subcore_axis_name="s"),
        scratch_types=[pltpu.VMEM((1, rows), jnp.int32), pltpu.VMEM((rows, d), src.dtype)],
        name="dispatch_rows",
    )(src, slots)


def _sc_gather_rows(table, slots):
    units = slots.shape[0]
    d = table.shape[1]
    n_cores, n_sub = _sc_workers()
    n_workers = n_cores * n_sub
    assert slots.shape[1] == SC_ROWS

    def body(table_hbm, slot_hbm, out_hbm, idx_v, rows_v):
        wid = lax.axis_index("s") * n_cores + lax.axis_index("c")

        @pl.loop(0, -(-units // n_workers))
        def _(j):
            u = j * n_workers + wid

            @pl.when(u < units)
            def _():
                pltpu.sync_copy(slot_hbm.at[pl.ds(u, 1)], idx_v)
                pltpu.sync_copy(table_hbm.at[idx_v.at[0]], rows_v)
                pltpu.sync_copy(rows_v, out_hbm.at[pl.ds(u * SC_ROWS, SC_ROWS)])

    return pl.kernel(
        body, out_type=jax.ShapeDtypeStruct((units * SC_ROWS, d), table.dtype),
        mesh=plsc.VectorSubcoreMesh(core_axis_name="c", subcore_axis_name="s"),
        scratch_types=[pltpu.VMEM((1, SC_ROWS), jnp.int32), pltpu.VMEM((SC_ROWS, d), table.dtype)],
        name="collect_rows",
    )(table, slots)


def _expert_kernel(blk_ref, xs_ref, *rest, per_share, n_shares):
    w_hbm, (ys_ref, wg_buf, wu_buf, wd_buf, sem) = rest[:3 * n_shares], rest[3 * n_shares:]
    b = pl.program_id(0)
    n_used = blk_ref[pl.num_programs(0)]

    def weight_copies(step, share):
        slot = step % EXPERT_WEIGHT_SLOTS
        local = blk_ref[step] - share * per_share
        return [pltpu.make_async_copy(w_hbm[3 * share + j].at[local], buf.at[slot], sem.at[j, slot])
                for j, buf in enumerate((wg_buf, wu_buf, wd_buf))]

    def for_weights_of(step, action):
        @pl.when(step < n_used)
        def _():
            for s in range(n_shares):
                @pl.when(blk_ref[step] // per_share == s)
                def _():
                    for copy in weight_copies(step, s):
                        action(copy)

    @pl.when(b == 0)
    def _():
        for first in range(EXPERT_LOOKAHEAD):
            for_weights_of(jnp.int32(first), lambda copy: copy.start())

    for_weights_of(b + EXPERT_LOOKAHEAD, lambda copy: copy.start())
    for_weights_of(b, lambda copy: copy.wait())

    @pl.when(b < n_used)
    def _():
        slot = b % EXPERT_WEIGHT_SLOTS
        for r0 in range(0, EXPERT_BLOCK, EXPERT_SUB):
            x = _unpack_bf16_pairs(xs_ref[r0:r0 + EXPERT_SUB, :]).astype(BF16)
            hdn = (jax.nn.silu(_dot(x, wg_buf[slot])) * _dot(x, wu_buf[slot])).astype(BF16)
            ys_ref[r0:r0 + EXPERT_SUB, :] = _pack_bf16_pairs(_dot(hdn, wd_buf[slot]))


def _expert_call(blk_exp, xs, weight_shares):
    p, half = xs.shape
    per_share = weight_shares[0][0].shape[0]
    wg, wu, wd = weight_shares[0]
    n_blocks = p // EXPERT_BLOCK
    assert blk_exp.shape == (n_blocks + 1,)

    def row_block(b, blk):
        return (jnp.minimum(b, blk[n_blocks] - 1), 0)

    rows = pl.BlockSpec((EXPERT_BLOCK, half), row_block)
    grid_spec = pltpu.PrefetchScalarGridSpec(
        num_scalar_prefetch=1,
        grid=(n_blocks,),
        in_specs=[rows] + [pl.BlockSpec(memory_space=pl.ANY) for share in weight_shares for _ in share],
        out_specs=rows,
        scratch_shapes=[pltpu.VMEM((EXPERT_WEIGHT_SLOTS,) + wg.shape[1:], BF16),
                        pltpu.VMEM((EXPERT_WEIGHT_SLOTS,) + wu.shape[1:], BF16),
                        pltpu.VMEM((EXPERT_WEIGHT_SLOTS,) + wd.shape[1:], BF16),
                        pltpu.SemaphoreType.DMA((3, EXPERT_WEIGHT_SLOTS))],
    )
    return pl.pallas_call(
        functools.partial(_expert_kernel, per_share=per_share, n_shares=len(weight_shares)),
        grid_spec=grid_spec,
        out_shape=jax.ShapeDtypeStruct((p, half), jnp.int32),
        compiler_params=pltpu.CompilerParams(
            dimension_semantics=("arbitrary",), vmem_limit_bytes=VMEM_LIMIT),
        name="experts",
    )(blk_exp, xs, *[a for share in weight_shares for a in share])


def _combine_kernel(g_ref, gates_ref, base_ref, g2_ref, b2_ref, *rest):
    o_ref = rest[-1]
    gates = gates_ref[...]
    acc = base_ref[...]
    for k in range(g_ref.shape[0]):
        acc = acc + _unpack_bf16_pairs(g_ref[k]) * gates[:, k:k + 1]
    o_ref[0] = _layer_norm_rows(acc, g2_ref[...], b2_ref[...])


def _combine_call(g, gates, base, g2, b2, result, batch, n_batch):
    k, n, half = g.shape
    D = base.shape[1]
    tr = SEQ_ALIGN
    seq = n - tr

    def row_spec(w):
        return pl.BlockSpec((tr, w), lambda i: (i, 0))

    vec = pl.BlockSpec((1, D), lambda i: (0, 0))
    in_specs = [pl.BlockSpec((k, tr, half), lambda i: (0, i, 0)), row_spec(LANES), row_spec(D), vec, vec]
    args = [g, gates, base, g2, b2]
    aliases = {}
    if result is not None:
        in_specs.append(pl.BlockSpec(memory_space=pl.ANY))
        args.append(result)
        aliases = {len(args) - 1: 0}
    return pl.pallas_call(
        _combine_kernel,
        grid=(n // tr,),
        in_specs=in_specs,
        out_specs=pl.BlockSpec((1, tr, D), lambda i: (batch, jnp.maximum(i - 1, 0), 0)),
        out_shape=jax.ShapeDtypeStruct((n_batch, seq, D), F32),
        input_output_aliases=aliases,
        compiler_params=pltpu.CompilerParams(
            dimension_semantics=("arbitrary",), vmem_limit_bytes=VMEM_LIMIT),
        name="combine",
    )(*args)


def _rope_tables(tp, lead):
    pos = jnp.arange(tp, dtype=F32) - lead
    inv = jnp.power(ROPE_THETA, -2.0 * jnp.arange(ROPE_HALF, dtype=F32) / ROPE_DIM)
    ang = pos[:, None] * inv[None, :]
    cos, sin = jnp.cos(ang), jnp.sin(ang)
    zeros = jnp.zeros((tp, HEAD_DIM - ROPE_DIM), F32)
    zh = jnp.zeros((tp, ROPE_HALF), F32)
    c64 = jnp.concatenate([cos, cos, jnp.ones_like(zeros)], axis=1)
    s1_64 = jnp.concatenate([-sin, zh, zeros], axis=1)
    s2_64 = jnp.concatenate([zh, sin, zeros], axis=1)
    rep = LANES // HEAD_DIM
    return (jnp.tile(c64, (1, rep)), jnp.tile(s1_64, (1, rep)), jnp.tile(s2_64, (1, rep)),
            cos.T, sin.T)


def kernel(x, meta_tokens, ln_emb_g, ln_emb_b, w_in, conv_w, conv_b, ln_conv_g, ln_conv_b, ln_kidx_g, ln_kidx_b, w_out, ln1_g, ln1_b, w_router, router_bias, w_gate, w_up, w_down, ws_gate, ws_up, ws_down, ln2_g, ln2_b):
    B, seq, D = x.shape
    assert w_in.shape[0] == DEPTH
    assert seq % SEQ_ALIGN == 0 and meta_tokens.shape[0] == N_META <= SEQ_ALIGN
    k_sel = min(INDEX_TOPK, seq // 4)
    tp = seq + SEQ_ALIGN
    n_pad = SEQ_ALIGN - N_META

    def row(a):
        return a.reshape(1, -1).astype(F32)

    w = w_in[0]
    o = 0
    parts = []
    for width in (CONV_CH, CONV_CH, ATTN_W, ATTN_W, ATTN_W, IDX_HEADS * IDX_DIM, IDX_DIM, IDX_HEADS):
        parts.append(w[:, o:o + width])
        o += width
    wa, wgl, wq, wk, wv, wqi, wki, wwi = parts
    wwi_t = jnp.concatenate([wwi.T, jnp.zeros((BF16_SUBLANES - IDX_HEADS, D), w.dtype)], axis=0)
    weights = (jnp.concatenate([wa, wgl], axis=1).astype(BF16), wq.T.astype(BF16), wk.astype(BF16),
               wv.T.astype(BF16), wqi.T.astype(BF16), jnp.concatenate([wki, wki], axis=1).astype(BF16),
               wwi_t.astype(BF16))

    def twice(a):
        return row(jnp.concatenate([a, a]))

    tabs = _rope_tables(tp, n_pad)
    inproj_consts = (row(ln_emb_g), row(ln_emb_b), weights, conv_w[0].astype(F32), row(conv_b[0]),
                     row(ln_conv_g[0]), row(ln_conv_b[0]), twice(ln_kidx_g[0]), twice(ln_kidx_b[0]))
    wr_t = w_router[0].T.astype(F32)
    wr_hi = wr_t.astype(BF16)
    wr_lo = (wr_t - wr_hi.astype(F32)).astype(BF16)
    route_consts = (w_out[0][:CONV_CH].astype(BF16), w_out[0][CONV_CH:].astype(BF16), row(ln1_g[0]),
                    row(ln1_b[0]), ws_gate[0].astype(BF16), ws_up[0].astype(BF16), ws_down[0].astype(BF16),
                    wr_hi, wr_lo, router_bias[0].reshape(-1, 1).astype(F32))
    meta = meta_tokens.astype(F32)
    n_blocks = -(-tp * TOP_K // EXPERT_BLOCK) + N_EXPERTS

    expert_w = (w_gate[0], w_up[0], w_down[0])
    weight_shares = []
    routed = []
    for b in range(B):
        hn, conv, qt, k, vt, qit, ki, wit = _inproj_call(x, b, meta, tabs, *inproj_consts)
        attn, *share = _dsa_call(qt, qit, wit, k, ki, vt, k_sel, n_pad, expert_w, b, B)
        weight_shares.append(share)
        xp, base, gates, ek, rk, cnt = _route_call(conv[0], attn[0], hn[0], *route_consts)

        slot, blk = _plan_call(cnt, ek, rk, n_blocks)
        slots = slot.reshape(TOP_K * tp // SC_ROWS, SC_ROWS)
        xs = _sc_scatter_rows(xp, slot, n_blocks * EXPERT_BLOCK)
        routed.append((xs, blk, slots, gates, base))

    result = None
    for b, (xs, blk, slots, gates, base) in enumerate(routed):
        ys = _expert_call(blk[0, :n_blocks + 1], xs, weight_shares)
        picked = _sc_gather_rows(ys, slots).reshape(TOP_K, tp, D // 2)
        result = _combine_call(picked, gates, base, row(ln2_g[0]), row(ln2_b[0]), result, b, B)
    return result
```

```python
import functools

import numpy as np
import jax
import jax.numpy as jnp
from jax import lax
from jax.experimental import pallas as pl
from jax.experimental.pallas import tpu as pltpu
from jax.experimental.pallas import tpu_sc as plsc

N_META = 16
CONV_CH = 512
CONV_WIDTH = 31
N_HEADS = 8
HEAD_DIM = 64
ATTN_W = N_HEADS * HEAD_DIM
IDX_HEADS = 8
IDX_DIM = 64
INDEX_TOPK = 256
ROPE_DIM = HEAD_DIM // 4
ROPE_HALF = ROPE_DIM // 2
ROPE_THETA = 500000.0
N_EXPERTS = 64
TOP_K = 8
N_GROUPS = 8
GROUP_SIZE = N_EXPERTS // N_GROUPS
TOPK_GROUPS = 4
ROUTED_SCALE = 2.5
LN_EPS = 1e-5
DEPTH = 1
DN_ALPHA = (2.0 * DEPTH) ** 0.25

LANES = 128
Q_TILE = 256
SUBLANES = 8
BF16_SUBLANES = 16
EXPERT_BLOCK = 1152
EXPERT_SUB = 384
EXPERT_LOOKAHEAD = 3
EXPERT_WEIGHT_SLOTS = EXPERT_LOOKAHEAD + 1
_INV_EXPERT_BLOCK = float(np.nextafter(np.float32(1.0 / EXPERT_BLOCK), np.float32(1.0)))
SC_ROWS = 128
SC_SCATTER_ROWS = 64
K_CHUNK = Q_TILE
K_SUB = 128
COUNT_UNROLL = 4
SEQ_ALIGN = 256
CONV_HALO = 32
CONV_ROWS = 128
VMEM_LIMIT = 56 * 1024 * 1024

F32 = jnp.float32
BF16 = jnp.bfloat16
COARSE = jnp.bfloat16
NEG_INF = float("-inf")
INT_MIN = -2 ** 31
KEY_NEG_INF = -2139095041
LOG2_E = 1.4426950408889634


def _dot(a, b):
    return jnp.dot(a, b, preferred_element_type=F32)


def _dot_nt(a, b):
    return lax.dot_general(a, b, (((1,), (1,)), ((), ())), preferred_element_type=F32)


def _layer_norm_rows(x, g, b):
    mu = jnp.mean(x, axis=-1, keepdims=True)
    xc = x - mu
    var = jnp.mean(xc * xc, axis=-1, keepdims=True)
    return xc * lax.rsqrt(var + LN_EPS) * g + b


def _pick_tile(n, candidates):
    for c in candidates:
        if n % c == 0:
            return c
    raise ValueError(f"no tile for {n}")


def _rope_rows(x, c_tab, s1_tab, s2_tab):
    outs = []
    for j in range(x.shape[1] // LANES):
        xs = x[:, j * LANES:(j + 1) * LANES]
        up = pltpu.roll(xs, LANES - ROPE_HALF, axis=1)
        dn = pltpu.roll(xs, ROPE_HALF, axis=1)
        outs.append(xs * c_tab + up * s1_tab + dn * s2_tab)
    return jnp.concatenate(outs, axis=1)


def _rope_cols(xt, cos_t, sin_t, heads):
    r = xt.shape[1]
    x3 = xt.reshape(heads, HEAD_DIM, r)
    x1 = x3[:, 0:ROPE_HALF, :]
    x2 = x3[:, ROPE_HALF:ROPE_DIM, :]
    n1 = x1 * cos_t - x2 * sin_t
    n2 = x2 * cos_t + x1 * sin_t
    out = jnp.concatenate([n1, n2, x3[:, ROPE_DIM:, :]], axis=1)
    return out.reshape(heads * HEAD_DIM, r)


def _inproj_kernel(x_ref, meta_ref, ctab_ref, s1tab_ref, s2tab_ref, cost_ref, sint_ref,
                   lng_ref, lnb_ref, wag_ref, wqt_ref, wk_ref, wvt_ref, wqit_ref, wki_ref, wwit_ref,
                   cw_ref, cb_ref, lncg_ref, lncb_ref, lnkg_ref, lnkb_ref,
                   hn_ref, conv_ref, qt_ref, k_ref, vt_ref, qit_ref, ki_ref, wit_ref,
                   ubuf_ref, wbuf_ref, cbuf_ref):
    t = pl.program_id(1)
    tr = x_ref.shape[1]
    n_meta = meta_ref.shape[0]

    @pl.when(t == 0)
    def _():
        ubuf_ref[0:CONV_HALO, :] = jnp.zeros((CONV_HALO, CONV_CH), F32)

    first = jnp.concatenate([jnp.zeros((tr - n_meta, x_ref.shape[2]), F32), meta_ref[...]], axis=0)
    h = jnp.where(t == 0, first, x_ref[0])

    hn = _layer_norm_rows(h, lng_ref[...], lnb_ref[...])
    hn_ref[0] = hn
    xb = hn.astype(BF16)

    ag = _dot(xb, wag_ref[...])
    u = ag[:, :CONV_CH] * jax.nn.sigmoid(ag[:, CONV_CH:])
    row = lax.broadcasted_iota(jnp.int32, (tr, CONV_CH), 0)
    u = jnp.where(jnp.logical_or(t > 0, row >= tr - n_meta), u, 0.0)
    ubuf_ref[CONV_HALO:CONV_HALO + tr, :] = u
    base = CONV_HALO - (CONV_WIDTH - 1)
    for r in range(SUBLANES):
        rows = max(o for o in range(base, base + CONV_WIDTH) if o % SUBLANES == r) - r + tr
        wbuf_ref[r, 0:rows, :] = ubuf_ref[r:r + rows, :]
    ubuf_ref[0:CONV_HALO, :] = ubuf_ref[tr:tr + CONV_HALO, :]
    for lo in range(0, CONV_CH, LANES):
        for r0 in range(0, tr, CONV_ROWS):
            acc = jnp.zeros((CONV_ROWS, LANES), F32)
            for o in range(base, base + CONV_WIDTH):
                r = o % SUBLANES
                acc = acc + (cw_ref[o - base:o - base + 1, lo:lo + LANES]
                             * wbuf_ref[r, o - r + r0:o - r + r0 + CONV_ROWS, lo:lo + LANES])
            cbuf_ref[r0:r0 + CONV_ROWS, lo:lo + LANES] = acc
    c = _layer_norm_rows(cbuf_ref[...] + cb_ref[...], lncg_ref[...], lncb_ref[...])
    conv_ref[0] = (c * jax.nn.sigmoid(c)).astype(conv_ref.dtype)

    ctab, s1tab, s2tab = ctab_ref[...], s1tab_ref[...], s2tab_ref[...]
    cos_t, sin_t = cost_ref[...], sint_ref[...]

    qt = _rope_cols(_dot_nt(wqt_ref[...], xb), cos_t, sin_t, N_HEADS)
    qt_ref[0] = (qt * (HEAD_DIM ** -0.5 * LOG2_E)).astype(qt_ref.dtype)
    k = _rope_rows(_dot(xb, wk_ref[...]), ctab, s1tab, s2tab)
    k_ref[0] = k.astype(k_ref.dtype)
    vt_ref[0] = _dot_nt(wvt_ref[...], xb).astype(vt_ref.dtype)

    qit = _rope_cols(_dot_nt(wqit_ref[...], xb), cos_t, sin_t, IDX_HEADS)
    qit_ref[0] = qit.astype(qit_ref.dtype)
    ki = _layer_norm_rows(_dot(xb, wki_ref[...]), lnkg_ref[...], lnkb_ref[...])
    ki_ref[0] = _rope_rows(ki, ctab, s1tab, s2tab).astype(ki_ref.dtype)
    wit = _dot_nt(wwit_ref[...], xb) * (IDX_HEADS ** -0.5)
    wit_ref[0] = wit[:IDX_HEADS]


def _inproj_call(x, batch, meta, tabs, ln_g, ln_b, weights, conv_w, conv_b, lnc_g, lnc_b, lnk_g, lnk_b):
    _, seq, D = x.shape
    B = 1
    tr = SEQ_ALIGN
    tp = seq + tr
    nt = tp // tr
    ctab, s1tab, s2tab, cos_t, sin_t = tabs

    def row_spec(w):
        return pl.BlockSpec((1, tr, w), lambda b, t: (b, t, 0))

    def col_spec(r):
        return pl.BlockSpec((1, r, tr), lambda b, t: (b, 0, t))

    def full(a):
        return pl.BlockSpec(a.shape, lambda b, t: (0,) * a.ndim)

    tab_row = pl.BlockSpec((tr, LANES), lambda b, t: (t, 0))
    tab_col = pl.BlockSpec((ROPE_HALF, tr), lambda b, t: (0, t))
    consts = [ln_g, ln_b, *weights, conv_w, conv_b, lnc_g, lnc_b, lnk_g, lnk_b]
    out_shape = [
        jax.ShapeDtypeStruct((B, tp, D), F32),
        jax.ShapeDtypeStruct((B, tp, CONV_CH), BF16),
        jax.ShapeDtypeStruct((B, ATTN_W, tp), BF16),
        jax.ShapeDtypeStruct((B, tp, ATTN_W), BF16),
        jax.ShapeDtypeStruct((B, ATTN_W, tp), BF16),
        jax.ShapeDtypeStruct((B, IDX_HEADS * IDX_DIM, tp), BF16),
        jax.ShapeDtypeStruct((B, tp, 2 * IDX_DIM), BF16),
        jax.ShapeDtypeStruct((B, IDX_HEADS, tp), F32),
    ]
    out_specs = [row_spec(D), row_spec(CONV_CH), col_spec(ATTN_W), row_spec(ATTN_W), col_spec(ATTN_W),
                 col_spec(IDX_HEADS * IDX_DIM), row_spec(2 * IDX_DIM), col_spec(IDX_HEADS)]
    return pl.pallas_call(
        _inproj_kernel,
        grid=(B, nt),
        in_specs=[pl.BlockSpec((1, tr, D), lambda b, t: (batch, jnp.maximum(t - 1, 0), 0)), full(meta),
                  tab_row, tab_row, tab_row, tab_col, tab_col] + [full(a) for a in consts],
        out_specs=out_specs,
        out_shape=out_shape,
        scratch_shapes=[pltpu.VMEM((CONV_HALO + tr, CONV_CH), F32),
                        pltpu.VMEM((SUBLANES, CONV_HALO + tr, CONV_CH), F32),
                        pltpu.VMEM((tr, CONV_CH), F32)],
        compiler_params=pltpu.CompilerParams(
            dimension_semantics=("arbitrary", "arbitrary"), vmem_limit_bytes=VMEM_LIMIT),
        name="inproj",
    )(x, meta, ctab, s1tab, s2tab, cos_t, sin_t, *consts)


def _key_to_float(key):
    bits = jnp.where(key >= 0, key, key ^ jnp.int32(0x7FFFFFFF))
    f = pltpu.bitcast(bits, F32)
    return jnp.where(key < jnp.int32(KEY_NEG_INF), NEG_INF, f)


def _tree(parts, op):
    parts = list(parts)
    while len(parts) > 1:
        nxt = [op(parts[j], parts[j + 1]) for j in range(0, len(parts) - 1, 2)]
        if len(parts) % 2:
            nxt.append(parts[-1])
        parts = nxt
    return parts[0]


def _fold_rows(x, rows, op):
    return _tree([x[j * rows:(j + 1) * rows] for j in range(x.shape[0] // rows)], op)


def _dsa_kernel(qt_ref, qit_ref, wit_ref, k_ref, ki_ref, vt_ref, *rest, k_sel, n_pad, n_convert):
    f32_w, rest = rest[:n_convert], rest[n_convert:]
    o_ref, bf16_w, rest = rest[0], rest[1:1 + n_convert], rest[1 + n_convert:]
    sc_ref, sh_ref, qm_ref, qim_ref, m_ref, alpha_ref, shift_ref, lg_ref, acc_ref = rest

    for src_ref, dst_ref in zip(f32_w, bf16_w):
        dst_ref[...] = src_ref[...].astype(BF16)

    i = pl.program_id(1)
    tq = qt_ref.shape[2]
    n_chunks = (i * tq + tq + K_CHUNK - 1) // K_CHUNK
    v_rows = HEAD_DIM + BF16_SUBLANES

    def causal_mask(k0, rows=K_CHUNK):
        kpos = k0 + lax.broadcasted_iota(jnp.int32, (rows, tq), 0)
        return kpos <= i * tq + lax.broadcasted_iota(jnp.int32, (rows, tq), 1)

    def rows8(x):
        return jnp.broadcast_to(x, (SUBLANES, tq))

    def tiles(x):
        return x.reshape(x.shape[0] // SUBLANES, SUBLANES, tq)

    def head_slab(ref, h):
        slab = ref[0, (h // 2) * LANES:(h // 2 + 1) * LANES, :]
        zeros = jnp.zeros((HEAD_DIM, tq), slab.dtype)
        if h % 2 == 0:
            return jnp.concatenate([slab[:HEAD_DIM], zeros], axis=0)
        return jnp.concatenate([zeros, slab[HEAD_DIM:]], axis=0)

    for h in range(N_HEADS):
        qm_ref[h] = head_slab(qt_ref, h)
    for h in range(IDX_HEADS):
        qim_ref[h] = head_slab(qit_ref, h)
    wit = wit_ref[0]
    w_heads = [rows8(wit[h:h + 1] * (IDX_DIM ** -0.5)) for h in range(IDX_HEADS)]

    def for_each_chunk(chunk_fn, unroll, count=n_chunks):
        def group(j, carry):
            for u in range(unroll):
                chunk_fn(unroll * j + u, u % 2)
            return carry

        def single(c, carry):
            chunk_fn(c, 0)
            return carry

        n_groups = count // unroll
        lax.fori_loop(0, n_groups, group, 0)
        lax.fori_loop(n_groups * unroll, count, single, 0)

    def diagonal_visible(first_row, rows):
        r = first_row + lax.broadcasted_iota(jnp.int32, (rows, tq), 0)
        return r <= lax.broadcasted_iota(jnp.int32, (rows, tq), 1)

    def score_chunk(c, slot, diagonal=False):
        del slot
        for s in range(K_CHUNK // K_SUB):
            k0 = pl.multiple_of(c * K_CHUNK + s * K_SUB, K_SUB)
            kic = ki_ref[0, pl.ds(k0, K_SUB), :]
            acc = jnp.zeros((K_SUB // SUBLANES, SUBLANES, tq), F32)
            for h in range(IDX_HEADS):
                acc = acc + w_heads[h][None] * jnp.maximum(tiles(_dot(kic, qim_ref[h])), 0.0)
            acc = acc.reshape(K_SUB, tq)
            if diagonal:
                acc = jnp.where(diagonal_visible(s * K_SUB, K_SUB), acc, NEG_INF)
            sc_ref[pl.ds(k0, K_SUB), :] = acc
            hi_bits = pltpu.bitcast(acc, jnp.int32) & jnp.int32(-65536)
            sh_ref[pl.ds(k0, K_SUB), :] = pltpu.bitcast(hi_bits, F32).astype(COARSE)

    for_each_chunk(score_chunk, 4, n_chunks - 1)
    score_chunk(n_chunks - 1, 0, diagonal=True)

    def drop_padding_keys():
        sh_ref[0:n_pad, :] = jnp.full((n_pad, tq), NEG_INF, COARSE)

    sc_ref[0:n_pad, :] = jnp.full((n_pad, tq), NEG_INF, F32)
    drop_padding_keys()

    def count_all(ref, thr_tile, preds):
        rows = thr_tile.shape[0]
        one, zero = jnp.ones((), ref.dtype), jnp.zeros((), ref.dtype)

        def body(c, cnts):
            k0 = pl.multiple_of(c * K_CHUNK, K_CHUNK)
            s = ref[pl.ds(k0, K_CHUNK), :]
            out = []
            for cnt, p in zip(cnts, preds):
                hits = [jnp.where(p(s[j * rows:(j + 1) * rows], thr_tile), one, zero)
                        for j in range(K_CHUNK // rows)]
                out.append(cnt + _tree(hits, jnp.add).astype(F32))
            return tuple(out)

        def group(j, cnts):
            for u in range(COUNT_UNROLL):
                cnts = body(COUNT_UNROLL * j + u, cnts)
            return cnts

        init = tuple(jnp.zeros((rows, tq), F32) for _ in preds)
        n_groups = n_chunks // COUNT_UNROLL
        cnts = lax.fori_loop(0, n_groups, group, init)
        cnts = lax.fori_loop(n_groups * COUNT_UNROLL, n_chunks, body, cnts)
        return [rows8(jnp.sum(cnt, axis=0, keepdims=True)) for cnt in cnts]

    def search_body(it, tkey, coarse):
        cand = tkey + lax.shift_left(jnp.int32(1), 31 - it)
        cf = _key_to_float(cand)
        if coarse:
            cf = pltpu.bitcast(pltpu.bitcast(cf, jnp.int32) & jnp.int32(-65536), F32).astype(COARSE)
            cf = jnp.concatenate([cf] * (BF16_SUBLANES // SUBLANES), axis=0)
        cnt, = count_all(sh_ref if coarse else sc_ref, cf, [lambda s, t: s >= t])
        return jnp.where(cnt >= k_sel, cand, tkey)

    tkey = jnp.full((SUBLANES, tq), INT_MIN, jnp.int32)
    tkey = lax.fori_loop(0, 16, functools.partial(search_body, coarse=True), tkey)
    tkey = lax.fori_loop(16, 32, functools.partial(search_body, coarse=False), tkey)
    thr = _key_to_float(tkey)

    n_ge, n_gt = count_all(sc_ref, thr, [lambda s, t: s >= t, lambda s, t: s > t])
    need = k_sel - n_gt
    has_ties = jnp.max(n_ge) > k_sel

    @pl.when(jnp.logical_not(has_ties))
    def _():
        def mask_chunk(c, slot, diagonal=False):
            del slot
            k0 = pl.multiple_of(c * K_CHUNK, K_CHUNK)
            s = sc_ref[pl.ds(k0, K_CHUNK), :]
            bias = jnp.where(tiles(s) >= thr[None], 0.0, NEG_INF).reshape(K_CHUNK, tq)
            if diagonal:
                bias = jnp.where(diagonal_visible(0, K_CHUNK), bias, NEG_INF)
            sh_ref[pl.ds(k0, K_CHUNK), :] = bias.astype(COARSE)

        for_each_chunk(mask_chunk, 4, n_chunks - 1)
        mask_chunk(n_chunks - 1, 0, diagonal=True)

    @pl.when(has_ties)
    def _():
        r_i = lax.broadcasted_iota(jnp.int32, (K_CHUNK, K_CHUNK), 0)
        c_i = lax.broadcasted_iota(jnp.int32, (K_CHUNK, K_CHUNK), 1)
        lower = jnp.where(c_i <= r_i, 1.0, 0.0).astype(BF16)
        thr_row, need_row = thr[0:1], need[0:1]

        def body(c, seen):
            k0 = pl.multiple_of(c * K_CHUNK, K_CHUNK)
            s = sc_ref[pl.ds(k0, K_CHUNK), :]
            eq = jnp.where(s == thr_row, 1.0, 0.0)
            rank = _dot(lower, eq.astype(BF16)) + seen
            keep_tie = jnp.where(rank <= need_row, eq, 0.0)
            sel = jnp.where(s > thr_row, 1.0, keep_tie)
            sh_ref[pl.ds(k0, K_CHUNK), :] = jnp.where(
                sel > 0.0, jnp.where(causal_mask(k0), 0.0, NEG_INF), NEG_INF).astype(COARSE)
            return rank[K_CHUNK - 1:K_CHUNK, :]
        lax.fori_loop(0, n_chunks, body, jnp.zeros((1, tq), F32))

    drop_padding_keys()

    m_ref[...] = jnp.full(m_ref.shape, NEG_INF, F32)
    acc_ref[...] = jnp.zeros(acc_ref.shape, F32)
    ones_rows = jnp.ones((BF16_SUBLANES, K_CHUNK), BF16)

    def logits_stage(c, slot):
        k0 = pl.multiple_of(c * K_CHUNK, K_CHUNK)
        for h in range(N_HEADS):
            pair = slice((h // 2) * LANES, (h // 2 + 1) * LANES)
            cmax = []
            for s in range(K_CHUNK // K_SUB):
                rows = pl.ds(pl.multiple_of(k0 + s * K_SUB, K_SUB), K_SUB)
                lg = _dot(k_ref[0, rows, pair], qm_ref[h]).astype(BF16) + sh_ref[rows, :].astype(BF16)
                lg_ref[slot, h, s * K_SUB:(s + 1) * K_SUB, :] = lg
                cmax.append(_fold_rows(lg, BF16_SUBLANES, jnp.maximum))
            cmax = _tree(cmax, jnp.maximum).astype(F32)
            m_old = m_ref[h]
            m_new = jnp.maximum(m_old, rows8(jnp.max(cmax, axis=0, keepdims=True)))
            m_safe = jnp.where(m_new == NEG_INF, 0.0, m_new)
            alpha_ref[slot, h] = jnp.exp2(m_old - m_safe)
            shift_ref[slot, h] = m_safe
            m_ref[h] = m_new

    def values_stage(c, slot):
        k0 = pl.multiple_of(c * K_CHUNK, K_CHUNK)
        for h in range(N_HEADS):
            m_safe = shift_ref[slot, h].astype(BF16)
            m_tile = jnp.concatenate([m_safe] * (BF16_SUBLANES // SUBLANES), axis=0)
            lg = lg_ref[slot, h].reshape(K_CHUNK // BF16_SUBLANES, BF16_SUBLANES, tq)
            p = jnp.exp2(lg - m_tile[None]).reshape(K_CHUNK, tq)
            v_aug = jnp.concatenate(
                [vt_ref[0, h * HEAD_DIM:(h + 1) * HEAD_DIM, pl.ds(k0, K_CHUNK)], ones_rows], axis=0)
            pv = _dot(v_aug, p)
            acc = acc_ref[h].reshape(v_rows // SUBLANES, SUBLANES, tq) * alpha_ref[slot, h][None]
            acc_ref[h] = acc.reshape(v_rows, tq) + pv

    def attn_chunk(c, slot):
        logits_stage(c, slot)
        values_stage(c, slot)

    for_each_chunk(attn_chunk, 4)

    for pair in range(N_HEADS // 2):
        halves = []
        for h in (2 * pair, 2 * pair + 1):
            a = acc_ref[h]
            total = a[HEAD_DIM:HEAD_DIM + 1]
            halves.append(a[:HEAD_DIM] / jnp.where(total == 0.0, 1.0, total))
        out_t = jnp.concatenate(halves, axis=0)
        o_ref[0, :, pair * LANES:(pair + 1) * LANES] = out_t.T.astype(o_ref.dtype)


def _dsa_call(qt, qit, wit, k, ki, vt, k_sel, n_pad, expert_w, part, n_parts):
    B, tp, _ = k.shape
    nq = tp // Q_TILE
    n_conv = expert_w[0].shape[0] // n_parts
    assert n_conv * n_parts == expert_w[0].shape[0]
    per_step = next(d for d in range(1, n_conv + 1) if n_conv % d == 0 and n_conv // d <= nq)
    n_steps = n_conv // per_step

    def q_cols(r):
        return pl.BlockSpec((1, r, Q_TILE), lambda b, i: (b, 0, i))

    def per_batch(a):
        return pl.BlockSpec((1,) + a.shape[1:], lambda b, i: (b, 0, 0), pipeline_mode=pl.Buffered(1))

    def expert_spec(a, first_block):
        return pl.BlockSpec((per_step,) + a.shape[1:],
                            lambda b, i: (first_block + jnp.minimum(i, n_steps - 1), 0, 0))

    in_specs = [q_cols(ATTN_W), q_cols(IDX_HEADS * IDX_DIM), q_cols(IDX_HEADS),
                per_batch(k), per_batch(ki), per_batch(vt)]
    in_specs += [expert_spec(a, part * n_steps) for a in expert_w]

    return pl.pallas_call(
        functools.partial(_dsa_kernel, k_sel=k_sel, n_pad=n_pad, n_convert=len(expert_w)),
        grid=(B, nq),
        in_specs=in_specs,
        out_specs=[pl.BlockSpec((1, Q_TILE, ATTN_W), lambda b, i: (b, i, 0))]
        + [expert_spec(a, 0) for a in expert_w],
        out_shape=[jax.ShapeDtypeStruct((B, tp, ATTN_W), BF16)]
        + [jax.ShapeDtypeStruct((n_conv,) + a.shape[1:], BF16) for a in expert_w],
        scratch_shapes=[pltpu.VMEM((tp, Q_TILE), F32),
                        pltpu.VMEM((tp, Q_TILE), COARSE),
                        pltpu.VMEM((N_HEADS, LANES, Q_TILE), BF16),
                        pltpu.VMEM((IDX_HEADS, LANES, Q_TILE), BF16),
                        pltpu.VMEM((N_HEADS, SUBLANES, Q_TILE), F32),
                        pltpu.VMEM((2, N_HEADS, SUBLANES, Q_TILE), F32),
                        pltpu.VMEM((2, N_HEADS, SUBLANES, Q_TILE), F32),
                        pltpu.VMEM((2, N_HEADS, K_CHUNK, Q_TILE), BF16),
                        pltpu.VMEM((N_HEADS, HEAD_DIM + BF16_SUBLANES, Q_TILE), F32)],
        compiler_params=pltpu.CompilerParams(
            dimension_semantics=("arbitrary", "arbitrary"), vmem_limit_bytes=VMEM_LIMIT),
        name="dsa",
    )(qt, qit, wit, k, ki, vt, *expert_w)


def _max_all(x):
    return jnp.max(jnp.max(x, axis=0, keepdims=True), axis=1, keepdims=True)


def _sum_all(x):
    return jnp.sum(jnp.sum(x, axis=0, keepdims=True), axis=1, keepdims=True)


def _pack_bf16_pairs(x):
    w = x.shape[1] // 2
    lo = pltpu.bitcast(x[:, :w].astype(BF16).astype(F32), jnp.int32)
    hi = pltpu.bitcast(x[:, w:].astype(BF16).astype(F32), jnp.int32)
    return lax.shift_right_logical(lo, 16) | (hi & jnp.int32(-65536))


def _unpack_bf16_pairs(p):
    lo = pltpu.bitcast(lax.shift_left(p, 16), F32)
    hi = pltpu.bitcast(p & jnp.int32(-65536), F32)
    return jnp.concatenate([lo, hi], axis=1)


def _router_topk(logits_t, rbias):
    r = logits_t.shape[1]
    shape3 = (N_GROUPS, GROUP_SIZE, r)
    scores = jax.nn.sigmoid(logits_t).reshape(shape3)
    biased = scores + rbias.reshape(N_GROUPS, GROUP_SIZE, 1)
    in_grp = lax.broadcasted_iota(jnp.int32, shape3, 1).astype(F32)
    m1 = jnp.max(biased, axis=1, keepdims=True)
    first = jnp.min(jnp.where(biased == m1, in_grp, float(GROUP_SIZE)), axis=1, keepdims=True)
    m2 = jnp.max(jnp.where(in_grp == first, NEG_INF, biased), axis=1, keepdims=True)
    cur = m1 + m2

    grp_idx = lax.broadcasted_iota(jnp.int32, (N_GROUPS, 1, r), 0).astype(F32)
    grp_sel = jnp.zeros((N_GROUPS, 1, r), F32)
    for _ in range(TOPK_GROUPS):
        m = jnp.max(cur, axis=0, keepdims=True)
        pick = grp_idx == jnp.min(jnp.where(cur == m, grp_idx, float(N_GROUPS)), axis=0, keepdims=True)
        grp_sel = jnp.where(pick, 1.0, grp_sel)
        cur = jnp.where(pick, NEG_INF, cur)

    cur = jnp.where(jnp.broadcast_to(grp_sel, shape3) > 0.0, biased, NEG_INF)
    exp_idx = lax.broadcasted_iota(jnp.int32, shape3, 0).astype(F32) * GROUP_SIZE + in_grp
    chosen = jnp.zeros(shape3, F32)
    ids, wts = [], []
    for _ in range(TOP_K):
        m = _max_all(cur)
        first = -_max_all(-jnp.where(cur == m, exp_idx, float(N_EXPERTS)))
        pick = exp_idx == first
        chosen = jnp.where(pick, 1.0, chosen)
        cur = jnp.where(pick, NEG_INF, cur)
        ids.append(first.reshape(1, r))
        wts.append(_sum_all(jnp.where(pick, scores, 0.0)).reshape(1, r))
    ids = jnp.concatenate(ids, axis=0)
    wts = jnp.concatenate(wts, axis=0)
    gates = wts / jnp.sum(wts, axis=0, keepdims=True) * ROUTED_SCALE
    return ids, gates, chosen.reshape(N_EXPERTS, r), exp_idx


def _route_kernel(conv_ref, attn_ref, hn_ref, woc_ref, woa_ref, g1_ref, b1_ref,
                  wsg_ref, wsu_ref, wsd_ref, wrh_ref, wrl_ref, rb_ref,
                  xp_ref, base_ref, gates_ref, ek_ref, rk_ref, cnt_ref, before_ref):
    step = pl.program_id(0)
    tr = hn_ref.shape[0]
    mix = _dot(conv_ref[...], woc_ref[...]) + _dot(attn_ref[...], woa_ref[...])
    h1 = _layer_norm_rows(DN_ALPHA * hn_ref[...] + mix, g1_ref[...], b1_ref[...])
    xb = h1.astype(BF16)
    xp_ref[...] = _pack_bf16_pairs(h1)

    shared = jax.nn.silu(_dot(xb, wsg_ref[...])) * _dot(xb, wsu_ref[...])
    base_ref[...] = DN_ALPHA * h1 + _dot(shared.astype(BF16), wsd_ref[...])

    x_lo = (h1 - xb.astype(F32)).astype(BF16)
    logits_t = (_dot_nt(wrh_ref[...], xb) + _dot_nt(wrh_ref[...], x_lo) + _dot_nt(wrl_ref[...], xb))
    ids, gates, chosen, exp_idx = _router_topk(logits_t, rb_ref[...])
    padded = jnp.concatenate([gates, jnp.zeros((LANES - TOP_K, tr), F32)], axis=0)
    gates_ref[...] = padded.T
    ek_ref[...] = ids.astype(jnp.int32)

    @pl.when(step == 0)
    def _():
        cnt_ref[...] = jnp.zeros(cnt_ref.shape, F32)
        t_i = lax.broadcasted_iota(jnp.int32, (tr, tr), 0)
        t_j = lax.broadcasted_iota(jnp.int32, (tr, tr), 1)
        before_ref[...] = jnp.where(t_i < t_j, 1.0, 0.0).astype(BF16)

    chosen_b = chosen.astype(BF16)
    running = cnt_ref[...]
    rank = _dot(chosen_b, before_ref[...]) + jnp.concatenate([running] * (tr // LANES), axis=1)
    rank3 = rank.reshape(N_GROUPS, GROUP_SIZE, tr)
    rk = [_sum_all(jnp.where(exp_idx == ids[k:k + 1].reshape(1, 1, tr), rank3, 0.0)).reshape(1, tr)
          for k in range(TOP_K)]
    rk_ref[...] = jnp.concatenate(rk, axis=0).astype(jnp.int32)
    cnt_ref[...] = running + _dot(chosen_b, jnp.ones((tr, LANES), BF16))


def _route_call(conv, attn, hn, woc, woa, g1, b1, wsg, wsu, wsd, wrh, wrl, rbias):
    n, D = hn.shape
    tr = _pick_tile(n, (768, 512, 256))

    def row_spec(w):
        return pl.BlockSpec((tr, w), lambda i: (i, 0))

    def col_spec(r):
        return pl.BlockSpec((r, tr), lambda i: (0, i))

    def full(a):
        return pl.BlockSpec(a.shape, lambda i: (0,) * a.ndim)

    consts = [woc, woa, g1, b1, wsg, wsu, wsd, wrh, wrl, rbias]
    return pl.pallas_call(
        _route_kernel,
        grid=(n // tr,),
        in_specs=[row_spec(CONV_CH), row_spec(ATTN_W), row_spec(D)] + [full(a) for a in consts],
        out_specs=[row_spec(D // 2), row_spec(D), row_spec(LANES), col_spec(TOP_K), col_spec(TOP_K),
                   pl.BlockSpec((N_EXPERTS, LANES), lambda i: (0, 0))],
        out_shape=[jax.ShapeDtypeStruct((n, D // 2), jnp.int32),
                   jax.ShapeDtypeStruct((n, D), F32),
                   jax.ShapeDtypeStruct((n, LANES), F32),
                   jax.ShapeDtypeStruct((TOP_K, n), jnp.int32),
                   jax.ShapeDtypeStruct((TOP_K, n), jnp.int32),
                   jax.ShapeDtypeStruct((N_EXPERTS, LANES), F32)],
        scratch_shapes=[pltpu.VMEM((tr, tr), BF16)],
        compiler_params=pltpu.CompilerParams(
            dimension_semantics=("arbitrary",), vmem_limit_bytes=VMEM_LIMIT),
        name="route",
    )(conv, attn, hn, *consts)


def _plan_kernel(cnt_ref, ek_ref, rk_ref, slot_ref, blk_ref, *, n_blocks):
    tr = ek_ref.shape[1]
    counts = cnt_ref[...]
    blocks = jnp.floor((counts + (EXPERT_BLOCK - 1.0)) * _INV_EXPERT_BLOCK)
    first_blk = [jnp.zeros((1, LANES), F32)]
    for e in range(1, N_EXPERTS):
        first_blk.append(first_blk[-1] + blocks[e - 1:e])
    ek = ek_ref[...]
    seg = jnp.zeros(ek.shape, F32)
    for e in range(N_EXPERTS):
        start = first_blk[e] * EXPERT_BLOCK
        seg = jnp.where(ek == e, jnp.concatenate([start] * (tr // LANES), axis=1), seg)
    slot_ref[...] = seg.astype(jnp.int32) + rk_ref[...]

    end_blk = jnp.concatenate(first_blk, axis=0) + blocks
    w = blk_ref.shape[1]
    blk_idx = lax.broadcasted_iota(jnp.int32, (N_EXPERTS, w), 1).astype(F32)
    done = jnp.where(jnp.concatenate([end_blk] * (w // LANES), axis=1) <= blk_idx, 1.0, 0.0)
    owner = jnp.minimum(jnp.sum(done, axis=0, keepdims=True), N_EXPERTS - 1.0)
    used = jnp.concatenate([end_blk[N_EXPERTS - 1:]] * (w // LANES), axis=1)
    lane = lax.broadcasted_iota(jnp.int32, (1, w), 1)
    table = jnp.where(lane == n_blocks, used, owner)
    blk_ref[...] = jnp.broadcast_to(table, blk_ref.shape).astype(jnp.int32)


def _plan_call(cnt, ek, rk, n_blocks):
    k, n = ek.shape
    tr = _pick_tile(n, (768, 512, 256))
    w = -(-(n_blocks + 1) // LANES) * LANES
    col = pl.BlockSpec((k, tr), lambda i: (0, i))
    return pl.pallas_call(
        functools.partial(_plan_kernel, n_blocks=n_blocks),
        grid=(n // tr,),
        in_specs=[pl.BlockSpec(cnt.shape, lambda i: (0, 0)), col, col],
        out_specs=[col, pl.BlockSpec((SUBLANES, w), lambda i: (0, 0))],
        out_shape=[jax.ShapeDtypeStruct((k, n), jnp.int32), jax.ShapeDtypeStruct((SUBLANES, w), jnp.int32)],
        compiler_params=pltpu.CompilerParams(dimension_semantics=("arbitrary",)),
        name="plan",
    )(cnt, ek, rk)


def _sc_workers():
    info = plsc.get_sparse_core_info()
    return info.num_cores, info.num_subcores


def _sc_scatter_rows(src, slot, n_out):
    n_src, d = src.shape
    n_choices = slot.shape[0]
    rows = SC_SCATTER_ROWS
    n_chunks = n_src // rows
    slots = slot.reshape(n_choices * n_chunks, rows)
    n_cores, n_sub = _sc_workers()
    n_workers = n_cores * n_sub

    def body(src_hbm, slot_hbm, out_hbm, idx_v, rows_v):
        wid = lax.axis_index("s") * n_cores + lax.axis_index("c")

        @pl.loop(0, -(-n_chunks // n_workers))
        def _(j):
            u = j * n_workers + wid

            @pl.when(u < n_chunks)
            def _():
                pltpu.sync_copy(src_hbm.at[pl.ds(u * rows, rows)], rows_v)
                for k in range(n_choices):
                    pltpu.sync_copy(slot_hbm.at[pl.ds(k * n_chunks + u, 1)], idx_v)
                    pltpu.sync_copy(rows_v, out_hbm.at[idx_v.at[0]])

    return pl.kernel(
        body, out_type=jax.ShapeDtypeStruct((n_out, d), src.dtype),
        mesh=plsc.VectorSubcoreMesh(core_axis_name="c", subcore_axis_name="s"),
        scratch_types=[pltpu.VMEM((1, rows), jnp.int32), pltpu.VMEM((rows, d), src.dtype)],
        name="dispatch_rows",
    )(src, slots)


def _sc_gather_rows(table, slots):
    units = slots.shape[0]
    d = table.shape[1]
    n_cores, n_sub = _sc_workers()
    n_workers = n_cores * n_sub
    assert slots.shape[1] == SC_ROWS

    def body(table_hbm, slot_hbm, out_hbm, idx_v, rows_v):
        wid = lax.axis_index("s") * n_cores + lax.axis_index("c")

        @pl.loop(0, -(-units // n_workers))
        def _(j):
            u = j * n_workers + wid

            @pl.when(u < units)
            def _():
                pltpu.sync_copy(slot_hbm.at[pl.ds(u, 1)], idx_v)
                pltpu.sync_copy(table_hbm.at[idx_v.at[0]], rows_v)
                pltpu.sync_copy(rows_v, out_hbm.at[pl.ds(u * SC_ROWS, SC_ROWS)])

    return pl.kernel(
        body, out_type=jax.ShapeDtypeStruct((units * SC_ROWS, d), table.dtype),
        mesh=plsc.VectorSubcoreMesh(core_axis_name="c", subcore_axis_name="s"),
        scratch_types=[pltpu.VMEM((1, SC_ROWS), jnp.int32), pltpu.VMEM((SC_ROWS, d), table.dtype)],
        name="collect_rows",
    )(table, slots)


def _expert_kernel(blk_ref, xs_ref, *rest, per_share, n_shares):
    w_hbm, (ys_ref, wg_buf, wu_buf, wd_buf, sem) = rest[:3 * n_shares], rest[3 * n_shares:]
    b = pl.program_id(0)
    n_used = blk_ref[pl.num_programs(0)]

    def weight_copies(step, share):
        slot = step % EXPERT_WEIGHT_SLOTS
        local = blk_ref[step] - share * per_share
        return [pltpu.make_async_copy(w_hbm[3 * share + j].at[local], buf.at[slot], sem.at[j, slot])
                for j, buf in enumerate((wg_buf, wu_buf, wd_buf))]

    def for_weights_of(step, action):
        @pl.when(step < n_used)
        def _():
            for s in range(n_shares):
                @pl.when(blk_ref[step] // per_share == s)
                def _():
                    for copy in weight_copies(step, s):
                        action(copy)

    @pl.when(b == 0)
    def _():
        for first in range(EXPERT_LOOKAHEAD):
            for_weights_of(jnp.int32(first), lambda copy: copy.start())

    for_weights_of(b + EXPERT_LOOKAHEAD, lambda copy: copy.start())
    for_weights_of(b, lambda copy: copy.wait())

    @pl.when(b < n_used)
    def _():
        slot = b % EXPERT_WEIGHT_SLOTS
        for r0 in range(0, EXPERT_BLOCK, EXPERT_SUB):
            x = _unpack_bf16_pairs(xs_ref[r0:r0 + EXPERT_SUB, :]).astype(BF16)
            hdn = (jax.nn.silu(_dot(x, wg_buf[slot])) * _dot(x, wu_buf[slot])).astype(BF16)
            ys_ref[r0:r0 + EXPERT_SUB, :] = _pack_bf16_pairs(_dot(hdn, wd_buf[slot]))


def _expert_call(blk_exp, xs, weight_shares):
    p, half = xs.shape
    per_share = weight_shares[0][0].shape[0]
    wg, wu, wd = weight_shares[0]
    n_blocks = p // EXPERT_BLOCK
    assert blk_exp.shape == (n_blocks + 1,)

    def row_block(b, blk):
        return (jnp.minimum(b, blk[n_blocks] - 1), 0)

    rows = pl.BlockSpec((EXPERT_BLOCK, half), row_block)
    grid_spec = pltpu.PrefetchScalarGridSpec(
        num_scalar_prefetch=1,
        grid=(n_blocks,),
        in_specs=[rows] + [pl.BlockSpec(memory_space=pl.ANY) for share in weight_shares for _ in share],
        out_specs=rows,
        scratch_shapes=[pltpu.VMEM((EXPERT_WEIGHT_SLOTS,) + wg.shape[1:], BF16),
                        pltpu.VMEM((EXPERT_WEIGHT_SLOTS,) + wu.shape[1:], BF16),
                        pltpu.VMEM((EXPERT_WEIGHT_SLOTS,) + wd.shape[1:], BF16),
                        pltpu.SemaphoreType.DMA((3, EXPERT_WEIGHT_SLOTS))],
    )
    return pl.pallas_call(
        functools.partial(_expert_kernel, per_share=per_share, n_shares=len(weight_shares)),
        grid_spec=grid_spec,
        out_shape=jax.ShapeDtypeStruct((p, half), jnp.int32),
        compiler_params=pltpu.CompilerParams(
            dimension_semantics=("arbitrary",), vmem_limit_bytes=VMEM_LIMIT),
        name="experts",
    )(blk_exp, xs, *[a for share in weight_shares for a in share])


def _combine_kernel(g_ref, gates_ref, base_ref, g2_ref, b2_ref, *rest):
    o_ref = rest[-1]
    gates = gates_ref[...]
    acc = base_ref[...]
    for k in range(g_ref.shape[0]):
        acc = acc + _unpack_bf16_pairs(g_ref[k]) * gates[:, k:k + 1]
    o_ref[0] = _layer_norm_rows(acc, g2_ref[...], b2_ref[...])


def _combine_call(g, gates, base, g2, b2, result, batch, n_batch):
    k, n, half = g.shape
    D = base.shape[1]
    tr = SEQ_ALIGN
    seq = n - tr

    def row_spec(w):
        return pl.BlockSpec((tr, w), lambda i: (i, 0))

    vec = pl.BlockSpec((1, D), lambda i: (0, 0))
    in_specs = [pl.BlockSpec((k, tr, half), lambda i: (0, i, 0)), row_spec(LANES), row_spec(D), vec, vec]
    args = [g, gates, base, g2, b2]
    aliases = {}
    if result is not None:
        in_specs.append(pl.BlockSpec(memory_space=pl.ANY))
        args.append(result)
        aliases = {len(args) - 1: 0}
    return pl.pallas_call(
        _combine_kernel,
        grid=(n // tr,),
        in_specs=in_specs,
        out_specs=pl.BlockSpec((1, tr, D), lambda i: (batch, jnp.maximum(i - 1, 0), 0)),
        out_shape=jax.ShapeDtypeStruct((n_batch, seq, D), F32),
        input_output_aliases=aliases,
        compiler_params=pltpu.CompilerParams(
            dimension_semantics=("arbitrary",), vmem_limit_bytes=VMEM_LIMIT),
        name="combine",
    )(*args)


def _rope_tables(tp, lead):
    pos = jnp.arange(tp, dtype=F32) - lead
    inv = jnp.power(ROPE_THETA, -2.0 * jnp.arange(ROPE_HALF, dtype=F32) / ROPE_DIM)
    ang = pos[:, None] * inv[None, :]
    cos, sin = jnp.cos(ang), jnp.sin(ang)
    zeros = jnp.zeros((tp, HEAD_DIM - ROPE_DIM), F32)
    zh = jnp.zeros((tp, ROPE_HALF), F32)
    c64 = jnp.concatenate([cos, cos, jnp.ones_like(zeros)], axis=1)
    s1_64 = jnp.concatenate([-sin, zh, zeros], axis=1)
    s2_64 = jnp.concatenate([zh, sin, zeros], axis=1)
    rep = LANES // HEAD_DIM
    return (jnp.tile(c64, (1, rep)), jnp.tile(s1_64, (1, rep)), jnp.tile(s2_64, (1, rep)),
            cos.T, sin.T)


def kernel(x, meta_tokens, ln_emb_g, ln_emb_b, w_in, conv_w, conv_b, ln_conv_g, ln_conv_b, ln_kidx_g, ln_kidx_b, w_out, ln1_g, ln1_b, w_router, router_bias, w_gate, w_up, w_down, ws_gate, ws_up, ws_down, ln2_g, ln2_b):
    B, seq, D = x.shape
    assert w_in.shape[0] == DEPTH
    assert seq % SEQ_ALIGN == 0 and meta_tokens.shape[0] == N_META <= SEQ_ALIGN
    k_sel = min(INDEX_TOPK, seq // 4)
    tp = seq + SEQ_ALIGN
    n_pad = SEQ_ALIGN - N_META

    def row(a):
        return a.reshape(1, -1).astype(F32)

    w = w_in[0]
    o = 0
    parts = []
    for width in (CONV_CH, CONV_CH, ATTN_W, ATTN_W, ATTN_W, IDX_HEADS * IDX_DIM, IDX_DIM, IDX_HEADS):
        parts.append(w[:, o:o + width])
        o += width
    wa, wgl, wq, wk, wv, wqi, wki, wwi = parts
    wwi_t = jnp.concatenate([wwi.T, jnp.zeros((BF16_SUBLANES - IDX_HEADS, D), w.dtype)], axis=0)
    weights = (jnp.concatenate([wa, wgl], axis=1).astype(BF16), wq.T.astype(BF16), wk.astype(BF16),
               wv.T.astype(BF16), wqi.T.astype(BF16), jnp.concatenate([wki, wki], axis=1).astype(BF16),
               wwi_t.astype(BF16))

    def twice(a):
        return row(jnp.concatenate([a, a]))

    tabs = _rope_tables(tp, n_pad)
    inproj_consts = (row(ln_emb_g), row(ln_emb_b), weights, conv_w[0].astype(F32), row(conv_b[0]),
                     row(ln_conv_g[0]), row(ln_conv_b[0]), twice(ln_kidx_g[0]), twice(ln_kidx_b[0]))
    wr_t = w_router[0].T.astype(F32)
    wr_hi = wr_t.astype(BF16)
    wr_lo = (wr_t - wr_hi.astype(F32)).astype(BF16)
    route_consts = (w_out[0][:CONV_CH].astype(BF16), w_out[0][CONV_CH:].astype(BF16), row(ln1_g[0]),
                    row(ln1_b[0]), ws_gate[0].astype(BF16), ws_up[0].astype(BF16), ws_down[0].astype(BF16),
                    wr_hi, wr_lo, router_bias[0].reshape(-1, 1).astype(F32))
    meta = meta_tokens.astype(F32)
    n_blocks = -(-tp * TOP_K // EXPERT_BLOCK) + N_EXPERTS

    expert_w = (w_gate[0], w_up[0], w_down[0])
    weight_shares = []
    routed = []
    for b in range(B):
        hn, conv, qt, k, vt, qit, ki, wit = _inproj_call(x, b, meta, tabs, *inproj_consts)
        attn, *share = _dsa_call(qt, qit, wit, k, ki, vt, k_sel, n_pad, expert_w, b, B)
        weight_shares.append(share)
        xp, base, gates, ek, rk, cnt = _route_call(conv[0], attn[0], hn[0], *route_consts)

        slot, blk = _plan_call(cnt, ek, rk, n_blocks)
        slots = slot.reshape(TOP_K * tp // SC_ROWS, SC_ROWS)
        xs = _sc_scatter_rows(xp, slot, n_blocks * EXPERT_BLOCK)
        routed.append((xs, blk, slots, gates, base))

    result = None
    for b, (xs, blk, slots, gates, base) in enumerate(routed):
        ys = _expert_call(blk[0, :n_blocks + 1], xs, weight_shares)
        picked = _sc_gather_rows(ys, slots).reshape(TOP_K, tp, D // 2)
        result = _combine_call(picked, gates, base, row(ln2_g[0]), row(ln2_b[0]), result, b, B)
    return result
```
